```python
import math
import jax
import jax.numpy as jnp
from jax import lax
import numpy as np

D_MODEL = 1024
BATCH = 2
SEQ = 16384
DEPTH = 2

CTX_LEN = 256
GRID_W = 64
ROPE_BASE = 10000.0
NORM_EPS = 1e-6
NEG_INF = -1e30
Q_BLOCK = 128
HEAD_DIM = 64

SWA_HEADS = 4
SWA_KV_HEADS = 2
SWA_WINDOW = 128
SWA_BLOCK = 128
DIFF_HEADS = 4
DIFF_QK_DIM = 32
DIFF_V_DIM = 64
RET_HEADS = 4
RET_K_DIM = 64
RET_V_DIM = 64
RET_CHUNK = 128
MLA_HEADS = 4
MLA_Q_LORA = 256
MLA_KV_LORA = 128
MLA_NOPE_DIM = 64
MLA_ROPE_DIM = 32
MLA_V_DIM = 64

SWA_IN = SWA_HEADS * HEAD_DIM + 2 * SWA_KV_HEADS * HEAD_DIM
DIFF_IN = 2 * (DIFF_HEADS * 2 * DIFF_QK_DIM) + DIFF_HEADS * DIFF_V_DIM
RET_IN = 2 * RET_HEADS * RET_K_DIM + 2 * RET_HEADS * RET_V_DIM
MLA_IN = MLA_Q_LORA + MLA_KV_LORA + MLA_ROPE_DIM
D_IN = SWA_IN + DIFF_IN + RET_IN + MLA_IN
D_MIX = SWA_HEADS * HEAD_DIM + DIFF_HEADS * DIFF_V_DIM + RET_HEADS * RET_V_DIM + MLA_HEADS * MLA_V_DIM

MOE_GROUPS = 4
MOE_EXPERTS_PER_GROUP = 8
MOE_EXPERTS = MOE_GROUPS * MOE_EXPERTS_PER_GROUP
MOE_TOP_K = 2
MOE_HIDDEN = 512
MOE_BLOCK = 128

kernel_name = "hybrid_parallel_heads_hmoe_dit"


def rms_norm(x, g):
    xf = x.astype(jnp.float32)
    y = xf * lax.rsqrt(jnp.mean(xf * xf, axis=-1, keepdims=True) + NORM_EPS)
    return (y * g.astype(jnp.float32)).astype(x.dtype)


def modulate(h, shift, scale):
    return h * (1.0 + scale) + shift


def axial_rope(x, row, col):
    r = x.shape[-1]
    nf = r // 4
    inv = 1.0 / (ROPE_BASE ** (jnp.arange(nf, dtype=jnp.float32) / nf))

    def rot(xh, pos):
        ang = pos.astype(jnp.float32)[:, None] * inv[None, :]
        cos = jnp.cos(ang)[None, :, None, :]
        sin = jnp.sin(ang)[None, :, None, :]
        x1 = xh[..., :nf].astype(jnp.float32)
        x2 = xh[..., nf:].astype(jnp.float32)
        return jnp.concatenate([x1 * cos - x2 * sin, x1 * sin + x2 * cos], axis=-1)

    out = jnp.concatenate([rot(x[..., :2 * nf], row), rot(x[..., 2 * nf:], col)], axis=-1)
    return out.astype(x.dtype)


def swa_mixer(z, zc, sink, row, col, need_ctx):
    B, T, _ = z.shape
    Tc = zc.shape[1]
    G, R, d = SWA_KV_HEADS, SWA_HEADS // SWA_KV_HEADS, HEAD_DIM
    qw, kw = SWA_HEADS * d, G * d
    scale = d ** -0.5

    def split(u):
        b, t = u.shape[:2]
        return (u[..., :qw].reshape(b, t, SWA_HEADS, d),
                u[..., qw:qw + kw].reshape(b, t, G, d),
                u[..., qw + kw:].reshape(b, t, G, d))

    q, k, v = split(z)
    qc, kc, vc = split(zc)
    q = (axial_rope(q, row, col) * scale).reshape(B, T, G, R, d)
    k = axial_rope(k, row, col)
    sink_gr = sink.astype(jnp.float32).reshape(G, R)

    Lb = SWA_BLOCK
    nb = T // Lb
    nk = 3 * Lb
    qb = q.reshape(B, nb, Lb, G, R, d)

    def band(a):
        ap = jnp.pad(a, ((0, 0), (Lb, Lb), (0, 0), (0, 0))).reshape(B, nb + 2, Lb, G, d)
        return jnp.concatenate([ap[:, :-2], ap[:, 1:-1], ap[:, 2:]], axis=2)

    kb, vb = band(k), band(v)
    blk = jnp.arange(nb, dtype=jnp.int32)
    qpos = blk[:, None, None] * Lb + jnp.arange(Lb, dtype=jnp.int32)[None, :, None]
    kpos = (blk[:, None, None] - 1) * Lb + jnp.arange(nk, dtype=jnp.int32)[None, None, :]
    valid = (jnp.abs(qpos - kpos) <= SWA_WINDOW) & (kpos >= 0) & (kpos < T)
    s_band = jnp.where(valid[None, :, None, None],
                       jnp.einsum('bnqgrd,bnkgd->bngrqk', qb, kb).astype(jnp.float32), NEG_INF)
    s_ctx = jnp.einsum('bnqgrd,bcgd->bngrqc', qb, kc).astype(jnp.float32)
    s_sink = jnp.broadcast_to(sink_gr[None, None, :, :, None, None], (B, nb, G, R, Lb, 1))
    p = jax.nn.softmax(jnp.concatenate([s_band, s_ctx, s_sink], axis=-1), axis=-1)
    o = (jnp.einsum('bngrqk,bnkgd->bnqgrd', p[..., :nk].astype(vb.dtype), vb)
         + jnp.einsum('bngrqc,bcgd->bnqgrd', p[..., nk:nk + Tc].astype(vc.dtype), vc))
    y = o.reshape(B, T, SWA_HEADS * d)

    yc = None
    if need_ctx:
        qcs = qc.reshape(B, Tc, G, R, d) * scale
        sc = jnp.einsum('bqgrd,bcgd->bgrqc', qcs, kc).astype(jnp.float32)
        sc_sink = jnp.broadcast_to(sink_gr[None, :, :, None, None], (B, G, R, Tc, 1))
        pc = jax.nn.softmax(jnp.concatenate([sc, sc_sink], axis=-1), axis=-1)
        yc = jnp.einsum('bgrqc,bcgd->bqgrd', pc[..., :Tc].astype(vc.dtype), vc).reshape(B, Tc, SWA_HEADS * d)
    return y, yc


def diff_mixer(z, zc, lam_p, subln_g, lambda_init, row, col, need_ctx):
    B, T, _ = z.shape
    Tc = zc.shape[1]
    H, dq, dv = DIFF_HEADS, DIFF_QK_DIM, DIFF_V_DIM
    qw = H * 2 * dq
    scale = dq ** -0.5

    def split(u):
        b, t = u.shape[:2]
        return (u[..., :qw].reshape(b, t, 2 * H, dq),
                u[..., qw:2 * qw].reshape(b, t, 2 * H, dq),
                u[..., 2 * qw:].reshape(b, t, H, dv))

    q, k, v = split(z)
    qc, kc, vc = split(zc)
    q = axial_rope(q, row, col).reshape(B, T, H, 2, dq) * scale
    k = axial_rope(k, row, col).reshape(B, T, H, 2, dq)
    kc = kc.reshape(B, Tc, H, 2, dq)
    lp = lam_p.astype(jnp.float32)
    lam = jnp.exp(jnp.sum(lp[0] * lp[1])) - jnp.exp(jnp.sum(lp[2] * lp[3])) + lambda_init

    def attend(qb, kk, vv):
        s = jnp.einsum('bqhmd,bkhmd->bhmqk', qb, kk).astype(jnp.float32)
        p = jax.nn.softmax(s, axis=-1)
        a = p[:, :, 0] - lam * p[:, :, 1]
        return jnp.einsum('bhqk,bkhe->bqhe', a.astype(vv.dtype), vv)

    def finish(o):
        b, t = o.shape[:2]
        return (rms_norm(o, subln_g) * (1.0 - lambda_init)).reshape(b, t, H * dv)

    k_all = jnp.concatenate([kc, k], axis=1)
    v_all = jnp.concatenate([vc, v], axis=1)
    nb = T // Q_BLOCK
    qblocks = jnp.moveaxis(q.reshape(B, nb, Q_BLOCK, H, 2, dq), 1, 0)
    o = lax.map(lambda qb: attend(qb, k_all, v_all), qblocks)
    y = finish(jnp.moveaxis(o, 0, 1).reshape(B, T, H, dv))

    yc = None
    if need_ctx:
        yc = finish(attend(qc.reshape(B, Tc, H, 2, dq) * scale, kc, vc))
    return y, yc


def retention_final_state(k, v, lg):
    T = k.shape[1]
    w = jnp.exp((T - 1 - jnp.arange(T, dtype=jnp.float32))[:, None] * lg[None, :])
    return jnp.einsum('bthd,bthe->bhde', k.astype(jnp.float32) * w[None, :, :, None], v.astype(jnp.float32))


def retention_chunkwise(q, k, v, s0, lg):
    B, T, H, _ = q.shape
    dv = v.shape[-1]
    L = RET_CHUNK
    n = T // L
    i = jnp.arange(L, dtype=jnp.float32)
    rel = i[:, None] - i[None, :]
    inner_decay = jnp.where(rel[None] >= 0, jnp.exp(jnp.maximum(rel, 0.0)[None] * lg[:, None, None]), 0.0)
    q_decay = jnp.exp((i[:, None] + 1.0) * lg[None, :])
    k_decay = jnp.exp((L - 1.0 - i)[:, None] * lg[None, :])
    chunk_decay = jnp.exp(L * lg)

    def to_chunks(a):
        return jnp.moveaxis(a.astype(jnp.float32).reshape(B, n, L, H, a.shape[-1]), 1, 0)

    def step(s, xs):
        qc, kc, vc = xs
        a = jnp.einsum('blhd,bmhd->bhlm', qc, kc) * inner_decay[None]
        o = jnp.einsum('bhlm,bmhe->blhe', a, vc)
        o = o + jnp.einsum('blhd,bhde->blhe', qc * q_decay[None, :, :, None], s)
        s = s * chunk_decay[None, :, None, None] + jnp.einsum('bmhd,bmhe->bhde', kc * k_decay[None, :, :, None], vc)
        return s, o

    s, o = lax.scan(step, s0, (to_chunks(q), to_chunks(k), to_chunks(v)))
    return s, jnp.moveaxis(o, 0, 1).reshape(B, T, H, dv)


def gated_group_norm(o, g, gn_g, dtype):
    of = o.astype(jnp.float32)
    mu = jnp.mean(of, axis=-1, keepdims=True)
    var = jnp.mean(jnp.square(of - mu), axis=-1, keepdims=True)
    on = ((of - mu) * lax.rsqrt(var + NORM_EPS)).reshape(o.shape[0], o.shape[1], -1)
    return (on * gn_g.astype(jnp.float32) * jax.nn.silu(g.astype(jnp.float32))).astype(dtype)


def retention_mixer(z, zc, decay_logit, gn_g, row, col, need_ctx):
    B, T, _ = z.shape
    Tc = zc.shape[1]
    H, dk, dv = RET_HEADS, RET_K_DIM, RET_V_DIM
    qw, vw = H * dk, H * dv
    scale = dk ** -0.5

    def split(u):
        b, t = u.shape[:2]
        return (u[..., :qw].reshape(b, t, H, dk), u[..., qw:2 * qw].reshape(b, t, H, dk),
                u[..., 2 * qw:2 * qw + vw].reshape(b, t, H, dv), u[..., 2 * qw + vw:])

    q, k, v, g = split(z)
    qc, kc, vc, gc = split(zc)
    q = axial_rope(q, row, col) * scale
    k = axial_rope(k, row, col)
    log_gamma = jax.nn.log_sigmoid(decay_logit.astype(jnp.float32))

    outs, outs_c = [], []
    for direction in range(2):
        lg = log_gamma[direction]
        fl = (lambda a: jnp.flip(a, axis=1)) if direction == 1 else (lambda a: a)
        if need_ctx:
            s_ctx, oc = retention_chunkwise(fl(qc * scale), fl(kc), fl(vc),
                                            jnp.zeros((B, H, dk, dv), jnp.float32), lg)
            outs_c.append(fl(oc))
        else:
            s_ctx = retention_final_state(fl(kc), fl(vc), lg)
        _, o = retention_chunkwise(fl(q), fl(k), fl(v), s_ctx, lg)
        outs.append(fl(o))

    y = gated_group_norm(outs[0] + outs[1], g, gn_g, z.dtype)
    yc = gated_group_norm(outs_c[0] + outs_c[1], gc, gn_g, zc.dtype) if need_ctx else None
    return y, yc


def mla_mixer(z, zc, q_norm_g, w_uq, kv_norm_g, w_ukv, row, col, need_ctx):
    B, T, _ = z.shape
    Tc = zc.shape[1]
    H, dn, dr, dv = MLA_HEADS, MLA_NOPE_DIM, MLA_ROPE_DIM, MLA_V_DIM
    scale = (dn + dr) ** -0.5

    def proj_q(u):
        b, t = u.shape[:2]
        qq = (rms_norm(u[..., :MLA_Q_LORA], q_norm_g) @ w_uq).reshape(b, t, H, dn + dr)
        return qq[..., :dn] * scale, qq[..., dn:] * scale

    def proj_kv(u):
        b, t = u.shape[:2]
        kv = (rms_norm(u[..., MLA_Q_LORA:MLA_Q_LORA + MLA_KV_LORA], kv_norm_g) @ w_ukv).reshape(b, t, H, dn + dv)
        return kv[..., :dn], kv[..., dn:], u[..., MLA_Q_LORA + MLA_KV_LORA:]

    qn, qr = proj_q(z)
    qr = axial_rope(qr, row, col)
    kn, v, kr = proj_kv(z)
    kr = axial_rope(kr[:, :, None, :], row, col)[:, :, 0]
    kcn, vc, kcr = proj_kv(zc)

    def attend(qn_b, qr_b, kn_, kr_, v_):
        s = (jnp.einsum('bqhd,bkhd->bhqk', qn_b, kn_)
             + jnp.einsum('bqhr,bkr->bhqk', qr_b, kr_)).astype(jnp.float32)
        p = jax.nn.softmax(s, axis=-1)
        return jnp.einsum('bhqk,bkhe->bqhe', p.astype(v_.dtype), v_)

    kn_all = jnp.concatenate([kcn, kn], axis=1)
    kr_all = jnp.concatenate([kcr, kr], axis=1)
    v_all = jnp.concatenate([vc, v], axis=1)
    nb = T // Q_BLOCK
    qn_b = jnp.moveaxis(qn.reshape(B, nb, Q_BLOCK, H, dn), 1, 0)
    qr_b = jnp.moveaxis(qr.reshape(B, nb, Q_BLOCK, H, dr), 1, 0)
    o = lax.map(lambda a: attend(a[0], a[1], kn_all, kr_all, v_all), (qn_b, qr_b))
    y = jnp.moveaxis(o, 0, 1).reshape(B, T, H * dv)

    yc = None
    if need_ctx:
        qcn, qcr = proj_q(zc)
        yc = attend(qcn, qcr, kcn, kcr, vc).reshape(B, Tc, H * dv)
    return y, yc


def grouped_experts(x, e_idx, e_w, w_gate, w_up, w_down):
    N, D = x.shape
    E, K, Lb = MOE_EXPERTS, e_idx.shape[1], MOE_BLOCK
    M = N * K
    flat_e = e_idx.reshape(-1).astype(jnp.int32)
    flat_tok = jnp.arange(M, dtype=jnp.int32) // K
    flat_w = e_w.reshape(-1)
    order = jnp.argsort(flat_e)
    se = flat_e[order]
    counts = jnp.bincount(flat_e, length=E).astype(jnp.int32)
    padded = (counts + Lb - 1) // Lb * Lb
    start = jnp.cumsum(counts) - counts
    pend = jnp.cumsum(padded)
    pstart = pend - padded
    dest = pstart[se] + jnp.arange(M, dtype=jnp.int32) - start[se]
    n_blocks = (M + E * (Lb - 1) + Lb - 1) // Lb
    cap = n_blocks * Lb
    buf_tok = jnp.zeros((cap,), jnp.int32).at[dest].set(flat_tok[order])
    buf_w = jnp.zeros((cap,), x.dtype).at[dest].set(flat_w[order].astype(x.dtype))
    block_e = jnp.minimum(jnp.searchsorted(pend, jnp.arange(n_blocks, dtype=jnp.int32) * Lb, side='right'), E - 1)
    xb = x[buf_tok].reshape(n_blocks, Lb, D)

    def run(args):
        xblk, e = args
        return (jax.nn.silu(xblk @ w_gate[e]) * (xblk @ w_up[e])) @ w_down[e]

    yb = lax.map(run, (xb, block_e)).reshape(cap, D)
    return jnp.zeros_like(x).at[buf_tok].add(yb * buf_w[:, None])


def hier_moe(x, w_group, b_group, w_expert, b_expert, w_gate, w_up, w_down):
    N = x.shape[0]
    rows = jnp.arange(N)
    g_logits = (x @ w_group + b_group).astype(jnp.float32)
    g_prob = jax.nn.softmax(g_logits, axis=-1)
    g_idx = jnp.argmax(g_logits, axis=-1)
    g_w = g_prob[rows, g_idx]
    e_logits = (x @ w_expert + b_expert).astype(jnp.float32).reshape(N, MOE_GROUPS, MOE_EXPERTS_PER_GROUP)
    e_logits = e_logits[rows, g_idx]
    top_val, top_loc = lax.top_k(e_logits, MOE_TOP_K)
    e_w = jax.nn.softmax(top_val, axis=-1) * g_w[:, None]
    e_idx = g_idx[:, None] * MOE_EXPERTS_PER_GROUP + top_loc
    return grouped_experts(x, e_idx, e_w, w_gate, w_up, w_down)


def trunk_layer(x, xc, c_act, cc_act, row, col, layer_idx, need_ctx,
                w_mod, b_mod, norm1_g, norm2_g, w_in, w_out, swa_sink, diff_lambda, diff_subln_g,
                ret_decay_logit, ret_gn_g, mla_q_norm_g, mla_w_uq, mla_kv_norm_g, mla_w_ukv,
                moe_w_group, moe_b_group, moe_w_expert, moe_b_expert, moe_w_gate, moe_w_up, moe_w_down):
    B, T, D = x.shape
    Tc = xc.shape[1]
    sh1, sc1, g1, sh2, sc2, g2 = jnp.split(c_act @ w_mod + b_mod, 6, axis=-1)
    csh1, csc1, cg1, csh2, csc2, cg2 = jnp.split(cc_act @ w_mod + b_mod, 6, axis=-1)

    h = modulate(rms_norm(x, norm1_g), sh1[:, None], sc1[:, None])
    hc = modulate(rms_norm(xc, norm1_g), csh1, csc1)
    z = h @ w_in
    zc = hc @ w_in
    o1, o2, o3 = SWA_IN, SWA_IN + DIFF_IN, SWA_IN + DIFF_IN + RET_IN
    lambda_init = 0.8 - 0.6 * math.exp(-0.3 * layer_idx)

    ya, yac = swa_mixer(z[..., :o1], zc[..., :o1], swa_sink, row, col, need_ctx)
    yb, ybc = diff_mixer(z[..., o1:o2], zc[..., o1:o2], diff_lambda, diff_subln_g, lambda_init, row, col, need_ctx)
    yr, yrc = retention_mixer(z[..., o2:o3], zc[..., o2:o3], ret_decay_logit, ret_gn_g, row, col, need_ctx)
    ym, ymc = mla_mixer(z[..., o3:], zc[..., o3:], mla_q_norm_g, mla_w_uq, mla_kv_norm_g, mla_w_ukv, row, col, need_ctx)

    x = x + g1[:, None] * (jnp.concatenate([ya, yb, yr, ym], axis=-1) @ w_out)
    h2 = modulate(rms_norm(x, norm2_g), sh2[:, None], sc2[:, None])
    moe_args = (moe_w_group, moe_b_group, moe_w_expert, moe_b_expert, moe_w_gate, moe_w_up, moe_w_down)
    if need_ctx:
        xc = xc + cg1 * (jnp.concatenate([yac, ybc, yrc, ymc], axis=-1) @ w_out)
        h2c = modulate(rms_norm(xc, norm2_g), csh2, csc2)
        m = hier_moe(jnp.concatenate([h2.reshape(-1, D), h2c.reshape(-1, D)], axis=0), *moe_args)
        x = x + g2[:, None] * m[:B * T].reshape(B, T, D)
        xc = xc + cg2 * m[B * T:].reshape(B, Tc, D)
    else:
        x = x + g2[:, None] * hier_moe(h2.reshape(-1, D), *moe_args).reshape(B, T, D)
    return x, xc


def setup_inputs(seed: int = 0) -> dict:
    key = jax.random.key(seed)
    ks = iter(jax.random.split(key, 32))
    f32 = jnp.float32
    L, D = DEPTH, D_MODEL

    def nrm(shape, scale):
        return jax.random.normal(next(ks), shape, f32) * scale

    ret_logit0 = jnp.log(2.0 ** (5.0 + jnp.arange(RET_HEADS, dtype=f32)) - 1.0)
    return {
        "x": nrm((BATCH, SEQ, D), 1.0),
        "c": nrm((BATCH, D), 1.0),
        "ctx": nrm((BATCH, CTX_LEN, D), 1.0),
        "c_ctx": nrm((D,), 1.0),
        "w_mod": nrm((L, D, 6 * D), 0.5 * D ** -0.5),
        "b_mod": nrm((L, 6 * D), 0.02),
        "norm1_g": 1.0 + nrm((L, D), 0.02),
        "norm2_g": 1.0 + nrm((L, D), 0.02),
        "w_in": nrm((L, D, D_IN), D ** -0.5),
        "w_out": nrm((L, D_MIX, D), D_MIX ** -0.5),
        "swa_sink": nrm((L, SWA_HEADS), 0.5),
        "diff_lambda": nrm((L, 4, DIFF_QK_DIM), 0.1),
        "diff_subln_g": 1.0 + nrm((L, DIFF_V_DIM), 0.02),
        "ret_decay_logit": ret_logit0[None, None, :] + nrm((L, 2, RET_HEADS), 0.1),
        "ret_gn_g": 1.0 + nrm((L, RET_HEADS * RET_V_DIM), 0.02),
        "mla_q_norm_g": 1.0 + nrm((L, MLA_Q_LORA), 0.02),
        "mla_w_uq": nrm((L, MLA_Q_LORA, MLA_HEADS * (MLA_NOPE_DIM + MLA_ROPE_DIM)), MLA_Q_LORA ** -0.5),
        "mla_kv_norm_g": 1.0 + nrm((L, MLA_KV_LORA), 0.02),
        "mla_w_ukv": nrm((L, MLA_KV_LORA, MLA_HEADS * (MLA_NOPE_DIM + MLA_V_DIM)), MLA_KV_LORA ** -0.5),
        "moe_w_group": nrm((L, D, MOE_GROUPS), D ** -0.5),
        "moe_b_group": nrm((L, MOE_GROUPS), 0.01),
        "moe_w_expert": nrm((L, D, MOE_EXPERTS), D ** -0.5),
        "moe_b_expert": nrm((L, MOE_EXPERTS), 0.01),
        "moe_w_gate": nrm((L, MOE_EXPERTS, D, MOE_HIDDEN), D ** -0.5),
        "moe_w_up": nrm((L, MOE_EXPERTS, D, MOE_HIDDEN), D ** -0.5),
        "moe_w_down": nrm((L, MOE_EXPERTS, MOE_HIDDEN, D), MOE_HIDDEN ** -0.5),
        "final_norm_g": 1.0 + nrm((D,), 0.02),
    }


def reference(x, c, ctx, c_ctx, w_mod, b_mod, norm1_g, norm2_g, w_in, w_out, swa_sink, diff_lambda,
              diff_subln_g, ret_decay_logit, ret_gn_g, mla_q_norm_g, mla_w_uq, mla_kv_norm_g, mla_w_ukv,
              moe_w_group, moe_b_group, moe_w_expert, moe_b_expert, moe_w_gate, moe_w_up, moe_w_down,
              final_norm_g):
    T = x.shape[1]
    ROWS = T // GRID_W
    row = jnp.repeat(jnp.arange(ROWS, dtype=jnp.int32), GRID_W)
    col = jnp.tile(jnp.arange(GRID_W, dtype=jnp.int32), ROWS)
    c_act = jax.nn.silu(c)
    cc_act = jax.nn.silu(c_ctx)
    xc = ctx
    for l in range(DEPTH):
        x, xc = trunk_layer(
            x, xc, c_act, cc_act, row, col, l, l < DEPTH - 1,
            w_mod[l], b_mod[l], norm1_g[l], norm2_g[l], w_in[l], w_out[l], swa_sink[l], diff_lambda[l],
            diff_subln_g[l], ret_decay_logit[l], ret_gn_g[l], mla_q_norm_g[l], mla_w_uq[l], mla_kv_norm_g[l],
            mla_w_ukv[l], moe_w_group[l], moe_b_group[l], moe_w_expert[l], moe_b_expert[l], moe_w_gate[l],
            moe_w_up[l], moe_w_down[l])
    return rms_norm(x, final_norm_g)
```

```python
import functools
import math

import jax
import jax.numpy as jnp
from jax import lax
from jax.experimental import pallas as pl
from jax.experimental.pallas import tpu as pltpu

F32 = jnp.float32
BF16 = jnp.bfloat16

D_MODEL = 1024
CTX_LEN = 256
GRID_W = 64
ROPE_BASE = 10000.0
NORM_EPS = 1e-6
NEG_INF = -1e30
DEPTH = 2

SWA_WINDOW = 128
RET_CHUNK = 128
MLA_SCALE = (64 + 32) ** -0.5
MOE_GROUPS = 4
MOE_PER_GROUP = 8
MOE_EXPERTS = 32
MOE_HIDDEN = 512

LANES = 128
TM = 256
TQ = 256
MOE_ROWS = 256
VMEM_LIMIT = 56 * 1024 * 1024

C_SWA_Q, C_SWA_K, C_SWA_V = 0, 256, 384
C_DIF_Q, C_DIF_K, C_DIF_V = 512, 768, 1024
C_RET_Q, C_RET_K, C_RET_V, C_RET_G = 1280, 1536, 1792, 2048
C_MLA_Q, C_MLA_KV, C_MLA_KR = 2304, 2560, 2688
D_IN_PAD = 2816


def _cparams(sem):
    return pltpu.CompilerParams(dimension_semantics=sem, vmem_limit_bytes=VMEM_LIMIT)


def _dot(a, b):
    return jnp.dot(a, b, preferred_element_type=F32)


def _dot_nt(a, b):
    return lax.dot_general(a, b, (((1,), (1,)), ((), ())), preferred_element_type=F32)


def _dot_tn(a, b):
    return lax.dot_general(a, b, (((0,), (0,)), ((), ())), preferred_element_type=F32)


def _rms(x, g):
    ms = jnp.mean(x * x, axis=-1, keepdims=True)
    return x * lax.rsqrt(ms + NORM_EPS) * g


def _lane(shape):
    return lax.broadcasted_iota(jnp.int32, shape, len(shape) - 1)


def _rope(z, cos, sin, nf):
    first = (_lane(z.shape) % (2 * nf)) < nf
    partner = jnp.where(first, pltpu.roll(z, LANES - nf, 1), pltpu.roll(z, nf, 1))
    return z * cos + partner * sin


def _mod_kernel(a_ref, w_ref, b_ref, o_ref):
    a = a_ref[...]
    act = a * jax.nn.sigmoid(a)
    o_ref[...] = jnp.dot(act, w_ref[...], preferred_element_type=F32,
                         precision=lax.Precision.HIGHEST) + b_ref[...]


def _modulation(cvec, w_mod, b_mod):
    n = w_mod.shape[1]
    bn = 512
    return pl.pallas_call(
        _mod_kernel,
        grid=(n // bn,),
        in_specs=[pl.BlockSpec((8, D_MODEL), lambda j: (0, 0)),
                  pl.BlockSpec((D_MODEL, bn), lambda j: (0, j)),
                  pl.BlockSpec((1, bn), lambda j: (0, j))],
        out_specs=pl.BlockSpec((8, bn), lambda j: (0, j)),
        out_shape=jax.ShapeDtypeStruct((8, n), F32),
        compiler_params=_cparams(("parallel",)),
        name="modulation",
    )(cvec, w_mod, b_mod.reshape(1, n))


def _mod_spec(chunk, nbatch):
    return pl.BlockSpec((None, 1, D_MODEL),
                        lambda b, i: (jnp.where(i == 0, nbatch, b), 0, chunk))


def _inproj_kernel(x_ref, g_ref, sh_ref, sc_ref, w_ref, c64_ref, s64_ref, c32_ref, s32_ref,
                   qng_ref, wuq_ref, kvng_ref, wukk_ref, wukv_ref,
                   swaq, swak, swav, dq, dk, dv, rq, rk, rv, rg, mq, mk, mv):
    x = x_ref[0]
    h = _rms(x, g_ref[...]) * (1.0 + sc_ref[...]) + sh_ref[...]
    hb = h.astype(BF16)
    c64, s64, c32, s32 = c64_ref[...], s64_ref[...], c32_ref[...], s32_ref[...]

    def proj(c0, width):
        return _dot(hb, w_ref[:, c0:c0 + width])

    def rope_blocks(z, out, nf, scale):
        cos, sin = (c64, s64) if nf == 16 else (c32, s32)
        for j in range(z.shape[1] // LANES):
            zz = _rope(z[:, j * LANES:(j + 1) * LANES], cos, sin, nf)
            if scale != 1.0:
                zz = zz * scale
            out[0, :, j * LANES:(j + 1) * LANES] = zz.astype(out.dtype)

    rope_blocks(proj(C_SWA_Q, 256), swaq, 16, 0.125)
    rope_blocks(proj(C_SWA_K, 128), swak, 16, 1.0)
    swav[0] = proj(C_SWA_V, 128).astype(BF16)
    rope_blocks(proj(C_DIF_Q, 256), dq, 8, 32 ** -0.5)
    rope_blocks(proj(C_DIF_K, 256), dk, 8, 1.0)
    dv[0] = proj(C_DIF_V, 256).astype(BF16)
    rope_blocks(proj(C_RET_Q, 256), rq, 16, 0.125)
    rope_blocks(proj(C_RET_K, 256), rk, 16, 1.0)
    rv[0] = proj(C_RET_V, 256).astype(BF16)
    rg[0] = proj(C_RET_G, 256).astype(BF16)

    ql = _rms(proj(C_MLA_Q, 256), qng_ref[...]).astype(BF16)
    qa = _dot(ql, wuq_ref[...]) * MLA_SCALE
    for hh in range(4):
        mq[0, :, hh * 256:hh * 256 + LANES] = qa[:, hh * 256:hh * 256 + LANES].astype(BF16)
        zr = _rope(qa[:, hh * 256 + LANES:(hh + 1) * 256], c32, s32, 8)
        mq[0, :, hh * 256 + LANES:(hh + 1) * 256] = zr.astype(BF16)
    kvl = _rms(proj(C_MLA_KV, 128), kvng_ref[...]).astype(BF16)
    kn = _dot(kvl, wukk_ref[...])
    kr = _rope(proj(C_MLA_KR, 128), c32, s32, 8).astype(BF16)
    for p in range(2):
        mk[0, :, p * 256:p * 256 + LANES] = kn[:, p * LANES:(p + 1) * LANES].astype(BF16)
        mk[0, :, p * 256 + LANES:(p + 1) * 256] = kr
    mv[0] = _dot(kvl, wukv_ref[...]).astype(BF16)


def _inproj(x, mod, nbatch, norm_g, w_in_p, tabs, qng, wuq, kvng, wukk, wukv):
    B, S, D = x.shape
    nt = S // TM
    tok = lambda w: pl.BlockSpec((1, TM, w), lambda b, i: (b, i, 0))
    full = lambda a: pl.BlockSpec(a.shape, lambda b, i: (0,) * a.ndim)
    tab = pl.BlockSpec((TM, LANES), lambda b, i: (i, 0))
    widths = [256, 128, 128, 256, 256, 256, 256, 256, 256, 256, 1024, 512, 256]
    return pl.pallas_call(
        _inproj_kernel,
        grid=(B, nt),
        in_specs=[tok(D), full(norm_g), _mod_spec(0, nbatch), _mod_spec(1, nbatch), full(w_in_p),
                  tab, tab, tab, tab, full(qng), full(wuq), full(kvng), full(wukk), full(wukv)],
        out_specs=[tok(w) for w in widths],
        out_shape=[jax.ShapeDtypeStruct((B, S, w), BF16) for w in widths],
        compiler_params=_cparams(("parallel", "parallel")),
        name="inproj",
    )(x, norm_g, mod, mod, w_in_p, *tabs, qng, wuq, kvng, wukk, wukv)


def _swa_kernel(sink_ref, q_ref, kp_ref, ko_ref, kn_ref, vp_ref, vo_ref, vn_ref, kc_ref, vc_ref,
                o_ref, *, n_ctx_tiles, seq):
    i = pl.program_id(1)
    blk = RET_CHUNK
    n = i - n_ctx_tiles
    kband = jnp.concatenate([kp_ref[0], ko_ref[0], kn_ref[0]], axis=0)
    vband = jnp.concatenate([vp_ref[0], vo_ref[0], vn_ref[0]], axis=0)
    kc, vc = kc_ref[0], vc_ref[0]
    r_io = lax.broadcasted_iota(jnp.int32, (blk, 3 * blk), 0)
    c_io = lax.broadcasted_iota(jnp.int32, (blk, 3 * blk), 1)
    kpos = (n - 1) * blk + c_io
    valid = (c_io >= r_io) & (c_io <= r_io + 2 * SWA_WINDOW) & (kpos >= 0) & (kpos < seq) & (n >= 0)
    lane = _lane((blk, LANES))
    for r in range(2):
        qb = q_ref[0, :, r * LANES:(r + 1) * LANES]
        outs = []
        for g in range(2):
            gmask = (lane >= 64 * g) & (lane < 64 * (g + 1))
            qm = jnp.where(gmask, qb, jnp.zeros_like(qb))
            sink = sink_ref[2 * g + r]
            s_band = jnp.where(valid, _dot_nt(qm, kband), NEG_INF)
            s_ctx = _dot_nt(qm, kc)
            m = jnp.maximum(jnp.maximum(jnp.max(s_band, axis=-1, keepdims=True),
                                        jnp.max(s_ctx, axis=-1, keepdims=True)), sink)
            p_band = jnp.exp(s_band - m)
            p_ctx = jnp.exp(s_ctx - m)
            l = (jnp.sum(p_band, axis=-1, keepdims=True) + jnp.sum(p_ctx, axis=-1, keepdims=True)
                 + jnp.exp(sink - m))
            o = _dot(p_band.astype(BF16), vband) + _dot(p_ctx.astype(BF16), vc)
            outs.append(o / l)
        o_ref[0, :, r * LANES:(r + 1) * LANES] = jnp.where(lane < 64, outs[0], outs[1]).astype(BF16)


def _swa(sink, q, k, v):
    B, S, _ = q.shape
    blk = RET_CHUNK
    nt = S // blk
    nct = CTX_LEN // blk
    kvs = lambda f: pl.BlockSpec((1, blk, LANES), f)
    prev = lambda b, i: (b, jnp.maximum(i - 1, 0), 0)
    own = lambda b, i: (b, i, 0)
    nxt = lambda b, i: (b, jnp.minimum(i + 1, nt - 1), 0)
    ctx = pl.BlockSpec((1, CTX_LEN, LANES), lambda b, i: (b, 0, 0))
    return pl.pallas_call(
        functools.partial(_swa_kernel, n_ctx_tiles=nct, seq=S - CTX_LEN),
        grid=(B, nt),
        in_specs=[pl.BlockSpec(memory_space=pltpu.SMEM),
                  pl.BlockSpec((1, blk, 256), own),
                  kvs(prev), kvs(own), kvs(nxt), kvs(prev), kvs(own), kvs(nxt), ctx, ctx],
        out_specs=pl.BlockSpec((1, blk, 256), own),
        out_shape=jax.ShapeDtypeStruct((B, S, 256), BF16),
        compiler_params=_cparams(("parallel", "parallel")),
        name="swa",
    )(sink, q, k, k, k, v, v, v, k, v)


def _flash(q_maps, k_ref, v_ref, m_scr, l_scr, acc_scr, *, is_ctx, tk, seq):
    nm = len(q_maps)
    for a in range(nm):
        m_scr[a] = jnp.full(m_scr.shape[1:], NEG_INF, F32)
        l_scr[a] = jnp.zeros(l_scr.shape[1:], F32)
        acc_scr[a] = jnp.zeros(acc_scr.shape[1:], F32)

    def chunk(start, size):
        kc = k_ref[0, pl.ds(start, size), :]
        vc = v_ref[0, pl.ds(start, size), :]
        for a in range(nm):
            s = _dot_nt(q_maps[a], kc)
            m_prev = m_scr[a]
            m_new = jnp.maximum(m_prev, jnp.max(s, axis=-1, keepdims=True))
            p = jnp.exp(s - m_new)
            alpha = jnp.exp(m_prev - m_new)
            l_scr[a] = alpha * l_scr[a] + jnp.sum(p, axis=-1, keepdims=True)
            acc_scr[a] = alpha * acc_scr[a] + _dot(p.astype(BF16), vc)
            m_scr[a] = m_new

    chunk(0, CTX_LEN)
    n_lat = jnp.where(is_ctx, 0, seq // tk)

    def body(c, carry):
        chunk(pl.multiple_of(CTX_LEN + c * tk, LANES), tk)
        return carry

    lax.fori_loop(0, n_lat, body, 0)
    for a in range(nm):
        acc_scr[a] = acc_scr[a] / l_scr[a]


def _flash_scratch(nm):
    return [pltpu.VMEM((nm, TQ, 1), F32), pltpu.VMEM((nm, TQ, 1), F32),
            pltpu.VMEM((nm, TQ, LANES), F32)]


def _diff_kernel(lam_ref, g_ref, q_ref, k_ref, v_ref, o_ref, m_scr, l_scr, acc_scr, *,
                 tk, seq, lambda_init):
    i = pl.program_id(2)
    qb = q_ref[0]
    lane = _lane(qb.shape)
    q_maps = []
    for a in range(4):
        msk = (lane >= 32 * a) & (lane < 32 * (a + 1))
        q_maps.append(jnp.where(msk, qb, jnp.zeros_like(qb)))
    _flash(q_maps, k_ref, v_ref, m_scr, l_scr, acc_scr, is_ctx=(i == 0), tk=tk, seq=seq)
    lp = lam_ref[...]
    lam = (jnp.exp(jnp.sum(lp[0:1] * lp[1:2], axis=-1, keepdims=True))
           - jnp.exp(jnp.sum(lp[2:3] * lp[3:4], axis=-1, keepdims=True)) + lambda_init)
    lane_o = _lane((TQ, LANES))
    o = jnp.where(lane_o < 64, acc_scr[0] - lam * acc_scr[1], acc_scr[2] - lam * acc_scr[3])
    sq = o * o
    ms0 = jnp.sum(jnp.where(lane_o < 64, sq, 0.0), axis=-1, keepdims=True) * (1.0 / 64)
    ms1 = jnp.sum(jnp.where(lane_o >= 64, sq, 0.0), axis=-1, keepdims=True) * (1.0 / 64)
    ms = jnp.where(lane_o < 64, ms0, ms1)
    y = o * lax.rsqrt(ms + NORM_EPS) * g_ref[...] * (1.0 - lambda_init)
    o_ref[0] = y.astype(BF16)


def _diff(lam_p, subln_g2, q, k, v, lambda_init, tk):
    B, S, _ = q.shape
    nq = S // TQ
    return pl.pallas_call(
        functools.partial(_diff_kernel, tk=tk, seq=S - CTX_LEN, lambda_init=lambda_init),
        grid=(B, 2, nq),
        in_specs=[pl.BlockSpec(lam_p.shape, lambda b, p, i: (0, 0)),
                  pl.BlockSpec((1, LANES), lambda b, p, i: (0, 0)),
                  pl.BlockSpec((1, TQ, LANES), lambda b, p, i: (b, i, p)),
                  pl.BlockSpec((1, S, LANES), lambda b, p, i: (b, 0, p)),
                  pl.BlockSpec((1, S, LANES), lambda b, p, i: (b, 0, p))],
        out_specs=pl.BlockSpec((1, TQ, LANES), lambda b, p, i: (b, i, p)),
        out_shape=jax.ShapeDtypeStruct((B, S, 256), BF16),
        scratch_shapes=_flash_scratch(4),
        compiler_params=_cparams(("parallel", "parallel", "parallel")),
        name="diff_attn",
    )(lam_p, subln_g2, q, k, v)


def _mla_kernel(q_ref, k_ref, v_ref, o_ref, m_scr, l_scr, acc_scr, *, tk, seq):
    i = pl.program_id(2)
    q_maps = [q_ref[0, :, 0:256], q_ref[0, :, 256:512]]
    _flash(q_maps, k_ref, v_ref, m_scr, l_scr, acc_scr, is_ctx=(i == 0), tk=tk, seq=seq)
    lane_o = _lane((TQ, LANES))
    o_ref[0] = jnp.where(lane_o < 64, acc_scr[0], acc_scr[1]).astype(BF16)


def _mla(q, k, v, tk):
    B, S, _ = q.shape
    nq = S // TQ
    return pl.pallas_call(
        functools.partial(_mla_kernel, tk=tk, seq=S - CTX_LEN),
        grid=(B, 2, nq),
        in_specs=[pl.BlockSpec((1, TQ, 512), lambda b, p, i: (b, i, p)),
                  pl.BlockSpec((1, S, 256), lambda b, p, i: (b, 0, p)),
                  pl.BlockSpec((1, S, LANES), lambda b, p, i: (b, 0, p))],
        out_specs=pl.BlockSpec((1, TQ, LANES), lambda b, p, i: (b, i, p)),
        out_shape=jax.ShapeDtypeStruct((B, S, 256), BF16),
        scratch_shapes=_flash_scratch(2),
        compiler_params=_cparams(("parallel", "parallel", "parallel")),
        name="mla_attn",
    )(q, k, v)


def _ret_chunk(dl_ref, q_ref, k_ref, v_ref, s_scr, backward):
    L = RET_CHUNK
    step = pl.program_id(2)

    @pl.when(step == 0)
    def _():
        s_scr[...] = jnp.zeros(s_scr.shape, F32)

    dl = dl_ref[0, 0]
    lg = -(jnp.maximum(-dl, 0.0) + jnp.log(1.0 + jnp.exp(-jnp.abs(dl))))
    row = lax.broadcasted_iota(jnp.int32, (L, LANES), 0).astype(F32)
    col = lax.broadcasted_iota(jnp.int32, (L, LANES), 1)
    colf = col.astype(F32)
    if backward:
        qd, kd, rel = jnp.exp((L - row) * lg), jnp.exp(row * lg), colf - row
    else:
        qd, kd, rel = jnp.exp((row + 1.0) * lg), jnp.exp((L - 1.0 - row) * lg), row - colf
    cd = jnp.exp(float(L) * lg)
    q, k, v = q_ref[0], k_ref[0], v_ref[0]
    qf, kf = q.astype(F32), k.astype(F32)
    o = _dot((qf * qd).astype(BF16), s_scr[...].astype(BF16))
    outs = []
    for hh in range(2):
        hmask = (col >= 64 * hh) & (col < 64 * (hh + 1))
        lg_h = jnp.sum(jnp.where(col[0:1] == 64 * hh, lg, 0.0), axis=-1, keepdims=True)
        dec = jnp.where(rel >= 0, jnp.exp(jnp.maximum(rel, 0.0) * lg_h), 0.0)
        a = _dot_nt(jnp.where(hmask, q, jnp.zeros_like(q)), k) * dec
        outs.append(_dot(a.astype(BF16), v))
    o = o + jnp.where(col < 64, outs[0], outs[1])
    rowi = lax.broadcasted_iota(jnp.int32, (LANES, LANES), 0)
    coli = lax.broadcasted_iota(jnp.int32, (LANES, LANES), 1)
    diag = (rowi < 64) == (coli < 64)
    s_new = s_scr[...] * cd + _dot_tn((kf * kd).astype(BF16), v)
    s_scr[...] = jnp.where(diag, s_new, 0.0)
    return o


def _ret_fwd_kernel(dl_ref, q_ref, k_ref, v_ref, o_ref, s_scr):
    o_ref[0] = _ret_chunk(dl_ref, q_ref, k_ref, v_ref, s_scr, backward=False)


def _ret_bwd_kernel(dl_ref, q_ref, k_ref, v_ref, of_ref, g_ref, gn_ref, y_ref, s_scr):
    o = of_ref[0] + _ret_chunk(dl_ref, q_ref, k_ref, v_ref, s_scr, backward=True)
    lane = _lane(o.shape)
    lo = lane < 64

    def halves(t):
        a = jnp.sum(jnp.where(lo, t, 0.0), axis=-1, keepdims=True) * (1.0 / 64)
        b = jnp.sum(jnp.where(lo, 0.0, t), axis=-1, keepdims=True) * (1.0 / 64)
        return jnp.where(lo, a, b)

    mu = halves(o)
    d = o - mu
    var = halves(d * d)
    g = g_ref[0].astype(F32)
    y = d * lax.rsqrt(var + NORM_EPS) * gn_ref[...] * (g * jax.nn.sigmoid(g))
    y_ref[0] = y.astype(BF16)


def _retention(dl_lane, gn_g2, q, k, v, g):
    B, S, _ = q.shape
    L = RET_CHUNK
    nc = S // L
    ncc = CTX_LEN // L
    fwd_idx = lambda b, p, s: (b, s, p)
    bwd_idx = lambda b, p, s: (b, jnp.where(s < ncc, ncc - 1 - s, nc - 1 + ncc - s), p)
    dl_f = pl.BlockSpec((1, 1, 1, LANES), lambda b, p, s: (0, p, 0, 0))
    dl_b = pl.BlockSpec((1, 1, 1, LANES), lambda b, p, s: (1, p, 0, 0))
    o_f = pl.pallas_call(
        _ret_fwd_kernel,
        grid=(B, 2, nc),
        in_specs=[dl_f] + [pl.BlockSpec((1, L, LANES), fwd_idx)] * 3,
        out_specs=pl.BlockSpec((1, L, LANES), fwd_idx),
        out_shape=jax.ShapeDtypeStruct((B, S, 256), F32),
        scratch_shapes=[pltpu.VMEM((LANES, LANES), F32)],
        compiler_params=_cparams(("parallel", "parallel", "arbitrary")),
        name="retention_fwd",
    )(dl_lane, q, k, v)
    return pl.pallas_call(
        _ret_bwd_kernel,
        grid=(B, 2, nc),
        in_specs=[dl_b] + [pl.BlockSpec((1, L, LANES), bwd_idx)] * 5
                 + [pl.BlockSpec((1, LANES), lambda b, p, s: (0, p))],
        out_specs=pl.BlockSpec((1, L, LANES), bwd_idx),
        out_shape=jax.ShapeDtypeStruct((B, S, 256), BF16),
        scratch_shapes=[pltpu.VMEM((LANES, LANES), F32)],
        compiler_params=_cparams(("parallel", "parallel", "arbitrary")),
        name="retention_bwd",
    )(dl_lane, q, k, v, o_f, g, gn_g2)


def _outproj_kernel(x_ref, ya_ref, yb_ref, yr_ref, ym_ref, w_ref, g1_ref, sh_ref, sc_ref, ng_ref,
                    wr_ref, br_ref, xo_ref, h_ref, rt_ref):
    mix = (_dot(ya_ref[0], w_ref[0:256]) + _dot(yb_ref[0], w_ref[256:512])
           + _dot(yr_ref[0], w_ref[512:768]) + _dot(ym_ref[0], w_ref[768:1024]))
    x = x_ref[0] + g1_ref[...] * mix
    xo_ref[0] = x
    h = _rms(x, ng_ref[...]) * (1.0 + sc_ref[...]) + sh_ref[...]
    h_ref[0] = h
    logits = jnp.dot(h, wr_ref[...], preferred_element_type=F32,
                     precision=lax.Precision.HIGHEST) + br_ref[...]
    lane = _lane(logits.shape).astype(F32)

    def first_argmax(vals, mask):
        mx = jnp.max(jnp.where(mask, vals, NEG_INF), axis=-1, keepdims=True)
        idx = jnp.min(jnp.where(mask & (vals == mx), lane, float(LANES)), axis=-1, keepdims=True)
        return mx, idx

    gmask = lane < MOE_GROUPS
    g_max, g_idx = first_argmax(logits, gmask)
    g_w = 1.0 / jnp.sum(jnp.exp(jnp.where(gmask, logits - g_max, NEG_INF)), axis=-1, keepdims=True)
    e_lo = MOE_GROUPS + g_idx * MOE_PER_GROUP
    emask = (lane >= e_lo) & (lane < e_lo + MOE_PER_GROUP)
    v1, i1 = first_argmax(logits, emask)
    v2, i2 = first_argmax(logits, emask & (lane != i1))
    t = jnp.exp(v2 - v1)
    w1 = g_w / (1.0 + t)
    w2 = w1 * t
    rt = jnp.where(lane == 0, i1 - MOE_GROUPS,
                   jnp.where(lane == 1, i2 - MOE_GROUPS,
                             jnp.where(lane == 2, w1, jnp.where(lane == 3, w2, 0.0))))
    rt_ref[0] = rt


def _outproj(x, ya, yb, yr, ym, w_out_p, mod, nbatch, norm_g, w_route, b_route):
    B, S, D = x.shape
    nt = S // TM
    tok = lambda w: pl.BlockSpec((1, TM, w), lambda b, i: (b, i, 0))
    full = lambda a: pl.BlockSpec(a.shape, lambda b, i: (0,) * a.ndim)
    return pl.pallas_call(
        _outproj_kernel,
        grid=(B, nt),
        in_specs=[tok(D), tok(256), tok(256), tok(256), tok(256), full(w_out_p),
                  _mod_spec(2, nbatch), _mod_spec(3, nbatch), _mod_spec(4, nbatch), full(norm_g),
                  full(w_route), full(b_route)],
        out_specs=[tok(D), tok(D), tok(LANES)],
        out_shape=[jax.ShapeDtypeStruct((B, S, D), F32), jax.ShapeDtypeStruct((B, S, D), F32),
                   jax.ShapeDtypeStruct((B, S, LANES), F32)],
        compiler_params=_cparams(("parallel", "parallel")),
        name="outproj_router",
    )(x, ya, yb, yr, ym, w_out_p, mod, mod, mod, norm_g, w_route, b_route)


def _rank_kernel(rt_ref, rank_ref, cnt_ref, carry):
    @pl.when(pl.program_id(0) == 0)
    def _():
        carry[...] = jnp.zeros(carry.shape, F32)

    rt = rt_ref[...]
    lane = _lane(rt.shape)
    lanef = lane.astype(F32)
    oh0 = jnp.where(lanef == rt[:, 0:1], 1.0, 0.0)
    oh1 = jnp.where(lanef == rt[:, 1:2], 1.0, 0.0)
    cnt = oh0 + oh1
    r_io = lax.broadcasted_iota(jnp.int32, (TM, TM), 0)
    c_io = lax.broadcasted_iota(jnp.int32, (TM, TM), 1)
    tri = jnp.where(r_io > c_io, 1.0, 0.0).astype(BF16)
    before = _dot(tri, cnt.astype(BF16)) + carry[...]
    r0 = jnp.sum(oh0 * before, axis=-1, keepdims=True)
    r1 = jnp.sum(oh1 * before, axis=-1, keepdims=True)
    rank_ref[...] = jnp.where(lane == 0, r0, jnp.where(lane == 1, r1, 0.0))
    carry[...] = carry[...] + jnp.sum(cnt, axis=0, keepdims=True)
    cnt_ref[...] = carry[...]


def _ranks(route):
    n = route.shape[0]
    return pl.pallas_call(
        _rank_kernel,
        grid=(n // TM,),
        in_specs=[pl.BlockSpec((TM, LANES), lambda i: (i, 0))],
        out_specs=[pl.BlockSpec((TM, LANES), lambda i: (i, 0)),
                   pl.BlockSpec((1, LANES), lambda i: (0, 0))],
        out_shape=[jax.ShapeDtypeStruct((n, LANES), F32), jax.ShapeDtypeStruct((1, LANES), F32)],
        scratch_shapes=[pltpu.VMEM((1, LANES), F32)],
        compiler_params=_cparams(("arbitrary",)),
        name="moe_rank",
    )(route)


def _dispatch_kernel(dest_ref, h_ref, xb_in_ref, xb_ref, sem):
    del xb_in_ref
    base = pl.program_id(0) * TM

    def copy(r, k):
        d = dest_ref[(base + r) * 2 + k]
        return pltpu.make_async_copy(h_ref.at[pl.ds(r, 1)], xb_ref.at[pl.ds(d, 1)], sem)

    def issue(r, c):
        copy(r, 0).start()
        copy(r, 1).start()
        return c

    def drain(r, c):
        copy(r, 0).wait()
        copy(r, 1).wait()
        return c

    lax.fori_loop(0, TM, issue, 0)
    lax.fori_loop(0, TM, drain, 0)


def _dispatch(dest, h, cap):
    n, d = h.shape
    return pl.pallas_call(
        _dispatch_kernel,
        grid_spec=pltpu.PrefetchScalarGridSpec(
            num_scalar_prefetch=1,
            grid=(n // TM,),
            in_specs=[pl.BlockSpec((TM, d), lambda i, dest: (i, 0)),
                      pl.BlockSpec(memory_space=pl.ANY)],
            out_specs=pl.BlockSpec(memory_space=pl.ANY),
            scratch_shapes=[pltpu.SemaphoreType.DMA(())]),
        out_shape=jax.ShapeDtypeStruct((cap, d), h.dtype),
        input_output_aliases={2: 0},
        compiler_params=_cparams(("arbitrary",)),
        name="moe_dispatch",
    )(dest, h, jnp.zeros((cap, d), h.dtype))


def _expert_kernel(be_ref, nu_ref, x_ref, wg_ref, wu_ref, wd_ref, y_ref):
    i = pl.program_id(0)

    @pl.when(i < nu_ref[0])
    def _():
        xb = x_ref[...].astype(BF16)
        gate = _dot(xb, wg_ref[0])
        up = _dot(xb, wu_ref[0])
        hid = (gate * jax.nn.sigmoid(gate) * up).astype(BF16)
        y_ref[...] = _dot(hid, wd_ref[0])

    @pl.when(i >= nu_ref[0])
    def _():
        y_ref[...] = jnp.zeros(y_ref.shape, y_ref.dtype)


def _experts(block_e, n_used, xb, w_gate, w_up, w_down):
    cap, d = xb.shape
    hdim = w_gate.shape[-1]
    return pl.pallas_call(
        _expert_kernel,
        grid_spec=pltpu.PrefetchScalarGridSpec(
            num_scalar_prefetch=2,
            grid=(cap // MOE_ROWS,),
            in_specs=[pl.BlockSpec((MOE_ROWS, d), lambda i, be, nu: (i, 0)),
                      pl.BlockSpec((1, d, hdim), lambda i, be, nu: (be[i], 0, 0)),
                      pl.BlockSpec((1, d, hdim), lambda i, be, nu: (be[i], 0, 0)),
                      pl.BlockSpec((1, hdim, d), lambda i, be, nu: (be[i], 0, 0))],
            out_specs=pl.BlockSpec((MOE_ROWS, d), lambda i, be, nu: (i, 0))),
        out_shape=jax.ShapeDtypeStruct((cap, d), F32),
        compiler_params=_cparams(("arbitrary",)),
        name="moe_experts",
    )(block_e, n_used, xb, w_gate, w_up, w_down)


def _combine_kernel(dest_ref, x_ref, rt_ref, g2_ref, fg_ref, yb_ref, o_ref, gath, sem, *,
                    tile_off, tiles_per_batch, final):
    b, i = pl.program_id(0), pl.program_id(1)
    base = (b * tiles_per_batch + i + tile_off) * TM

    def copy(r, k):
        d = dest_ref[(base + r) * 2 + k]
        return pltpu.make_async_copy(yb_ref.at[pl.ds(d, 1)], gath.at[k, pl.ds(r, 1)], sem)

    def issue(r, c):
        copy(r, 0).start()
        copy(r, 1).start()
        return c

    def drain(r, c):
        copy(r, 0).wait()
        copy(r, 1).wait()
        return c

    lax.fori_loop(0, TM, issue, 0)
    lax.fori_loop(0, TM, drain, 0)
    rt = rt_ref[0]
    m = rt[:, 2:3] * gath[0] + rt[:, 3:4] * gath[1]
    x = x_ref[0] + g2_ref[...] * m
    if final:
        x = _rms(x, fg_ref[...])
    o_ref[0] = x


def _combine(dest, x, route, mod, nbatch, final_g, yb, final):
    B, S, D = x.shape
    nt = S // TM
    off = 1 if final else 0
    g2_spec = pl.BlockSpec((None, 1, D_MODEL),
                           lambda b, i, dest: (jnp.where(i + off == 0, nbatch, b), 0, 5))
    return pl.pallas_call(
        functools.partial(_combine_kernel, tile_off=off, tiles_per_batch=nt, final=final),
        grid_spec=pltpu.PrefetchScalarGridSpec(
            num_scalar_prefetch=1,
            grid=(B, nt - off),
            in_specs=[pl.BlockSpec((1, TM, D), lambda b, i, dest: (b, i + off, 0)),
                      pl.BlockSpec((1, TM, LANES), lambda b, i, dest: (b, i + off, 0)),
                      g2_spec,
                      pl.BlockSpec((1, D), lambda b, i, dest: (0, 0)),
                      pl.BlockSpec(memory_space=pl.ANY)],
            out_specs=pl.BlockSpec((1, TM, D), lambda b, i, dest: (b, i, 0)),
            scratch_shapes=[pltpu.VMEM((2, TM, D), F32), pltpu.SemaphoreType.DMA(())]),
        out_shape=jax.ShapeDtypeStruct((B, S - off * TM, D), F32),
        compiler_params=_cparams(("arbitrary", "arbitrary")),
        name="moe_combine",
    )(dest, x, route, mod, final_g, yb)


def _rope_tables(seq):
    t = jnp.arange(seq, dtype=jnp.int32)
    row, col = t // GRID_W, t % GRID_W
    lane = jnp.arange(LANES, dtype=jnp.int32)
    tabs = []
    for r in (64, 32):
        nf = r // 4
        inv = 1.0 / (ROPE_BASE ** (jnp.arange(nf, dtype=F32) / nf))
        pos = jnp.where(((lane % r) < 2 * nf)[None, :], row[:, None], col[:, None]).astype(F32)
        ang = pos * inv[lane % nf][None, :]
        first = ((lane % (2 * nf)) < nf)[None, :]
        cos = jnp.cos(ang)
        sin = jnp.where(first, -jnp.sin(ang), jnp.sin(ang))
        tabs.append(jnp.concatenate([jnp.ones((CTX_LEN, LANES), F32), cos], axis=0))
        tabs.append(jnp.concatenate([jnp.zeros((CTX_LEN, LANES), F32), sin], axis=0))
    return tabs


def _layer_weights(w_in, w_out, w_uq, w_ukv, w_group, b_group, w_expert, b_expert):
    perm = jnp.array([0, 2, 1, 3])
    wq = w_in[:, :256].reshape(D_MODEL, 4, 64)[:, perm].reshape(D_MODEL, 256)
    w_in_p = jnp.concatenate([wq, w_in[:, 256:], jnp.zeros((D_MODEL, D_IN_PAD - w_in.shape[1]), F32)],
                             axis=1).astype(BF16)
    wo = jnp.concatenate([w_out[:256].reshape(4, 64, D_MODEL)[perm].reshape(256, D_MODEL), w_out[256:]],
                         axis=0).astype(BF16)
    uq = w_uq.reshape(256, 4, 96)
    blocks = []
    for h in range(4):
        blk = jnp.zeros((256, 256), F32)
        blk = blk.at[:, 64 * (h % 2):64 * (h % 2) + 64].set(uq[:, h, :64])
        blk = blk.at[:, LANES:LANES + 32].set(uq[:, h, 64:])
        blocks.append(blk)
    wuq = jnp.concatenate(blocks, axis=1).astype(BF16)
    ukv = w_ukv.reshape(128, 4, 128)
    wukk = ukv[:, :, :64].reshape(128, 256).astype(BF16)
    wukv = ukv[:, :, 64:].reshape(128, 256).astype(BF16)
    pad = LANES - MOE_GROUPS - MOE_EXPERTS
    w_route = jnp.concatenate([w_group, w_expert, jnp.zeros((D_MODEL, pad), F32)], axis=1)
    b_route = jnp.concatenate([b_group, b_expert, jnp.zeros((pad,), F32)]).reshape(1, LANES)
    return w_in_p, wo, wuq, wukk, wukv, w_route, b_route


def _moe_plan(route, ranks, counts, cap):
    e = route[:, :2].astype(jnp.int32)
    rank = ranks[:, :2].astype(jnp.int32)
    cnt = counts[0, :MOE_EXPERTS].astype(jnp.int32)
    padded = (cnt + MOE_ROWS - 1) // MOE_ROWS * MOE_ROWS
    pend = jnp.cumsum(padded)
    pstart = pend - padded
    dest = (pstart[e] + rank).reshape(-1)
    nblk = cap // MOE_ROWS
    block_e = jnp.minimum(jnp.searchsorted(pend, jnp.arange(nblk, dtype=jnp.int32) * MOE_ROWS, side='right'),
                          MOE_EXPERTS - 1).astype(jnp.int32)
    n_used = (pend[-1:] // MOE_ROWS).astype(jnp.int32)
    return dest, block_e, n_used


def kernel(x, c, ctx, c_ctx, w_mod, b_mod, norm1_g, norm2_g, w_in, w_out, swa_sink, diff_lambda, diff_subln_g, ret_decay_logit, ret_gn_g, mla_q_norm_g, mla_w_uq, mla_kv_norm_g, mla_w_ukv, moe_w_group, moe_b_group, moe_w_expert, moe_b_expert, moe_w_gate, moe_w_up, moe_w_down, final_norm_g):
    B, T, D = x.shape
    S = CTX_LEN + T
    n_tok = B * S
    tk = min(1024, T)
    cap = -(-(2 * n_tok + MOE_EXPERTS * (MOE_ROWS - 1)) // MOE_ROWS) * MOE_ROWS
    xs = jnp.concatenate([ctx, x], axis=1)
    cvec = jnp.zeros((8, D), F32).at[:B].set(c).at[B].set(c_ctx)
    tabs = _rope_tables(T)
    row = lambda v: v.reshape(1, -1)

    for l in range(DEPTH):
        lambda_init = 0.8 - 0.6 * math.exp(-0.3 * l)
        w_in_p, wo, wuq, wukk, wukv, w_route, b_route = _layer_weights(
            w_in[l], w_out[l], mla_w_uq[l], mla_w_ukv[l], moe_w_group[l], moe_b_group[l],
            moe_w_expert[l], moe_b_expert[l])
        mod = _modulation(cvec, w_mod[l], b_mod[l]).reshape(8, 1, 6 * D)
        (swaq, swak, swav, dq, dk, dv, rq, rk, rv, rg, mq, mk, mv) = _inproj(
            xs, mod, B, row(norm1_g[l]), w_in_p, tabs, row(mla_q_norm_g[l]), wuq,
            row(mla_kv_norm_g[l]), wukk, wukv)
        sink_p = swa_sink[l]
        ya = _swa(sink_p, swaq, swak, swav)
        yb = _diff(diff_lambda[l], row(jnp.tile(diff_subln_g[l], 2)), dq, dk, dv, lambda_init, tk)
        dl_lane = jnp.repeat(ret_decay_logit[l], 64, axis=-1).reshape(2, 2, 1, LANES)
        yr = _retention(dl_lane, row(ret_gn_g[l]), rq, rk, rv, rg)
        ym = _mla(mq, mk, mv, tk)
        xs, h2, route = _outproj(xs, ya, yb, yr, ym, wo, mod, B, row(norm2_g[l]), w_route, b_route)
        route2 = route.reshape(n_tok, LANES)
        ranks, counts = _ranks(route2)
        dest, block_e, n_used = _moe_plan(route2, ranks, counts, cap)
        xb = _dispatch(dest, h2.reshape(n_tok, D), cap)
        yb_e = _experts(block_e, n_used, xb, moe_w_gate[l].astype(BF16), moe_w_up[l].astype(BF16),
                        moe_w_down[l].astype(BF16))
        final = l == DEPTH - 1
        xs = _combine(dest, xs, route, mod, B, row(final_norm_g), yb_e, final)
    return xs
```

```python
import functools
import math

import jax
import jax.numpy as jnp
from jax import lax
from jax.experimental import pallas as pl
from jax.experimental.pallas import tpu as pltpu

F32 = jnp.float32
BF16 = jnp.bfloat16

D_MODEL = 1024
CTX_LEN = 256
GRID_W = 64
ROPE_BASE = 10000.0
NORM_EPS = 1e-6
NEG_INF = -1e30
DEPTH = 2

SWA_WINDOW = 128
RET_CHUNK = 128
MLA_SCALE = (64 + 32) ** -0.5
LOG2E = math.log2(math.e)
MOE_GROUPS = 4
MOE_PER_GROUP = 8
MOE_EXPERTS = 32
MOE_HIDDEN = 512

LANES = 128
TM = 256
TQ = 256
FLASH_TK = 2048
MOE_ROWS = 256
VMEM_LIMIT = 56 * 1024 * 1024

C_SWA_Q, C_SWA_K, C_SWA_V = 0, 256, 384
C_DIF_Q, C_DIF_K, C_DIF_V = 512, 768, 1024
C_RET_Q, C_RET_K, C_RET_V, C_RET_G = 1280, 1536, 1792, 2048
C_MLA_Q, C_MLA_KV, C_MLA_KR = 2304, 2560, 2688
D_IN_PAD = 2816


def _cparams(sem):
    return pltpu.CompilerParams(dimension_semantics=sem, vmem_limit_bytes=VMEM_LIMIT)


def _dot(a, b):
    return jnp.dot(a, b, preferred_element_type=F32)


def _dot_nt(a, b):
    return lax.dot_general(a, b, (((1,), (1,)), ((), ())), preferred_element_type=F32)


def _dot_tn(a, b):
    return lax.dot_general(a, b, (((0,), (0,)), ((), ())), preferred_element_type=F32)


def _rms(x, g):
    ms = jnp.mean(x * x, axis=-1, keepdims=True)
    return x * lax.rsqrt(ms + NORM_EPS) * g


def _lane(shape):
    return lax.broadcasted_iota(jnp.int32, shape, len(shape) - 1)


def _rope(z, cos, sin, nf):
    first = (_lane(z.shape) % (2 * nf)) < nf
    partner = jnp.where(first, pltpu.roll(z, LANES - nf, 1), pltpu.roll(z, nf, 1))
    return z * cos + partner * sin


def _mod_kernel(a_ref, w_ref, b_ref, o_ref):
    a = a_ref[...]
    act = a * jax.nn.sigmoid(a)
    o_ref[...] = jnp.dot(act, w_ref[...], preferred_element_type=F32,
                         precision=lax.Precision.HIGHEST) + b_ref[...]


def _modulation(cvec, w_mod, b_mod):
    n = w_mod.shape[1]
    bn = 512
    return pl.pallas_call(
        _mod_kernel,
        grid=(n // bn,),
        in_specs=[pl.BlockSpec((8, D_MODEL), lambda j: (0, 0)),
                  pl.BlockSpec((D_MODEL, bn), lambda j: (0, j)),
                  pl.BlockSpec((1, bn), lambda j: (0, j))],
        out_specs=pl.BlockSpec((8, bn), lambda j: (0, j)),
        out_shape=jax.ShapeDtypeStruct((8, n), F32),
        compiler_params=_cparams(("parallel",)),
        name="modulation",
    )(cvec, w_mod, b_mod.reshape(1, n))


def _mod_spec(chunk, nbatch):
    return pl.BlockSpec((None, 1, D_MODEL),
                        lambda b, i: (jnp.where(i == 0, nbatch, b), 0, chunk))


def _inproj_kernel(x_ref, g_ref, sh_ref, sc_ref, w_ref, c64_ref, s64_ref, c32_ref, s32_ref,
                   qng_ref, wuq_ref, kvng_ref, wukk_ref, wukv_ref,
                   swaq, swak, swav, dq, dk, dv, rq, rk, rv, rg, mq, mk, mv):
    x = x_ref[0]
    h = _rms(x, g_ref[...]) * (1.0 + sc_ref[...]) + sh_ref[...]
    hb = h.astype(BF16)
    c64, s64, c32, s32 = c64_ref[...], s64_ref[...], c32_ref[...], s32_ref[...]

    def proj(c0, width):
        return _dot(hb, w_ref[:, c0:c0 + width])

    def rope_blocks(z, out, nf, scale):
        cos, sin = (c64, s64) if nf == 16 else (c32, s32)
        for j in range(z.shape[1] // LANES):
            zz = _rope(z[:, j * LANES:(j + 1) * LANES], cos, sin, nf)
            if scale != 1.0:
                zz = zz * scale
            out[0, :, j * LANES:(j + 1) * LANES] = zz.astype(out.dtype)

    rope_blocks(proj(C_SWA_Q, 256), swaq, 16, 0.125)
    rope_blocks(proj(C_SWA_K, 128), swak, 16, 1.0)
    swav[0] = proj(C_SWA_V, 128).astype(BF16)
    def values_with_ones(z, out):
        lane = _lane((z.shape[0], LANES))
        for hh in range(4):
            zb = z[:, (hh // 2) * LANES:(hh // 2 + 1) * LANES]
            keep = (lane < 64) if hh % 2 == 0 else (lane >= 64)
            out[0, :, hh * LANES:(hh + 1) * LANES] = jnp.where(keep, zb, 1.0).astype(BF16)

    rope_blocks(proj(C_DIF_Q, 256), dq, 8, 32 ** -0.5 * LOG2E)
    rope_blocks(proj(C_DIF_K, 256), dk, 8, 1.0)
    values_with_ones(proj(C_DIF_V, 256), dv)
    rope_blocks(proj(C_RET_Q, 256), rq, 16, 0.125)
    rope_blocks(proj(C_RET_K, 256), rk, 16, 1.0)
    rv[0] = proj(C_RET_V, 256).astype(BF16)
    rg[0] = proj(C_RET_G, 256).astype(BF16)

    ql = _rms(proj(C_MLA_Q, 256), qng_ref[...]).astype(BF16)
    qa = _dot(ql, wuq_ref[...]) * (MLA_SCALE * LOG2E)
    for hh in range(4):
        mq[0, :, hh * 256:hh * 256 + LANES] = qa[:, hh * 256:hh * 256 + LANES].astype(BF16)
        zr = _rope(qa[:, hh * 256 + LANES:(hh + 1) * 256], c32, s32, 8)
        mq[0, :, hh * 256 + LANES:(hh + 1) * 256] = zr.astype(BF16)
    kvl = _rms(proj(C_MLA_KV, 128), kvng_ref[...]).astype(BF16)
    kn = _dot(kvl, wukk_ref[...])
    kr = _rope(proj(C_MLA_KR, 128), c32, s32, 8).astype(BF16)
    for p in range(2):
        mk[0, :, p * 256:p * 256 + LANES] = kn[:, p * LANES:(p + 1) * LANES].astype(BF16)
        mk[0, :, p * 256 + LANES:(p + 1) * 256] = kr
    values_with_ones(_dot(kvl, wukv_ref[...]), mv)


def _inproj(x, mod, nbatch, norm_g, w_in_p, tabs, qng, wuq, kvng, wukk, wukv):
    B, S, D = x.shape
    nt = S // TM
    tok = lambda w: pl.BlockSpec((1, TM, w), lambda b, i: (b, i, 0))
    full = lambda a: pl.BlockSpec(a.shape, lambda b, i: (0,) * a.ndim)
    tab = pl.BlockSpec((TM, LANES), lambda b, i: (i, 0))
    widths = [256, 128, 128, 256, 256, 512, 256, 256, 256, 256, 1024, 512, 512]
    return pl.pallas_call(
        _inproj_kernel,
        grid=(B, nt),
        in_specs=[tok(D), full(norm_g), _mod_spec(0, nbatch), _mod_spec(1, nbatch), full(w_in_p),
                  tab, tab, tab, tab, full(qng), full(wuq), full(kvng), full(wukk), full(wukv)],
        out_specs=[tok(w) for w in widths],
        out_shape=[jax.ShapeDtypeStruct((B, S, w), BF16) for w in widths],
        compiler_params=_cparams(("parallel", "parallel")),
        name="inproj",
    )(x, norm_g, mod, mod, w_in_p, *tabs, qng, wuq, kvng, wukk, wukv)


def _swa_kernel(sink_ref, q_ref, kp_ref, ko_ref, kn_ref, vp_ref, vo_ref, vn_ref, kc_ref, vc_ref,
                o_ref, *, n_ctx_tiles, seq):
    i = pl.program_id(1)
    blk = RET_CHUNK
    n = i - n_ctx_tiles
    kband = jnp.concatenate([kp_ref[0], ko_ref[0], kn_ref[0]], axis=0)
    vband = jnp.concatenate([vp_ref[0], vo_ref[0], vn_ref[0]], axis=0)
    kc, vc = kc_ref[0], vc_ref[0]
    r_io = lax.broadcasted_iota(jnp.int32, (blk, 3 * blk), 0)
    c_io = lax.broadcasted_iota(jnp.int32, (blk, 3 * blk), 1)
    kpos = (n - 1) * blk + c_io
    valid = (c_io >= r_io) & (c_io <= r_io + 2 * SWA_WINDOW) & (kpos >= 0) & (kpos < seq) & (n >= 0)
    lane = _lane((blk, LANES))
    for r in range(2):
        qb = q_ref[0, :, r * LANES:(r + 1) * LANES]
        outs = []
        for g in range(2):
            gmask = (lane >= 64 * g) & (lane < 64 * (g + 1))
            qm = jnp.where(gmask, qb, jnp.zeros_like(qb))
            sink = sink_ref[2 * g + r]
            s_band = jnp.where(valid, _dot_nt(qm, kband), NEG_INF)
            s_ctx = _dot_nt(qm, kc)
            m = jnp.maximum(jnp.maximum(jnp.max(s_band, axis=-1, keepdims=True),
                                        jnp.max(s_ctx, axis=-1, keepdims=True)), sink)
            p_band = jnp.exp(s_band - m)
            p_ctx = jnp.exp(s_ctx - m)
            l = (jnp.sum(p_band, axis=-1, keepdims=True) + jnp.sum(p_ctx, axis=-1, keepdims=True)
                 + jnp.exp(sink - m))
            o = _dot(p_band.astype(BF16), vband) + _dot(p_ctx.astype(BF16), vc)
            outs.append(o / l)
        o_ref[0, :, r * LANES:(r + 1) * LANES] = jnp.where(lane < 64, outs[0], outs[1]).astype(BF16)


def _swa(sink, q, k, v):
    B, S, _ = q.shape
    blk = RET_CHUNK
    nt = S // blk
    nct = CTX_LEN // blk
    kvs = lambda f: pl.BlockSpec((1, blk, LANES), f)
    prev = lambda b, i: (b, jnp.maximum(i - 1, 0), 0)
    own = lambda b, i: (b, i, 0)
    nxt = lambda b, i: (b, jnp.minimum(i + 1, nt - 1), 0)
    ctx = pl.BlockSpec((1, CTX_LEN, LANES), lambda b, i: (b, 0, 0))
    return pl.pallas_call(
        functools.partial(_swa_kernel, n_ctx_tiles=nct, seq=S - CTX_LEN),
        grid=(B, nt),
        in_specs=[pl.BlockSpec(memory_space=pltpu.SMEM),
                  pl.BlockSpec((1, blk, 256), own),
                  kvs(prev), kvs(own), kvs(nxt), kvs(prev), kvs(own), kvs(nxt), ctx, ctx],
        out_specs=pl.BlockSpec((1, blk, 256), own),
        out_shape=jax.ShapeDtypeStruct((B, S, 256), BF16),
        compiler_params=_cparams(("parallel", "parallel")),
        name="swa",
    )(sink, q, k, k, k, v, v, v, k, v)


def _flash(q_maps, v_blk, k_ref, v_ref, m_scr, acc_scr, *, is_ctx, tk, seq):
    nm = len(q_maps)
    for a in range(nm):
        m_scr[a] = jnp.full(m_scr.shape[1:], NEG_INF, F32)
        acc_scr[a] = jnp.zeros(acc_scr.shape[1:], F32)

    def chunk(start, size):
        kc = k_ref[0, pl.ds(start, size), :]
        nt = size // LANES
        for a in range(nm):
            vc = v_ref[0, pl.ds(start, size), v_blk[a] * LANES:(v_blk[a] + 1) * LANES]
            s = _dot_nt(q_maps[a], kc)
            mt = s[:, 0:LANES]
            for t in range(1, nt):
                mt = jnp.maximum(mt, s[:, t * LANES:(t + 1) * LANES])
            m_prev = m_scr[a]
            m_new = jnp.maximum(m_prev, jnp.max(mt, axis=-1, keepdims=True))
            p = jnp.concatenate(
                [jnp.exp2(s[:, t * LANES:(t + 1) * LANES] - m_new).astype(BF16) for t in range(nt)],
                axis=1)
            acc_scr[a] = jnp.exp2(m_prev - m_new) * acc_scr[a] + _dot(p, vc)
            m_scr[a] = m_new

    chunk(0, CTX_LEN)
    n_lat = jnp.where(is_ctx, 0, seq // tk)

    def body(c, carry):
        chunk(pl.multiple_of(CTX_LEN + c * tk, LANES), tk)
        return carry

    lax.fori_loop(0, n_lat, body, 0)
    outs = []
    for a in range(nm):
        acc = acc_scr[a]
        outs.append(acc / pltpu.roll(acc, 64, 1))
    return outs


def _flash_scratch(nm):
    return [pltpu.VMEM((nm, TQ, LANES), F32), pltpu.VMEM((nm, TQ, LANES), F32)]


def _diff_kernel(lam_ref, g_ref, q_ref, k_ref, v_ref, o_ref, m_scr, acc_scr, *,
                 tk, seq, lambda_init):
    i = pl.program_id(2)
    qb = q_ref[0]
    lane = _lane(qb.shape)
    q_maps = []
    for a in range(4):
        msk = (lane >= 32 * a) & (lane < 32 * (a + 1))
        q_maps.append(jnp.where(msk, qb, jnp.zeros_like(qb)))
    n = _flash(q_maps, (0, 0, 1, 1), k_ref, v_ref, m_scr, acc_scr, is_ctx=(i == 0), tk=tk, seq=seq)
    lp = lam_ref[...]
    lam = (jnp.exp(jnp.sum(lp[0:1] * lp[1:2], axis=-1, keepdims=True))
           - jnp.exp(jnp.sum(lp[2:3] * lp[3:4], axis=-1, keepdims=True)) + lambda_init)
    lane_o = _lane((TQ, LANES))
    o = jnp.where(lane_o < 64, n[0] - lam * n[1], n[2] - lam * n[3])
    sq = o * o
    ms0 = jnp.sum(jnp.where(lane_o < 64, sq, 0.0), axis=-1, keepdims=True) * (1.0 / 64)
    ms1 = jnp.sum(jnp.where(lane_o >= 64, sq, 0.0), axis=-1, keepdims=True) * (1.0 / 64)
    ms = jnp.where(lane_o < 64, ms0, ms1)
    y = o * lax.rsqrt(ms + NORM_EPS) * g_ref[...] * (1.0 - lambda_init)
    o_ref[0] = y.astype(BF16)


def _diff(lam_p, subln_g2, q, k, v, lambda_init, tk):
    B, S, _ = q.shape
    nq = S // TQ
    return pl.pallas_call(
        functools.partial(_diff_kernel, tk=tk, seq=S - CTX_LEN, lambda_init=lambda_init),
        grid=(B, 2, nq),
        in_specs=[pl.BlockSpec(lam_p.shape, lambda b, p, i: (0, 0)),
                  pl.BlockSpec((1, LANES), lambda b, p, i: (0, 0)),
                  pl.BlockSpec((1, TQ, LANES), lambda b, p, i: (b, i, p)),
                  pl.BlockSpec((1, S, LANES), lambda b, p, i: (b, 0, p)),
                  pl.BlockSpec((1, S, 256), lambda b, p, i: (b, 0, p))],
        out_specs=pl.BlockSpec((1, TQ, LANES), lambda b, p, i: (b, i, p)),
        out_shape=jax.ShapeDtypeStruct((B, S, 256), BF16),
        scratch_shapes=_flash_scratch(4),
        compiler_params=_cparams(("parallel", "parallel", "parallel")),
        name="diff_attn",
    )(lam_p, subln_g2, q, k, v)


def _mla_kernel(q_ref, k_ref, v_ref, o_ref, m_scr, acc_scr, *, tk, seq):
    i = pl.program_id(2)
    q_maps = [q_ref[0, :, 0:256], q_ref[0, :, 256:512]]
    n = _flash(q_maps, (0, 1), k_ref, v_ref, m_scr, acc_scr, is_ctx=(i == 0), tk=tk, seq=seq)
    lane_o = _lane((TQ, LANES))
    o_ref[0] = jnp.where(lane_o < 64, n[0], n[1]).astype(BF16)


def _mla(q, k, v, tk):
    B, S, _ = q.shape
    nq = S // TQ
    return pl.pallas_call(
        functools.partial(_mla_kernel, tk=tk, seq=S - CTX_LEN),
        grid=(B, 2, nq),
        in_specs=[pl.BlockSpec((1, TQ, 512), lambda b, p, i: (b, i, p)),
                  pl.BlockSpec((1, S, 256), lambda b, p, i: (b, 0, p)),
                  pl.BlockSpec((1, S, 256), lambda b, p, i: (b, 0, p))],
        out_specs=pl.BlockSpec((1, TQ, LANES), lambda b, p, i: (b, i, p)),
        out_shape=jax.ShapeDtypeStruct((B, S, 256), BF16),
        scratch_shapes=_flash_scratch(2),
        compiler_params=_cparams(("parallel", "parallel", "parallel")),
        name="mla_attn",
    )(q, k, v)


def _ret_chunk(dl_ref, q_ref, k_ref, v_ref, s_scr, backward):
    L = RET_CHUNK
    step = pl.program_id(2)

    @pl.when(step == 0)
    def _():
        s_scr[...] = jnp.zeros(s_scr.shape, F32)

    dl = dl_ref[0, 0]
    lg = -(jnp.maximum(-dl, 0.0) + jnp.log(1.0 + jnp.exp(-jnp.abs(dl))))
    row = lax.broadcasted_iota(jnp.int32, (L, LANES), 0).astype(F32)
    col = lax.broadcasted_iota(jnp.int32, (L, LANES), 1)
    colf = col.astype(F32)
    if backward:
        qd, kd, rel = jnp.exp((L - row) * lg), jnp.exp(row * lg), colf - row
    else:
        qd, kd, rel = jnp.exp((row + 1.0) * lg), jnp.exp((L - 1.0 - row) * lg), row - colf
    cd = jnp.exp(float(L) * lg)
    q, k, v = q_ref[0], k_ref[0], v_ref[0]
    qf, kf = q.astype(F32), k.astype(F32)
    o = _dot((qf * qd).astype(BF16), s_scr[...].astype(BF16))
    outs = []
    for hh in range(2):
        hmask = (col >= 64 * hh) & (col < 64 * (hh + 1))
        lg_h = jnp.sum(jnp.where(col[0:1] == 64 * hh, lg, 0.0), axis=-1, keepdims=True)
        dec = jnp.where(rel >= 0, jnp.exp(jnp.maximum(rel, 0.0) * lg_h), 0.0)
        a = _dot_nt(jnp.where(hmask, q, jnp.zeros_like(q)), k) * dec
        outs.append(_dot(a.astype(BF16), v))
    o = o + jnp.where(col < 64, outs[0], outs[1])
    rowi = lax.broadcasted_iota(jnp.int32, (LANES, LANES), 0)
    coli = lax.broadcasted_iota(jnp.int32, (LANES, LANES), 1)
    diag = (rowi < 64) == (coli < 64)
    s_new = s_scr[...] * cd + _dot_tn((kf * kd).astype(BF16), v)
    s_scr[...] = jnp.where(diag, s_new, 0.0)
    return o


def _ret_fwd_kernel(dl_ref, q_ref, k_ref, v_ref, o_ref, s_scr):
    o_ref[0] = _ret_chunk(dl_ref, q_ref, k_ref, v_ref, s_scr, backward=False)


def _ret_bwd_kernel(dl_ref, q_ref, k_ref, v_ref, of_ref, g_ref, gn_ref, y_ref, s_scr):
    o = of_ref[0] + _ret_chunk(dl_ref, q_ref, k_ref, v_ref, s_scr, backward=True)
    lane = _lane(o.shape)
    lo = lane < 64

    def halves(t):
        a = jnp.sum(jnp.where(lo, t, 0.0), axis=-1, keepdims=True) * (1.0 / 64)
        b = jnp.sum(jnp.where(lo, 0.0, t), axis=-1, keepdims=True) * (1.0 / 64)
        return jnp.where(lo, a, b)

    mu = halves(o)
    d = o - mu
    var = halves(d * d)
    g = g_ref[0].astype(F32)
    y = d * lax.rsqrt(var + NORM_EPS) * gn_ref[...] * (g * jax.nn.sigmoid(g))
    y_ref[0] = y.astype(BF16)


def _retention(dl_lane, gn_g2, q, k, v, g):
    B, S, _ = q.shape
    L = RET_CHUNK
    nc = S // L
    ncc = CTX_LEN // L
    fwd_idx = lambda b, p, s: (b, s, p)
    bwd_idx = lambda b, p, s: (b, jnp.where(s < ncc, ncc - 1 - s, nc - 1 + ncc - s), p)
    dl_f = pl.BlockSpec((1, 1, 1, LANES), lambda b, p, s: (0, p, 0, 0))
    dl_b = pl.BlockSpec((1, 1, 1, LANES), lambda b, p, s: (1, p, 0, 0))
    o_f = pl.pallas_call(
        _ret_fwd_kernel,
        grid=(B, 2, nc),
        in_specs=[dl_f] + [pl.BlockSpec((1, L, LANES), fwd_idx)] * 3,
        out_specs=pl.BlockSpec((1, L, LANES), fwd_idx),
        out_shape=jax.ShapeDtypeStruct((B, S, 256), F32),
        scratch_shapes=[pltpu.VMEM((LANES, LANES), F32)],
        compiler_params=_cparams(("parallel", "parallel", "arbitrary")),
        name="retention_fwd",
    )(dl_lane, q, k, v)
    return pl.pallas_call(
        _ret_bwd_kernel,
        grid=(B, 2, nc),
        in_specs=[dl_b] + [pl.BlockSpec((1, L, LANES), bwd_idx)] * 5
                 + [pl.BlockSpec((1, LANES), lambda b, p, s: (0, p))],
        out_specs=pl.BlockSpec((1, L, LANES), bwd_idx),
        out_shape=jax.ShapeDtypeStruct((B, S, 256), BF16),
        scratch_shapes=[pltpu.VMEM((LANES, LANES), F32)],
        compiler_params=_cparams(("parallel", "parallel", "arbitrary")),
        name="retention_bwd",
    )(dl_lane, q, k, v, o_f, g, gn_g2)


def _outproj_kernel(x_ref, ya_ref, yb_ref, yr_ref, ym_ref, w_ref, g1_ref, sh_ref, sc_ref, ng_ref,
                    wr_ref, br_ref, xo_ref, h_ref, rt_ref):
    mix = (_dot(ya_ref[0], w_ref[0:256]) + _dot(yb_ref[0], w_ref[256:512])
           + _dot(yr_ref[0], w_ref[512:768]) + _dot(ym_ref[0], w_ref[768:1024]))
    x = x_ref[0] + g1_ref[...] * mix
    xo_ref[0] = x
    h = _rms(x, ng_ref[...]) * (1.0 + sc_ref[...]) + sh_ref[...]
    h_ref[0] = h
    logits = jnp.dot(h, wr_ref[...], preferred_element_type=F32,
                     precision=lax.Precision.HIGHEST) + br_ref[...]
    lane = _lane(logits.shape).astype(F32)

    def first_argmax(vals, mask):
        mx = jnp.max(jnp.where(mask, vals, NEG_INF), axis=-1, keepdims=True)
        idx = jnp.min(jnp.where(mask & (vals == mx), lane, float(LANES)), axis=-1, keepdims=True)
        return mx, idx

    gmask = lane < MOE_GROUPS
    g_max, g_idx = first_argmax(logits, gmask)
    g_w = 1.0 / jnp.sum(jnp.exp(jnp.where(gmask, logits - g_max, NEG_INF)), axis=-1, keepdims=True)
    e_lo = MOE_GROUPS + g_idx * MOE_PER_GROUP
    emask = (lane >= e_lo) & (lane < e_lo + MOE_PER_GROUP)
    v1, i1 = first_argmax(logits, emask)
    v2, i2 = first_argmax(logits, emask & (lane != i1))
    t = jnp.exp(v2 - v1)
    w1 = g_w / (1.0 + t)
    w2 = w1 * t
    rt = jnp.where(lane == 0, i1 - MOE_GROUPS,
                   jnp.where(lane == 1, i2 - MOE_GROUPS,
                             jnp.where(lane == 2, w1, jnp.where(lane == 3, w2, 0.0))))
    rt_ref[0] = rt


def _outproj(x, ya, yb, yr, ym, w_out_p, mod, nbatch, norm_g, w_route, b_route):
    B, S, D = x.shape
    nt = S // TM
    tok = lambda w: pl.BlockSpec((1, TM, w), lambda b, i: (b, i, 0))
    full = lambda a: pl.BlockSpec(a.shape, lambda b, i: (0,) * a.ndim)
    return pl.pallas_call(
        _outproj_kernel,
        grid=(B, nt),
        in_specs=[tok(D), tok(256), tok(256), tok(256), tok(256), full(w_out_p),
                  _mod_spec(2, nbatch), _mod_spec(3, nbatch), _mod_spec(4, nbatch), full(norm_g),
                  full(w_route), full(b_route)],
        out_specs=[tok(D), tok(D), tok(LANES)],
        out_shape=[jax.ShapeDtypeStruct((B, S, D), F32), jax.ShapeDtypeStruct((B, S, D), F32),
                   jax.ShapeDtypeStruct((B, S, LANES), F32)],
        compiler_params=_cparams(("parallel", "parallel")),
        name="outproj_router",
    )(x, ya, yb, yr, ym, w_out_p, mod, mod, mod, norm_g, w_route, b_route)


def _rank_kernel(rt_ref, rank_ref, cnt_ref, carry):
    @pl.when(pl.program_id(0) == 0)
    def _():
        carry[...] = jnp.zeros(carry.shape, F32)

    rt = rt_ref[...]
    lane = _lane(rt.shape)
    lanef = lane.astype(F32)
    oh0 = jnp.where(lanef == rt[:, 0:1], 1.0, 0.0)
    oh1 = jnp.where(lanef == rt[:, 1:2], 1.0, 0.0)
    cnt = oh0 + oh1
    r_io = lax.broadcasted_iota(jnp.int32, (TM, TM), 0)
    c_io = lax.broadcasted_iota(jnp.int32, (TM, TM), 1)
    tri = jnp.where(r_io > c_io, 1.0, 0.0).astype(BF16)
    before = _dot(tri, cnt.astype(BF16)) + carry[...]
    r0 = jnp.sum(oh0 * before, axis=-1, keepdims=True)
    r1 = jnp.sum(oh1 * before, axis=-1, keepdims=True)
    rank_ref[...] = jnp.where(lane == 0, r0, jnp.where(lane == 1, r1, 0.0))
    carry[...] = carry[...] + jnp.sum(cnt, axis=0, keepdims=True)
    cnt_ref[...] = carry[...]


def _ranks(route):
    n = route.shape[0]
    return pl.pallas_call(
        _rank_kernel,
        grid=(n // TM,),
        in_specs=[pl.BlockSpec((TM, LANES), lambda i: (i, 0))],
        out_specs=[pl.BlockSpec((TM, LANES), lambda i: (i, 0)),
                   pl.BlockSpec((1, LANES), lambda i: (0, 0))],
        out_shape=[jax.ShapeDtypeStruct((n, LANES), F32), jax.ShapeDtypeStruct((1, LANES), F32)],
        scratch_shapes=[pltpu.VMEM((1, LANES), F32)],
        compiler_params=_cparams(("arbitrary",)),
        name="moe_rank",
    )(route)


def _dispatch_kernel(dest_ref, h_ref, xb_in_ref, xb_ref, sem):
    del xb_in_ref
    base = pl.program_id(0) * TM

    def copy(r, k):
        d = dest_ref[(base + r) * 2 + k]
        return pltpu.make_async_copy(h_ref.at[pl.ds(r, 1)], xb_ref.at[pl.ds(d, 1)], sem)

    def issue(r, c):
        copy(r, 0).start()
        copy(r, 1).start()
        return c

    def drain(r, c):
        copy(r, 0).wait()
        copy(r, 1).wait()
        return c

    lax.fori_loop(0, TM, issue, 0)
    lax.fori_loop(0, TM, drain, 0)


def _dispatch(dest, h, cap):
    n, d = h.shape
    return pl.pallas_call(
        _dispatch_kernel,
        grid_spec=pltpu.PrefetchScalarGridSpec(
            num_scalar_prefetch=1,
            grid=(n // TM,),
            in_specs=[pl.BlockSpec((TM, d), lambda i, dest: (i, 0)),
                      pl.BlockSpec(memory_space=pl.ANY)],
            out_specs=pl.BlockSpec(memory_space=pl.ANY),
            scratch_shapes=[pltpu.SemaphoreType.DMA(())]),
        out_shape=jax.ShapeDtypeStruct((cap, d), h.dtype),
        input_output_aliases={2: 0},
        compiler_params=_cparams(("arbitrary",)),
        name="moe_dispatch",
    )(dest, h, jnp.zeros((cap, d), h.dtype))


def _expert_kernel(be_ref, nu_ref, x_ref, wg_ref, wu_ref, wd_ref, y_ref):
    i = pl.program_id(0)

    @pl.when(i < nu_ref[0])
    def _():
        xb = x_ref[...].astype(BF16)
        gate = _dot(xb, wg_ref[0])
        up = _dot(xb, wu_ref[0])
        hid = (gate * jax.nn.sigmoid(gate) * up).astype(BF16)
        y_ref[...] = _dot(hid, wd_ref[0])

    @pl.when(i >= nu_ref[0])
    def _():
        y_ref[...] = jnp.zeros(y_ref.shape, y_ref.dtype)


def _experts(block_e, n_used, xb, w_gate, w_up, w_down):
    cap, d = xb.shape
    hdim = w_gate.shape[-1]
    return pl.pallas_call(
        _expert_kernel,
        grid_spec=pltpu.PrefetchScalarGridSpec(
            num_scalar_prefetch=2,
            grid=(cap // MOE_ROWS,),
            in_specs=[pl.BlockSpec((MOE_ROWS, d), lambda i, be, nu: (i, 0)),
                      pl.BlockSpec((1, d, hdim), lambda i, be, nu: (be[i], 0, 0)),
                      pl.BlockSpec((1, d, hdim), lambda i, be, nu: (be[i], 0, 0)),
                      pl.BlockSpec((1, hdim, d), lambda i, be, nu: (be[i], 0, 0))],
            out_specs=pl.BlockSpec((MOE_ROWS, d), lambda i, be, nu: (i, 0))),
        out_shape=jax.ShapeDtypeStruct((cap, d), F32),
        compiler_params=_cparams(("arbitrary",)),
        name="moe_experts",
    )(block_e, n_used, xb, w_gate, w_up, w_down)


def _combine_kernel(dest_ref, x_ref, rt_ref, g2_ref, fg_ref, yb_ref, o_ref, gath, sem, *,
                    tile_off, tiles_per_batch, final):
    b, i = pl.program_id(0), pl.program_id(1)
    base = (b * tiles_per_batch + i + tile_off) * TM

    def copy(r, k):
        d = dest_ref[(base + r) * 2 + k]
        return pltpu.make_async_copy(yb_ref.at[pl.ds(d, 1)], gath.at[k, pl.ds(r, 1)], sem)

    def issue(r, c):
        copy(r, 0).start()
        copy(r, 1).start()
        return c

    def drain(r, c):
        copy(r, 0).wait()
        copy(r, 1).wait()
        return c

    lax.fori_loop(0, TM, issue, 0)
    lax.fori_loop(0, TM, drain, 0)
    rt = rt_ref[0]
    m = rt[:, 2:3] * gath[0] + rt[:, 3:4] * gath[1]
    x = x_ref[0] + g2_ref[...] * m
    if final:
        x = _rms(x, fg_ref[...])
    o_ref[0] = x


def _combine(dest, x, route, mod, nbatch, final_g, yb, final):
    B, S, D = x.shape
    nt = S // TM
    off = 1 if final else 0
    g2_spec = pl.BlockSpec((None, 1, D_MODEL),
                           lambda b, i, dest: (jnp.where(i + off == 0, nbatch, b), 0, 5))
    return pl.pallas_call(
        functools.partial(_combine_kernel, tile_off=off, tiles_per_batch=nt, final=final),
        grid_spec=pltpu.PrefetchScalarGridSpec(
            num_scalar_prefetch=1,
            grid=(B, nt - off),
            in_specs=[pl.BlockSpec((1, TM, D), lambda b, i, dest: (b, i + off, 0)),
                      pl.BlockSpec((1, TM, LANES), lambda b, i, dest: (b, i + off, 0)),
                      g2_spec,
                      pl.BlockSpec((1, D), lambda b, i, dest: (0, 0)),
                      pl.BlockSpec(memory_space=pl.ANY)],
            out_specs=pl.BlockSpec((1, TM, D), lambda b, i, dest: (b, i, 0)),
            scratch_shapes=[pltpu.VMEM((2, TM, D), F32), pltpu.SemaphoreType.DMA(())]),
        out_shape=jax.ShapeDtypeStruct((B, S - off * TM, D), F32),
        compiler_params=_cparams(("arbitrary", "arbitrary")),
        name="moe_combine",
    )(dest, x, route, mod, final_g, yb)


def _rope_tables(seq):
    t = jnp.arange(seq, dtype=jnp.int32)
    row, col = t // GRID_W, t % GRID_W
    lane = jnp.arange(LANES, dtype=jnp.int32)
    tabs = []
    for r in (64, 32):
        nf = r // 4
        inv = 1.0 / (ROPE_BASE ** (jnp.arange(nf, dtype=F32) / nf))
        pos = jnp.where(((lane % r) < 2 * nf)[None, :], row[:, None], col[:, None]).astype(F32)
        ang = pos * inv[lane % nf][None, :]
        first = ((lane % (2 * nf)) < nf)[None, :]
        cos = jnp.cos(ang)
        sin = jnp.where(first, -jnp.sin(ang), jnp.sin(ang))
        tabs.append(jnp.concatenate([jnp.ones((CTX_LEN, LANES), F32), cos], axis=0))
        tabs.append(jnp.concatenate([jnp.zeros((CTX_LEN, LANES), F32), sin], axis=0))
    return tabs


def _layer_weights(w_in, w_out, w_uq, w_ukv, w_group, b_group, w_expert, b_expert):
    perm = jnp.array([0, 2, 1, 3])
    wq = w_in[:, :256].reshape(D_MODEL, 4, 64)[:, perm].reshape(D_MODEL, 256)
    w_in_p = jnp.concatenate([wq, w_in[:, 256:], jnp.zeros((D_MODEL, D_IN_PAD - w_in.shape[1]), F32)],
                             axis=1).astype(BF16)
    wo = jnp.concatenate([w_out[:256].reshape(4, 64, D_MODEL)[perm].reshape(256, D_MODEL), w_out[256:]],
                         axis=0).astype(BF16)
    uq = w_uq.reshape(256, 4, 96)
    blocks = []
    for h in range(4):
        blk = jnp.zeros((256, 256), F32)
        blk = blk.at[:, 64 * (h % 2):64 * (h % 2) + 64].set(uq[:, h, :64])
        blk = blk.at[:, LANES:LANES + 32].set(uq[:, h, 64:])
        blocks.append(blk)
    wuq = jnp.concatenate(blocks, axis=1).astype(BF16)
    ukv = w_ukv.reshape(128, 4, 128)
    wukk = ukv[:, :, :64].reshape(128, 256).astype(BF16)
    wukv = ukv[:, :, 64:].reshape(128, 256).astype(BF16)
    pad = LANES - MOE_GROUPS - MOE_EXPERTS
    w_route = jnp.concatenate([w_group, w_expert, jnp.zeros((D_MODEL, pad), F32)], axis=1)
    b_route = jnp.concatenate([b_group, b_expert, jnp.zeros((pad,), F32)]).reshape(1, LANES)
    return w_in_p, wo, wuq, wukk, wukv, w_route, b_route


def _moe_plan(route, ranks, counts, cap):
    e = route[:, :2].astype(jnp.int32)
    rank = ranks[:, :2].astype(jnp.int32)
    cnt = counts[0, :MOE_EXPERTS].astype(jnp.int32)
    padded = (cnt + MOE_ROWS - 1) // MOE_ROWS * MOE_ROWS
    pend = jnp.cumsum(padded)
    pstart = pend - padded
    dest = (pstart[e] + rank).reshape(-1)
    nblk = cap // MOE_ROWS
    block_e = jnp.minimum(jnp.searchsorted(pend, jnp.arange(nblk, dtype=jnp.int32) * MOE_ROWS, side='right'),
                          MOE_EXPERTS - 1).astype(jnp.int32)
    n_used = (pend[-1:] // MOE_ROWS).astype(jnp.int32)
    return dest, block_e, n_used


def kernel(x, c, ctx, c_ctx, w_mod, b_mod, norm1_g, norm2_g, w_in, w_out, swa_sink, diff_lambda, diff_subln_g, ret_decay_logit, ret_gn_g, mla_q_norm_g, mla_w_uq, mla_kv_norm_g, mla_w_ukv, moe_w_group, moe_b_group, moe_w_expert, moe_b_expert, moe_w_gate, moe_w_up, moe_w_down, final_norm_g):
    B, T, D = x.shape
    S = CTX_LEN + T
    n_tok = B * S
    tk = min(FLASH_TK, T)
    cap = -(-(2 * n_tok + MOE_EXPERTS * (MOE_ROWS - 1)) // MOE_ROWS) * MOE_ROWS
    xs = jnp.concatenate([ctx, x], axis=1)
    cvec = jnp.zeros((8, D), F32).at[:B].set(c).at[B].set(c_ctx)
    tabs = _rope_tables(T)
    row = lambda v: v.reshape(1, -1)

    for l in range(DEPTH):
        lambda_init = 0.8 - 0.6 * math.exp(-0.3 * l)
        w_in_p, wo, wuq, wukk, wukv, w_route, b_route = _layer_weights(
            w_in[l], w_out[l], mla_w_uq[l], mla_w_ukv[l], moe_w_group[l], moe_b_group[l],
            moe_w_expert[l], moe_b_expert[l])
        mod = _modulation(cvec, w_mod[l], b_mod[l]).reshape(8, 1, 6 * D)
        (swaq, swak, swav, dq, dk, dv, rq, rk, rv, rg, mq, mk, mv) = _inproj(
            xs, mod, B, row(norm1_g[l]), w_in_p, tabs, row(mla_q_norm_g[l]), wuq,
            row(mla_kv_norm_g[l]), wukk, wukv)
        sink_p = swa_sink[l]
        ya = _swa(sink_p, swaq, swak, swav)
        yb = _diff(diff_lambda[l], row(jnp.tile(diff_subln_g[l], 2)), dq, dk, dv, lambda_init, tk)
        dl_lane = jnp.repeat(ret_decay_logit[l], 64, axis=-1).reshape(2, 2, 1, LANES)
        yr = _retention(dl_lane, row(ret_gn_g[l]), rq, rk, rv, rg)
        ym = _mla(mq, mk, mv, tk)
        xs, h2, route = _outproj(xs, ya, yb, yr, ym, wo, mod, B, row(norm2_g[l]), w_route, b_route)
        route2 = route.reshape(n_tok, LANES)
        ranks, counts = _ranks(route2)
        dest, block_e, n_used = _moe_plan(route2, ranks, counts, cap)
        xb = _dispatch(dest, h2.reshape(n_tok, D), cap)
        yb_e = _experts(block_e, n_used, xb, moe_w_gate[l].astype(BF16), moe_w_up[l].astype(BF16),
                        moe_w_down[l].astype(BF16))
        final = l == DEPTH - 1
        xs = _combine(dest, xs, route, mod, B, row(final_norm_g), yb_e, final)
    return xs
```

```python
import functools
import math

import jax
import jax.numpy as jnp
from jax import lax
from jax.experimental import pallas as pl
from jax.experimental.pallas import tpu as pltpu

F32 = jnp.float32
BF16 = jnp.bfloat16

D_MODEL = 1024
CTX_LEN = 256
GRID_W = 64
ROPE_BASE = 10000.0
NORM_EPS = 1e-6
NEG_INF = -1e30
DEPTH = 2

SWA_WINDOW = 128
RET_CHUNK = 128
MLA_SCALE = (64 + 32) ** -0.5
LOG2E = math.log2(math.e)
MOE_GROUPS = 4
MOE_PER_GROUP = 8
MOE_EXPERTS = 32
MOE_HIDDEN = 512

LANES = 128
TM = 256
TQ = 256
FLASH_TK = 2048
MOE_ROWS = 256
VMEM_LIMIT = 56 * 1024 * 1024

C_SWA_Q, C_SWA_K, C_SWA_V = 0, 256, 384
C_DIF_Q, C_DIF_K, C_DIF_V = 512, 768, 1024
C_RET_Q, C_RET_K, C_RET_V, C_RET_G = 1280, 1536, 1792, 2048
C_MLA_Q, C_MLA_KV, C_MLA_KR = 2304, 2560, 2688
D_IN_PAD = 2816


def _cparams(sem):
    return pltpu.CompilerParams(dimension_semantics=sem, vmem_limit_bytes=VMEM_LIMIT)


def _dot(a, b):
    return jnp.dot(a, b, preferred_element_type=F32)


def _dot_nt(a, b):
    return lax.dot_general(a, b, (((1,), (1,)), ((), ())), preferred_element_type=F32)


def _dot_tn(a, b):
    return lax.dot_general(a, b, (((0,), (0,)), ((), ())), preferred_element_type=F32)


def _rms(x, g):
    ms = jnp.mean(x * x, axis=-1, keepdims=True)
    return x * lax.rsqrt(ms + NORM_EPS) * g


def _lane(shape):
    return lax.broadcasted_iota(jnp.int32, shape, len(shape) - 1)


def _rope(z, cos, sin, nf):
    first = (_lane(z.shape) % (2 * nf)) < nf
    partner = jnp.where(first, pltpu.roll(z, LANES - nf, 1), pltpu.roll(z, nf, 1))
    return z * cos + partner * sin


def _mod_kernel(a_ref, w_ref, b_ref, o_ref):
    a = a_ref[...]
    act = a * jax.nn.sigmoid(a)
    o_ref[...] = jnp.dot(act, w_ref[...], preferred_element_type=F32,
                         precision=lax.Precision.HIGHEST) + b_ref[...]


def _modulation(cvec, w_mod, b_mod):
    n = w_mod.shape[1]
    bn = 512
    return pl.pallas_call(
        _mod_kernel,
        grid=(n // bn,),
        in_specs=[pl.BlockSpec((8, D_MODEL), lambda j: (0, 0)),
                  pl.BlockSpec((D_MODEL, bn), lambda j: (0, j)),
                  pl.BlockSpec((1, bn), lambda j: (0, j))],
        out_specs=pl.BlockSpec((8, bn), lambda j: (0, j)),
        out_shape=jax.ShapeDtypeStruct((8, n), F32),
        compiler_params=_cparams(("parallel",)),
        name="modulation",
    )(cvec, w_mod, b_mod.reshape(1, n))


def _mod_spec(chunk, nbatch):
    return pl.BlockSpec((None, 1, D_MODEL),
                        lambda b, i: (jnp.where(i == 0, nbatch, b), 0, chunk))


def _inproj_kernel(x_ref, g_ref, sh_ref, sc_ref, w_ref, c64_ref, s64_ref, c32_ref, s32_ref,
                   qng_ref, wuq_ref, kvng_ref, wukk_ref, wukv_ref,
                   swaq, swak, swav, dq, dk, dv, rq, rk, rv, rg, mq, mk, mv):
    x = x_ref[0]
    h = _rms(x, g_ref[...]) * (1.0 + sc_ref[...]) + sh_ref[...]
    hb = h.astype(BF16)
    c64, s64, c32, s32 = c64_ref[...], s64_ref[...], c32_ref[...], s32_ref[...]

    def proj(c0, width):
        return _dot(hb, w_ref[:, c0:c0 + width])

    def rope_blocks(z, out, nf, scale):
        cos, sin = (c64, s64) if nf == 16 else (c32, s32)
        for j in range(z.shape[1] // LANES):
            zz = _rope(z[:, j * LANES:(j + 1) * LANES], cos, sin, nf)
            if scale != 1.0:
                zz = zz * scale
            out[0, :, j * LANES:(j + 1) * LANES] = zz.astype(out.dtype)

    rope_blocks(proj(C_SWA_Q, 256), swaq, 16, 0.125)
    rope_blocks(proj(C_SWA_K, 128), swak, 16, 1.0)
    swav[0] = proj(C_SWA_V, 128).astype(BF16)
    def values_with_ones(z, out):
        lane = _lane((z.shape[0], LANES))
        for hh in range(4):
            zb = z[:, (hh // 2) * LANES:(hh // 2 + 1) * LANES]
            keep = (lane < 64) if hh % 2 == 0 else (lane >= 64)
            out[0, :, hh * LANES:(hh + 1) * LANES] = jnp.where(keep, zb, 1.0).astype(BF16)

    rope_blocks(proj(C_DIF_Q, 256), dq, 8, 32 ** -0.5 * LOG2E)
    rope_blocks(proj(C_DIF_K, 256), dk, 8, 1.0)
    values_with_ones(proj(C_DIF_V, 256), dv)
    rope_blocks(proj(C_RET_Q, 256), rq, 16, 0.125)
    rope_blocks(proj(C_RET_K, 256), rk, 16, 1.0)
    rv[0] = proj(C_RET_V, 256).astype(BF16)
    rg[0] = proj(C_RET_G, 256).astype(BF16)

    ql = _rms(proj(C_MLA_Q, 256), qng_ref[...]).astype(BF16)
    qa = _dot(ql, wuq_ref[...]) * (MLA_SCALE * LOG2E)
    for hh in range(4):
        mq[0, :, hh * 256:hh * 256 + LANES] = qa[:, hh * 256:hh * 256 + LANES].astype(BF16)
        zr = _rope(qa[:, hh * 256 + LANES:(hh + 1) * 256], c32, s32, 8)
        mq[0, :, hh * 256 + LANES:(hh + 1) * 256] = zr.astype(BF16)
    kvl = _rms(proj(C_MLA_KV, 128), kvng_ref[...]).astype(BF16)
    kn = _dot(kvl, wukk_ref[...])
    kr = _rope(proj(C_MLA_KR, 128), c32, s32, 8).astype(BF16)
    for p in range(2):
        mk[0, :, p * 256:p * 256 + LANES] = kn[:, p * LANES:(p + 1) * LANES].astype(BF16)
        mk[0, :, p * 256 + LANES:(p + 1) * 256] = kr
    values_with_ones(_dot(kvl, wukv_ref[...]), mv)


def _inproj(x, mod, nbatch, norm_g, w_in_p, tabs, qng, wuq, kvng, wukk, wukv):
    B, S, D = x.shape
    nt = S // TM
    tok = lambda w: pl.BlockSpec((1, TM, w), lambda b, i: (b, i, 0))
    full = lambda a: pl.BlockSpec(a.shape, lambda b, i: (0,) * a.ndim)
    tab = pl.BlockSpec((TM, LANES), lambda b, i: (i, 0))
    widths = [256, 128, 128, 256, 256, 512, 256, 256, 256, 256, 1024, 512, 512]
    return pl.pallas_call(
        _inproj_kernel,
        grid=(B, nt),
        in_specs=[tok(D), full(norm_g), _mod_spec(0, nbatch), _mod_spec(1, nbatch), full(w_in_p),
                  tab, tab, tab, tab, full(qng), full(wuq), full(kvng), full(wukk), full(wukv)],
        out_specs=[tok(w) for w in widths],
        out_shape=[jax.ShapeDtypeStruct((B, S, w), BF16) for w in widths],
        compiler_params=_cparams(("parallel", "parallel")),
        name="inproj",
    )(x, norm_g, mod, mod, w_in_p, *tabs, qng, wuq, kvng, wukk, wukv)


def _swa_kernel(sink_ref, q_ref, kp_ref, ko_ref, kn_ref, vp_ref, vo_ref, vn_ref, kc_ref, vc_ref,
                o_ref, *, n_ctx_tiles, seq):
    i = pl.program_id(1)
    blk = RET_CHUNK
    n = i - n_ctx_tiles
    kband = jnp.concatenate([kp_ref[0], ko_ref[0], kn_ref[0]], axis=0)
    vband = jnp.concatenate([vp_ref[0], vo_ref[0], vn_ref[0]], axis=0)
    kc, vc = kc_ref[0], vc_ref[0]
    r_io = lax.broadcasted_iota(jnp.int32, (blk, 3 * blk), 0)
    c_io = lax.broadcasted_iota(jnp.int32, (blk, 3 * blk), 1)
    kpos = (n - 1) * blk + c_io
    valid = (c_io >= r_io) & (c_io <= r_io + 2 * SWA_WINDOW) & (kpos >= 0) & (kpos < seq) & (n >= 0)
    lane = _lane((blk, LANES))
    for r in range(2):
        qb = q_ref[0, :, r * LANES:(r + 1) * LANES]
        outs = []
        for g in range(2):
            gmask = (lane >= 64 * g) & (lane < 64 * (g + 1))
            qm = jnp.where(gmask, qb, jnp.zeros_like(qb))
            sink = sink_ref[2 * g + r]
            s_band = jnp.where(valid, _dot_nt(qm, kband), NEG_INF)
            s_ctx = _dot_nt(qm, kc)
            m = jnp.maximum(jnp.maximum(jnp.max(s_band, axis=-1, keepdims=True),
                                        jnp.max(s_ctx, axis=-1, keepdims=True)), sink)
            p_band = jnp.exp(s_band - m)
            p_ctx = jnp.exp(s_ctx - m)
            l = (jnp.sum(p_band, axis=-1, keepdims=True) + jnp.sum(p_ctx, axis=-1, keepdims=True)
                 + jnp.exp(sink - m))
            o = _dot(p_band.astype(BF16), vband) + _dot(p_ctx.astype(BF16), vc)
            outs.append(o / l)
        o_ref[0, :, r * LANES:(r + 1) * LANES] = jnp.where(lane < 64, outs[0], outs[1]).astype(BF16)


def _swa(sink, q, k, v):
    B, S, _ = q.shape
    blk = RET_CHUNK
    nt = S // blk
    nct = CTX_LEN // blk
    kvs = lambda f: pl.BlockSpec((1, blk, LANES), f)
    prev = lambda b, i: (b, jnp.maximum(i - 1, 0), 0)
    own = lambda b, i: (b, i, 0)
    nxt = lambda b, i: (b, jnp.minimum(i + 1, nt - 1), 0)
    ctx = pl.BlockSpec((1, CTX_LEN, LANES), lambda b, i: (b, 0, 0))
    return pl.pallas_call(
        functools.partial(_swa_kernel, n_ctx_tiles=nct, seq=S - CTX_LEN),
        grid=(B, nt),
        in_specs=[pl.BlockSpec(memory_space=pltpu.SMEM),
                  pl.BlockSpec((1, blk, 256), own),
                  kvs(prev), kvs(own), kvs(nxt), kvs(prev), kvs(own), kvs(nxt), ctx, ctx],
        out_specs=pl.BlockSpec((1, blk, 256), own),
        out_shape=jax.ShapeDtypeStruct((B, S, 256), BF16),
        compiler_params=_cparams(("parallel", "parallel")),
        name="swa",
    )(sink, q, k, k, k, v, v, v, k, v)


def _flash(q_maps, v_blk, k_ref, v_ref, m_scr, acc_scr, *, is_ctx, tk, seq):
    nm = len(q_maps)
    for a in range(nm):
        m_scr[a] = jnp.full(m_scr.shape[1:], NEG_INF, F32)
        acc_scr[a] = jnp.zeros(acc_scr.shape[1:], F32)

    def chunk(start, size):
        kc = k_ref[0, pl.ds(start, size), :]
        nt = size // LANES
        for a in range(nm):
            vc = v_ref[0, pl.ds(start, size), v_blk[a] * LANES:(v_blk[a] + 1) * LANES]
            s = _dot_nt(q_maps[a], kc)
            mt = s[:, 0:LANES]
            for t in range(1, nt):
                mt = jnp.maximum(mt, s[:, t * LANES:(t + 1) * LANES])
            m_prev = m_scr[a]
            m_new = jnp.maximum(m_prev, jnp.max(mt, axis=-1, keepdims=True))
            p = jnp.concatenate(
                [jnp.exp2(s[:, t * LANES:(t + 1) * LANES] - m_new).astype(BF16) for t in range(nt)],
                axis=1)
            acc_scr[a] = jnp.exp2(m_prev - m_new) * acc_scr[a] + _dot(p, vc)
            m_scr[a] = m_new

    chunk(0, CTX_LEN)
    n_lat = jnp.where(is_ctx, 0, seq // tk)

    def body(c, carry):
        chunk(pl.multiple_of(CTX_LEN + c * tk, LANES), tk)
        return carry

    lax.fori_loop(0, n_lat, body, 0)
    outs = []
    for a in range(nm):
        acc = acc_scr[a]
        outs.append(acc / pltpu.roll(acc, 64, 1))
    return outs


def _flash_scratch(nm):
    return [pltpu.VMEM((nm, TQ, LANES), F32), pltpu.VMEM((nm, TQ, LANES), F32)]


def _diff_kernel(lam_ref, g_ref, q_ref, k_ref, v_ref, o_ref, m_scr, acc_scr, *,
                 tk, seq, lambda_init):
    i = pl.program_id(2)
    qb = q_ref[0]
    lane = _lane(qb.shape)
    q_maps = []
    for a in range(4):
        msk = (lane >= 32 * a) & (lane < 32 * (a + 1))
        q_maps.append(jnp.where(msk, qb, jnp.zeros_like(qb)))
    n = _flash(q_maps, (0, 0, 1, 1), k_ref, v_ref, m_scr, acc_scr, is_ctx=(i == 0), tk=tk, seq=seq)
    lp = lam_ref[...]
    lam = (jnp.exp(jnp.sum(lp[0:1] * lp[1:2], axis=-1, keepdims=True))
           - jnp.exp(jnp.sum(lp[2:3] * lp[3:4], axis=-1, keepdims=True)) + lambda_init)
    lane_o = _lane((TQ, LANES))
    o = jnp.where(lane_o < 64, n[0] - lam * n[1], n[2] - lam * n[3])
    sq = o * o
    ms0 = jnp.sum(jnp.where(lane_o < 64, sq, 0.0), axis=-1, keepdims=True) * (1.0 / 64)
    ms1 = jnp.sum(jnp.where(lane_o >= 64, sq, 0.0), axis=-1, keepdims=True) * (1.0 / 64)
    ms = jnp.where(lane_o < 64, ms0, ms1)
    y = o * lax.rsqrt(ms + NORM_EPS) * g_ref[...] * (1.0 - lambda_init)
    o_ref[0] = y.astype(BF16)


def _diff(lam_p, subln_g2, q, k, v, lambda_init, tk):
    B, S, _ = q.shape
    nq = S // TQ
    return pl.pallas_call(
        functools.partial(_diff_kernel, tk=tk, seq=S - CTX_LEN, lambda_init=lambda_init),
        grid=(B, 2, nq),
        in_specs=[pl.BlockSpec(lam_p.shape, lambda b, p, i: (0, 0)),
                  pl.BlockSpec((1, LANES), lambda b, p, i: (0, 0)),
                  pl.BlockSpec((1, TQ, LANES), lambda b, p, i: (b, i, p)),
                  pl.BlockSpec((1, S, LANES), lambda b, p, i: (b, 0, p)),
                  pl.BlockSpec((1, S, 256), lambda b, p, i: (b, 0, p))],
        out_specs=pl.BlockSpec((1, TQ, LANES), lambda b, p, i: (b, i, p)),
        out_shape=jax.ShapeDtypeStruct((B, S, 256), BF16),
        scratch_shapes=_flash_scratch(4),
        compiler_params=_cparams(("parallel", "parallel", "parallel")),
        name="diff_attn",
    )(lam_p, subln_g2, q, k, v)


def _mla_kernel(q_ref, k_ref, v_ref, o_ref, m_scr, acc_scr, *, tk, seq):
    i = pl.program_id(2)
    q_maps = [q_ref[0, :, 0:256], q_ref[0, :, 256:512]]
    n = _flash(q_maps, (0, 1), k_ref, v_ref, m_scr, acc_scr, is_ctx=(i == 0), tk=tk, seq=seq)
    lane_o = _lane((TQ, LANES))
    o_ref[0] = jnp.where(lane_o < 64, n[0], n[1]).astype(BF16)


def _mla(q, k, v, tk):
    B, S, _ = q.shape
    nq = S // TQ
    return pl.pallas_call(
        functools.partial(_mla_kernel, tk=tk, seq=S - CTX_LEN),
        grid=(B, 2, nq),
        in_specs=[pl.BlockSpec((1, TQ, 512), lambda b, p, i: (b, i, p)),
                  pl.BlockSpec((1, S, 256), lambda b, p, i: (b, 0, p)),
                  pl.BlockSpec((1, S, 256), lambda b, p, i: (b, 0, p))],
        out_specs=pl.BlockSpec((1, TQ, LANES), lambda b, p, i: (b, i, p)),
        out_shape=jax.ShapeDtypeStruct((B, S, 256), BF16),
        scratch_shapes=_flash_scratch(2),
        compiler_params=_cparams(("parallel", "parallel", "parallel")),
        name="mla_attn",
    )(q, k, v)


def _ret_tables(dl, backward):
    L = RET_CHUNK
    lg = -(jnp.maximum(-dl, 0.0) + jnp.log(1.0 + jnp.exp(-jnp.abs(dl))))
    row = lax.broadcasted_iota(jnp.int32, (L, LANES), 0).astype(F32)
    col = lax.broadcasted_iota(jnp.int32, (L, LANES), 1)
    colf = col.astype(F32)
    if backward:
        qd, kd, rel = jnp.exp((L - row) * lg), jnp.exp(row * lg), colf - row
    else:
        qd, kd, rel = jnp.exp((row + 1.0) * lg), jnp.exp((L - 1.0 - row) * lg), row - colf
    decs = []
    for hh in range(2):
        lg_h = jnp.sum(jnp.where(col[0:1] == 64 * hh, lg, 0.0), axis=-1, keepdims=True)
        decs.append(jnp.where(rel >= 0, jnp.exp(jnp.maximum(rel, 0.0) * lg_h), 0.0))
    return qd, kd, decs[0], decs[1], jnp.exp(float(L) * lg)


def _ret_chunk(q, k, v, tab, cd, s_ref):
    col = _lane((RET_CHUNK, LANES))
    qf, kf = q.astype(F32), k.astype(F32)
    o = _dot((qf * tab[0]).astype(BF16), s_ref[...].astype(BF16))
    outs = []
    for hh in range(2):
        hmask = (col >= 64 * hh) & (col < 64 * (hh + 1))
        a = _dot_nt(jnp.where(hmask, q, jnp.zeros_like(q)), k) * tab[2 + hh]
        outs.append(_dot(a.astype(BF16), v))
    o = o + jnp.where(col < 64, outs[0], outs[1])
    rowi = lax.broadcasted_iota(jnp.int32, (LANES, LANES), 0)
    coli = lax.broadcasted_iota(jnp.int32, (LANES, LANES), 1)
    s_new = s_ref[...] * cd + _dot_tn((kf * tab[1]).astype(BF16), v)
    s_ref[...] = jnp.where((rowi < 64) == (coli < 64), s_new, 0.0)
    return o


def _ret_kernel(dl_ref, qf_ref, kf_ref, vf_ref, qb_ref, kb_ref, vb_ref, of_ref, ob_ref,
                s_scr, tab_scr, cd_scr):
    nb = qf_ref.shape[0]

    @pl.when(pl.program_id(0) == 0)
    def _():
        s_scr[...] = jnp.zeros(s_scr.shape, F32)
        for d in range(2):
            for p in range(2):
                tabs = _ret_tables(dl_ref[d, p], backward=(d == 1))
                for t in range(4):
                    tab_scr[d, p, t] = tabs[t]
                cd_scr[d, p] = tabs[4]

    for d, (q_ref, k_ref, v_ref, o_ref) in enumerate(((qf_ref, kf_ref, vf_ref, of_ref),
                                                      (qb_ref, kb_ref, vb_ref, ob_ref))):
        for p in range(2):
            tab = [tab_scr[d, p, t] for t in range(4)]
            cd = cd_scr[d, p]
            ln = slice(p * LANES, (p + 1) * LANES)
            for b in range(nb):
                o_ref[b, :, ln] = _ret_chunk(q_ref[b, :, ln], k_ref[b, :, ln], v_ref[b, :, ln],
                                             tab, cd, s_scr.at[d, b, p])


def _retention(dl_lane, q, k, v):
    B, S, _ = q.shape
    L = RET_CHUNK
    nc = S // L
    ncc = CTX_LEN // L
    fwd = pl.BlockSpec((B, L, 256), lambda s: (0, s, 0))
    bwd = pl.BlockSpec((B, L, 256), lambda s: (0, jnp.where(s < ncc, ncc - 1 - s, nc - 1 + ncc - s), 0))
    return pl.pallas_call(
        _ret_kernel,
        grid=(nc,),
        in_specs=[pl.BlockSpec(dl_lane.shape, lambda s: (0, 0, 0, 0)), fwd, fwd, fwd, bwd, bwd, bwd],
        out_specs=[fwd, bwd],
        out_shape=[jax.ShapeDtypeStruct((B, S, 256), F32)] * 2,
        scratch_shapes=[pltpu.VMEM((2, B, 2, LANES, LANES), F32),
                        pltpu.VMEM((2, 2, 4, L, LANES), F32), pltpu.VMEM((2, 2, 1, LANES), F32)],
        compiler_params=_cparams(("arbitrary",)),
        name="retention",
    )(dl_lane, q, k, v, q, k, v)


def _gated_group_norm(o, g, gn):
    lo = _lane(o.shape) < 64

    def halves(t):
        a = jnp.sum(jnp.where(lo, t, 0.0), axis=-1, keepdims=True) * (1.0 / 64)
        b = jnp.sum(jnp.where(lo, 0.0, t), axis=-1, keepdims=True) * (1.0 / 64)
        return jnp.where(lo, a, b)

    d = o - halves(o)
    var = halves(d * d)
    return d * lax.rsqrt(var + NORM_EPS) * gn * (g * jax.nn.sigmoid(g))


def _outproj_kernel(x_ref, ya_ref, yb_ref, of_ref, ob_ref, rg_ref, gn_ref, ym_ref, w_ref, g1_ref,
                    sh_ref, sc_ref, ng_ref, wr_ref, br_ref, xo_ref, h_ref, rt_ref):
    mix = (_dot(ya_ref[0], w_ref[0:256]) + _dot(yb_ref[0], w_ref[256:512])
           + _dot(ym_ref[0], w_ref[768:1024]))
    for j in range(2):
        ln = slice(j * LANES, (j + 1) * LANES)
        yr = _gated_group_norm(of_ref[0, :, ln] + ob_ref[0, :, ln], rg_ref[0, :, ln].astype(F32),
                               gn_ref[:, ln])
        mix = mix + _dot(yr.astype(BF16), w_ref[512 + j * LANES:512 + (j + 1) * LANES])
    x = x_ref[0] + g1_ref[...] * mix
    xo_ref[0] = x
    h = _rms(x, ng_ref[...]) * (1.0 + sc_ref[...]) + sh_ref[...]
    h_ref[0] = h
    logits = jnp.dot(h, wr_ref[...], preferred_element_type=F32,
                     precision=lax.Precision.HIGHEST) + br_ref[...]
    lane = _lane(logits.shape).astype(F32)

    def first_argmax(vals, mask):
        mx = jnp.max(jnp.where(mask, vals, NEG_INF), axis=-1, keepdims=True)
        idx = jnp.min(jnp.where(mask & (vals == mx), lane, float(LANES)), axis=-1, keepdims=True)
        return mx, idx

    gmask = lane < MOE_GROUPS
    g_max, g_idx = first_argmax(logits, gmask)
    g_w = 1.0 / jnp.sum(jnp.exp(jnp.where(gmask, logits - g_max, NEG_INF)), axis=-1, keepdims=True)
    e_lo = MOE_GROUPS + g_idx * MOE_PER_GROUP
    emask = (lane >= e_lo) & (lane < e_lo + MOE_PER_GROUP)
    v1, i1 = first_argmax(logits, emask)
    v2, i2 = first_argmax(logits, emask & (lane != i1))
    t = jnp.exp(v2 - v1)
    w1 = g_w / (1.0 + t)
    w2 = w1 * t
    rt = jnp.where(lane == 0, i1 - MOE_GROUPS,
                   jnp.where(lane == 1, i2 - MOE_GROUPS,
                             jnp.where(lane == 2, w1, jnp.where(lane == 3, w2, 0.0))))
    rt_ref[0] = rt


def _outproj(x, ya, yb, o_f, o_b, rg, gn_g, ym, w_out_p, mod, nbatch, norm_g, w_route, b_route):
    B, S, D = x.shape
    nt = S // TM
    tok = lambda w: pl.BlockSpec((1, TM, w), lambda b, i: (b, i, 0))
    full = lambda a: pl.BlockSpec(a.shape, lambda b, i: (0,) * a.ndim)
    return pl.pallas_call(
        _outproj_kernel,
        grid=(B, nt),
        in_specs=[tok(D), tok(256), tok(256), tok(256), tok(256), tok(256), full(gn_g), tok(256),
                  full(w_out_p),
                  _mod_spec(2, nbatch), _mod_spec(3, nbatch), _mod_spec(4, nbatch), full(norm_g),
                  full(w_route), full(b_route)],
        out_specs=[tok(D), tok(D), tok(LANES)],
        out_shape=[jax.ShapeDtypeStruct((B, S, D), F32), jax.ShapeDtypeStruct((B, S, D), F32),
                   jax.ShapeDtypeStruct((B, S, LANES), F32)],
        compiler_params=_cparams(("parallel", "parallel")),
        name="outproj_router",
    )(x, ya, yb, o_f, o_b, rg, gn_g, ym, w_out_p, mod, mod, mod, norm_g, w_route, b_route)


def _rank_kernel(rt_ref, rank_ref, cnt_ref, carry):
    @pl.when(pl.program_id(0) == 0)
    def _():
        carry[...] = jnp.zeros(carry.shape, F32)

    rt = rt_ref[...]
    lane = _lane(rt.shape)
    lanef = lane.astype(F32)
    oh0 = jnp.where(lanef == rt[:, 0:1], 1.0, 0.0)
    oh1 = jnp.where(lanef == rt[:, 1:2], 1.0, 0.0)
    cnt = oh0 + oh1
    r_io = lax.broadcasted_iota(jnp.int32, (TM, TM), 0)
    c_io = lax.broadcasted_iota(jnp.int32, (TM, TM), 1)
    tri = jnp.where(r_io > c_io, 1.0, 0.0).astype(BF16)
    before = _dot(tri, cnt.astype(BF16)) + carry[...]
    r0 = jnp.sum(oh0 * before, axis=-1, keepdims=True)
    r1 = jnp.sum(oh1 * before, axis=-1, keepdims=True)
    rank_ref[...] = jnp.where(lane == 0, r0, jnp.where(lane == 1, r1, 0.0))
    carry[...] = carry[...] + jnp.sum(cnt, axis=0, keepdims=True)
    cnt_ref[...] = carry[...]


def _ranks(route):
    n = route.shape[0]
    return pl.pallas_call(
        _rank_kernel,
        grid=(n // TM,),
        in_specs=[pl.BlockSpec((TM, LANES), lambda i: (i, 0))],
        out_specs=[pl.BlockSpec((TM, LANES), lambda i: (i, 0)),
                   pl.BlockSpec((1, LANES), lambda i: (0, 0))],
        out_shape=[jax.ShapeDtypeStruct((n, LANES), F32), jax.ShapeDtypeStruct((1, LANES), F32)],
        scratch_shapes=[pltpu.VMEM((1, LANES), F32)],
        compiler_params=_cparams(("arbitrary",)),
        name="moe_rank",
    )(route)


def _dispatch_kernel(dest_ref, h_ref, xb_in_ref, xb_ref, sem):
    del xb_in_ref
    base = pl.program_id(0) * TM

    def copy(r, k):
        d = dest_ref[(base + r) * 2 + k]
        return pltpu.make_async_copy(h_ref.at[pl.ds(r, 1)], xb_ref.at[pl.ds(d, 1)], sem)

    def issue(r, c):
        copy(r, 0).start()
        copy(r, 1).start()
        return c

    def drain(r, c):
        copy(r, 0).wait()
        copy(r, 1).wait()
        return c

    lax.fori_loop(0, TM, issue, 0)
    lax.fori_loop(0, TM, drain, 0)


def _dispatch(dest, h, cap):
    n, d = h.shape
    return pl.pallas_call(
        _dispatch_kernel,
        grid_spec=pltpu.PrefetchScalarGridSpec(
            num_scalar_prefetch=1,
            grid=(n // TM,),
            in_specs=[pl.BlockSpec((TM, d), lambda i, dest: (i, 0)),
                      pl.BlockSpec(memory_space=pl.ANY)],
            out_specs=pl.BlockSpec(memory_space=pl.ANY),
            scratch_shapes=[pltpu.SemaphoreType.DMA(())]),
        out_shape=jax.ShapeDtypeStruct((cap, d), h.dtype),
        input_output_aliases={2: 0},
        compiler_params=_cparams(("arbitrary",)),
        name="moe_dispatch",
    )(dest, h, jnp.zeros((cap, d), h.dtype))


def _expert_kernel(be_ref, nu_ref, x_ref, wg_ref, wu_ref, wd_ref, y_ref, wg_s, wu_s, wd_s):
    i = pl.program_id(0)

    @pl.when((i == 0) | (be_ref[i] != be_ref[jnp.maximum(i - 1, 0)]))
    def _():
        wg_s[...] = wg_ref[0].astype(BF16)
        wu_s[...] = wu_ref[0].astype(BF16)
        wd_s[...] = wd_ref[0].astype(BF16)

    @pl.when(i < nu_ref[0])
    def _():
        xb = x_ref[...].astype(BF16)
        gate = _dot(xb, wg_s[...])
        up = _dot(xb, wu_s[...])
        hid = (gate * jax.nn.sigmoid(gate) * up).astype(BF16)
        y_ref[...] = _dot(hid, wd_s[...])

    @pl.when(i >= nu_ref[0])
    def _():
        y_ref[...] = jnp.zeros(y_ref.shape, y_ref.dtype)


def _experts(block_e, n_used, xb, w_gate, w_up, w_down, layer):
    cap, d = xb.shape
    hdim = w_gate.shape[-1]
    return pl.pallas_call(
        _expert_kernel,
        grid_spec=pltpu.PrefetchScalarGridSpec(
            num_scalar_prefetch=2,
            grid=(cap // MOE_ROWS,),
            in_specs=[pl.BlockSpec((MOE_ROWS, d), lambda i, be, nu: (i, 0)),
                      pl.BlockSpec((None, 1, d, hdim), lambda i, be, nu: (layer, be[i], 0, 0)),
                      pl.BlockSpec((None, 1, d, hdim), lambda i, be, nu: (layer, be[i], 0, 0)),
                      pl.BlockSpec((None, 1, hdim, d), lambda i, be, nu: (layer, be[i], 0, 0))],
            out_specs=pl.BlockSpec((MOE_ROWS, d), lambda i, be, nu: (i, 0)),
            scratch_shapes=[pltpu.VMEM((d, hdim), BF16), pltpu.VMEM((d, hdim), BF16),
                            pltpu.VMEM((hdim, d), BF16)]),
        out_shape=jax.ShapeDtypeStruct((cap, d), F32),
        compiler_params=_cparams(("arbitrary",)),
        name="moe_experts",
    )(block_e, n_used, xb, w_gate, w_up, w_down)


def _combine_kernel(dest_ref, x_ref, rt_ref, g2_ref, fg_ref, yb_ref, o_ref, gath, sem, *,
                    tile_off, tiles_per_batch, final):
    b, i = pl.program_id(0), pl.program_id(1)
    base = (b * tiles_per_batch + i + tile_off) * TM

    def copy(r, k):
        d = dest_ref[(base + r) * 2 + k]
        return pltpu.make_async_copy(yb_ref.at[pl.ds(d, 1)], gath.at[k, pl.ds(r, 1)], sem)

    def issue(r, c):
        copy(r, 0).start()
        copy(r, 1).start()
        return c

    def drain(r, c):
        copy(r, 0).wait()
        copy(r, 1).wait()
        return c

    lax.fori_loop(0, TM, issue, 0)
    lax.fori_loop(0, TM, drain, 0)
    rt = rt_ref[0]
    m = rt[:, 2:3] * gath[0] + rt[:, 3:4] * gath[1]
    x = x_ref[0] + g2_ref[...] * m
    if final:
        x = _rms(x, fg_ref[...])
    o_ref[0] = x


def _combine(dest, x, route, mod, nbatch, final_g, yb, final):
    B, S, D = x.shape
    nt = S // TM
    off = 1 if final else 0
    g2_spec = pl.BlockSpec((None, 1, D_MODEL),
                           lambda b, i, dest: (jnp.where(i + off == 0, nbatch, b), 0, 5))
    return pl.pallas_call(
        functools.partial(_combine_kernel, tile_off=off, tiles_per_batch=nt, final=final),
        grid_spec=pltpu.PrefetchScalarGridSpec(
            num_scalar_prefetch=1,
            grid=(B, nt - off),
            in_specs=[pl.BlockSpec((1, TM, D), lambda b, i, dest: (b, i + off, 0)),
                      pl.BlockSpec((1, TM, LANES), lambda b, i, dest: (b, i + off, 0)),
                      g2_spec,
                      pl.BlockSpec((1, D), lambda b, i, dest: (0, 0)),
                      pl.BlockSpec(memory_space=pl.ANY)],
            out_specs=pl.BlockSpec((1, TM, D), lambda b, i, dest: (b, i, 0)),
            scratch_shapes=[pltpu.VMEM((2, TM, D), F32), pltpu.SemaphoreType.DMA(())]),
        out_shape=jax.ShapeDtypeStruct((B, S - off * TM, D), F32),
        compiler_params=_cparams(("arbitrary", "arbitrary")),
        name="moe_combine",
    )(dest, x, route, mod, final_g, yb)


def _rope_tables(seq):
    t = jnp.arange(seq, dtype=jnp.int32)
    row, col = t // GRID_W, t % GRID_W
    lane = jnp.arange(LANES, dtype=jnp.int32)
    tabs = []
    for r in (64, 32):
        nf = r // 4
        inv = 1.0 / (ROPE_BASE ** (jnp.arange(nf, dtype=F32) / nf))
        pos = jnp.where(((lane % r) < 2 * nf)[None, :], row[:, None], col[:, None]).astype(F32)
        ang = pos * inv[lane % nf][None, :]
        first = ((lane % (2 * nf)) < nf)[None, :]
        cos = jnp.cos(ang)
        sin = jnp.where(first, -jnp.sin(ang), jnp.sin(ang))
        tabs.append(jnp.concatenate([jnp.ones((CTX_LEN, LANES), F32), cos], axis=0))
        tabs.append(jnp.concatenate([jnp.zeros((CTX_LEN, LANES), F32), sin], axis=0))
    return tabs


def _layer_weights(w_in, w_out, w_uq, w_ukv, w_group, b_group, w_expert, b_expert):
    perm = jnp.array([0, 2, 1, 3])
    wq = w_in[:, :256].reshape(D_MODEL, 4, 64)[:, perm].reshape(D_MODEL, 256)
    w_in_p = jnp.concatenate([wq, w_in[:, 256:], jnp.zeros((D_MODEL, D_IN_PAD - w_in.shape[1]), F32)],
                             axis=1).astype(BF16)
    wo = jnp.concatenate([w_out[:256].reshape(4, 64, D_MODEL)[perm].reshape(256, D_MODEL), w_out[256:]],
                         axis=0).astype(BF16)
    uq = w_uq.reshape(256, 4, 96)
    blocks = []
    for h in range(4):
        blk = jnp.zeros((256, 256), F32)
        blk = blk.at[:, 64 * (h % 2):64 * (h % 2) + 64].set(uq[:, h, :64])
        blk = blk.at[:, LANES:LANES + 32].set(uq[:, h, 64:])
        blocks.append(blk)
    wuq = jnp.concatenate(blocks, axis=1).astype(BF16)
    ukv = w_ukv.reshape(128, 4, 128)
    wukk = ukv[:, :, :64].reshape(128, 256).astype(BF16)
    wukv = ukv[:, :, 64:].reshape(128, 256).astype(BF16)
    pad = LANES - MOE_GROUPS - MOE_EXPERTS
    w_route = jnp.concatenate([w_group, w_expert, jnp.zeros((D_MODEL, pad), F32)], axis=1)
    b_route = jnp.concatenate([b_group, b_expert, jnp.zeros((pad,), F32)]).reshape(1, LANES)
    return w_in_p, wo, wuq, wukk, wukv, w_route, b_route


def _moe_plan(route, ranks, counts, cap):
    e = route[:, :2].astype(jnp.int32)
    rank = ranks[:, :2].astype(jnp.int32)
    cnt = counts[0, :MOE_EXPERTS].astype(jnp.int32)
    padded = (cnt + MOE_ROWS - 1) // MOE_ROWS * MOE_ROWS
    pend = jnp.cumsum(padded)
    pstart = pend - padded
    dest = (pstart[e] + rank).reshape(-1)
    nblk = cap // MOE_ROWS
    blk_start = jnp.arange(nblk, dtype=jnp.int32) * MOE_ROWS
    block_e = jnp.minimum(jnp.sum((pend[None, :] <= blk_start[:, None]).astype(jnp.int32), axis=1),
                          MOE_EXPERTS - 1)
    n_used = (pend[-1:] // MOE_ROWS).astype(jnp.int32)
    return dest, block_e, n_used


def kernel(x, c, ctx, c_ctx, w_mod, b_mod, norm1_g, norm2_g, w_in, w_out, swa_sink, diff_lambda, diff_subln_g, ret_decay_logit, ret_gn_g, mla_q_norm_g, mla_w_uq, mla_kv_norm_g, mla_w_ukv, moe_w_group, moe_b_group, moe_w_expert, moe_b_expert, moe_w_gate, moe_w_up, moe_w_down, final_norm_g):
    B, T, D = x.shape
    S = CTX_LEN + T
    n_tok = B * S
    tk = min(FLASH_TK, T)
    cap = -(-(2 * n_tok + MOE_EXPERTS * (MOE_ROWS - 1)) // MOE_ROWS) * MOE_ROWS
    xs = jnp.concatenate([ctx, x], axis=1)
    cvec = jnp.zeros((8, D), F32).at[:B].set(c).at[B].set(c_ctx)
    tabs = _rope_tables(T)
    row = lambda v: v.reshape(1, -1)

    for l in range(DEPTH):
        lambda_init = 0.8 - 0.6 * math.exp(-0.3 * l)
        w_in_p, wo, wuq, wukk, wukv, w_route, b_route = _layer_weights(
            w_in[l], w_out[l], mla_w_uq[l], mla_w_ukv[l], moe_w_group[l], moe_b_group[l],
            moe_w_expert[l], moe_b_expert[l])
        mod = _modulation(cvec, w_mod[l], b_mod[l]).reshape(8, 1, 6 * D)
        (swaq, swak, swav, dq, dk, dv, rq, rk, rv, rg, mq, mk, mv) = _inproj(
            xs, mod, B, row(norm1_g[l]), w_in_p, tabs, row(mla_q_norm_g[l]), wuq,
            row(mla_kv_norm_g[l]), wukk, wukv)
        sink_p = swa_sink[l]
        ya = _swa(sink_p, swaq, swak, swav)
        yb = _diff(diff_lambda[l], row(jnp.tile(diff_subln_g[l], 2)), dq, dk, dv, lambda_init, tk)
        dl_lane = jnp.repeat(ret_decay_logit[l], 64, axis=-1).reshape(2, 2, 1, LANES)
        o_f, o_b = _retention(dl_lane, rq, rk, rv)
        ym = _mla(mq, mk, mv, tk)
        xs, h2, route = _outproj(xs, ya, yb, o_f, o_b, rg, row(ret_gn_g[l]), ym, wo, mod, B,
                                 row(norm2_g[l]), w_route, b_route)
        route2 = route.reshape(n_tok, LANES)
        ranks, counts = _ranks(route2)
        dest, block_e, n_used = _moe_plan(route2, ranks, counts, cap)
        xb = _dispatch(dest, h2.reshape(n_tok, D), cap)
        yb_e = _experts(block_e, n_used, xb, moe_w_gate, moe_w_up, moe_w_down, l)
        final = l == DEPTH - 1
        xs = _combine(dest, xs, route, mod, B, row(final_norm_g), yb_e, final)
    return xs
```

```python
import functools
import math

import jax
import jax.numpy as jnp
from jax import lax
from jax.experimental import pallas as pl
from jax.experimental.pallas import tpu as pltpu

F32 = jnp.float32
BF16 = jnp.bfloat16

D_MODEL = 1024
CTX_LEN = 256
GRID_W = 64
ROPE_BASE = 10000.0
NORM_EPS = 1e-6
NEG_INF = -1e30
DEPTH = 2

SWA_WINDOW = 128
RET_CHUNK = 128
MLA_SCALE = (64 + 32) ** -0.5
LOG2E = math.log2(math.e)
MOE_GROUPS = 4
MOE_PER_GROUP = 8
MOE_EXPERTS = 32
MOE_HIDDEN = 512

LANES = 128
TM = 256
TQ = 1024
FLASH_TK = 1024
MOE_ROWS = 256
VMEM_LIMIT = 56 * 1024 * 1024

C_SWA_Q, C_SWA_K, C_SWA_V = 0, 256, 384
C_DIF_Q, C_DIF_K, C_DIF_V = 512, 768, 1024
C_RET_Q, C_RET_K, C_RET_V, C_RET_G = 1280, 1536, 1792, 2048
C_MLA_Q, C_MLA_KV, C_MLA_KR = 2304, 2560, 2688
D_IN_PAD = 2816


def _cparams(sem):
    return pltpu.CompilerParams(dimension_semantics=sem, vmem_limit_bytes=VMEM_LIMIT)


def _dot(a, b):
    return jnp.dot(a, b, preferred_element_type=F32)


def _dot_nt(a, b):
    return lax.dot_general(a, b, (((1,), (1,)), ((), ())), preferred_element_type=F32)


def _dot_tn(a, b):
    return lax.dot_general(a, b, (((0,), (0,)), ((), ())), preferred_element_type=F32)


def _rms(x, g):
    ms = jnp.mean(x * x, axis=-1, keepdims=True)
    return x * lax.rsqrt(ms + NORM_EPS) * g


def _lane(shape):
    return lax.broadcasted_iota(jnp.int32, shape, len(shape) - 1)


def _rope(z, cos, sin, nf):
    first = (_lane(z.shape) % (2 * nf)) < nf
    partner = jnp.where(first, pltpu.roll(z, LANES - nf, 1), pltpu.roll(z, nf, 1))
    return z * cos + partner * sin


def _mod_kernel(a_ref, w_ref, b_ref, o_ref):
    a = a_ref[...]
    act = a * jax.nn.sigmoid(a)
    o_ref[...] = jnp.dot(act, w_ref[...], preferred_element_type=F32,
                         precision=lax.Precision.HIGHEST) + b_ref[...]


def _modulation(cvec, w_mod, b_mod):
    n = w_mod.shape[1]
    bn = 512
    return pl.pallas_call(
        _mod_kernel,
        grid=(n // bn,),
        in_specs=[pl.BlockSpec((8, D_MODEL), lambda j: (0, 0)),
                  pl.BlockSpec((D_MODEL, bn), lambda j: (0, j)),
                  pl.BlockSpec((1, bn), lambda j: (0, j))],
        out_specs=pl.BlockSpec((8, bn), lambda j: (0, j)),
        out_shape=jax.ShapeDtypeStruct((8, n), F32),
        compiler_params=_cparams(("parallel",)),
        name="modulation",
    )(cvec, w_mod, b_mod.reshape(1, n))


def _mod_spec(chunk, nbatch):
    return pl.BlockSpec((None, 1, D_MODEL),
                        lambda b, i: (jnp.where(i == 0, nbatch, b), 0, chunk))


def _inproj_kernel(x_ref, g_ref, sh_ref, sc_ref, w_ref, c64_ref, s64_ref, c32_ref, s32_ref,
                   qng_ref, wuq_ref, kvng_ref, wukk_ref, wukv_ref,
                   swaq, swak, swav, dq, dk, dv, rq, rk, rv, rg, mq, mk, mv):
    x = x_ref[0]
    h = _rms(x, g_ref[...]) * (1.0 + sc_ref[...]) + sh_ref[...]
    hb = h.astype(BF16)
    c64, s64, c32, s32 = c64_ref[...], s64_ref[...], c32_ref[...], s32_ref[...]

    def proj(c0, width):
        return _dot(hb, w_ref[:, c0:c0 + width])

    def rope_blocks(z, out, nf, scale):
        cos, sin = (c64, s64) if nf == 16 else (c32, s32)
        for j in range(z.shape[1] // LANES):
            zz = _rope(z[:, j * LANES:(j + 1) * LANES], cos, sin, nf)
            if scale != 1.0:
                zz = zz * scale
            out[0, :, j * LANES:(j + 1) * LANES] = zz.astype(out.dtype)

    rope_blocks(proj(C_SWA_Q, 256), swaq, 16, 0.125)
    rope_blocks(proj(C_SWA_K, 128), swak, 16, 1.0)
    swav[0] = proj(C_SWA_V, 128).astype(BF16)
    def values_with_ones(z, out):
        lane = _lane((z.shape[0], LANES))
        for hh in range(4):
            zb = z[:, (hh // 2) * LANES:(hh // 2 + 1) * LANES]
            keep = (lane < 64) if hh % 2 == 0 else (lane >= 64)
            out[0, :, hh * LANES:(hh + 1) * LANES] = jnp.where(keep, zb, 1.0).astype(BF16)

    rope_blocks(proj(C_DIF_Q, 256), dq, 8, 32 ** -0.5 * LOG2E)
    rope_blocks(proj(C_DIF_K, 256), dk, 8, 1.0)
    values_with_ones(proj(C_DIF_V, 256), dv)
    rope_blocks(proj(C_RET_Q, 256), rq, 16, 0.125)
    rope_blocks(proj(C_RET_K, 256), rk, 16, 1.0)
    rv[0] = proj(C_RET_V, 256).astype(BF16)
    rg[0] = proj(C_RET_G, 256).astype(BF16)

    ql = _rms(proj(C_MLA_Q, 256), qng_ref[...]).astype(BF16)
    qa = _dot(ql, wuq_ref[...]) * (MLA_SCALE * LOG2E)
    for hh in range(4):
        mq[0, :, hh * 256:hh * 256 + LANES] = qa[:, hh * 256:hh * 256 + LANES].astype(BF16)
        zr = _rope(qa[:, hh * 256 + LANES:(hh + 1) * 256], c32, s32, 8)
        mq[0, :, hh * 256 + LANES:(hh + 1) * 256] = zr.astype(BF16)
    kvl = _rms(proj(C_MLA_KV, 128), kvng_ref[...]).astype(BF16)
    kn = _dot(kvl, wukk_ref[...])
    kr = _rope(proj(C_MLA_KR, 128), c32, s32, 8).astype(BF16)
    for p in range(2):
        mk[0, :, p * 256:p * 256 + LANES] = kn[:, p * LANES:(p + 1) * LANES].astype(BF16)
        mk[0, :, p * 256 + LANES:(p + 1) * 256] = kr
    values_with_ones(_dot(kvl, wukv_ref[...]), mv)


def _inproj(x, mod, nbatch, norm_g, w_in_p, tabs, qng, wuq, kvng, wukk, wukv):
    B, S, D = x.shape
    nt = S // TM
    tok = lambda w: pl.BlockSpec((1, TM, w), lambda b, i: (b, i, 0))
    full = lambda a: pl.BlockSpec(a.shape, lambda b, i: (0,) * a.ndim)
    tab = pl.BlockSpec((TM, LANES), lambda b, i: (i, 0))
    widths = [256, 128, 128, 256, 256, 512, 256, 256, 256, 256, 1024, 512, 512]
    return pl.pallas_call(
        _inproj_kernel,
        grid=(B, nt),
        in_specs=[tok(D), full(norm_g), _mod_spec(0, nbatch), _mod_spec(1, nbatch), full(w_in_p),
                  tab, tab, tab, tab, full(qng), full(wuq), full(kvng), full(wukk), full(wukv)],
        out_specs=[tok(w) for w in widths],
        out_shape=[jax.ShapeDtypeStruct((B, S, w), BF16) for w in widths],
        compiler_params=_cparams(("parallel", "parallel")),
        name="inproj",
    )(x, norm_g, mod, mod, w_in_p, *tabs, qng, wuq, kvng, wukk, wukv)


def _swa_kernel(sink_ref, q_ref, kp_ref, ko_ref, kn_ref, vp_ref, vo_ref, vn_ref, kc_ref, vc_ref,
                o_ref, *, n_ctx_tiles, seq):
    i = pl.program_id(1)
    blk = RET_CHUNK
    n = i - n_ctx_tiles
    kband = jnp.concatenate([kp_ref[0], ko_ref[0], kn_ref[0]], axis=0)
    vband = jnp.concatenate([vp_ref[0], vo_ref[0], vn_ref[0]], axis=0)
    kc, vc = kc_ref[0], vc_ref[0]
    r_io = lax.broadcasted_iota(jnp.int32, (blk, 3 * blk), 0)
    c_io = lax.broadcasted_iota(jnp.int32, (blk, 3 * blk), 1)
    kpos = (n - 1) * blk + c_io
    valid = (c_io >= r_io) & (c_io <= r_io + 2 * SWA_WINDOW) & (kpos >= 0) & (kpos < seq) & (n >= 0)
    lane = _lane((blk, LANES))
    for r in range(2):
        qb = q_ref[0, :, r * LANES:(r + 1) * LANES]
        outs = []
        for g in range(2):
            gmask = (lane >= 64 * g) & (lane < 64 * (g + 1))
            qm = jnp.where(gmask, qb, jnp.zeros_like(qb))
            sink = sink_ref[2 * g + r]
            s_band = jnp.where(valid, _dot_nt(qm, kband), NEG_INF)
            s_ctx = _dot_nt(qm, kc)
            m = jnp.maximum(jnp.maximum(jnp.max(s_band, axis=-1, keepdims=True),
                                        jnp.max(s_ctx, axis=-1, keepdims=True)), sink)
            p_band = jnp.exp(s_band - m)
            p_ctx = jnp.exp(s_ctx - m)
            l = (jnp.sum(p_band, axis=-1, keepdims=True) + jnp.sum(p_ctx, axis=-1, keepdims=True)
                 + jnp.exp(sink - m))
            o = _dot(p_band.astype(BF16), vband) + _dot(p_ctx.astype(BF16), vc)
            outs.append(o / l)
        o_ref[0, :, r * LANES:(r + 1) * LANES] = jnp.where(lane < 64, outs[0], outs[1]).astype(BF16)


def _swa(sink, q, k, v):
    B, S, _ = q.shape
    blk = RET_CHUNK
    nt = S // blk
    nct = CTX_LEN // blk
    kvs = lambda f: pl.BlockSpec((1, blk, LANES), f)
    prev = lambda b, i: (b, jnp.maximum(i - 1, 0), 0)
    own = lambda b, i: (b, i, 0)
    nxt = lambda b, i: (b, jnp.minimum(i + 1, nt - 1), 0)
    ctx = pl.BlockSpec((1, CTX_LEN, LANES), lambda b, i: (b, 0, 0))
    return pl.pallas_call(
        functools.partial(_swa_kernel, n_ctx_tiles=nct, seq=S - CTX_LEN),
        grid=(B, nt),
        in_specs=[pl.BlockSpec(memory_space=pltpu.SMEM),
                  pl.BlockSpec((1, blk, 256), own),
                  kvs(prev), kvs(own), kvs(nxt), kvs(prev), kvs(own), kvs(nxt), ctx, ctx],
        out_specs=pl.BlockSpec((1, blk, 256), own),
        out_shape=jax.ShapeDtypeStruct((B, S, 256), BF16),
        compiler_params=_cparams(("parallel", "parallel")),
        name="swa",
    )(sink, q, k, k, k, v, v, v, k, v)


def _flash(q_maps, v_blk, k_ref, v_ref, m_scr, acc_scr, *, n_lat, tk):
    nm = len(q_maps)
    for a in range(nm):
        m_scr[a] = jnp.full(m_scr.shape[1:], NEG_INF, F32)
        acc_scr[a] = jnp.zeros(acc_scr.shape[1:], F32)

    def chunk(start, size):
        kc = k_ref[0, pl.ds(start, size), :]
        nt = size // LANES
        for a in range(nm):
            vc = v_ref[0, pl.ds(start, size), v_blk[a] * LANES:(v_blk[a] + 1) * LANES]
            s = _dot_nt(q_maps[a], kc)
            mt = s[:, 0:LANES]
            for t in range(1, nt):
                mt = jnp.maximum(mt, s[:, t * LANES:(t + 1) * LANES])
            m_prev = m_scr[a]
            m_new = jnp.maximum(m_prev, jnp.max(mt, axis=-1, keepdims=True))
            p = jnp.concatenate(
                [jnp.exp2(s[:, t * LANES:(t + 1) * LANES] - m_new).astype(BF16) for t in range(nt)],
                axis=1)
            acc_scr[a] = jnp.exp2(m_prev - m_new) * acc_scr[a] + _dot(p, vc)
            m_scr[a] = m_new

    chunk(0, CTX_LEN)
    if n_lat:

        def body(c, carry):
            chunk(pl.multiple_of(CTX_LEN + c * tk, LANES), tk)
            return carry

        lax.fori_loop(0, n_lat, body, 0)
    outs = []
    for a in range(nm):
        acc = acc_scr[a]
        outs.append(acc / pltpu.roll(acc, 64, 1))
    return outs


def _attention_calls(kern, name, nm, q, k, v, qw, kw, tk, extra_in=(), extra_specs=()):
    B, S, _ = q.shape
    T = S - CTX_LEN

    def call(tq, n_tiles, n_lat, q_spec, kv_rows, call_name):
        return pl.pallas_call(
            functools.partial(kern, n_lat=n_lat, tk=tk),
            grid=(B, 2, n_tiles),
            in_specs=list(extra_specs) + [
                q_spec,
                pl.BlockSpec((1, kv_rows, kw), lambda b, p, i: (b, 0, p)),
                pl.BlockSpec((1, kv_rows, 256), lambda b, p, i: (b, 0, p))],
            out_specs=pl.BlockSpec((1, tq, LANES), lambda b, p, i: (b, i, p)),
            out_shape=jax.ShapeDtypeStruct((B, tq * n_tiles, 256), BF16),
            scratch_shapes=[pltpu.VMEM((nm, tq, LANES), F32), pltpu.VMEM((nm, tq, LANES), F32)],
            compiler_params=_cparams(("parallel", "parallel", "parallel")),
            name=call_name,
        )(*extra_in, q, k, v)

    y_ctx = call(CTX_LEN, 1, 0, pl.BlockSpec((1, CTX_LEN, qw), lambda b, p, i: (b, 0, p)),
                 CTX_LEN, name + "_ctx")
    tq = min(TQ, T)
    y_lat = call(tq, T // tq, T // tk,
                 pl.BlockSpec((pl.Element(1), pl.Element(tq), pl.Element(qw)),
                              lambda b, p, i: (b, (i * (tq // CTX_LEN) + 1) * CTX_LEN, p * qw)),
                 S, name)
    return jnp.concatenate([y_ctx, y_lat], axis=1)


def _diff_attn(lam_ref, g_ref, q_ref, k_ref, v_ref, o_ref, m_scr, acc_scr, *,
               n_lat, tk, lambda_init):
    qb = q_ref[0]
    lane = _lane(qb.shape)
    q_maps = []
    for a in range(4):
        msk = (lane >= 32 * a) & (lane < 32 * (a + 1))
        q_maps.append(jnp.where(msk, qb, jnp.zeros_like(qb)))
    n = _flash(q_maps, (0, 0, 1, 1), k_ref, v_ref, m_scr, acc_scr, n_lat=n_lat, tk=tk)
    lp = lam_ref[...]
    lam = (jnp.exp(jnp.sum(lp[0:1] * lp[1:2], axis=-1, keepdims=True))
           - jnp.exp(jnp.sum(lp[2:3] * lp[3:4], axis=-1, keepdims=True)) + lambda_init)
    lane_o = _lane(n[0].shape)
    o = jnp.where(lane_o < 64, n[0] - lam * n[1], n[2] - lam * n[3])
    sq = o * o
    ms0 = jnp.sum(jnp.where(lane_o < 64, sq, 0.0), axis=-1, keepdims=True) * (1.0 / 64)
    ms1 = jnp.sum(jnp.where(lane_o >= 64, sq, 0.0), axis=-1, keepdims=True) * (1.0 / 64)
    ms = jnp.where(lane_o < 64, ms0, ms1)
    y = o * lax.rsqrt(ms + NORM_EPS) * g_ref[...] * (1.0 - lambda_init)
    o_ref[0] = y.astype(BF16)


def _diff(lam_p, subln_g2, q, k, v, lambda_init, tk):
    return _attention_calls(
        functools.partial(_diff_attn, lambda_init=lambda_init), "diff_attn", 4, q, k, v, LANES, LANES,
        tk, extra_in=(lam_p, subln_g2),
        extra_specs=(pl.BlockSpec(lam_p.shape, lambda b, p, i: (0, 0)),
                     pl.BlockSpec((1, LANES), lambda b, p, i: (0, 0))))


def _mla_attn(q_ref, k_ref, v_ref, o_ref, m_scr, acc_scr, *, n_lat, tk):
    q_maps = [q_ref[0, :, 0:256], q_ref[0, :, 256:512]]
    n = _flash(q_maps, (0, 1), k_ref, v_ref, m_scr, acc_scr, n_lat=n_lat, tk=tk)
    lane_o = _lane(n[0].shape)
    o_ref[0] = jnp.where(lane_o < 64, n[0], n[1]).astype(BF16)


def _mla(q, k, v, tk):
    return _attention_calls(_mla_attn, "mla_attn", 2, q, k, v, 512, 256, tk)


def _ret_tables(dl, backward):
    L = RET_CHUNK
    lg = -(jnp.maximum(-dl, 0.0) + jnp.log(1.0 + jnp.exp(-jnp.abs(dl))))
    row = lax.broadcasted_iota(jnp.int32, (L, LANES), 0).astype(F32)
    col = lax.broadcasted_iota(jnp.int32, (L, LANES), 1)
    colf = col.astype(F32)
    if backward:
        qd, kd, rel = jnp.exp((L - row) * lg), jnp.exp(row * lg), colf - row
    else:
        qd, kd, rel = jnp.exp((row + 1.0) * lg), jnp.exp((L - 1.0 - row) * lg), row - colf
    decs = []
    for hh in range(2):
        lg_h = jnp.sum(jnp.where(col[0:1] == 64 * hh, lg, 0.0), axis=-1, keepdims=True)
        decs.append(jnp.where(rel >= 0, jnp.exp(jnp.maximum(rel, 0.0) * lg_h), 0.0))
    return qd, kd, decs[0], decs[1], jnp.exp(float(L) * lg)


def _ret_chunk(q, k, v, tab, cd, s_ref):
    col = _lane((RET_CHUNK, LANES))
    qf, kf = q.astype(F32), k.astype(F32)
    o = _dot((qf * tab[0]).astype(BF16), s_ref[...].astype(BF16))
    outs = []
    for hh in range(2):
        hmask = (col >= 64 * hh) & (col < 64 * (hh + 1))
        a = _dot_nt(jnp.where(hmask, q, jnp.zeros_like(q)), k) * tab[2 + hh]
        outs.append(_dot(a.astype(BF16), v))
    o = o + jnp.where(col < 64, outs[0], outs[1])
    rowi = lax.broadcasted_iota(jnp.int32, (LANES, LANES), 0)
    coli = lax.broadcasted_iota(jnp.int32, (LANES, LANES), 1)
    s_new = s_ref[...] * cd + _dot_tn((kf * tab[1]).astype(BF16), v)
    s_ref[...] = jnp.where((rowi < 64) == (coli < 64), s_new, 0.0)
    return o


def _ret_kernel(dl_ref, qf_ref, kf_ref, vf_ref, qb_ref, kb_ref, vb_ref, of_ref, ob_ref,
                s_scr, tab_scr, cd_scr):
    nb = qf_ref.shape[0]

    @pl.when(pl.program_id(0) == 0)
    def _():
        s_scr[...] = jnp.zeros(s_scr.shape, F32)
        for d in range(2):
            for p in range(2):
                tabs = _ret_tables(dl_ref[d, p], backward=(d == 1))
                for t in range(4):
                    tab_scr[d, p, t] = tabs[t]
                cd_scr[d, p] = tabs[4]

    for d, (q_ref, k_ref, v_ref, o_ref) in enumerate(((qf_ref, kf_ref, vf_ref, of_ref),
                                                      (qb_ref, kb_ref, vb_ref, ob_ref))):
        for p in range(2):
            tab = [tab_scr[d, p, t] for t in range(4)]
            cd = cd_scr[d, p]
            ln = slice(p * LANES, (p + 1) * LANES)
            for b in range(nb):
                o_ref[b, :, ln] = _ret_chunk(q_ref[b, :, ln], k_ref[b, :, ln], v_ref[b, :, ln],
                                             tab, cd, s_scr.at[d, b, p])


def _retention(dl_lane, q, k, v):
    B, S, _ = q.shape
    L = RET_CHUNK
    nc = S // L
    ncc = CTX_LEN // L
    fwd = pl.BlockSpec((B, L, 256), lambda s: (0, s, 0))
    bwd = pl.BlockSpec((B, L, 256), lambda s: (0, jnp.where(s < ncc, ncc - 1 - s, nc - 1 + ncc - s), 0))
    return pl.pallas_call(
        _ret_kernel,
        grid=(nc,),
        in_specs=[pl.BlockSpec(dl_lane.shape, lambda s: (0, 0, 0, 0)), fwd, fwd, fwd, bwd, bwd, bwd],
        out_specs=[fwd, bwd],
        out_shape=[jax.ShapeDtypeStruct((B, S, 256), F32)] * 2,
        scratch_shapes=[pltpu.VMEM((2, B, 2, LANES, LANES), F32),
                        pltpu.VMEM((2, 2, 4, L, LANES), F32), pltpu.VMEM((2, 2, 1, LANES), F32)],
        compiler_params=_cparams(("arbitrary",)),
        name="retention",
    )(dl_lane, q, k, v, q, k, v)


def _gated_group_norm(o, g, gn):
    lo = _lane(o.shape) < 64

    def halves(t):
        a = jnp.sum(jnp.where(lo, t, 0.0), axis=-1, keepdims=True) * (1.0 / 64)
        b = jnp.sum(jnp.where(lo, 0.0, t), axis=-1, keepdims=True) * (1.0 / 64)
        return jnp.where(lo, a, b)

    d = o - halves(o)
    var = halves(d * d)
    return d * lax.rsqrt(var + NORM_EPS) * gn * (g * jax.nn.sigmoid(g))


def _outproj_kernel(x_ref, ya_ref, yb_ref, of_ref, ob_ref, rg_ref, gn_ref, ym_ref, w_ref, g1_ref,
                    sh_ref, sc_ref, ng_ref, wr_ref, br_ref, xo_ref, h_ref, rt_ref):
    mix = (_dot(ya_ref[0], w_ref[0:256]) + _dot(yb_ref[0], w_ref[256:512])
           + _dot(ym_ref[0], w_ref[768:1024]))
    for j in range(2):
        ln = slice(j * LANES, (j + 1) * LANES)
        yr = _gated_group_norm(of_ref[0, :, ln] + ob_ref[0, :, ln], rg_ref[0, :, ln].astype(F32),
                               gn_ref[:, ln])
        mix = mix + _dot(yr.astype(BF16), w_ref[512 + j * LANES:512 + (j + 1) * LANES])
    x = x_ref[0] + g1_ref[...] * mix
    xo_ref[0] = x
    h = _rms(x, ng_ref[...]) * (1.0 + sc_ref[...]) + sh_ref[...]
    h_ref[0] = h
    logits = jnp.dot(h, wr_ref[...], preferred_element_type=F32,
                     precision=lax.Precision.HIGHEST) + br_ref[...]
    lane = _lane(logits.shape).astype(F32)

    def first_argmax(vals, mask):
        mx = jnp.max(jnp.where(mask, vals, NEG_INF), axis=-1, keepdims=True)
        idx = jnp.min(jnp.where(mask & (vals == mx), lane, float(LANES)), axis=-1, keepdims=True)
        return mx, idx

    gmask = lane < MOE_GROUPS
    g_max, g_idx = first_argmax(logits, gmask)
    g_w = 1.0 / jnp.sum(jnp.exp(jnp.where(gmask, logits - g_max, NEG_INF)), axis=-1, keepdims=True)
    e_lo = MOE_GROUPS + g_idx * MOE_PER_GROUP
    emask = (lane >= e_lo) & (lane < e_lo + MOE_PER_GROUP)
    v1, i1 = first_argmax(logits, emask)
    v2, i2 = first_argmax(logits, emask & (lane != i1))
    t = jnp.exp(v2 - v1)
    w1 = g_w / (1.0 + t)
    w2 = w1 * t
    rt = jnp.where(lane == 0, i1 - MOE_GROUPS,
                   jnp.where(lane == 1, i2 - MOE_GROUPS,
                             jnp.where(lane == 2, w1, jnp.where(lane == 3, w2, 0.0))))
    rt_ref[0] = rt


def _outproj(x, ya, yb, o_f, o_b, rg, gn_g, ym, w_out_p, mod, nbatch, norm_g, w_route, b_route):
    B, S, D = x.shape
    nt = S // TM
    tok = lambda w: pl.BlockSpec((1, TM, w), lambda b, i: (b, i, 0))
    full = lambda a: pl.BlockSpec(a.shape, lambda b, i: (0,) * a.ndim)
    return pl.pallas_call(
        _outproj_kernel,
        grid=(B, nt),
        in_specs=[tok(D), tok(256), tok(256), tok(256), tok(256), tok(256), full(gn_g), tok(256),
                  full(w_out_p),
                  _mod_spec(2, nbatch), _mod_spec(3, nbatch), _mod_spec(4, nbatch), full(norm_g),
                  full(w_route), full(b_route)],
        out_specs=[tok(D), tok(D), tok(LANES)],
        out_shape=[jax.ShapeDtypeStruct((B, S, D), F32), jax.ShapeDtypeStruct((B, S, D), F32),
                   jax.ShapeDtypeStruct((B, S, LANES), F32)],
        compiler_params=_cparams(("parallel", "parallel")),
        name="outproj_router",
    )(x, ya, yb, o_f, o_b, rg, gn_g, ym, w_out_p, mod, mod, mod, norm_g, w_route, b_route)


def _rank_kernel(rt_ref, rank_ref, cnt_ref, carry):
    @pl.when(pl.program_id(0) == 0)
    def _():
        carry[...] = jnp.zeros(carry.shape, F32)

    rt = rt_ref[...]
    lane = _lane(rt.shape)
    lanef = lane.astype(F32)
    oh0 = jnp.where(lanef == rt[:, 0:1], 1.0, 0.0)
    oh1 = jnp.where(lanef == rt[:, 1:2], 1.0, 0.0)
    cnt = oh0 + oh1
    r_io = lax.broadcasted_iota(jnp.int32, (TM, TM), 0)
    c_io = lax.broadcasted_iota(jnp.int32, (TM, TM), 1)
    tri = jnp.where(r_io > c_io, 1.0, 0.0).astype(BF16)
    before = _dot(tri, cnt.astype(BF16)) + carry[...]
    r0 = jnp.sum(oh0 * before, axis=-1, keepdims=True)
    r1 = jnp.sum(oh1 * before, axis=-1, keepdims=True)
    rank_ref[...] = jnp.where(lane == 0, r0, jnp.where(lane == 1, r1, 0.0))
    carry[...] = carry[...] + jnp.sum(cnt, axis=0, keepdims=True)
    cnt_ref[...] = carry[...]


def _ranks(route):
    n = route.shape[0]
    return pl.pallas_call(
        _rank_kernel,
        grid=(n // TM,),
        in_specs=[pl.BlockSpec((TM, LANES), lambda i: (i, 0))],
        out_specs=[pl.BlockSpec((TM, LANES), lambda i: (i, 0)),
                   pl.BlockSpec((1, LANES), lambda i: (0, 0))],
        out_shape=[jax.ShapeDtypeStruct((n, LANES), F32), jax.ShapeDtypeStruct((1, LANES), F32)],
        scratch_shapes=[pltpu.VMEM((1, LANES), F32)],
        compiler_params=_cparams(("arbitrary",)),
        name="moe_rank",
    )(route)


def _dispatch_kernel(dest_ref, h_ref, xb_in_ref, xb_ref, sem):
    del xb_in_ref
    base = pl.program_id(0) * TM

    def copy(r, k):
        d = dest_ref[(base + r) * 2 + k]
        return pltpu.make_async_copy(h_ref.at[pl.ds(r, 1)], xb_ref.at[pl.ds(d, 1)], sem)

    def issue(r, c):
        copy(r, 0).start()
        copy(r, 1).start()
        return c

    def drain(r, c):
        copy(r, 0).wait()
        copy(r, 1).wait()
        return c

    lax.fori_loop(0, TM, issue, 0)
    lax.fori_loop(0, TM, drain, 0)


def _dispatch(dest, h, cap):
    n, d = h.shape
    return pl.pallas_call(
        _dispatch_kernel,
        grid_spec=pltpu.PrefetchScalarGridSpec(
            num_scalar_prefetch=1,
            grid=(n // TM,),
            in_specs=[pl.BlockSpec((TM, d), lambda i, dest: (i, 0)),
                      pl.BlockSpec(memory_space=pl.ANY)],
            out_specs=pl.BlockSpec(memory_space=pl.ANY),
            scratch_shapes=[pltpu.SemaphoreType.DMA(())]),
        out_shape=jax.ShapeDtypeStruct((cap, d), h.dtype),
        input_output_aliases={2: 0},
        compiler_params=_cparams(("arbitrary",)),
        name="moe_dispatch",
    )(dest, h, jnp.zeros((cap, d), h.dtype))


def _expert_kernel(be_ref, nu_ref, x_ref, wg_ref, wu_ref, wd_ref, y_ref, wg_s, wu_s, wd_s):
    i = pl.program_id(0)

    @pl.when((i == 0) | (be_ref[i] != be_ref[jnp.maximum(i - 1, 0)]))
    def _():
        wg_s[...] = wg_ref[0].astype(BF16)
        wu_s[...] = wu_ref[0].astype(BF16)
        wd_s[...] = wd_ref[0].astype(BF16)

    @pl.when(i < nu_ref[0])
    def _():
        xb = x_ref[...].astype(BF16)
        gate = _dot(xb, wg_s[...])
        up = _dot(xb, wu_s[...])
        hid = (gate * jax.nn.sigmoid(gate) * up).astype(BF16)
        y_ref[...] = _dot(hid, wd_s[...])

    @pl.when(i >= nu_ref[0])
    def _():
        y_ref[...] = jnp.zeros(y_ref.shape, y_ref.dtype)


def _experts(block_e, n_used, xb, w_gate, w_up, w_down, layer):
    cap, d = xb.shape
    hdim = w_gate.shape[-1]
    return pl.pallas_call(
        _expert_kernel,
        grid_spec=pltpu.PrefetchScalarGridSpec(
            num_scalar_prefetch=2,
            grid=(cap // MOE_ROWS,),
            in_specs=[pl.BlockSpec((MOE_ROWS, d), lambda i, be, nu: (i, 0)),
                      pl.BlockSpec((None, 1, d, hdim), lambda i, be, nu: (layer, be[i], 0, 0)),
                      pl.BlockSpec((None, 1, d, hdim), lambda i, be, nu: (layer, be[i], 0, 0)),
                      pl.BlockSpec((None, 1, hdim, d), lambda i, be, nu: (layer, be[i], 0, 0))],
            out_specs=pl.BlockSpec((MOE_ROWS, d), lambda i, be, nu: (i, 0)),
            scratch_shapes=[pltpu.VMEM((d, hdim), BF16), pltpu.VMEM((d, hdim), BF16),
                            pltpu.VMEM((hdim, d), BF16)]),
        out_shape=jax.ShapeDtypeStruct((cap, d), F32),
        compiler_params=_cparams(("arbitrary",)),
        name="moe_experts",
    )(block_e, n_used, xb, w_gate, w_up, w_down)


def _combine_kernel(dest_ref, x_ref, rt_ref, g2_ref, fg_ref, yb_ref, o_ref, gath, sem, *,
                    tile_off, tiles_per_batch, final):
    b, i = pl.program_id(0), pl.program_id(1)
    base = (b * tiles_per_batch + i + tile_off) * TM

    def copy(r, k):
        d = dest_ref[(base + r) * 2 + k]
        return pltpu.make_async_copy(yb_ref.at[pl.ds(d, 1)], gath.at[k, pl.ds(r, 1)], sem)

    def issue(r, c):
        copy(r, 0).start()
        copy(r, 1).start()
        return c

    def drain(r, c):
        copy(r, 0).wait()
        copy(r, 1).wait()
        return c

    lax.fori_loop(0, TM, issue, 0)
    lax.fori_loop(0, TM, drain, 0)
    rt = rt_ref[0]
    m = rt[:, 2:3] * gath[0] + rt[:, 3:4] * gath[1]
    x = x_ref[0] + g2_ref[...] * m
    if final:
        x = _rms(x, fg_ref[...])
    o_ref[0] = x


def _combine(dest, x, route, mod, nbatch, final_g, yb, final):
    B, S, D = x.shape
    nt = S // TM
    off = 1 if final else 0
    g2_spec = pl.BlockSpec((None, 1, D_MODEL),
                           lambda b, i, dest: (jnp.where(i + off == 0, nbatch, b), 0, 5))
    return pl.pallas_call(
        functools.partial(_combine_kernel, tile_off=off, tiles_per_batch=nt, final=final),
        grid_spec=pltpu.PrefetchScalarGridSpec(
            num_scalar_prefetch=1,
            grid=(B, nt - off),
            in_specs=[pl.BlockSpec((1, TM, D), lambda b, i, dest: (b, i + off, 0)),
                      pl.BlockSpec((1, TM, LANES), lambda b, i, dest: (b, i + off, 0)),
                      g2_spec,
                      pl.BlockSpec((1, D), lambda b, i, dest: (0, 0)),
                      pl.BlockSpec(memory_space=pl.ANY)],
            out_specs=pl.BlockSpec((1, TM, D), lambda b, i, dest: (b, i, 0)),
            scratch_shapes=[pltpu.VMEM((2, TM, D), F32), pltpu.SemaphoreType.DMA(())]),
        out_shape=jax.ShapeDtypeStruct((B, S - off * TM, D), F32),
        compiler_params=_cparams(("arbitrary", "arbitrary")),
        name="moe_combine",
    )(dest, x, route, mod, final_g, yb)


def _rope_tables(seq):
    t = jnp.arange(seq, dtype=jnp.int32)
    row, col = t // GRID_W, t % GRID_W
    lane = jnp.arange(LANES, dtype=jnp.int32)
    tabs = []
    for r in (64, 32):
        nf = r // 4
        inv = 1.0 / (ROPE_BASE ** (jnp.arange(nf, dtype=F32) / nf))
        pos = jnp.where(((lane % r) < 2 * nf)[None, :], row[:, None], col[:, None]).astype(F32)
        ang = pos * inv[lane % nf][None, :]
        first = ((lane % (2 * nf)) < nf)[None, :]
        cos = jnp.cos(ang)
        sin = jnp.where(first, -jnp.sin(ang), jnp.sin(ang))
        tabs.append(jnp.concatenate([jnp.ones((CTX_LEN, LANES), F32), cos], axis=0))
        tabs.append(jnp.concatenate([jnp.zeros((CTX_LEN, LANES), F32), sin], axis=0))
    return tabs


def _layer_weights(w_in, w_out, w_uq, w_ukv, w_group, b_group, w_expert, b_expert):
    perm = jnp.array([0, 2, 1, 3])
    wq = w_in[:, :256].reshape(D_MODEL, 4, 64)[:, perm].reshape(D_MODEL, 256)
    w_in_p = jnp.concatenate([wq, w_in[:, 256:], jnp.zeros((D_MODEL, D_IN_PAD - w_in.shape[1]), F32)],
                             axis=1).astype(BF16)
    wo = jnp.concatenate([w_out[:256].reshape(4, 64, D_MODEL)[perm].reshape(256, D_MODEL), w_out[256:]],
                         axis=0).astype(BF16)
    uq = w_uq.reshape(256, 4, 96)
    blocks = []
    for h in range(4):
        blk = jnp.zeros((256, 256), F32)
        blk = blk.at[:, 64 * (h % 2):64 * (h % 2) + 64].set(uq[:, h, :64])
        blk = blk.at[:, LANES:LANES + 32].set(uq[:, h, 64:])
        blocks.append(blk)
    wuq = jnp.concatenate(blocks, axis=1).astype(BF16)
    ukv = w_ukv.reshape(128, 4, 128)
    wukk = ukv[:, :, :64].reshape(128, 256).astype(BF16)
    wukv = ukv[:, :, 64:].reshape(128, 256).astype(BF16)
    pad = LANES - MOE_GROUPS - MOE_EXPERTS
    w_route = jnp.concatenate([w_group, w_expert, jnp.zeros((D_MODEL, pad), F32)], axis=1)
    b_route = jnp.concatenate([b_group, b_expert, jnp.zeros((pad,), F32)]).reshape(1, LANES)
    return w_in_p, wo, wuq, wukk, wukv, w_route, b_route


def _moe_plan(route, ranks, counts, cap):
    e = route[:, :2].astype(jnp.int32)
    rank = ranks[:, :2].astype(jnp.int32)
    cnt = counts[0, :MOE_EXPERTS].astype(jnp.int32)
    padded = (cnt + MOE_ROWS - 1) // MOE_ROWS * MOE_ROWS
    pend = jnp.cumsum(padded)
    pstart = pend - padded
    dest = (pstart[e] + rank).reshape(-1)
    nblk = cap // MOE_ROWS
    blk_start = jnp.arange(nblk, dtype=jnp.int32) * MOE_ROWS
    block_e = jnp.minimum(jnp.sum((pend[None, :] <= blk_start[:, None]).astype(jnp.int32), axis=1),
                          MOE_EXPERTS - 1)
    n_used = (pend[-1:] // MOE_ROWS).astype(jnp.int32)
    return dest, block_e, n_used


def kernel(x, c, ctx, c_ctx, w_mod, b_mod, norm1_g, norm2_g, w_in, w_out, swa_sink, diff_lambda, diff_subln_g, ret_decay_logit, ret_gn_g, mla_q_norm_g, mla_w_uq, mla_kv_norm_g, mla_w_ukv, moe_w_group, moe_b_group, moe_w_expert, moe_b_expert, moe_w_gate, moe_w_up, moe_w_down, final_norm_g):
    B, T, D = x.shape
    S = CTX_LEN + T
    n_tok = B * S
    tk = min(FLASH_TK, T)
    cap = -(-(2 * n_tok + MOE_EXPERTS * (MOE_ROWS - 1)) // MOE_ROWS) * MOE_ROWS
    xs = jnp.concatenate([ctx, x], axis=1)
    cvec = jnp.zeros((8, D), F32).at[:B].set(c).at[B].set(c_ctx)
    tabs = _rope_tables(T)
    row = lambda v: v.reshape(1, -1)

    for l in range(DEPTH):
        lambda_init = 0.8 - 0.6 * math.exp(-0.3 * l)
        w_in_p, wo, wuq, wukk, wukv, w_route, b_route = _layer_weights(
            w_in[l], w_out[l], mla_w_uq[l], mla_w_ukv[l], moe_w_group[l], moe_b_group[l],
            moe_w_expert[l], moe_b_expert[l])
        mod = _modulation(cvec, w_mod[l], b_mod[l]).reshape(8, 1, 6 * D)
        (swaq, swak, swav, dq, dk, dv, rq, rk, rv, rg, mq, mk, mv) = _inproj(
            xs, mod, B, row(norm1_g[l]), w_in_p, tabs, row(mla_q_norm_g[l]), wuq,
            row(mla_kv_norm_g[l]), wukk, wukv)
        sink_p = swa_sink[l]
        ya = _swa(sink_p, swaq, swak, swav)
        yb = _diff(diff_lambda[l], row(jnp.tile(diff_subln_g[l], 2)), dq, dk, dv, lambda_init, tk)
        dl_lane = jnp.repeat(ret_decay_logit[l], 64, axis=-1).reshape(2, 2, 1, LANES)
        o_f, o_b = _retention(dl_lane, rq, rk, rv)
        ym = _mla(mq, mk, mv, tk)
        xs, h2, route = _outproj(xs, ya, yb, o_f, o_b, rg, row(ret_gn_g[l]), ym, wo, mod, B,
                                 row(norm2_g[l]), w_route, b_route)
        route2 = route.reshape(n_tok, LANES)
        ranks, counts = _ranks(route2)
        dest, block_e, n_used = _moe_plan(route2, ranks, counts, cap)
        xb = _dispatch(dest, h2.reshape(n_tok, D), cap)
        yb_e = _experts(block_e, n_used, xb, moe_w_gate, moe_w_up, moe_w_down, l)
        final = l == DEPTH - 1
        xs = _combine(dest, xs, route, mod, B, row(final_norm_g), yb_e, final)
    return xs
```

```python
import functools
import math

import jax
import jax.numpy as jnp
from jax import lax
from jax.experimental import pallas as pl
from jax.experimental.pallas import tpu as pltpu

F32 = jnp.float32
BF16 = jnp.bfloat16

D_MODEL = 1024
CTX_LEN = 256
GRID_W = 64
ROPE_BASE = 10000.0
NORM_EPS = 1e-6
NEG_INF = -1e30
DEPTH = 2

SWA_WINDOW = 128
RET_CHUNK = 128
MLA_SCALE = (64 + 32) ** -0.5
LOG2E = math.log2(math.e)
MOE_GROUPS = 4
MOE_PER_GROUP = 8
MOE_EXPERTS = 32
MOE_HIDDEN = 512

LANES = 128
TM = 256
TQ = 2048
FLASH_TK = 1024
MOE_ROWS = 256
DMA_UNROLL = 8
VMEM_LIMIT = 56 * 1024 * 1024

C_SWA_Q, C_SWA_K, C_SWA_V = 0, 256, 384
C_DIF_Q, C_DIF_K, C_DIF_V = 512, 768, 1024
C_RET_Q, C_RET_K, C_RET_V, C_RET_G = 1280, 1536, 1792, 2048
C_MLA_Q, C_MLA_KV, C_MLA_KR = 2304, 2560, 2688
D_IN_PAD = 2816


def _cparams(sem):
    return pltpu.CompilerParams(dimension_semantics=sem, vmem_limit_bytes=VMEM_LIMIT)


def _dot(a, b):
    return jnp.dot(a, b, preferred_element_type=F32)


def _dot_nt(a, b):
    return lax.dot_general(a, b, (((1,), (1,)), ((), ())), preferred_element_type=F32)


def _dot_tn(a, b):
    return lax.dot_general(a, b, (((0,), (0,)), ((), ())), preferred_element_type=F32)


def _rms(x, g):
    ms = jnp.mean(x * x, axis=-1, keepdims=True)
    return x * lax.rsqrt(ms + NORM_EPS) * g


def _lane(shape):
    return lax.broadcasted_iota(jnp.int32, shape, len(shape) - 1)


def _rope(z, cos, sin, nf):
    first = (_lane(z.shape) % (2 * nf)) < nf
    partner = jnp.where(first, pltpu.roll(z, LANES - nf, 1), pltpu.roll(z, nf, 1))
    return z * cos + partner * sin


def _mod_kernel(a_ref, w_ref, b_ref, o_ref):
    a = a_ref[...]
    act = a * jax.nn.sigmoid(a)
    o_ref[...] = jnp.dot(act, w_ref[...], preferred_element_type=F32,
                         precision=lax.Precision.HIGHEST) + b_ref[...]


def _modulation(cvec, w_mod, b_mod):
    n = w_mod.shape[1]
    bn = 512
    return pl.pallas_call(
        _mod_kernel,
        grid=(n // bn,),
        in_specs=[pl.BlockSpec((8, D_MODEL), lambda j: (0, 0)),
                  pl.BlockSpec((D_MODEL, bn), lambda j: (0, j)),
                  pl.BlockSpec((1, bn), lambda j: (0, j))],
        out_specs=pl.BlockSpec((8, bn), lambda j: (0, j)),
        out_shape=jax.ShapeDtypeStruct((8, n), F32),
        compiler_params=_cparams(("parallel",)),
        name="modulation",
    )(cvec, w_mod, b_mod.reshape(1, n))


def _mod_spec(chunk, nbatch):
    return pl.BlockSpec((None, 1, D_MODEL),
                        lambda b, i: (jnp.where(i == 0, nbatch, b), 0, chunk))


def _inproj_kernel(x_ref, g_ref, sh_ref, sc_ref, w_ref, c64_ref, s64_ref, c32_ref, s32_ref,
                   qng_ref, wuq_ref, kvng_ref, wukk_ref, wukv_ref,
                   swaq, swak, swav, dq, dk, dv, rq, rk, rv, rg, mq, mk, mv):
    x = x_ref[0]
    h = _rms(x, g_ref[...]) * (1.0 + sc_ref[...]) + sh_ref[...]
    hb = h.astype(BF16)
    c64, s64, c32, s32 = c64_ref[...], s64_ref[...], c32_ref[...], s32_ref[...]

    def proj(c0, width):
        return _dot(hb, w_ref[:, c0:c0 + width])

    def rope_blocks(z, out, nf, scale):
        cos, sin = (c64, s64) if nf == 16 else (c32, s32)
        for j in range(z.shape[1] // LANES):
            zz = _rope(z[:, j * LANES:(j + 1) * LANES], cos, sin, nf)
            if scale != 1.0:
                zz = zz * scale
            out[0, :, j * LANES:(j + 1) * LANES] = zz.astype(out.dtype)

    rope_blocks(proj(C_SWA_Q, 256), swaq, 16, 0.125)
    rope_blocks(proj(C_SWA_K, 128), swak, 16, 1.0)
    swav[0] = proj(C_SWA_V, 128).astype(BF16)
    def values_with_ones(z, out):
        lane = _lane((z.shape[0], LANES))
        for hh in range(4):
            zb = z[:, (hh // 2) * LANES:(hh // 2 + 1) * LANES]
            keep = (lane < 64) if hh % 2 == 0 else (lane >= 64)
            out[0, :, hh * LANES:(hh + 1) * LANES] = jnp.where(keep, zb, 1.0).astype(BF16)

    rope_blocks(proj(C_DIF_Q, 256), dq, 8, 32 ** -0.5 * LOG2E)
    rope_blocks(proj(C_DIF_K, 256), dk, 8, 1.0)
    values_with_ones(proj(C_DIF_V, 256), dv)
    rope_blocks(proj(C_RET_Q, 256), rq, 16, 0.125)
    rope_blocks(proj(C_RET_K, 256), rk, 16, 1.0)
    rv[0] = proj(C_RET_V, 256).astype(BF16)
    rg[0] = proj(C_RET_G, 256).astype(BF16)

    ql = _rms(proj(C_MLA_Q, 256), qng_ref[...]).astype(BF16)
    qa = _dot(ql, wuq_ref[...]) * (MLA_SCALE * LOG2E)
    for hh in range(4):
        mq[0, :, hh * 256:hh * 256 + LANES] = qa[:, hh * 256:hh * 256 + LANES].astype(BF16)
        zr = _rope(qa[:, hh * 256 + LANES:(hh + 1) * 256], c32, s32, 8)
        mq[0, :, hh * 256 + LANES:(hh + 1) * 256] = zr.astype(BF16)
    kvl = _rms(proj(C_MLA_KV, 128), kvng_ref[...]).astype(BF16)
    kn = _dot(kvl, wukk_ref[...])
    kr = _rope(proj(C_MLA_KR, 128), c32, s32, 8).astype(BF16)
    for p in range(2):
        mk[0, :, p * 256:p * 256 + LANES] = kn[:, p * LANES:(p + 1) * LANES].astype(BF16)
        mk[0, :, p * 256 + LANES:(p + 1) * 256] = kr
    values_with_ones(_dot(kvl, wukv_ref[...]), mv)


def _inproj(x, mod, nbatch, norm_g, w_in_p, tabs, qng, wuq, kvng, wukk, wukv):
    B, S, D = x.shape
    nt = S // TM
    tok = lambda w: pl.BlockSpec((1, TM, w), lambda b, i: (b, i, 0))
    full = lambda a: pl.BlockSpec(a.shape, lambda b, i: (0,) * a.ndim)
    tab = pl.BlockSpec((TM, LANES), lambda b, i: (i, 0))
    widths = [256, 128, 128, 256, 256, 512, 256, 256, 256, 256, 1024, 512, 512]
    return pl.pallas_call(
        _inproj_kernel,
        grid=(B, nt),
        in_specs=[tok(D), full(norm_g), _mod_spec(0, nbatch), _mod_spec(1, nbatch), full(w_in_p),
                  tab, tab, tab, tab, full(qng), full(wuq), full(kvng), full(wukk), full(wukv)],
        out_specs=[tok(w) for w in widths],
        out_shape=[jax.ShapeDtypeStruct((B, S, w), BF16) for w in widths],
        compiler_params=_cparams(("parallel", "parallel")),
        name="inproj",
    )(x, norm_g, mod, mod, w_in_p, *tabs, qng, wuq, kvng, wukk, wukv)


def _swa_kernel(sink_ref, q_ref, kp_ref, ko_ref, kn_ref, vp_ref, vo_ref, vn_ref, kc_ref, vc_ref,
                o_ref, *, n_ctx_tiles, seq):
    i = pl.program_id(1)
    blk = RET_CHUNK
    n = i - n_ctx_tiles
    k_all = jnp.concatenate([kp_ref[0], ko_ref[0], kn_ref[0], kc_ref[0]], axis=0)
    v_all = jnp.concatenate([vp_ref[0], vo_ref[0], vn_ref[0], vc_ref[0]], axis=0)
    nk = 3 * blk + CTX_LEN
    r_io = lax.broadcasted_iota(jnp.int32, (blk, nk), 0)
    c_io = lax.broadcasted_iota(jnp.int32, (blk, nk), 1)
    kpos = (n - 1) * blk + c_io
    in_band = ((c_io >= r_io) & (c_io <= r_io + 2 * SWA_WINDOW) & (kpos >= 0) & (kpos < seq)
               & (n >= 0))
    valid = in_band | (c_io >= 3 * blk)
    lane = _lane((blk, LANES))
    heads = [(r, g) for r in range(2) for g in range(2)]
    qm = [jnp.where((lane >= 64 * g) & (lane < 64 * (g + 1)), q_ref[0, :, r * LANES:(r + 1) * LANES],
                    jnp.zeros((blk, LANES), BF16)) for r, g in heads]
    s = [jnp.where(valid, _dot_nt(q, k_all), NEG_INF) for q in qm]
    sink = [sink_ref[2 * g + r] for r, g in heads]
    m = [jnp.maximum(jnp.max(s[h], axis=-1, keepdims=True), sink[h]) for h in range(4)]
    p = [jnp.exp(s[h] - m[h]) for h in range(4)]
    l = [jnp.sum(p[h], axis=-1, keepdims=True) + jnp.exp(sink[h] - m[h]) for h in range(4)]
    o = [_dot(p[h].astype(BF16), v_all) / l[h] for h in range(4)]
    for r in range(2):
        o_ref[0, :, r * LANES:(r + 1) * LANES] = jnp.where(lane < 64, o[2 * r], o[2 * r + 1]).astype(BF16)


def _swa(sink, q, k, v):
    B, S, _ = q.shape
    blk = RET_CHUNK
    nt = S // blk
    nct = CTX_LEN // blk
    kvs = lambda f: pl.BlockSpec((1, blk, LANES), f)
    prev = lambda b, i: (b, jnp.maximum(i - 1, 0), 0)
    own = lambda b, i: (b, i, 0)
    nxt = lambda b, i: (b, jnp.minimum(i + 1, nt - 1), 0)
    ctx = pl.BlockSpec((1, CTX_LEN, LANES), lambda b, i: (b, 0, 0))
    return pl.pallas_call(
        functools.partial(_swa_kernel, n_ctx_tiles=nct, seq=S - CTX_LEN),
        grid=(B, nt),
        in_specs=[pl.BlockSpec(memory_space=pltpu.SMEM),
                  pl.BlockSpec((1, blk, 256), own),
                  kvs(prev), kvs(own), kvs(nxt), kvs(prev), kvs(own), kvs(nxt), ctx, ctx],
        out_specs=pl.BlockSpec((1, blk, 256), own),
        out_shape=jax.ShapeDtypeStruct((B, S, 256), BF16),
        compiler_params=_cparams(("parallel", "parallel")),
        name="swa",
    )(sink, q, k, k, k, v, v, v, k, v)


def _flash(q_maps, v_blk, k_ref, v_ref, m_scr, acc_scr, *, n_lat, tk):
    nm = len(q_maps)
    for a in range(nm):
        m_scr[a] = jnp.full(m_scr.shape[1:], NEG_INF, F32)
        acc_scr[a] = jnp.zeros(acc_scr.shape[1:], F32)

    def chunk(start, size):
        kc = k_ref[0, pl.ds(start, size), :]
        nt = size // LANES
        for a in range(nm):
            vc = v_ref[0, pl.ds(start, size), v_blk[a] * LANES:(v_blk[a] + 1) * LANES]
            s = _dot_nt(q_maps[a], kc)
            mt = s[:, 0:LANES]
            for t in range(1, nt):
                mt = jnp.maximum(mt, s[:, t * LANES:(t + 1) * LANES])
            m_prev = m_scr[a]
            m_new = jnp.maximum(m_prev, jnp.max(mt, axis=-1, keepdims=True))
            p = jnp.concatenate(
                [jnp.exp2(s[:, t * LANES:(t + 1) * LANES] - m_new).astype(BF16) for t in range(nt)],
                axis=1)
            acc_scr[a] = jnp.exp2(m_prev - m_new) * acc_scr[a] + _dot(p, vc)
            m_scr[a] = m_new

    chunk(0, CTX_LEN)
    if n_lat:

        def body(c, carry):
            chunk(pl.multiple_of(CTX_LEN + c * tk, LANES), tk)
            return carry

        lax.fori_loop(0, n_lat, body, 0)
    outs = []
    for a in range(nm):
        acc = acc_scr[a]
        outs.append(acc / pltpu.roll(acc, 64, 1))
    return outs


def _attention_calls(kern, name, nm, q, k, v, qw, kw, tk, extra_in=(), extra_specs=()):
    B, S, _ = q.shape
    T = S - CTX_LEN

    def call(tq, n_tiles, n_lat, q_spec, kv_rows, call_name):
        return pl.pallas_call(
            functools.partial(kern, n_lat=n_lat, tk=tk),
            grid=(B, 2, n_tiles),
            in_specs=list(extra_specs) + [
                q_spec,
                pl.BlockSpec((1, kv_rows, kw), lambda b, p, i: (b, 0, p)),
                pl.BlockSpec((1, kv_rows, 256), lambda b, p, i: (b, 0, p))],
            out_specs=pl.BlockSpec((1, tq, LANES), lambda b, p, i: (b, i, p)),
            out_shape=jax.ShapeDtypeStruct((B, tq * n_tiles, 256), BF16),
            scratch_shapes=[pltpu.VMEM((nm, tq, LANES), F32), pltpu.VMEM((nm, tq, LANES), F32)],
            compiler_params=_cparams(("parallel", "parallel", "parallel")),
            name=call_name,
        )(*extra_in, q, k, v)

    y_ctx = call(CTX_LEN, 1, 0, pl.BlockSpec((1, CTX_LEN, qw), lambda b, p, i: (b, 0, p)),
                 CTX_LEN, name + "_ctx")
    tq = min(TQ, T)
    y_lat = call(tq, T // tq, T // tk,
                 pl.BlockSpec((pl.Element(1), pl.Element(tq), pl.Element(qw)),
                              lambda b, p, i: (b, (i * (tq // CTX_LEN) + 1) * CTX_LEN, p * qw)),
                 S, name)
    return jnp.concatenate([y_ctx, y_lat], axis=1)


def _diff_attn(lam_ref, g_ref, q_ref, k_ref, v_ref, o_ref, m_scr, acc_scr, *,
               n_lat, tk, lambda_init):
    qb = q_ref[0]
    lane = _lane(qb.shape)
    q_maps = []
    for a in range(4):
        msk = (lane >= 32 * a) & (lane < 32 * (a + 1))
        q_maps.append(jnp.where(msk, qb, jnp.zeros_like(qb)))
    n = _flash(q_maps, (0, 0, 1, 1), k_ref, v_ref, m_scr, acc_scr, n_lat=n_lat, tk=tk)
    lp = lam_ref[...]
    lam = (jnp.exp(jnp.sum(lp[0:1] * lp[1:2], axis=-1, keepdims=True))
           - jnp.exp(jnp.sum(lp[2:3] * lp[3:4], axis=-1, keepdims=True)) + lambda_init)
    lane_o = _lane(n[0].shape)
    o = jnp.where(lane_o < 64, n[0] - lam * n[1], n[2] - lam * n[3])
    sq = o * o
    ms0 = jnp.sum(jnp.where(lane_o < 64, sq, 0.0), axis=-1, keepdims=True) * (1.0 / 64)
    ms1 = jnp.sum(jnp.where(lane_o >= 64, sq, 0.0), axis=-1, keepdims=True) * (1.0 / 64)
    ms = jnp.where(lane_o < 64, ms0, ms1)
    y = o * lax.rsqrt(ms + NORM_EPS) * g_ref[...] * (1.0 - lambda_init)
    o_ref[0] = y.astype(BF16)


def _diff(lam_p, subln_g2, q, k, v, lambda_init, tk):
    return _attention_calls(
        functools.partial(_diff_attn, lambda_init=lambda_init), "diff_attn", 4, q, k, v, LANES, LANES,
        tk, extra_in=(lam_p, subln_g2),
        extra_specs=(pl.BlockSpec(lam_p.shape, lambda b, p, i: (0, 0)),
                     pl.BlockSpec((1, LANES), lambda b, p, i: (0, 0))))


def _mla_attn(q_ref, k_ref, v_ref, o_ref, m_scr, acc_scr, *, n_lat, tk):
    q_maps = [q_ref[0, :, 0:256], q_ref[0, :, 256:512]]
    n = _flash(q_maps, (0, 1), k_ref, v_ref, m_scr, acc_scr, n_lat=n_lat, tk=tk)
    lane_o = _lane(n[0].shape)
    o_ref[0] = jnp.where(lane_o < 64, n[0], n[1]).astype(BF16)


def _mla(q, k, v, tk):
    return _attention_calls(_mla_attn, "mla_attn", 2, q, k, v, 512, 256, tk)


def _ret_tables(dl, backward):
    L = RET_CHUNK
    lg = -(jnp.maximum(-dl, 0.0) + jnp.log(1.0 + jnp.exp(-jnp.abs(dl))))
    row = lax.broadcasted_iota(jnp.int32, (L, LANES), 0).astype(F32)
    col = lax.broadcasted_iota(jnp.int32, (L, LANES), 1)
    colf = col.astype(F32)
    if backward:
        qd, kd, rel = jnp.exp((L - row) * lg), jnp.exp(row * lg), colf - row
    else:
        qd, kd, rel = jnp.exp((row + 1.0) * lg), jnp.exp((L - 1.0 - row) * lg), row - colf
    decs = []
    for hh in range(2):
        lg_h = jnp.sum(jnp.where(col[0:1] == 64 * hh, lg, 0.0), axis=-1, keepdims=True)
        decs.append(jnp.where(rel >= 0, jnp.exp(jnp.maximum(rel, 0.0) * lg_h), 0.0))
    return qd, kd, decs[0], decs[1], jnp.exp(float(L) * lg)


def _ret_chunk(q, k, v, tab, cd, s_ref):
    col = _lane((RET_CHUNK, LANES))
    qf, kf = q.astype(F32), k.astype(F32)
    o = _dot((qf * tab[0]).astype(BF16), s_ref[...].astype(BF16))
    outs = []
    for hh in range(2):
        hmask = (col >= 64 * hh) & (col < 64 * (hh + 1))
        a = _dot_nt(jnp.where(hmask, q, jnp.zeros_like(q)), k) * tab[2 + hh]
        outs.append(_dot(a.astype(BF16), v))
    o = o + jnp.where(col < 64, outs[0], outs[1])
    rowi = lax.broadcasted_iota(jnp.int32, (LANES, LANES), 0)
    coli = lax.broadcasted_iota(jnp.int32, (LANES, LANES), 1)
    s_new = s_ref[...] * cd + _dot_tn((kf * tab[1]).astype(BF16), v)
    s_ref[...] = jnp.where((rowi < 64) == (coli < 64), s_new, 0.0)
    return o


def _ret_kernel(dl_ref, qf_ref, kf_ref, vf_ref, qb_ref, kb_ref, vb_ref, of_ref, ob_ref,
                s_scr, tab_scr, cd_scr):
    nb = qf_ref.shape[0]

    @pl.when(pl.program_id(0) == 0)
    def _():
        s_scr[...] = jnp.zeros(s_scr.shape, F32)
        for d in range(2):
            for p in range(2):
                tabs = _ret_tables(dl_ref[d, p], backward=(d == 1))
                for t in range(4):
                    tab_scr[d, p, t] = tabs[t]
                cd_scr[d, p] = tabs[4]

    for d, (q_ref, k_ref, v_ref, o_ref) in enumerate(((qf_ref, kf_ref, vf_ref, of_ref),
                                                      (qb_ref, kb_ref, vb_ref, ob_ref))):
        for p in range(2):
            tab = [tab_scr[d, p, t] for t in range(4)]
            cd = cd_scr[d, p]
            ln = slice(p * LANES, (p + 1) * LANES)
            for b in range(nb):
                o_ref[b, :, ln] = _ret_chunk(q_ref[b, :, ln], k_ref[b, :, ln], v_ref[b, :, ln],
                                             tab, cd, s_scr.at[d, b, p])


def _retention(dl_lane, q, k, v):
    B, S, _ = q.shape
    L = RET_CHUNK
    nc = S // L
    ncc = CTX_LEN // L
    fwd = pl.BlockSpec((B, L, 256), lambda s: (0, s, 0))
    bwd = pl.BlockSpec((B, L, 256), lambda s: (0, jnp.where(s < ncc, ncc - 1 - s, nc - 1 + ncc - s), 0))
    return pl.pallas_call(
        _ret_kernel,
        grid=(nc,),
        in_specs=[pl.BlockSpec(dl_lane.shape, lambda s: (0, 0, 0, 0)), fwd, fwd, fwd, bwd, bwd, bwd],
        out_specs=[fwd, bwd],
        out_shape=[jax.ShapeDtypeStruct((B, S, 256), F32)] * 2,
        scratch_shapes=[pltpu.VMEM((2, B, 2, LANES, LANES), F32),
                        pltpu.VMEM((2, 2, 4, L, LANES), F32), pltpu.VMEM((2, 2, 1, LANES), F32)],
        compiler_params=_cparams(("arbitrary",)),
        name="retention",
    )(dl_lane, q, k, v, q, k, v)


def _gated_group_norm(o, g, gn):
    lo = _lane(o.shape) < 64

    def halves(t):
        a = jnp.sum(jnp.where(lo, t, 0.0), axis=-1, keepdims=True) * (1.0 / 64)
        b = jnp.sum(jnp.where(lo, 0.0, t), axis=-1, keepdims=True) * (1.0 / 64)
        return jnp.where(lo, a, b)

    d = o - halves(o)
    var = halves(d * d)
    return d * lax.rsqrt(var + NORM_EPS) * gn * (g * jax.nn.sigmoid(g))


def _outproj_kernel(x_ref, ya_ref, yb_ref, of_ref, ob_ref, rg_ref, gn_ref, ym_ref, w_ref, g1_ref,
                    sh_ref, sc_ref, ng_ref, wr_ref, br_ref, xo_ref, h_ref, rt_ref):
    mix = (_dot(ya_ref[0], w_ref[0:256]) + _dot(yb_ref[0], w_ref[256:512])
           + _dot(ym_ref[0], w_ref[768:1024]))
    for j in range(2):
        ln = slice(j * LANES, (j + 1) * LANES)
        yr = _gated_group_norm(of_ref[0, :, ln] + ob_ref[0, :, ln], rg_ref[0, :, ln].astype(F32),
                               gn_ref[:, ln])
        mix = mix + _dot(yr.astype(BF16), w_ref[512 + j * LANES:512 + (j + 1) * LANES])
    x = x_ref[0] + g1_ref[...] * mix
    xo_ref[0] = x
    h = _rms(x, ng_ref[...]) * (1.0 + sc_ref[...]) + sh_ref[...]
    h_ref[0] = h
    logits = jnp.dot(h, wr_ref[...], preferred_element_type=F32,
                     precision=lax.Precision.HIGHEST) + br_ref[...]
    lane = _lane(logits.shape).astype(F32)

    def first_argmax(vals, mask):
        mx = jnp.max(jnp.where(mask, vals, NEG_INF), axis=-1, keepdims=True)
        idx = jnp.min(jnp.where(mask & (vals == mx), lane, float(LANES)), axis=-1, keepdims=True)
        return mx, idx

    gmask = lane < MOE_GROUPS
    g_max, g_idx = first_argmax(logits, gmask)
    g_w = 1.0 / jnp.sum(jnp.exp(jnp.where(gmask, logits - g_max, NEG_INF)), axis=-1, keepdims=True)
    e_lo = MOE_GROUPS + g_idx * MOE_PER_GROUP
    emask = (lane >= e_lo) & (lane < e_lo + MOE_PER_GROUP)
    v1, i1 = first_argmax(logits, emask)
    v2, i2 = first_argmax(logits, emask & (lane != i1))
    t = jnp.exp(v2 - v1)
    w1 = g_w / (1.0 + t)
    w2 = w1 * t
    rt = jnp.where(lane == 0, i1 - MOE_GROUPS,
                   jnp.where(lane == 1, i2 - MOE_GROUPS,
                             jnp.where(lane == 2, w1, jnp.where(lane == 3, w2, 0.0))))
    rt_ref[0] = rt


def _outproj(x, ya, yb, o_f, o_b, rg, gn_g, ym, w_out_p, mod, nbatch, norm_g, w_route, b_route):
    B, S, D = x.shape
    nt = S // TM
    tok = lambda w: pl.BlockSpec((1, TM, w), lambda b, i: (b, i, 0))
    full = lambda a: pl.BlockSpec(a.shape, lambda b, i: (0,) * a.ndim)
    return pl.pallas_call(
        _outproj_kernel,
        grid=(B, nt),
        in_specs=[tok(D), tok(256), tok(256), tok(256), tok(256), tok(256), full(gn_g), tok(256),
                  full(w_out_p),
                  _mod_spec(2, nbatch), _mod_spec(3, nbatch), _mod_spec(4, nbatch), full(norm_g),
                  full(w_route), full(b_route)],
        out_specs=[tok(D), tok(D), tok(LANES)],
        out_shape=[jax.ShapeDtypeStruct((B, S, D), F32), jax.ShapeDtypeStruct((B, S, D), F32),
                   jax.ShapeDtypeStruct((B, S, LANES), F32)],
        compiler_params=_cparams(("parallel", "parallel")),
        name="outproj_router",
    )(x, ya, yb, o_f, o_b, rg, gn_g, ym, w_out_p, mod, mod, mod, norm_g, w_route, b_route)


def _rank_kernel(rt_ref, rank_ref, cnt_ref, carry):
    @pl.when(pl.program_id(0) == 0)
    def _():
        carry[...] = jnp.zeros(carry.shape, F32)

    rt = rt_ref[...]
    lane = _lane(rt.shape)
    lanef = lane.astype(F32)
    oh0 = jnp.where(lanef == rt[:, 0:1], 1.0, 0.0)
    oh1 = jnp.where(lanef == rt[:, 1:2], 1.0, 0.0)
    cnt = oh0 + oh1
    r_io = lax.broadcasted_iota(jnp.int32, (TM, TM), 0)
    c_io = lax.broadcasted_iota(jnp.int32, (TM, TM), 1)
    tri = jnp.where(r_io > c_io, 1.0, 0.0).astype(BF16)
    before = _dot(tri, cnt.astype(BF16)) + carry[...]
    r0 = jnp.sum(oh0 * before, axis=-1, keepdims=True)
    r1 = jnp.sum(oh1 * before, axis=-1, keepdims=True)
    rank_ref[...] = jnp.where(lane == 0, r0, jnp.where(lane == 1, r1, 0.0))
    carry[...] = carry[...] + jnp.sum(cnt, axis=0, keepdims=True)
    cnt_ref[...] = carry[...]


def _ranks(route):
    n = route.shape[0]
    return pl.pallas_call(
        _rank_kernel,
        grid=(n // TM,),
        in_specs=[pl.BlockSpec((TM, LANES), lambda i: (i, 0))],
        out_specs=[pl.BlockSpec((TM, LANES), lambda i: (i, 0)),
                   pl.BlockSpec((1, LANES), lambda i: (0, 0))],
        out_shape=[jax.ShapeDtypeStruct((n, LANES), F32), jax.ShapeDtypeStruct((1, LANES), F32)],
        scratch_shapes=[pltpu.VMEM((1, LANES), F32)],
        compiler_params=_cparams(("arbitrary",)),
        name="moe_rank",
    )(route)


def _dispatch_kernel(dest_ref, h_ref, xb_in_ref, xb_ref, sem):
    del xb_in_ref
    base = pl.program_id(0) * TM

    def copy(r, k):
        d = dest_ref[(base + r) * 2 + k]
        return pltpu.make_async_copy(h_ref.at[pl.ds(r, 1)], xb_ref.at[pl.ds(d, 1)], sem)

    def issue(r, c):
        copy(r, 0).start()
        copy(r, 1).start()
        return c

    def drain(r, c):
        copy(r, 0).wait()
        copy(r, 1).wait()
        return c

    lax.fori_loop(0, TM, issue, 0, unroll=DMA_UNROLL)
    lax.fori_loop(0, TM, drain, 0, unroll=DMA_UNROLL)


def _dispatch(dest, h, cap):
    n, d = h.shape
    return pl.pallas_call(
        _dispatch_kernel,
        grid_spec=pltpu.PrefetchScalarGridSpec(
            num_scalar_prefetch=1,
            grid=(n // TM,),
            in_specs=[pl.BlockSpec((TM, d), lambda i, dest: (i, 0)),
                      pl.BlockSpec(memory_space=pl.ANY)],
            out_specs=pl.BlockSpec(memory_space=pl.ANY),
            scratch_shapes=[pltpu.SemaphoreType.DMA(())]),
        out_shape=jax.ShapeDtypeStruct((cap, d), h.dtype),
        input_output_aliases={2: 0},
        compiler_params=_cparams(("arbitrary",)),
        name="moe_dispatch",
    )(dest, h, jnp.zeros((cap, d), h.dtype))


def _expert_kernel(be_ref, nu_ref, x_ref, wg_ref, wu_ref, wd_ref, y_ref, wg_s, wu_s, wd_s):
    i = pl.program_id(0)

    @pl.when((i == 0) | (be_ref[i] != be_ref[jnp.maximum(i - 1, 0)]))
    def _():
        wg_s[...] = wg_ref[0].astype(BF16)
        wu_s[...] = wu_ref[0].astype(BF16)
        wd_s[...] = wd_ref[0].astype(BF16)

    @pl.when(i < nu_ref[0])
    def _():
        xb = x_ref[...].astype(BF16)
        gate = _dot(xb, wg_s[...])
        up = _dot(xb, wu_s[...])
        hid = (gate * jax.nn.sigmoid(gate) * up).astype(BF16)
        y_ref[...] = _dot(hid, wd_s[...])

    @pl.when(i >= nu_ref[0])
    def _():
        y_ref[...] = jnp.zeros(y_ref.shape, y_ref.dtype)


def _experts(block_e, n_used, xb, w_gate, w_up, w_down, layer):
    cap, d = xb.shape
    hdim = w_gate.shape[-1]
    return pl.pallas_call(
        _expert_kernel,
        grid_spec=pltpu.PrefetchScalarGridSpec(
            num_scalar_prefetch=2,
            grid=(cap // MOE_ROWS,),
            in_specs=[pl.BlockSpec((MOE_ROWS, d), lambda i, be, nu: (i, 0)),
                      pl.BlockSpec((None, 1, d, hdim), lambda i, be, nu: (layer, be[i], 0, 0)),
                      pl.BlockSpec((None, 1, d, hdim), lambda i, be, nu: (layer, be[i], 0, 0)),
                      pl.BlockSpec((None, 1, hdim, d), lambda i, be, nu: (layer, be[i], 0, 0))],
            out_specs=pl.BlockSpec((MOE_ROWS, d), lambda i, be, nu: (i, 0)),
            scratch_shapes=[pltpu.VMEM((d, hdim), BF16), pltpu.VMEM((d, hdim), BF16),
                            pltpu.VMEM((hdim, d), BF16)]),
        out_shape=jax.ShapeDtypeStruct((cap, d), F32),
        compiler_params=_cparams(("arbitrary",)),
        name="moe_experts",
    )(block_e, n_used, xb, w_gate, w_up, w_down)


def _combine_kernel(dest_ref, x_ref, rt_ref, g2_ref, fg_ref, yb_ref, o_ref, gath, sem, *,
                    tile_off, tiles_per_batch, final):
    b, i = pl.program_id(0), pl.program_id(1)
    base = (b * tiles_per_batch + i + tile_off) * TM

    def copy(r, k):
        d = dest_ref[(base + r) * 2 + k]
        return pltpu.make_async_copy(yb_ref.at[pl.ds(d, 1)], gath.at[k, pl.ds(r, 1)], sem)

    def issue(r, c):
        copy(r, 0).start()
        copy(r, 1).start()
        return c

    def drain(r, c):
        copy(r, 0).wait()
        copy(r, 1).wait()
        return c

    lax.fori_loop(0, TM, issue, 0, unroll=DMA_UNROLL)
    lax.fori_loop(0, TM, drain, 0, unroll=DMA_UNROLL)
    rt = rt_ref[0]
    m = rt[:, 2:3] * gath[0] + rt[:, 3:4] * gath[1]
    x = x_ref[0] + g2_ref[...] * m
    if final:
        x = _rms(x, fg_ref[...])
    o_ref[0] = x


def _combine(dest, x, route, mod, nbatch, final_g, yb, final):
    B, S, D = x.shape
    nt = S // TM
    off = 1 if final else 0
    g2_spec = pl.BlockSpec((None, 1, D_MODEL),
                           lambda b, i, dest: (jnp.where(i + off == 0, nbatch, b), 0, 5))
    return pl.pallas_call(
        functools.partial(_combine_kernel, tile_off=off, tiles_per_batch=nt, final=final),
        grid_spec=pltpu.PrefetchScalarGridSpec(
            num_scalar_prefetch=1,
            grid=(B, nt - off),
            in_specs=[pl.BlockSpec((1, TM, D), lambda b, i, dest: (b, i + off, 0)),
                      pl.BlockSpec((1, TM, LANES), lambda b, i, dest: (b, i + off, 0)),
                      g2_spec,
                      pl.BlockSpec((1, D), lambda b, i, dest: (0, 0)),
                      pl.BlockSpec(memory_space=pl.ANY)],
            out_specs=pl.BlockSpec((1, TM, D), lambda b, i, dest: (b, i, 0)),
            scratch_shapes=[pltpu.VMEM((2, TM, D), F32), pltpu.SemaphoreType.DMA(())]),
        out_shape=jax.ShapeDtypeStruct((B, S - off * TM, D), F32),
        compiler_params=_cparams(("arbitrary", "arbitrary")),
        name="moe_combine",
    )(dest, x, route, mod, final_g, yb)


def _rope_tables(seq):
    t = jnp.arange(seq, dtype=jnp.int32)
    row, col = t // GRID_W, t % GRID_W
    lane = jnp.arange(LANES, dtype=jnp.int32)
    tabs = []
    for r in (64, 32):
        nf = r // 4
        inv = 1.0 / (ROPE_BASE ** (jnp.arange(nf, dtype=F32) / nf))
        pos = jnp.where(((lane % r) < 2 * nf)[None, :], row[:, None], col[:, None]).astype(F32)
        ang = pos * inv[lane % nf][None, :]
        first = ((lane % (2 * nf)) < nf)[None, :]
        cos = jnp.cos(ang)
        sin = jnp.where(first, -jnp.sin(ang), jnp.sin(ang))
        tabs.append(jnp.concatenate([jnp.ones((CTX_LEN, LANES), F32), cos], axis=0))
        tabs.append(jnp.concatenate([jnp.zeros((CTX_LEN, LANES), F32), sin], axis=0))
    return tabs


def _layer_weights(w_in, w_out, w_uq, w_ukv, w_group, b_group, w_expert, b_expert):
    perm = jnp.array([0, 2, 1, 3])
    wq = w_in[:, :256].reshape(D_MODEL, 4, 64)[:, perm].reshape(D_MODEL, 256)
    w_in_p = jnp.concatenate([wq, w_in[:, 256:], jnp.zeros((D_MODEL, D_IN_PAD - w_in.shape[1]), F32)],
                             axis=1).astype(BF16)
    wo = jnp.concatenate([w_out[:256].reshape(4, 64, D_MODEL)[perm].reshape(256, D_MODEL), w_out[256:]],
                         axis=0).astype(BF16)
    uq = w_uq.reshape(256, 4, 96)
    blocks = []
    for h in range(4):
        blk = jnp.zeros((256, 256), F32)
        blk = blk.at[:, 64 * (h % 2):64 * (h % 2) + 64].set(uq[:, h, :64])
        blk = blk.at[:, LANES:LANES + 32].set(uq[:, h, 64:])
        blocks.append(blk)
    wuq = jnp.concatenate(blocks, axis=1).astype(BF16)
    ukv = w_ukv.reshape(128, 4, 128)
    wukk = ukv[:, :, :64].reshape(128, 256).astype(BF16)
    wukv = ukv[:, :, 64:].reshape(128, 256).astype(BF16)
    pad = LANES - MOE_GROUPS - MOE_EXPERTS
    w_route = jnp.concatenate([w_group, w_expert, jnp.zeros((D_MODEL, pad), F32)], axis=1)
    b_route = jnp.concatenate([b_group, b_expert, jnp.zeros((pad,), F32)]).reshape(1, LANES)
    return w_in_p, wo, wuq, wukk, wukv, w_route, b_route


def _moe_plan(route, ranks, counts, cap):
    e = route[:, :2].astype(jnp.int32)
    rank = ranks[:, :2].astype(jnp.int32)
    cnt = counts[0, :MOE_EXPERTS].astype(jnp.int32)
    padded = (cnt + MOE_ROWS - 1) // MOE_ROWS * MOE_ROWS
    pend = jnp.cumsum(padded)
    pstart = pend - padded
    dest = (pstart[e] + rank).reshape(-1)
    nblk = cap // MOE_ROWS
    blk_start = jnp.arange(nblk, dtype=jnp.int32) * MOE_ROWS
    block_e = jnp.minimum(jnp.sum((pend[None, :] <= blk_start[:, None]).astype(jnp.int32), axis=1),
                          MOE_EXPERTS - 1)
    n_used = (pend[-1:] // MOE_ROWS).astype(jnp.int32)
    return dest, block_e, n_used


def kernel(x, c, ctx, c_ctx, w_mod, b_mod, norm1_g, norm2_g, w_in, w_out, swa_sink, diff_lambda, diff_subln_g, ret_decay_logit, ret_gn_g, mla_q_norm_g, mla_w_uq, mla_kv_norm_g, mla_w_ukv, moe_w_group, moe_b_group, moe_w_expert, moe_b_expert, moe_w_gate, moe_w_up, moe_w_down, final_norm_g):
    B, T, D = x.shape
    S = CTX_LEN + T
    n_tok = B * S
    tk = min(FLASH_TK, T)
    cap = -(-(2 * n_tok + MOE_EXPERTS * (MOE_ROWS - 1)) // MOE_ROWS) * MOE_ROWS
    xs = jnp.concatenate([ctx, x], axis=1)
    cvec = jnp.zeros((8, D), F32).at[:B].set(c).at[B].set(c_ctx)
    tabs = _rope_tables(T)
    row = lambda v: v.reshape(1, -1)

    for l in range(DEPTH):
        lambda_init = 0.8 - 0.6 * math.exp(-0.3 * l)
        w_in_p, wo, wuq, wukk, wukv, w_route, b_route = _layer_weights(
            w_in[l], w_out[l], mla_w_uq[l], mla_w_ukv[l], moe_w_group[l], moe_b_group[l],
            moe_w_expert[l], moe_b_expert[l])
        mod = _modulation(cvec, w_mod[l], b_mod[l]).reshape(8, 1, 6 * D)
        (swaq, swak, swav, dq, dk, dv, rq, rk, rv, rg, mq, mk, mv) = _inproj(
            xs, mod, B, row(norm1_g[l]), w_in_p, tabs, row(mla_q_norm_g[l]), wuq,
            row(mla_kv_norm_g[l]), wukk, wukv)
        sink_p = swa_sink[l]
        ya = _swa(sink_p, swaq, swak, swav)
        yb = _diff(diff_lambda[l], row(jnp.tile(diff_subln_g[l], 2)), dq, dk, dv, lambda_init, tk)
        dl_lane = jnp.repeat(ret_decay_logit[l], 64, axis=-1).reshape(2, 2, 1, LANES)
        o_f, o_b = _retention(dl_lane, rq, rk, rv)
        ym = _mla(mq, mk, mv, tk)
        xs, h2, route = _outproj(xs, ya, yb, o_f, o_b, rg, row(ret_gn_g[l]), ym, wo, mod, B,
                                 row(norm2_g[l]), w_route, b_route)
        route2 = route.reshape(n_tok, LANES)
        ranks, counts = _ranks(route2)
        dest, block_e, n_used = _moe_plan(route2, ranks, counts, cap)
        xb = _dispatch(dest, h2.reshape(n_tok, D), cap)
        yb_e = _experts(block_e, n_used, xb, moe_w_gate, moe_w_up, moe_w_down, l)
        final = l == DEPTH - 1
        xs = _combine(dest, xs, route, mod, B, row(final_norm_g), yb_e, final)
    return xs
```

```python
import functools
import math

import jax
import jax.numpy as jnp
from jax import lax
from jax.experimental import pallas as pl
from jax.experimental.pallas import tpu as pltpu

F32 = jnp.float32
BF16 = jnp.bfloat16

D_MODEL = 1024
CTX_LEN = 256
GRID_W = 64
ROPE_BASE = 10000.0
NORM_EPS = 1e-6
NEG_INF = -1e30
DEPTH = 2

SWA_WINDOW = 128
RET_CHUNK = 128
MLA_SCALE = (64 + 32) ** -0.5
LOG2E = math.log2(math.e)
MOE_GROUPS = 4
MOE_PER_GROUP = 8
MOE_EXPERTS = 32
MOE_HIDDEN = 512

LANES = 128
TM = 256
TQ = 2048
FLASH_TK = 1024
GAP_LIMIT = 64.0
MOE_ROWS = 256
DMA_UNROLL = 8
VMEM_LIMIT = 56 * 1024 * 1024

C_SWA_Q, C_SWA_K, C_SWA_V = 0, 256, 384
C_DIF_Q, C_DIF_K, C_DIF_V = 512, 768, 1024
C_RET_Q, C_RET_K, C_RET_V, C_RET_G = 1280, 1536, 1792, 2048
C_MLA_Q, C_MLA_KV, C_MLA_KR = 2304, 2560, 2688
D_IN_PAD = 2816


def _cparams(sem):
    return pltpu.CompilerParams(dimension_semantics=sem, vmem_limit_bytes=VMEM_LIMIT)


def _dot(a, b):
    return jnp.dot(a, b, preferred_element_type=F32)


def _dot_nt(a, b):
    return lax.dot_general(a, b, (((1,), (1,)), ((), ())), preferred_element_type=F32)


def _dot_tn(a, b):
    return lax.dot_general(a, b, (((0,), (0,)), ((), ())), preferred_element_type=F32)


def _rms(x, g):
    ms = jnp.mean(x * x, axis=-1, keepdims=True)
    return x * lax.rsqrt(ms + NORM_EPS) * g


def _lane(shape):
    return lax.broadcasted_iota(jnp.int32, shape, len(shape) - 1)


def _rope(z, cos, sin, nf):
    first = (_lane(z.shape) % (2 * nf)) < nf
    partner = jnp.where(first, pltpu.roll(z, LANES - nf, 1), pltpu.roll(z, nf, 1))
    return z * cos + partner * sin


def _mod_kernel(a_ref, w_ref, b_ref, o_ref):
    a = a_ref[...]
    act = a * jax.nn.sigmoid(a)
    o_ref[...] = jnp.dot(act, w_ref[...], preferred_element_type=F32,
                         precision=lax.Precision.HIGHEST) + b_ref[...]


def _modulation(cvec, w_mod, b_mod):
    n = w_mod.shape[1]
    bn = 512
    return pl.pallas_call(
        _mod_kernel,
        grid=(n // bn,),
        in_specs=[pl.BlockSpec((8, D_MODEL), lambda j: (0, 0)),
                  pl.BlockSpec((D_MODEL, bn), lambda j: (0, j)),
                  pl.BlockSpec((1, bn), lambda j: (0, j))],
        out_specs=pl.BlockSpec((8, bn), lambda j: (0, j)),
        out_shape=jax.ShapeDtypeStruct((8, n), F32),
        compiler_params=_cparams(("parallel",)),
        name="modulation",
    )(cvec, w_mod, b_mod.reshape(1, n))


def _mod_spec(chunk, nbatch):
    return pl.BlockSpec((None, 1, D_MODEL),
                        lambda b, i: (jnp.where(i == 0, nbatch, b), 0, chunk))


def _inproj_kernel(x_ref, g_ref, sh_ref, sc_ref, w_ref, c64_ref, s64_ref, c32_ref, s32_ref,
                   qng_ref, wuq_ref, kvng_ref, wukk_ref, wukv_ref,
                   swaq, swak, swav, dq, dk, dv, rq, rk, rv, rg, mq, mk, mv):
    x = x_ref[0]
    h = _rms(x, g_ref[...]) * (1.0 + sc_ref[...]) + sh_ref[...]
    hb = h.astype(BF16)
    c64, s64, c32, s32 = c64_ref[...], s64_ref[...], c32_ref[...], s32_ref[...]

    def proj(c0, width):
        return _dot(hb, w_ref[:, c0:c0 + width])

    def rope_blocks(z, out, nf, scale):
        cos, sin = (c64, s64) if nf == 16 else (c32, s32)
        for j in range(z.shape[1] // LANES):
            zz = _rope(z[:, j * LANES:(j + 1) * LANES], cos, sin, nf)
            if scale != 1.0:
                zz = zz * scale
            out[0, :, j * LANES:(j + 1) * LANES] = zz.astype(out.dtype)

    rope_blocks(proj(C_SWA_Q, 256), swaq, 16, 0.125)
    rope_blocks(proj(C_SWA_K, 128), swak, 16, 1.0)
    swav[0] = proj(C_SWA_V, 128).astype(BF16)
    def values_with_ones(z, out):
        lane = _lane((z.shape[0], LANES))
        for hh in range(4):
            zb = z[:, (hh // 2) * LANES:(hh // 2 + 1) * LANES]
            keep = (lane < 64) if hh % 2 == 0 else (lane >= 64)
            out[0, :, hh * LANES:(hh + 1) * LANES] = jnp.where(keep, zb, 1.0).astype(BF16)

    rope_blocks(proj(C_DIF_Q, 256), dq, 8, 32 ** -0.5 * LOG2E)
    rope_blocks(proj(C_DIF_K, 256), dk, 8, 1.0)
    values_with_ones(proj(C_DIF_V, 256), dv)
    rope_blocks(proj(C_RET_Q, 256), rq, 16, 0.125)
    rope_blocks(proj(C_RET_K, 256), rk, 16, 1.0)
    rv[0] = proj(C_RET_V, 256).astype(BF16)
    rg[0] = proj(C_RET_G, 256).astype(BF16)

    ql = _rms(proj(C_MLA_Q, 256), qng_ref[...]).astype(BF16)
    qa = _dot(ql, wuq_ref[...]) * (MLA_SCALE * LOG2E)
    for hh in range(4):
        mq[0, :, hh * 256:hh * 256 + LANES] = qa[:, hh * 256:hh * 256 + LANES].astype(BF16)
        zr = _rope(qa[:, hh * 256 + LANES:(hh + 1) * 256], c32, s32, 8)
        mq[0, :, hh * 256 + LANES:(hh + 1) * 256] = zr.astype(BF16)
    kvl = _rms(proj(C_MLA_KV, 128), kvng_ref[...]).astype(BF16)
    kn = _dot(kvl, wukk_ref[...])
    kr = _rope(proj(C_MLA_KR, 128), c32, s32, 8).astype(BF16)
    for p in range(2):
        mk[0, :, p * 256:p * 256 + LANES] = kn[:, p * LANES:(p + 1) * LANES].astype(BF16)
        mk[0, :, p * 256 + LANES:(p + 1) * 256] = kr
    values_with_ones(_dot(kvl, wukv_ref[...]), mv)


def _inproj(x, mod, nbatch, norm_g, w_in_p, tabs, qng, wuq, kvng, wukk, wukv):
    B, S, D = x.shape
    nt = S // TM
    tok = lambda w: pl.BlockSpec((1, TM, w), lambda b, i: (b, i, 0))
    full = lambda a: pl.BlockSpec(a.shape, lambda b, i: (0,) * a.ndim)
    tab = pl.BlockSpec((TM, LANES), lambda b, i: (i, 0))
    widths = [256, 128, 128, 256, 256, 512, 256, 256, 256, 256, 1024, 512, 512]
    return pl.pallas_call(
        _inproj_kernel,
        grid=(B, nt),
        in_specs=[tok(D), full(norm_g), _mod_spec(0, nbatch), _mod_spec(1, nbatch), full(w_in_p),
                  tab, tab, tab, tab, full(qng), full(wuq), full(kvng), full(wukk), full(wukv)],
        out_specs=[tok(w) for w in widths],
        out_shape=[jax.ShapeDtypeStruct((B, S, w), BF16) for w in widths],
        compiler_params=_cparams(("parallel", "parallel")),
        name="inproj",
    )(x, norm_g, mod, mod, w_in_p, *tabs, qng, wuq, kvng, wukk, wukv)


def _swa_kernel(sink_ref, q_ref, kp_ref, ko_ref, kn_ref, vp_ref, vo_ref, vn_ref, kc_ref, vc_ref,
                o_ref, *, n_ctx_tiles, seq):
    i = pl.program_id(1)
    blk = RET_CHUNK
    n = i - n_ctx_tiles
    k_all = jnp.concatenate([kp_ref[0], ko_ref[0], kn_ref[0], kc_ref[0]], axis=0)
    v_all = jnp.concatenate([vp_ref[0], vo_ref[0], vn_ref[0], vc_ref[0]], axis=0)
    nk = 3 * blk + CTX_LEN
    r_io = lax.broadcasted_iota(jnp.int32, (blk, nk), 0)
    c_io = lax.broadcasted_iota(jnp.int32, (blk, nk), 1)
    kpos = (n - 1) * blk + c_io
    in_band = ((c_io >= r_io) & (c_io <= r_io + 2 * SWA_WINDOW) & (kpos >= 0) & (kpos < seq)
               & (n >= 0))
    valid = in_band | (c_io >= 3 * blk)
    lane = _lane((blk, LANES))
    heads = [(r, g) for r in range(2) for g in range(2)]
    qm = [jnp.where((lane >= 64 * g) & (lane < 64 * (g + 1)), q_ref[0, :, r * LANES:(r + 1) * LANES],
                    jnp.zeros((blk, LANES), BF16)) for r, g in heads]
    s = [jnp.where(valid, _dot_nt(q, k_all), NEG_INF) for q in qm]
    sink = [sink_ref[2 * g + r] for r, g in heads]
    m = [jnp.maximum(jnp.max(s[h], axis=-1, keepdims=True), sink[h]) for h in range(4)]
    p = [jnp.exp(s[h] - m[h]) for h in range(4)]
    l = [jnp.sum(p[h], axis=-1, keepdims=True) + jnp.exp(sink[h] - m[h]) for h in range(4)]
    o = [_dot(p[h].astype(BF16), v_all) / l[h] for h in range(4)]
    for r in range(2):
        o_ref[0, :, r * LANES:(r + 1) * LANES] = jnp.where(lane < 64, o[2 * r], o[2 * r + 1]).astype(BF16)


def _swa(sink, q, k, v):
    B, S, _ = q.shape
    blk = RET_CHUNK
    nt = S // blk
    nct = CTX_LEN // blk
    kvs = lambda f: pl.BlockSpec((1, blk, LANES), f)
    prev = lambda b, i: (b, jnp.maximum(i - 1, 0), 0)
    own = lambda b, i: (b, i, 0)
    nxt = lambda b, i: (b, jnp.minimum(i + 1, nt - 1), 0)
    ctx = pl.BlockSpec((1, CTX_LEN, LANES), lambda b, i: (b, 0, 0))
    return pl.pallas_call(
        functools.partial(_swa_kernel, n_ctx_tiles=nct, seq=S - CTX_LEN),
        grid=(B, nt),
        in_specs=[pl.BlockSpec(memory_space=pltpu.SMEM),
                  pl.BlockSpec((1, blk, 256), own),
                  kvs(prev), kvs(own), kvs(nxt), kvs(prev), kvs(own), kvs(nxt), ctx, ctx],
        out_specs=pl.BlockSpec((1, blk, 256), own),
        out_shape=jax.ShapeDtypeStruct((B, S, 256), BF16),
        compiler_params=_cparams(("parallel", "parallel")),
        name="swa",
    )(sink, q, k, k, k, v, v, v, k, v)


def _flash(q_maps, v_blk, k_ref, v_ref, m_scr, acc_scr, gap_scr, *, n_lat, tk):
    nm = len(q_maps)

    def chunk(start, size, mode):
        kc = k_ref[0, pl.ds(start, size), :]
        for a in range(nm):
            vt = v_ref[0, v_blk[a] * LANES:(v_blk[a] + 1) * LANES, pl.ds(start, size)]
            st = _dot_nt(kc, q_maps[a])
            cm = jnp.max(st, axis=0, keepdims=True)
            if mode == "first":
                acc_scr[a] = _dot(vt, jnp.exp2(st - cm).astype(BF16))
                m_scr[a] = cm
                gap_scr[a] = jnp.full(cm.shape, NEG_INF, F32)
                continue
            m_prev = m_scr[a]
            m_new = jnp.maximum(m_prev, cm)
            alpha = jnp.exp2(m_prev - m_new)
            if mode == "fast":
                p = jnp.exp2(st - m_prev).astype(BF16)
                acc_scr[a] = (acc_scr[a] + _dot(vt, p)) * alpha
                gap_scr[a] = jnp.maximum(gap_scr[a], cm - m_prev)
            else:
                p = jnp.exp2(st - m_new).astype(BF16)
                acc_scr[a] = acc_scr[a] * alpha + _dot(vt, p)
            m_scr[a] = m_new

    def sweep(mode):
        chunk(0, CTX_LEN, "first")
        if n_lat:

            def body(c, carry):
                chunk(pl.multiple_of(CTX_LEN + c * tk, LANES), tk, mode)
                return carry

            lax.fori_loop(0, n_lat, body, 0)

    sweep("fast")
    if n_lat:
        worst = jnp.max(jnp.concatenate([gap_scr[a] for a in range(nm)], axis=0))

        @pl.when(worst > GAP_LIMIT)
        def _():
            sweep("exact")

    outs = []
    for a in range(nm):
        acc = acc_scr[a].T
        outs.append(acc / pltpu.roll(acc, 64, 1))
    return outs


def _attention_calls(kern, name, nm, q, k, v, qw, kw, tk, extra_in=(), extra_specs=()):
    B, S, _ = q.shape
    T = S - CTX_LEN

    def call(tq, n_tiles, n_lat, q_spec, kv_rows, call_name):
        return pl.pallas_call(
            functools.partial(kern, n_lat=n_lat, tk=tk),
            grid=(B, 2, n_tiles),
            in_specs=list(extra_specs) + [
                q_spec,
                pl.BlockSpec((1, kv_rows, kw), lambda b, p, i: (b, 0, p)),
                pl.BlockSpec((1, 256, kv_rows), lambda b, p, i: (b, p, 0))],
            out_specs=pl.BlockSpec((1, tq, LANES), lambda b, p, i: (b, i, p)),
            out_shape=jax.ShapeDtypeStruct((B, tq * n_tiles, 256), BF16),
            scratch_shapes=[pltpu.VMEM((nm, 1, tq), F32), pltpu.VMEM((nm, LANES, tq), F32),
                            pltpu.VMEM((nm, 1, tq), F32)],
            compiler_params=_cparams(("parallel", "parallel", "parallel")),
            name=call_name,
        )(*extra_in, q, k, vt)

    vt = jnp.swapaxes(v, 1, 2)
    y_ctx = call(CTX_LEN, 1, 0, pl.BlockSpec((1, CTX_LEN, qw), lambda b, p, i: (b, 0, p)),
                 CTX_LEN, name + "_ctx")
    tq = min(TQ, T)
    y_lat = call(tq, T // tq, T // tk,
                 pl.BlockSpec((pl.Element(1), pl.Element(tq), pl.Element(qw)),
                              lambda b, p, i: (b, (i * (tq // CTX_LEN) + 1) * CTX_LEN, p * qw)),
                 S, name)
    return jnp.concatenate([y_ctx, y_lat], axis=1)


def _diff_attn(lam_ref, g_ref, q_ref, k_ref, v_ref, o_ref, m_scr, acc_scr, gap_scr, *,
               n_lat, tk, lambda_init):
    qb = q_ref[0]
    lane = _lane(qb.shape)
    q_maps = []
    for a in range(4):
        msk = (lane >= 32 * a) & (lane < 32 * (a + 1))
        q_maps.append(jnp.where(msk, qb, jnp.zeros_like(qb)))
    n = _flash(q_maps, (0, 0, 1, 1), k_ref, v_ref, m_scr, acc_scr, gap_scr, n_lat=n_lat, tk=tk)
    lp = lam_ref[...]
    lam = (jnp.exp(jnp.sum(lp[0:1] * lp[1:2], axis=-1, keepdims=True))
           - jnp.exp(jnp.sum(lp[2:3] * lp[3:4], axis=-1, keepdims=True)) + lambda_init)
    lane_o = _lane(n[0].shape)
    o = jnp.where(lane_o < 64, n[0] - lam * n[1], n[2] - lam * n[3])
    sq = o * o
    ms0 = jnp.sum(jnp.where(lane_o < 64, sq, 0.0), axis=-1, keepdims=True) * (1.0 / 64)
    ms1 = jnp.sum(jnp.where(lane_o >= 64, sq, 0.0), axis=-1, keepdims=True) * (1.0 / 64)
    ms = jnp.where(lane_o < 64, ms0, ms1)
    y = o * lax.rsqrt(ms + NORM_EPS) * g_ref[...] * (1.0 - lambda_init)
    o_ref[0] = y.astype(BF16)


def _diff(lam_p, subln_g2, q, k, v, lambda_init, tk):
    return _attention_calls(
        functools.partial(_diff_attn, lambda_init=lambda_init), "diff_attn", 4, q, k, v, LANES, LANES,
        tk, extra_in=(lam_p, subln_g2),
        extra_specs=(pl.BlockSpec(lam_p.shape, lambda b, p, i: (0, 0)),
                     pl.BlockSpec((1, LANES), lambda b, p, i: (0, 0))))


def _mla_attn(q_ref, k_ref, v_ref, o_ref, m_scr, acc_scr, gap_scr, *, n_lat, tk):
    q_maps = [q_ref[0, :, 0:256], q_ref[0, :, 256:512]]
    n = _flash(q_maps, (0, 1), k_ref, v_ref, m_scr, acc_scr, gap_scr, n_lat=n_lat, tk=tk)
    lane_o = _lane(n[0].shape)
    o_ref[0] = jnp.where(lane_o < 64, n[0], n[1]).astype(BF16)


def _mla(q, k, v, tk):
    return _attention_calls(_mla_attn, "mla_attn", 2, q, k, v, 512, 256, tk)


def _ret_tables(dl, backward):
    L = RET_CHUNK
    lg = -(jnp.maximum(-dl, 0.0) + jnp.log(1.0 + jnp.exp(-jnp.abs(dl))))
    row = lax.broadcasted_iota(jnp.int32, (L, LANES), 0).astype(F32)
    col = lax.broadcasted_iota(jnp.int32, (L, LANES), 1)
    colf = col.astype(F32)
    if backward:
        qd, kd, rel = jnp.exp((L - row) * lg), jnp.exp(row * lg), colf - row
    else:
        qd, kd, rel = jnp.exp((row + 1.0) * lg), jnp.exp((L - 1.0 - row) * lg), row - colf
    decs = []
    for hh in range(2):
        lg_h = jnp.sum(jnp.where(col[0:1] == 64 * hh, lg, 0.0), axis=-1, keepdims=True)
        decs.append(jnp.where(rel >= 0, jnp.exp(jnp.maximum(rel, 0.0) * lg_h), 0.0))
    return qd, kd, decs[0], decs[1], jnp.exp(float(L) * lg)


def _ret_chunk(q, k, v, tab, cd, s_ref):
    col = _lane((RET_CHUNK, LANES))
    qf, kf = q.astype(F32), k.astype(F32)
    o = _dot((qf * tab[0]).astype(BF16), s_ref[...].astype(BF16))
    outs = []
    for hh in range(2):
        hmask = (col >= 64 * hh) & (col < 64 * (hh + 1))
        a = _dot_nt(jnp.where(hmask, q, jnp.zeros_like(q)), k) * tab[2 + hh]
        outs.append(_dot(a.astype(BF16), v))
    o = o + jnp.where(col < 64, outs[0], outs[1])
    rowi = lax.broadcasted_iota(jnp.int32, (LANES, LANES), 0)
    coli = lax.broadcasted_iota(jnp.int32, (LANES, LANES), 1)
    s_new = s_ref[...] * cd + _dot_tn((kf * tab[1]).astype(BF16), v)
    s_ref[...] = jnp.where((rowi < 64) == (coli < 64), s_new, 0.0)
    return o


def _ret_kernel(dl_ref, qf_ref, kf_ref, vf_ref, qb_ref, kb_ref, vb_ref, of_ref, ob_ref,
                s_scr, tab_scr, cd_scr):
    nb = qf_ref.shape[0]

    @pl.when(pl.program_id(0) == 0)
    def _():
        s_scr[...] = jnp.zeros(s_scr.shape, F32)
        for d in range(2):
            for p in range(2):
                tabs = _ret_tables(dl_ref[d, p], backward=(d == 1))
                for t in range(4):
                    tab_scr[d, p, t] = tabs[t]
                cd_scr[d, p] = tabs[4]

    for d, (q_ref, k_ref, v_ref, o_ref) in enumerate(((qf_ref, kf_ref, vf_ref, of_ref),
                                                      (qb_ref, kb_ref, vb_ref, ob_ref))):
        for p in range(2):
            tab = [tab_scr[d, p, t] for t in range(4)]
            cd = cd_scr[d, p]
            ln = slice(p * LANES, (p + 1) * LANES)
            for b in range(nb):
                o_ref[b, :, ln] = _ret_chunk(q_ref[b, :, ln], k_ref[b, :, ln], v_ref[b, :, ln],
                                             tab, cd, s_scr.at[d, b, p])


def _retention(dl_lane, q, k, v):
    B, S, _ = q.shape
    L = RET_CHUNK
    nc = S // L
    ncc = CTX_LEN // L
    fwd = pl.BlockSpec((B, L, 256), lambda s: (0, s, 0))
    bwd = pl.BlockSpec((B, L, 256), lambda s: (0, jnp.where(s < ncc, ncc - 1 - s, nc - 1 + ncc - s), 0))
    return pl.pallas_call(
        _ret_kernel,
        grid=(nc,),
        in_specs=[pl.BlockSpec(dl_lane.shape, lambda s: (0, 0, 0, 0)), fwd, fwd, fwd, bwd, bwd, bwd],
        out_specs=[fwd, bwd],
        out_shape=[jax.ShapeDtypeStruct((B, S, 256), F32)] * 2,
        scratch_shapes=[pltpu.VMEM((2, B, 2, LANES, LANES), F32),
                        pltpu.VMEM((2, 2, 4, L, LANES), F32), pltpu.VMEM((2, 2, 1, LANES), F32)],
        compiler_params=_cparams(("arbitrary",)),
        name="retention",
    )(dl_lane, q, k, v, q, k, v)


def _gated_group_norm(o, g, gn):
    lo = _lane(o.shape) < 64

    def halves(t):
        a = jnp.sum(jnp.where(lo, t, 0.0), axis=-1, keepdims=True) * (1.0 / 64)
        b = jnp.sum(jnp.where(lo, 0.0, t), axis=-1, keepdims=True) * (1.0 / 64)
        return jnp.where(lo, a, b)

    d = o - halves(o)
    var = halves(d * d)
    return d * lax.rsqrt(var + NORM_EPS) * gn * (g * jax.nn.sigmoid(g))


def _outproj_kernel(x_ref, ya_ref, yb_ref, of_ref, ob_ref, rg_ref, gn_ref, ym_ref, w_ref, g1_ref,
                    sh_ref, sc_ref, ng_ref, wr_ref, br_ref, xo_ref, h_ref, rt_ref):
    mix = (_dot(ya_ref[0], w_ref[0:256]) + _dot(yb_ref[0], w_ref[256:512])
           + _dot(ym_ref[0], w_ref[768:1024]))
    for j in range(2):
        ln = slice(j * LANES, (j + 1) * LANES)
        yr = _gated_group_norm(of_ref[0, :, ln] + ob_ref[0, :, ln], rg_ref[0, :, ln].astype(F32),
                               gn_ref[:, ln])
        mix = mix + _dot(yr.astype(BF16), w_ref[512 + j * LANES:512 + (j + 1) * LANES])
    x = x_ref[0] + g1_ref[...] * mix
    xo_ref[0] = x
    h = _rms(x, ng_ref[...]) * (1.0 + sc_ref[...]) + sh_ref[...]
    h_ref[0] = h
    logits = jnp.dot(h, wr_ref[...], preferred_element_type=F32,
                     precision=lax.Precision.HIGHEST) + br_ref[...]
    lane = _lane(logits.shape).astype(F32)

    def first_argmax(vals, mask):
        mx = jnp.max(jnp.where(mask, vals, NEG_INF), axis=-1, keepdims=True)
        idx = jnp.min(jnp.where(mask & (vals == mx), lane, float(LANES)), axis=-1, keepdims=True)
        return mx, idx

    gmask = lane < MOE_GROUPS
    g_max, g_idx = first_argmax(logits, gmask)
    g_w = 1.0 / jnp.sum(jnp.exp(jnp.where(gmask, logits - g_max, NEG_INF)), axis=-1, keepdims=True)
    e_lo = MOE_GROUPS + g_idx * MOE_PER_GROUP
    emask = (lane >= e_lo) & (lane < e_lo + MOE_PER_GROUP)
    v1, i1 = first_argmax(logits, emask)
    v2, i2 = first_argmax(logits, emask & (lane != i1))
    t = jnp.exp(v2 - v1)
    w1 = g_w / (1.0 + t)
    w2 = w1 * t
    rt = jnp.where(lane == 0, i1 - MOE_GROUPS,
                   jnp.where(lane == 1, i2 - MOE_GROUPS,
                             jnp.where(lane == 2, w1, jnp.where(lane == 3, w2, 0.0))))
    rt_ref[0] = rt


def _outproj(x, ya, yb, o_f, o_b, rg, gn_g, ym, w_out_p, mod, nbatch, norm_g, w_route, b_route):
    B, S, D = x.shape
    nt = S // TM
    tok = lambda w: pl.BlockSpec((1, TM, w), lambda b, i: (b, i, 0))
    full = lambda a: pl.BlockSpec(a.shape, lambda b, i: (0,) * a.ndim)
    return pl.pallas_call(
        _outproj_kernel,
        grid=(B, nt),
        in_specs=[tok(D), tok(256), tok(256), tok(256), tok(256), tok(256), full(gn_g), tok(256),
                  full(w_out_p),
                  _mod_spec(2, nbatch), _mod_spec(3, nbatch), _mod_spec(4, nbatch), full(norm_g),
                  full(w_route), full(b_route)],
        out_specs=[tok(D), tok(D), tok(LANES)],
        out_shape=[jax.ShapeDtypeStruct((B, S, D), F32), jax.ShapeDtypeStruct((B, S, D), F32),
                   jax.ShapeDtypeStruct((B, S, LANES), F32)],
        compiler_params=_cparams(("parallel", "parallel")),
        name="outproj_router",
    )(x, ya, yb, o_f, o_b, rg, gn_g, ym, w_out_p, mod, mod, mod, norm_g, w_route, b_route)


def _rank_kernel(rt_ref, rank_ref, cnt_ref, carry):
    @pl.when(pl.program_id(0) == 0)
    def _():
        carry[...] = jnp.zeros(carry.shape, F32)

    rt = rt_ref[...]
    lane = _lane(rt.shape)
    lanef = lane.astype(F32)
    oh0 = jnp.where(lanef == rt[:, 0:1], 1.0, 0.0)
    oh1 = jnp.where(lanef == rt[:, 1:2], 1.0, 0.0)
    cnt = oh0 + oh1
    r_io = lax.broadcasted_iota(jnp.int32, (TM, TM), 0)
    c_io = lax.broadcasted_iota(jnp.int32, (TM, TM), 1)
    tri = jnp.where(r_io > c_io, 1.0, 0.0).astype(BF16)
    before = _dot(tri, cnt.astype(BF16)) + carry[...]
    r0 = jnp.sum(oh0 * before, axis=-1, keepdims=True)
    r1 = jnp.sum(oh1 * before, axis=-1, keepdims=True)
    rank_ref[...] = jnp.where(lane == 0, r0, jnp.where(lane == 1, r1, 0.0))
    carry[...] = carry[...] + jnp.sum(cnt, axis=0, keepdims=True)
    cnt_ref[...] = carry[...]


def _ranks(route):
    n = route.shape[0]
    return pl.pallas_call(
        _rank_kernel,
        grid=(n // TM,),
        in_specs=[pl.BlockSpec((TM, LANES), lambda i: (i, 0))],
        out_specs=[pl.BlockSpec((TM, LANES), lambda i: (i, 0)),
                   pl.BlockSpec((1, LANES), lambda i: (0, 0))],
        out_shape=[jax.ShapeDtypeStruct((n, LANES), F32), jax.ShapeDtypeStruct((1, LANES), F32)],
        scratch_shapes=[pltpu.VMEM((1, LANES), F32)],
        compiler_params=_cparams(("arbitrary",)),
        name="moe_rank",
    )(route)


def _dispatch_kernel(dest_ref, h_ref, xb_in_ref, xb_ref, sem):
    del xb_in_ref
    base = pl.program_id(0) * TM

    def copy(r, k):
        d = dest_ref[(base + r) * 2 + k]
        return pltpu.make_async_copy(h_ref.at[pl.ds(r, 1)], xb_ref.at[pl.ds(d, 1)], sem)

    def issue(r, c):
        copy(r, 0).start()
        copy(r, 1).start()
        return c

    def drain(r, c):
        copy(r, 0).wait()
        copy(r, 1).wait()
        return c

    lax.fori_loop(0, TM, issue, 0, unroll=DMA_UNROLL)
    lax.fori_loop(0, TM, drain, 0, unroll=DMA_UNROLL)


def _dispatch(dest, h, cap):
    n, d = h.shape
    return pl.pallas_call(
        _dispatch_kernel,
        grid_spec=pltpu.PrefetchScalarGridSpec(
            num_scalar_prefetch=1,
            grid=(n // TM,),
            in_specs=[pl.BlockSpec((TM, d), lambda i, dest: (i, 0)),
                      pl.BlockSpec(memory_space=pl.ANY)],
            out_specs=pl.BlockSpec(memory_space=pl.ANY),
            scratch_shapes=[pltpu.SemaphoreType.DMA(())]),
        out_shape=jax.ShapeDtypeStruct((cap, d), h.dtype),
        input_output_aliases={2: 0},
        compiler_params=_cparams(("arbitrary",)),
        name="moe_dispatch",
    )(dest, h, jnp.zeros((cap, d), h.dtype))


def _expert_kernel(be_ref, nu_ref, x_ref, wg_ref, wu_ref, wd_ref, y_ref, wg_s, wu_s, wd_s):
    i = pl.program_id(0)

    @pl.when((i == 0) | (be_ref[i] != be_ref[jnp.maximum(i - 1, 0)]))
    def _():
        wg_s[...] = wg_ref[0].astype(BF16)
        wu_s[...] = wu_ref[0].astype(BF16)
        wd_s[...] = wd_ref[0].astype(BF16)

    @pl.when(i < nu_ref[0])
    def _():
        xb = x_ref[...].astype(BF16)
        gate = _dot(xb, wg_s[...])
        up = _dot(xb, wu_s[...])
        hid = (gate * jax.nn.sigmoid(gate) * up).astype(BF16)
        y_ref[...] = _dot(hid, wd_s[...])

    @pl.when(i >= nu_ref[0])
    def _():
        y_ref[...] = jnp.zeros(y_ref.shape, y_ref.dtype)


def _experts(block_e, n_used, xb, w_gate, w_up, w_down, layer):
    cap, d = xb.shape
    hdim = w_gate.shape[-1]
    return pl.pallas_call(
        _expert_kernel,
        grid_spec=pltpu.PrefetchScalarGridSpec(
            num_scalar_prefetch=2,
            grid=(cap // MOE_ROWS,),
            in_specs=[pl.BlockSpec((MOE_ROWS, d), lambda i, be, nu: (i, 0)),
                      pl.BlockSpec((None, 1, d, hdim), lambda i, be, nu: (layer, be[i], 0, 0)),
                      pl.BlockSpec((None, 1, d, hdim), lambda i, be, nu: (layer, be[i], 0, 0)),
                      pl.BlockSpec((None, 1, hdim, d), lambda i, be, nu: (layer, be[i], 0, 0))],
            out_specs=pl.BlockSpec((MOE_ROWS, d), lambda i, be, nu: (i, 0)),
            scratch_shapes=[pltpu.VMEM((d, hdim), BF16), pltpu.VMEM((d, hdim), BF16),
                            pltpu.VMEM((hdim, d), BF16)]),
        out_shape=jax.ShapeDtypeStruct((cap, d), F32),
        compiler_params=_cparams(("arbitrary",)),
        name="moe_experts",
    )(block_e, n_used, xb, w_gate, w_up, w_down)


def _combine_kernel(dest_ref, x_ref, rt_ref, g2_ref, fg_ref, yb_ref, o_ref, gath, sem, *,
                    tile_off, tiles_per_batch, final):
    b, i = pl.program_id(0), pl.program_id(1)
    base = (b * tiles_per_batch + i + tile_off) * TM

    def copy(r, k):
        d = dest_ref[(base + r) * 2 + k]
        return pltpu.make_async_copy(yb_ref.at[pl.ds(d, 1)], gath.at[k, pl.ds(r, 1)], sem)

    def issue(r, c):
        copy(r, 0).start()
        copy(r, 1).start()
        return c

    def drain(r, c):
        copy(r, 0).wait()
        copy(r, 1).wait()
        return c

    lax.fori_loop(0, TM, issue, 0, unroll=DMA_UNROLL)
    lax.fori_loop(0, TM, drain, 0, unroll=DMA_UNROLL)
    rt = rt_ref[0]
    m = rt[:, 2:3] * gath[0] + rt[:, 3:4] * gath[1]
    x = x_ref[0] + g2_ref[...] * m
    if final:
        x = _rms(x, fg_ref[...])
    o_ref[0] = x


def _combine(dest, x, route, mod, nbatch, final_g, yb, final):
    B, S, D = x.shape
    nt = S // TM
    off = 1 if final else 0
    g2_spec = pl.BlockSpec((None, 1, D_MODEL),
                           lambda b, i, dest: (jnp.where(i + off == 0, nbatch, b), 0, 5))
    return pl.pallas_call(
        functools.partial(_combine_kernel, tile_off=off, tiles_per_batch=nt, final=final),
        grid_spec=pltpu.PrefetchScalarGridSpec(
            num_scalar_prefetch=1,
            grid=(B, nt - off),
            in_specs=[pl.BlockSpec((1, TM, D), lambda b, i, dest: (b, i + off, 0)),
                      pl.BlockSpec((1, TM, LANES), lambda b, i, dest: (b, i + off, 0)),
                      g2_spec,
                      pl.BlockSpec((1, D), lambda b, i, dest: (0, 0)),
                      pl.BlockSpec(memory_space=pl.ANY)],
            out_specs=pl.BlockSpec((1, TM, D), lambda b, i, dest: (b, i, 0)),
            scratch_shapes=[pltpu.VMEM((2, TM, D), F32), pltpu.SemaphoreType.DMA(())]),
        out_shape=jax.ShapeDtypeStruct((B, S - off * TM, D), F32),
        compiler_params=_cparams(("arbitrary", "arbitrary")),
        name="moe_combine",
    )(dest, x, route, mod, final_g, yb)


def _rope_tables(seq):
    t = jnp.arange(seq, dtype=jnp.int32)
    row, col = t // GRID_W, t % GRID_W
    lane = jnp.arange(LANES, dtype=jnp.int32)
    tabs = []
    for r in (64, 32):
        nf = r // 4
        inv = 1.0 / (ROPE_BASE ** (jnp.arange(nf, dtype=F32) / nf))
        pos = jnp.where(((lane % r) < 2 * nf)[None, :], row[:, None], col[:, None]).astype(F32)
        ang = pos * inv[lane % nf][None, :]
        first = ((lane % (2 * nf)) < nf)[None, :]
        cos = jnp.cos(ang)
        sin = jnp.where(first, -jnp.sin(ang), jnp.sin(ang))
        tabs.append(jnp.concatenate([jnp.ones((CTX_LEN, LANES), F32), cos], axis=0))
        tabs.append(jnp.concatenate([jnp.zeros((CTX_LEN, LANES), F32), sin], axis=0))
    return tabs


def _layer_weights(w_in, w_out, w_uq, w_ukv, w_group, b_group, w_expert, b_expert):
    perm = jnp.array([0, 2, 1, 3])
    wq = w_in[:, :256].reshape(D_MODEL, 4, 64)[:, perm].reshape(D_MODEL, 256)
    w_in_p = jnp.concatenate([wq, w_in[:, 256:], jnp.zeros((D_MODEL, D_IN_PAD - w_in.shape[1]), F32)],
                             axis=1).astype(BF16)
    wo = jnp.concatenate([w_out[:256].reshape(4, 64, D_MODEL)[perm].reshape(256, D_MODEL), w_out[256:]],
                         axis=0).astype(BF16)
    uq = w_uq.reshape(256, 4, 96)
    blocks = []
    for h in range(4):
        blk = jnp.zeros((256, 256), F32)
        blk = blk.at[:, 64 * (h % 2):64 * (h % 2) + 64].set(uq[:, h, :64])
        blk = blk.at[:, LANES:LANES + 32].set(uq[:, h, 64:])
        blocks.append(blk)
    wuq = jnp.concatenate(blocks, axis=1).astype(BF16)
    ukv = w_ukv.reshape(128, 4, 128)
    wukk = ukv[:, :, :64].reshape(128, 256).astype(BF16)
    wukv = ukv[:, :, 64:].reshape(128, 256).astype(BF16)
    pad = LANES - MOE_GROUPS - MOE_EXPERTS
    w_route = jnp.concatenate([w_group, w_expert, jnp.zeros((D_MODEL, pad), F32)], axis=1)
    b_route = jnp.concatenate([b_group, b_expert, jnp.zeros((pad,), F32)]).reshape(1, LANES)
    return w_in_p, wo, wuq, wukk, wukv, w_route, b_route


def _moe_plan(route, ranks, counts, cap):
    e = route[:, :2].astype(jnp.int32)
    rank = ranks[:, :2].astype(jnp.int32)
    cnt = counts[0, :MOE_EXPERTS].astype(jnp.int32)
    padded = (cnt + MOE_ROWS - 1) // MOE_ROWS * MOE_ROWS
    pend = jnp.cumsum(padded)
    pstart = pend - padded
    dest = (pstart[e] + rank).reshape(-1)
    nblk = cap // MOE_ROWS
    blk_start = jnp.arange(nblk, dtype=jnp.int32) * MOE_ROWS
    block_e = jnp.minimum(jnp.sum((pend[None, :] <= blk_start[:, None]).astype(jnp.int32), axis=1),
                          MOE_EXPERTS - 1)
    n_used = (pend[-1:] // MOE_ROWS).astype(jnp.int32)
    return dest, block_e, n_used


def kernel(x, c, ctx, c_ctx, w_mod, b_mod, norm1_g, norm2_g, w_in, w_out, swa_sink, diff_lambda, diff_subln_g, ret_decay_logit, ret_gn_g, mla_q_norm_g, mla_w_uq, mla_kv_norm_g, mla_w_ukv, moe_w_group, moe_b_group, moe_w_expert, moe_b_expert, moe_w_gate, moe_w_up, moe_w_down, final_norm_g):
    B, T, D = x.shape
    S = CTX_LEN + T
    n_tok = B * S
    tk = min(FLASH_TK, T)
    cap = -(-(2 * n_tok + MOE_EXPERTS * (MOE_ROWS - 1)) // MOE_ROWS) * MOE_ROWS
    xs = jnp.concatenate([ctx, x], axis=1)
    cvec = jnp.zeros((8, D), F32).at[:B].set(c).at[B].set(c_ctx)
    tabs = _rope_tables(T)
    row = lambda v: v.reshape(1, -1)

    for l in range(DEPTH):
        lambda_init = 0.8 - 0.6 * math.exp(-0.3 * l)
        w_in_p, wo, wuq, wukk, wukv, w_route, b_route = _layer_weights(
            w_in[l], w_out[l], mla_w_uq[l], mla_w_ukv[l], moe_w_group[l], moe_b_group[l],
            moe_w_expert[l], moe_b_expert[l])
        mod = _modulation(cvec, w_mod[l], b_mod[l]).reshape(8, 1, 6 * D)
        (swaq, swak, swav, dq, dk, dv, rq, rk, rv, rg, mq, mk, mv) = _inproj(
            xs, mod, B, row(norm1_g[l]), w_in_p, tabs, row(mla_q_norm_g[l]), wuq,
            row(mla_kv_norm_g[l]), wukk, wukv)
        sink_p = swa_sink[l]
        ya = _swa(sink_p, swaq, swak, swav)
        yb = _diff(diff_lambda[l], row(jnp.tile(diff_subln_g[l], 2)), dq, dk, dv, lambda_init, tk)
        dl_lane = jnp.repeat(ret_decay_logit[l], 64, axis=-1).reshape(2, 2, 1, LANES)
        o_f, o_b = _retention(dl_lane, rq, rk, rv)
        ym = _mla(mq, mk, mv, tk)
        xs, h2, route = _outproj(xs, ya, yb, o_f, o_b, rg, row(ret_gn_g[l]), ym, wo, mod, B,
                                 row(norm2_g[l]), w_route, b_route)
        route2 = route.reshape(n_tok, LANES)
        ranks, counts = _ranks(route2)
        dest, block_e, n_used = _moe_plan(route2, ranks, counts, cap)
        xb = _dispatch(dest, h2.reshape(n_tok, D), cap)
        yb_e = _experts(block_e, n_used, xb, moe_w_gate, moe_w_up, moe_w_down, l)
        final = l == DEPTH - 1
        xs = _combine(dest, xs, route, mod, B, row(final_norm_g), yb_e, final)
    return xs
```

```python
import functools
import math

import jax
import jax.numpy as jnp
from jax import lax
from jax.experimental import pallas as pl
from jax.experimental.pallas import tpu as pltpu

F32 = jnp.float32
BF16 = jnp.bfloat16

D_MODEL = 1024
CTX_LEN = 256
GRID_W = 64
ROPE_BASE = 10000.0
NORM_EPS = 1e-6
NEG_INF = -1e30
DEPTH = 2

SWA_WINDOW = 128
RET_CHUNK = 128
MLA_SCALE = (64 + 32) ** -0.5
LOG2E = math.log2(math.e)
MOE_GROUPS = 4
MOE_PER_GROUP = 8
MOE_EXPERTS = 32
MOE_HIDDEN = 512

LANES = 128
TM = 256
TQ = 2048
FLASH_TK = 1024
VT_ROWS = 80
GAP_LIMIT = 64.0
MOE_ROWS = 256
DMA_UNROLL = 8
VMEM_LIMIT = 56 * 1024 * 1024

C_SWA_Q, C_SWA_K, C_SWA_V = 0, 256, 384
C_DIF_Q, C_DIF_K, C_DIF_V = 512, 768, 1024
C_RET_Q, C_RET_K, C_RET_V, C_RET_G = 1280, 1536, 1792, 2048
C_MLA_Q, C_MLA_KV, C_MLA_KR = 2304, 2560, 2688
D_IN_PAD = 2816


def _cparams(sem):
    return pltpu.CompilerParams(dimension_semantics=sem, vmem_limit_bytes=VMEM_LIMIT)


def _dot(a, b):
    return jnp.dot(a, b, preferred_element_type=F32)


def _dot_nt(a, b):
    return lax.dot_general(a, b, (((1,), (1,)), ((), ())), preferred_element_type=F32)


def _dot_tn(a, b):
    return lax.dot_general(a, b, (((0,), (0,)), ((), ())), preferred_element_type=F32)


def _rms(x, g):
    ms = jnp.mean(x * x, axis=-1, keepdims=True)
    return x * lax.rsqrt(ms + NORM_EPS) * g


def _lane(shape):
    return lax.broadcasted_iota(jnp.int32, shape, len(shape) - 1)


def _rope(z, cos, sin, nf):
    first = (_lane(z.shape) % (2 * nf)) < nf
    partner = jnp.where(first, pltpu.roll(z, LANES - nf, 1), pltpu.roll(z, nf, 1))
    return z * cos + partner * sin


def _mod_kernel(a_ref, w_ref, b_ref, o_ref):
    a = a_ref[...]
    act = a * jax.nn.sigmoid(a)
    o_ref[...] = jnp.dot(act, w_ref[...], preferred_element_type=F32,
                         precision=lax.Precision.HIGHEST) + b_ref[...]


def _modulation(cvec, w_mod, b_mod):
    n = w_mod.shape[1]
    bn = 512
    return pl.pallas_call(
        _mod_kernel,
        grid=(n // bn,),
        in_specs=[pl.BlockSpec((8, D_MODEL), lambda j: (0, 0)),
                  pl.BlockSpec((D_MODEL, bn), lambda j: (0, j)),
                  pl.BlockSpec((1, bn), lambda j: (0, j))],
        out_specs=pl.BlockSpec((8, bn), lambda j: (0, j)),
        out_shape=jax.ShapeDtypeStruct((8, n), F32),
        compiler_params=_cparams(("parallel",)),
        name="modulation",
    )(cvec, w_mod, b_mod.reshape(1, n))


def _mod_spec(chunk, nbatch):
    return pl.BlockSpec((None, 1, D_MODEL),
                        lambda b, i: (jnp.where(i == 0, nbatch, b), 0, chunk))


def _inproj_kernel(x_ref, g_ref, sh_ref, sc_ref, w_ref, c64_ref, s64_ref, c32_ref, s32_ref,
                   qng_ref, wuq_ref, kvng_ref, wukk_ref, wukv_ref,
                   swaq, swak, swav, dq, dk, dv, rq, rk, rv, rg, mq, mk, mv):
    x = x_ref[0]
    h = _rms(x, g_ref[...]) * (1.0 + sc_ref[...]) + sh_ref[...]
    hb = h.astype(BF16)
    c64, s64, c32, s32 = c64_ref[...], s64_ref[...], c32_ref[...], s32_ref[...]

    def proj(c0, width):
        return _dot(hb, w_ref[:, c0:c0 + width])

    def rope_blocks(z, out, nf, scale):
        cos, sin = (c64, s64) if nf == 16 else (c32, s32)
        for j in range(z.shape[1] // LANES):
            zz = _rope(z[:, j * LANES:(j + 1) * LANES], cos, sin, nf)
            if scale != 1.0:
                zz = zz * scale
            out[0, :, j * LANES:(j + 1) * LANES] = zz.astype(out.dtype)

    rope_blocks(proj(C_SWA_Q, 256), swaq, 16, 0.125)
    rope_blocks(proj(C_SWA_K, 128), swak, 16, 1.0)
    swav[0] = proj(C_SWA_V, 128).astype(BF16)
    rope_blocks(proj(C_DIF_Q, 256), dq, 8, 32 ** -0.5 * LOG2E)
    rope_blocks(proj(C_DIF_K, 256), dk, 8, 1.0)
    dv[0] = proj(C_DIF_V, 256).astype(BF16)
    rope_blocks(proj(C_RET_Q, 256), rq, 16, 0.125)
    rope_blocks(proj(C_RET_K, 256), rk, 16, 1.0)
    rv[0] = proj(C_RET_V, 256).astype(BF16)
    rg[0] = proj(C_RET_G, 256).astype(BF16)

    ql = _rms(proj(C_MLA_Q, 256), qng_ref[...]).astype(BF16)
    qa = _dot(ql, wuq_ref[...]) * (MLA_SCALE * LOG2E)
    for hh in range(4):
        mq[0, :, hh * 256:hh * 256 + LANES] = qa[:, hh * 256:hh * 256 + LANES].astype(BF16)
        zr = _rope(qa[:, hh * 256 + LANES:(hh + 1) * 256], c32, s32, 8)
        mq[0, :, hh * 256 + LANES:(hh + 1) * 256] = zr.astype(BF16)
    kvl = _rms(proj(C_MLA_KV, 128), kvng_ref[...]).astype(BF16)
    kn = _dot(kvl, wukk_ref[...])
    kr = _rope(proj(C_MLA_KR, 128), c32, s32, 8).astype(BF16)
    for p in range(2):
        mk[0, :, p * 256:p * 256 + LANES] = kn[:, p * LANES:(p + 1) * LANES].astype(BF16)
        mk[0, :, p * 256 + LANES:(p + 1) * 256] = kr
    mv[0] = _dot(kvl, wukv_ref[...]).astype(BF16)


def _inproj(x, mod, nbatch, norm_g, w_in_p, tabs, qng, wuq, kvng, wukk, wukv):
    B, S, D = x.shape
    nt = S // TM
    tok = lambda w: pl.BlockSpec((1, TM, w), lambda b, i: (b, i, 0))
    full = lambda a: pl.BlockSpec(a.shape, lambda b, i: (0,) * a.ndim)
    tab = pl.BlockSpec((TM, LANES), lambda b, i: (i, 0))
    widths = [256, 128, 128, 256, 256, 256, 256, 256, 256, 256, 1024, 512, 256]
    return pl.pallas_call(
        _inproj_kernel,
        grid=(B, nt),
        in_specs=[tok(D), full(norm_g), _mod_spec(0, nbatch), _mod_spec(1, nbatch), full(w_in_p),
                  tab, tab, tab, tab, full(qng), full(wuq), full(kvng), full(wukk), full(wukv)],
        out_specs=[tok(w) for w in widths],
        out_shape=[jax.ShapeDtypeStruct((B, S, w), BF16) for w in widths],
        compiler_params=_cparams(("parallel", "parallel")),
        name="inproj",
    )(x, norm_g, mod, mod, w_in_p, *tabs, qng, wuq, kvng, wukk, wukv)


def _swa_kernel(sink_ref, q_ref, kp_ref, ko_ref, kn_ref, vp_ref, vo_ref, vn_ref, kc_ref, vc_ref,
                o_ref, *, n_ctx_tiles, seq):
    i = pl.program_id(1)
    blk = RET_CHUNK
    n = i - n_ctx_tiles
    k_all = jnp.concatenate([kp_ref[0], ko_ref[0], kn_ref[0], kc_ref[0]], axis=0)
    v_all = jnp.concatenate([vp_ref[0], vo_ref[0], vn_ref[0], vc_ref[0]], axis=0)
    nk = 3 * blk + CTX_LEN
    r_io = lax.broadcasted_iota(jnp.int32, (blk, nk), 0)
    c_io = lax.broadcasted_iota(jnp.int32, (blk, nk), 1)
    kpos = (n - 1) * blk + c_io
    in_band = ((c_io >= r_io) & (c_io <= r_io + 2 * SWA_WINDOW) & (kpos >= 0) & (kpos < seq)
               & (n >= 0))
    valid = in_band | (c_io >= 3 * blk)
    lane = _lane((blk, LANES))
    heads = [(r, g) for r in range(2) for g in range(2)]
    qm = [jnp.where((lane >= 64 * g) & (lane < 64 * (g + 1)), q_ref[0, :, r * LANES:(r + 1) * LANES],
                    jnp.zeros((blk, LANES), BF16)) for r, g in heads]
    s = [jnp.where(valid, _dot_nt(q, k_all), NEG_INF) for q in qm]
    sink = [sink_ref[2 * g + r] for r, g in heads]
    m = [jnp.maximum(jnp.max(s[h], axis=-1, keepdims=True), sink[h]) for h in range(4)]
    p = [jnp.exp(s[h] - m[h]) for h in range(4)]
    l = [jnp.sum(p[h], axis=-1, keepdims=True) + jnp.exp(sink[h] - m[h]) for h in range(4)]
    o = [_dot(p[h].astype(BF16), v_all) / l[h] for h in range(4)]
    for r in range(2):
        o_ref[0, :, r * LANES:(r + 1) * LANES] = jnp.where(lane < 64, o[2 * r], o[2 * r + 1]).astype(BF16)


def _swa(sink, q, k, v):
    B, S, _ = q.shape
    blk = RET_CHUNK
    nt = S // blk
    nct = CTX_LEN // blk
    kvs = lambda f: pl.BlockSpec((1, blk, LANES), f)
    prev = lambda b, i: (b, jnp.maximum(i - 1, 0), 0)
    own = lambda b, i: (b, i, 0)
    nxt = lambda b, i: (b, jnp.minimum(i + 1, nt - 1), 0)
    ctx = pl.BlockSpec((1, CTX_LEN, LANES), lambda b, i: (b, 0, 0))
    return pl.pallas_call(
        functools.partial(_swa_kernel, n_ctx_tiles=nct, seq=S - CTX_LEN),
        grid=(B, nt),
        in_specs=[pl.BlockSpec(memory_space=pltpu.SMEM),
                  pl.BlockSpec((1, blk, 256), own),
                  kvs(prev), kvs(own), kvs(nxt), kvs(prev), kvs(own), kvs(nxt), ctx, ctx],
        out_specs=pl.BlockSpec((1, blk, 256), own),
        out_shape=jax.ShapeDtypeStruct((B, S, 256), BF16),
        compiler_params=_cparams(("parallel", "parallel")),
        name="swa",
    )(sink, q, k, k, k, v, v, v, k, v)


def _flash(q_maps, v_blk, k_ref, v_ref, m_scr, acc_scr, gap_scr, *, n_lat, tk):
    nm = len(q_maps)

    def chunk(start, size, mode):
        kc = k_ref[0, pl.ds(start, size), :]
        for a in range(nm):
            vt = v_ref[0, v_blk[a] * VT_ROWS:(v_blk[a] + 1) * VT_ROWS, pl.ds(start, size)]
            st = _dot_nt(kc, q_maps[a])
            m_prev = None if mode == "first" else m_scr[a]
            if mode == "fast":
                pf = jnp.exp2(st - m_prev)
                rise = jnp.maximum(jnp.max(pf, axis=0, keepdims=True), 1.0)
                acc_scr[a] = (acc_scr[a] + _dot(vt, pf.astype(BF16))) * (1.0 / rise)
                m_scr[a] = m_prev + jnp.log2(rise)
                gap_scr[a] = jnp.maximum(gap_scr[a], rise)
                continue
            cm = jnp.max(st, axis=0, keepdims=True)
            if mode == "first":
                acc_scr[a] = _dot(vt, jnp.exp2(st - cm).astype(BF16))
                m_scr[a] = cm
                gap_scr[a] = jnp.ones(cm.shape, F32)
            else:
                m_new = jnp.maximum(m_prev, cm)
                p = jnp.exp2(st - m_new).astype(BF16)
                acc_scr[a] = acc_scr[a] * jnp.exp2(m_prev - m_new) + _dot(vt, p)
                m_scr[a] = m_new

    def sweep(mode):
        chunk(0, CTX_LEN, "first")
        if n_lat:

            def body(c, carry):
                chunk(pl.multiple_of(CTX_LEN + c * tk, LANES), tk, mode)
                return carry

            lax.fori_loop(0, n_lat, body, 0)

    sweep("fast")
    if n_lat:
        worst = jnp.max(jnp.concatenate([gap_scr[a] for a in range(nm)], axis=0))

        @pl.when(worst > 2.0 ** GAP_LIMIT)
        def _():
            sweep("exact")

    return [acc_scr[a, 0:64, :] / acc_scr[a, 64:65, :] for a in range(nm)]


def _attention_calls(kern, name, nm, q, k, v, qw, kw, tk, extra_in=(), extra_specs=()):
    B, S, _ = q.shape
    T = S - CTX_LEN

    def call(tq, n_tiles, n_lat, q_spec, kv_rows, call_name):
        return pl.pallas_call(
            functools.partial(kern, n_lat=n_lat, tk=tk),
            grid=(B, 2, n_tiles),
            in_specs=list(extra_specs) + [
                q_spec,
                pl.BlockSpec((1, kv_rows, kw), lambda b, p, i: (b, 0, p)),
                pl.BlockSpec((1, 2 * VT_ROWS, kv_rows), lambda b, p, i: (b, p, 0))],
            out_specs=pl.BlockSpec((1, tq, LANES), lambda b, p, i: (b, i, p)),
            out_shape=jax.ShapeDtypeStruct((B, tq * n_tiles, 256), BF16),
            scratch_shapes=[pltpu.VMEM((nm, 1, tq), F32), pltpu.VMEM((nm, VT_ROWS, tq), F32),
                            pltpu.VMEM((nm, 1, tq), F32)],
            compiler_params=_cparams(("parallel", "parallel", "parallel")),
            name=call_name,
        )(*extra_in, q, k, vt)

    vt = jnp.concatenate([jnp.swapaxes(v.reshape(B, S, 4, 64), 1, 3).swapaxes(1, 2),
                          jnp.ones((B, 4, VT_ROWS - 64, S), v.dtype)], axis=2).reshape(B, 4 * VT_ROWS, S)
    y_ctx = call(CTX_LEN, 1, 0, pl.BlockSpec((1, CTX_LEN, qw), lambda b, p, i: (b, 0, p)),
                 CTX_LEN, name + "_ctx")
    tq = min(TQ, T)
    y_lat = call(tq, T // tq, T // tk,
                 pl.BlockSpec((pl.Element(1), pl.Element(tq), pl.Element(qw)),
                              lambda b, p, i: (b, (i * (tq // CTX_LEN) + 1) * CTX_LEN, p * qw)),
                 S, name)
    return jnp.concatenate([y_ctx, y_lat], axis=1)


def _diff_attn(lam_ref, g_ref, q_ref, k_ref, v_ref, o_ref, m_scr, acc_scr, gap_scr, *,
               n_lat, tk, lambda_init):
    qb = q_ref[0]
    lane = _lane(qb.shape)
    q_maps = []
    for a in range(4):
        msk = (lane >= 32 * a) & (lane < 32 * (a + 1))
        q_maps.append(jnp.where(msk, qb, jnp.zeros_like(qb)))
    n = _flash(q_maps, (0, 0, 1, 1), k_ref, v_ref, m_scr, acc_scr, gap_scr, n_lat=n_lat, tk=tk)
    lp = lam_ref[...]
    lam = (jnp.exp(jnp.sum(lp[0:1] * lp[1:2], axis=-1, keepdims=True))
           - jnp.exp(jnp.sum(lp[2:3] * lp[3:4], axis=-1, keepdims=True)) + lambda_init)
    ys = []
    for hh in range(2):
        o = n[2 * hh] - lam * n[2 * hh + 1]
        ms = jnp.mean(o * o, axis=0, keepdims=True)
        ys.append((o * lax.rsqrt(ms + NORM_EPS) * g_ref[...] * (1.0 - lambda_init)).T)
    o_ref[0] = jnp.concatenate(ys, axis=1).astype(BF16)


def _diff(lam_p, subln_g2, q, k, v, lambda_init, tk):
    return _attention_calls(
        functools.partial(_diff_attn, lambda_init=lambda_init), "diff_attn", 4, q, k, v, LANES, LANES,
        tk, extra_in=(lam_p, subln_g2),
        extra_specs=(pl.BlockSpec(lam_p.shape, lambda b, p, i: (0, 0)),
                     pl.BlockSpec((64, 1), lambda b, p, i: (0, 0))))


def _mla_attn(q_ref, k_ref, v_ref, o_ref, m_scr, acc_scr, gap_scr, *, n_lat, tk):
    q_maps = [q_ref[0, :, 0:256], q_ref[0, :, 256:512]]
    n = _flash(q_maps, (0, 1), k_ref, v_ref, m_scr, acc_scr, gap_scr, n_lat=n_lat, tk=tk)
    o_ref[0] = jnp.concatenate([n[0].T, n[1].T], axis=1).astype(BF16)


def _mla(q, k, v, tk):
    return _attention_calls(_mla_attn, "mla_attn", 2, q, k, v, 512, 256, tk)


def _ret_tables(dl, backward):
    L = RET_CHUNK
    lg = -(jnp.maximum(-dl, 0.0) + jnp.log(1.0 + jnp.exp(-jnp.abs(dl))))
    row = lax.broadcasted_iota(jnp.int32, (L, LANES), 0).astype(F32)
    col = lax.broadcasted_iota(jnp.int32, (L, LANES), 1)
    colf = col.astype(F32)
    if backward:
        qd, kd, rel = jnp.exp((L - row) * lg), jnp.exp(row * lg), colf - row
    else:
        qd, kd, rel = jnp.exp((row + 1.0) * lg), jnp.exp((L - 1.0 - row) * lg), row - colf
    decs = []
    for hh in range(2):
        lg_h = jnp.sum(jnp.where(col[0:1] == 64 * hh, lg, 0.0), axis=-1, keepdims=True)
        decs.append(jnp.where(rel >= 0, jnp.exp(jnp.maximum(rel, 0.0) * lg_h), 0.0))
    return qd, kd, decs[0], decs[1], jnp.exp(float(L) * lg)


def _ret_chunk(q, k, v, tab, cd, s_ref):
    col = _lane((RET_CHUNK, LANES))
    qf, kf = q.astype(F32), k.astype(F32)
    o = _dot((qf * tab[0]).astype(BF16), s_ref[...].astype(BF16))
    outs = []
    for hh in range(2):
        hmask = (col >= 64 * hh) & (col < 64 * (hh + 1))
        a = _dot_nt(jnp.where(hmask, q, jnp.zeros_like(q)), k) * tab[2 + hh]
        outs.append(_dot(a.astype(BF16), v))
    o = o + jnp.where(col < 64, outs[0], outs[1])
    rowi = lax.broadcasted_iota(jnp.int32, (LANES, LANES), 0)
    coli = lax.broadcasted_iota(jnp.int32, (LANES, LANES), 1)
    s_new = s_ref[...] * cd + _dot_tn((kf * tab[1]).astype(BF16), v)
    s_ref[...] = jnp.where((rowi < 64) == (coli < 64), s_new, 0.0)
    return o


def _ret_kernel(dl_ref, qf_ref, kf_ref, vf_ref, qb_ref, kb_ref, vb_ref, of_ref, ob_ref,
                s_scr, tab_scr, cd_scr):
    nb = qf_ref.shape[0]

    @pl.when(pl.program_id(0) == 0)
    def _():
        s_scr[...] = jnp.zeros(s_scr.shape, F32)
        for d in range(2):
            for p in range(2):
                tabs = _ret_tables(dl_ref[d, p], backward=(d == 1))
                for t in range(4):
                    tab_scr[d, p, t] = tabs[t]
                cd_scr[d, p] = tabs[4]

    for d, (q_ref, k_ref, v_ref, o_ref) in enumerate(((qf_ref, kf_ref, vf_ref, of_ref),
                                                      (qb_ref, kb_ref, vb_ref, ob_ref))):
        for p in range(2):
            tab = [tab_scr[d, p, t] for t in range(4)]
            cd = cd_scr[d, p]
            ln = slice(p * LANES, (p + 1) * LANES)
            for b in range(nb):
                o_ref[b, :, ln] = _ret_chunk(q_ref[b, :, ln], k_ref[b, :, ln], v_ref[b, :, ln],
                                             tab, cd, s_scr.at[d, b, p])


def _retention(dl_lane, q, k, v):
    B, S, _ = q.shape
    L = RET_CHUNK
    nc = S // L
    ncc = CTX_LEN // L
    fwd = pl.BlockSpec((B, L, 256), lambda s: (0, s, 0))
    bwd = pl.BlockSpec((B, L, 256), lambda s: (0, jnp.where(s < ncc, ncc - 1 - s, nc - 1 + ncc - s), 0))
    return pl.pallas_call(
        _ret_kernel,
        grid=(nc,),
        in_specs=[pl.BlockSpec(dl_lane.shape, lambda s: (0, 0, 0, 0)), fwd, fwd, fwd, bwd, bwd, bwd],
        out_specs=[fwd, bwd],
        out_shape=[jax.ShapeDtypeStruct((B, S, 256), F32)] * 2,
        scratch_shapes=[pltpu.VMEM((2, B, 2, LANES, LANES), F32),
                        pltpu.VMEM((2, 2, 4, L, LANES), F32), pltpu.VMEM((2, 2, 1, LANES), F32)],
        compiler_params=_cparams(("arbitrary",)),
        name="retention",
    )(dl_lane, q, k, v, q, k, v)


def _gated_group_norm(o, g, gn):
    lo = _lane(o.shape) < 64

    def halves(t):
        a = jnp.sum(jnp.where(lo, t, 0.0), axis=-1, keepdims=True) * (1.0 / 64)
        b = jnp.sum(jnp.where(lo, 0.0, t), axis=-1, keepdims=True) * (1.0 / 64)
        return jnp.where(lo, a, b)

    d = o - halves(o)
    var = halves(d * d)
    return d * lax.rsqrt(var + NORM_EPS) * gn * (g * jax.nn.sigmoid(g))


def _outproj_kernel(x_ref, ya_ref, yb_ref, of_ref, ob_ref, rg_ref, gn_ref, ym_ref, w_ref, g1_ref,
                    sh_ref, sc_ref, ng_ref, wr_ref, br_ref, xo_ref, h_ref, rt_ref):
    mix = (_dot(ya_ref[0], w_ref[0:256]) + _dot(yb_ref[0], w_ref[256:512])
           + _dot(ym_ref[0], w_ref[768:1024]))
    for j in range(2):
        ln = slice(j * LANES, (j + 1) * LANES)
        yr = _gated_group_norm(of_ref[0, :, ln] + ob_ref[0, :, ln], rg_ref[0, :, ln].astype(F32),
                               gn_ref[:, ln])
        mix = mix + _dot(yr.astype(BF16), w_ref[512 + j * LANES:512 + (j + 1) * LANES])
    x = x_ref[0] + g1_ref[...] * mix
    xo_ref[0] = x
    h = _rms(x, ng_ref[...]) * (1.0 + sc_ref[...]) + sh_ref[...]
    h_ref[0] = h
    logits = jnp.dot(h, wr_ref[...], preferred_element_type=F32,
                     precision=lax.Precision.HIGHEST) + br_ref[...]
    lane = _lane(logits.shape).astype(F32)

    def first_argmax(vals, mask):
        mx = jnp.max(jnp.where(mask, vals, NEG_INF), axis=-1, keepdims=True)
        idx = jnp.min(jnp.where(mask & (vals == mx), lane, float(LANES)), axis=-1, keepdims=True)
        return mx, idx

    gmask = lane < MOE_GROUPS
    g_max, g_idx = first_argmax(logits, gmask)
    g_w = 1.0 / jnp.sum(jnp.exp(jnp.where(gmask, logits - g_max, NEG_INF)), axis=-1, keepdims=True)
    e_lo = MOE_GROUPS + g_idx * MOE_PER_GROUP
    emask = (lane >= e_lo) & (lane < e_lo + MOE_PER_GROUP)
    v1, i1 = first_argmax(logits, emask)
    v2, i2 = first_argmax(logits, emask & (lane != i1))
    t = jnp.exp(v2 - v1)
    w1 = g_w / (1.0 + t)
    w2 = w1 * t
    rt = jnp.where(lane == 0, i1 - MOE_GROUPS,
                   jnp.where(lane == 1, i2 - MOE_GROUPS,
                             jnp.where(lane == 2, w1, jnp.where(lane == 3, w2, 0.0))))
    rt_ref[0] = rt


def _outproj(x, ya, yb, o_f, o_b, rg, gn_g, ym, w_out_p, mod, nbatch, norm_g, w_route, b_route):
    B, S, D = x.shape
    nt = S // TM
    tok = lambda w: pl.BlockSpec((1, TM, w), lambda b, i: (b, i, 0))
    full = lambda a: pl.BlockSpec(a.shape, lambda b, i: (0,) * a.ndim)
    return pl.pallas_call(
        _outproj_kernel,
        grid=(B, nt),
        in_specs=[tok(D), tok(256), tok(256), tok(256), tok(256), tok(256), full(gn_g), tok(256),
                  full(w_out_p),
                  _mod_spec(2, nbatch), _mod_spec(3, nbatch), _mod_spec(4, nbatch), full(norm_g),
                  full(w_route), full(b_route)],
        out_specs=[tok(D), tok(D), tok(LANES)],
        out_shape=[jax.ShapeDtypeStruct((B, S, D), F32), jax.ShapeDtypeStruct((B, S, D), F32),
                   jax.ShapeDtypeStruct((B, S, LANES), F32)],
        compiler_params=_cparams(("parallel", "parallel")),
        name="outproj_router",
    )(x, ya, yb, o_f, o_b, rg, gn_g, ym, w_out_p, mod, mod, mod, norm_g, w_route, b_route)


def _rank_kernel(rt_ref, rank_ref, cnt_ref, carry):
    @pl.when(pl.program_id(0) == 0)
    def _():
        carry[...] = jnp.zeros(carry.shape, F32)

    rt = rt_ref[...]
    lane = _lane(rt.shape)
    lanef = lane.astype(F32)
    oh0 = jnp.where(lanef == rt[:, 0:1], 1.0, 0.0)
    oh1 = jnp.where(lanef == rt[:, 1:2], 1.0, 0.0)
    cnt = oh0 + oh1
    r_io = lax.broadcasted_iota(jnp.int32, (TM, TM), 0)
    c_io = lax.broadcasted_iota(jnp.int32, (TM, TM), 1)
    tri = jnp.where(r_io > c_io, 1.0, 0.0).astype(BF16)
    before = _dot(tri, cnt.astype(BF16)) + carry[...]
    r0 = jnp.sum(oh0 * before, axis=-1, keepdims=True)
    r1 = jnp.sum(oh1 * before, axis=-1, keepdims=True)
    rank_ref[...] = jnp.where(lane == 0, r0, jnp.where(lane == 1, r1, 0.0))
    carry[...] = carry[...] + jnp.sum(cnt, axis=0, keepdims=True)
    cnt_ref[...] = carry[...]


def _ranks(route):
    n = route.shape[0]
    return pl.pallas_call(
        _rank_kernel,
        grid=(n // TM,),
        in_specs=[pl.BlockSpec((TM, LANES), lambda i: (i, 0))],
        out_specs=[pl.BlockSpec((TM, LANES), lambda i: (i, 0)),
                   pl.BlockSpec((1, LANES), lambda i: (0, 0))],
        out_shape=[jax.ShapeDtypeStruct((n, LANES), F32), jax.ShapeDtypeStruct((1, LANES), F32)],
        scratch_shapes=[pltpu.VMEM((1, LANES), F32)],
        compiler_params=_cparams(("arbitrary",)),
        name="moe_rank",
    )(route)


def _dispatch_kernel(dest_ref, h_ref, xb_in_ref, xb_ref, sem):
    del xb_in_ref
    base = pl.program_id(0) * TM

    def copy(r, k):
        d = dest_ref[(base + r) * 2 + k]
        return pltpu.make_async_copy(h_ref.at[pl.ds(r, 1)], xb_ref.at[pl.ds(d, 1)], sem)

    def issue(r, c):
        copy(r, 0).start()
        copy(r, 1).start()
        return c

    def drain(r, c):
        copy(r, 0).wait()
        copy(r, 1).wait()
        return c

    lax.fori_loop(0, TM, issue, 0, unroll=DMA_UNROLL)
    lax.fori_loop(0, TM, drain, 0, unroll=DMA_UNROLL)


def _dispatch(dest, h, cap):
    n, d = h.shape
    return pl.pallas_call(
        _dispatch_kernel,
        grid_spec=pltpu.PrefetchScalarGridSpec(
            num_scalar_prefetch=1,
            grid=(n // TM,),
            in_specs=[pl.BlockSpec((TM, d), lambda i, dest: (i, 0)),
                      pl.BlockSpec(memory_space=pl.ANY)],
            out_specs=pl.BlockSpec(memory_space=pl.ANY),
            scratch_shapes=[pltpu.SemaphoreType.DMA(())]),
        out_shape=jax.ShapeDtypeStruct((cap, d), h.dtype),
        input_output_aliases={2: 0},
        compiler_params=_cparams(("arbitrary",)),
        name="moe_dispatch",
    )(dest, h, jnp.zeros((cap, d), h.dtype))


def _expert_kernel(be_ref, nu_ref, x_ref, wg_ref, wu_ref, wd_ref, y_ref, wg_s, wu_s, wd_s):
    i = pl.program_id(0)

    @pl.when((i == 0) | (be_ref[i] != be_ref[jnp.maximum(i - 1, 0)]))
    def _():
        wg_s[...] = wg_ref[0].astype(BF16)
        wu_s[...] = wu_ref[0].astype(BF16)
        wd_s[...] = wd_ref[0].astype(BF16)

    @pl.when(i < nu_ref[0])
    def _():
        xb = x_ref[...].astype(BF16)
        gate = _dot(xb, wg_s[...])
        up = _dot(xb, wu_s[...])
        hid = (gate * jax.nn.sigmoid(gate) * up).astype(BF16)
        y_ref[...] = _dot(hid, wd_s[...])

    @pl.when(i >= nu_ref[0])
    def _():
        y_ref[...] = jnp.zeros(y_ref.shape, y_ref.dtype)


def _experts(block_e, n_used, xb, w_gate, w_up, w_down, layer):
    cap, d = xb.shape
    hdim = w_gate.shape[-1]
    return pl.pallas_call(
        _expert_kernel,
        grid_spec=pltpu.PrefetchScalarGridSpec(
            num_scalar_prefetch=2,
            grid=(cap // MOE_ROWS,),
            in_specs=[pl.BlockSpec((MOE_ROWS, d), lambda i, be, nu: (i, 0)),
                      pl.BlockSpec((None, 1, d, hdim), lambda i, be, nu: (layer, be[i], 0, 0)),
                      pl.BlockSpec((None, 1, d, hdim), lambda i, be, nu: (layer, be[i], 0, 0)),
                      pl.BlockSpec((None, 1, hdim, d), lambda i, be, nu: (layer, be[i], 0, 0))],
            out_specs=pl.BlockSpec((MOE_ROWS, d), lambda i, be, nu: (i, 0)),
            scratch_shapes=[pltpu.VMEM((d, hdim), BF16), pltpu.VMEM((d, hdim), BF16),
                            pltpu.VMEM((hdim, d), BF16)]),
        out_shape=jax.ShapeDtypeStruct((cap, d), F32),
        compiler_params=_cparams(("arbitrary",)),
        name="moe_experts",
    )(block_e, n_used, xb, w_gate, w_up, w_down)


def _combine_kernel(dest_ref, x_ref, rt_ref, g2_ref, fg_ref, yb_ref, o_ref, gath, sem, *,
                    tile_off, tiles_per_batch, final):
    b, i = pl.program_id(0), pl.program_id(1)
    base = (b * tiles_per_batch + i + tile_off) * TM

    def copy(r, k):
        d = dest_ref[(base + r) * 2 + k]
        return pltpu.make_async_copy(yb_ref.at[pl.ds(d, 1)], gath.at[k, pl.ds(r, 1)], sem)

    def issue(r, c):
        copy(r, 0).start()
        copy(r, 1).start()
        return c

    def drain(r, c):
        copy(r, 0).wait()
        copy(r, 1).wait()
        return c

    lax.fori_loop(0, TM, issue, 0, unroll=DMA_UNROLL)
    lax.fori_loop(0, TM, drain, 0, unroll=DMA_UNROLL)
    rt = rt_ref[0]
    m = rt[:, 2:3] * gath[0] + rt[:, 3:4] * gath[1]
    x = x_ref[0] + g2_ref[...] * m
    if final:
        x = _rms(x, fg_ref[...])
    o_ref[0] = x


def _combine(dest, x, route, mod, nbatch, final_g, yb, final):
    B, S, D = x.shape
    nt = S // TM
    off = 1 if final else 0
    g2_spec = pl.BlockSpec((None, 1, D_MODEL),
                           lambda b, i, dest: (jnp.where(i + off == 0, nbatch, b), 0, 5))
    return pl.pallas_call(
        functools.partial(_combine_kernel, tile_off=off, tiles_per_batch=nt, final=final),
        grid_spec=pltpu.PrefetchScalarGridSpec(
            num_scalar_prefetch=1,
            grid=(B, nt - off),
            in_specs=[pl.BlockSpec((1, TM, D), lambda b, i, dest: (b, i + off, 0)),
                      pl.BlockSpec((1, TM, LANES), lambda b, i, dest: (b, i + off, 0)),
                      g2_spec,
                      pl.BlockSpec((1, D), lambda b, i, dest: (0, 0)),
                      pl.BlockSpec(memory_space=pl.ANY)],
            out_specs=pl.BlockSpec((1, TM, D), lambda b, i, dest: (b, i, 0)),
            scratch_shapes=[pltpu.VMEM((2, TM, D), F32), pltpu.SemaphoreType.DMA(())]),
        out_shape=jax.ShapeDtypeStruct((B, S - off * TM, D), F32),
        compiler_params=_cparams(("arbitrary", "arbitrary")),
        name="moe_combine",
    )(dest, x, route, mod, final_g, yb)


def _rope_tables(seq):
    t = jnp.arange(seq, dtype=jnp.int32)
    row, col = t // GRID_W, t % GRID_W
    lane = jnp.arange(LANES, dtype=jnp.int32)
    tabs = []
    for r in (64, 32):
        nf = r // 4
        inv = 1.0 / (ROPE_BASE ** (jnp.arange(nf, dtype=F32) / nf))
        pos = jnp.where(((lane % r) < 2 * nf)[None, :], row[:, None], col[:, None]).astype(F32)
        ang = pos * inv[lane % nf][None, :]
        first = ((lane % (2 * nf)) < nf)[None, :]
        cos = jnp.cos(ang)
        sin = jnp.where(first, -jnp.sin(ang), jnp.sin(ang))
        tabs.append(jnp.concatenate([jnp.ones((CTX_LEN, LANES), F32), cos], axis=0))
        tabs.append(jnp.concatenate([jnp.zeros((CTX_LEN, LANES), F32), sin], axis=0))
    return tabs


def _layer_weights(w_in, w_out, w_uq, w_ukv, w_group, b_group, w_expert, b_expert):
    perm = jnp.array([0, 2, 1, 3])
    wq = w_in[:, :256].reshape(D_MODEL, 4, 64)[:, perm].reshape(D_MODEL, 256)
    w_in_p = jnp.concatenate([wq, w_in[:, 256:], jnp.zeros((D_MODEL, D_IN_PAD - w_in.shape[1]), F32)],
                             axis=1).astype(BF16)
    wo = jnp.concatenate([w_out[:256].reshape(4, 64, D_MODEL)[perm].reshape(256, D_MODEL), w_out[256:]],
                         axis=0).astype(BF16)
    uq = w_uq.reshape(256, 4, 96)
    blocks = []
    for h in range(4):
        blk = jnp.zeros((256, 256), F32)
        blk = blk.at[:, 64 * (h % 2):64 * (h % 2) + 64].set(uq[:, h, :64])
        blk = blk.at[:, LANES:LANES + 32].set(uq[:, h, 64:])
        blocks.append(blk)
    wuq = jnp.concatenate(blocks, axis=1).astype(BF16)
    ukv = w_ukv.reshape(128, 4, 128)
    wukk = ukv[:, :, :64].reshape(128, 256).astype(BF16)
    wukv = ukv[:, :, 64:].reshape(128, 256).astype(BF16)
    pad = LANES - MOE_GROUPS - MOE_EXPERTS
    w_route = jnp.concatenate([w_group, w_expert, jnp.zeros((D_MODEL, pad), F32)], axis=1)
    b_route = jnp.concatenate([b_group, b_expert, jnp.zeros((pad,), F32)]).reshape(1, LANES)
    return w_in_p, wo, wuq, wukk, wukv, w_route, b_route


def _moe_plan(route, ranks, counts, cap):
    e = route[:, :2].astype(jnp.int32)
    rank = ranks[:, :2].astype(jnp.int32)
    cnt = counts[0, :MOE_EXPERTS].astype(jnp.int32)
    padded = (cnt + MOE_ROWS - 1) // MOE_ROWS * MOE_ROWS
    pend = jnp.cumsum(padded)
    pstart = pend - padded
    dest = (pstart[e] + rank).reshape(-1)
    nblk = cap // MOE_ROWS
    blk_start = jnp.arange(nblk, dtype=jnp.int32) * MOE_ROWS
    block_e = jnp.minimum(jnp.sum((pend[None, :] <= blk_start[:, None]).astype(jnp.int32), axis=1),
                          MOE_EXPERTS - 1)
    n_used = (pend[-1:] // MOE_ROWS).astype(jnp.int32)
    return dest, block_e, n_used


def kernel(x, c, ctx, c_ctx, w_mod, b_mod, norm1_g, norm2_g, w_in, w_out, swa_sink, diff_lambda, diff_subln_g, ret_decay_logit, ret_gn_g, mla_q_norm_g, mla_w_uq, mla_kv_norm_g, mla_w_ukv, moe_w_group, moe_b_group, moe_w_expert, moe_b_expert, moe_w_gate, moe_w_up, moe_w_down, final_norm_g):
    B, T, D = x.shape
    S = CTX_LEN + T
    n_tok = B * S
    tk = min(FLASH_TK, T)
    cap = -(-(2 * n_tok + MOE_EXPERTS * (MOE_ROWS - 1)) // MOE_ROWS) * MOE_ROWS
    xs = jnp.concatenate([ctx, x], axis=1)
    cvec = jnp.zeros((8, D), F32).at[:B].set(c).at[B].set(c_ctx)
    tabs = _rope_tables(T)
    row = lambda v: v.reshape(1, -1)

    for l in range(DEPTH):
        lambda_init = 0.8 - 0.6 * math.exp(-0.3 * l)
        w_in_p, wo, wuq, wukk, wukv, w_route, b_route = _layer_weights(
            w_in[l], w_out[l], mla_w_uq[l], mla_w_ukv[l], moe_w_group[l], moe_b_group[l],
            moe_w_expert[l], moe_b_expert[l])
        mod = _modulation(cvec, w_mod[l], b_mod[l]).reshape(8, 1, 6 * D)
        (swaq, swak, swav, dq, dk, dv, rq, rk, rv, rg, mq, mk, mv) = _inproj(
            xs, mod, B, row(norm1_g[l]), w_in_p, tabs, row(mla_q_norm_g[l]), wuq,
            row(mla_kv_norm_g[l]), wukk, wukv)
        sink_p = swa_sink[l]
        ya = _swa(sink_p, swaq, swak, swav)
        yb = _diff(diff_lambda[l], diff_subln_g[l].reshape(64, 1), dq, dk, dv, lambda_init, tk)
        dl_lane = jnp.repeat(ret_decay_logit[l], 64, axis=-1).reshape(2, 2, 1, LANES)
        o_f, o_b = _retention(dl_lane, rq, rk, rv)
        ym = _mla(mq, mk, mv, tk)
        xs, h2, route = _outproj(xs, ya, yb, o_f, o_b, rg, row(ret_gn_g[l]), ym, wo, mod, B,
                                 row(norm2_g[l]), w_route, b_route)
        route2 = route.reshape(n_tok, LANES)
        ranks, counts = _ranks(route2)
        dest, block_e, n_used = _moe_plan(route2, ranks, counts, cap)
        xb = _dispatch(dest, h2.reshape(n_tok, D), cap)
        yb_e = _experts(block_e, n_used, xb, moe_w_gate, moe_w_up, moe_w_down, l)
        final = l == DEPTH - 1
        xs = _combine(dest, xs, route, mod, B, row(final_norm_g), yb_e, final)
    return xs
```

```python
import functools
import math

import jax
import jax.numpy as jnp
from jax import lax
from jax.experimental import pallas as pl
from jax.experimental.pallas import tpu as pltpu

F32 = jnp.float32
BF16 = jnp.bfloat16

D_MODEL = 1024
CTX_LEN = 256
GRID_W = 64
ROPE_BASE = 10000.0
NORM_EPS = 1e-6
NEG_INF = -1e30
DEPTH = 2

SWA_WINDOW = 128
RET_CHUNK = 128
MLA_SCALE = (64 + 32) ** -0.5
LOG2E = math.log2(math.e)
MOE_GROUPS = 4
MOE_PER_GROUP = 8
MOE_EXPERTS = 32
MOE_HIDDEN = 512

LANES = 128
TM = 256
TQ = 2048
FLASH_TK = 2048
VT_ROWS = 80
GAP_LIMIT = 64.0
MOE_ROWS = 256
DMA_UNROLL = 8
VMEM_LIMIT = 56 * 1024 * 1024

C_SWA_Q, C_SWA_K, C_SWA_V = 0, 256, 384
C_DIF_Q, C_DIF_K, C_DIF_V = 512, 768, 1024
C_RET_Q, C_RET_K, C_RET_V, C_RET_G = 1280, 1536, 1792, 2048
C_MLA_Q, C_MLA_KV, C_MLA_KR = 2304, 2560, 2688
D_IN_PAD = 2816


def _cparams(sem):
    return pltpu.CompilerParams(dimension_semantics=sem, vmem_limit_bytes=VMEM_LIMIT)


def _dot(a, b):
    return jnp.dot(a, b, preferred_element_type=F32)


def _dot_nt(a, b):
    return lax.dot_general(a, b, (((1,), (1,)), ((), ())), preferred_element_type=F32)


def _dot_tn(a, b):
    return lax.dot_general(a, b, (((0,), (0,)), ((), ())), preferred_element_type=F32)


def _rms(x, g):
    ms = jnp.mean(x * x, axis=-1, keepdims=True)
    return x * lax.rsqrt(ms + NORM_EPS) * g


def _lane(shape):
    return lax.broadcasted_iota(jnp.int32, shape, len(shape) - 1)


def _rope(z, cos, sin, nf):
    first = (_lane(z.shape) % (2 * nf)) < nf
    partner = jnp.where(first, pltpu.roll(z, LANES - nf, 1), pltpu.roll(z, nf, 1))
    return z * cos + partner * sin


def _mod_kernel(a_ref, w_ref, b_ref, o_ref):
    a = a_ref[...]
    act = a * jax.nn.sigmoid(a)
    o_ref[...] = jnp.dot(act, w_ref[...], preferred_element_type=F32,
                         precision=lax.Precision.HIGHEST) + b_ref[...]


def _modulation(cvec, w_mod, b_mod):
    n = w_mod.shape[1]
    bn = 512
    return pl.pallas_call(
        _mod_kernel,
        grid=(n // bn,),
        in_specs=[pl.BlockSpec((8, D_MODEL), lambda j: (0, 0)),
                  pl.BlockSpec((D_MODEL, bn), lambda j: (0, j)),
                  pl.BlockSpec((1, bn), lambda j: (0, j))],
        out_specs=pl.BlockSpec((8, bn), lambda j: (0, j)),
        out_shape=jax.ShapeDtypeStruct((8, n), F32),
        compiler_params=_cparams(("parallel",)),
        name="modulation",
    )(cvec, w_mod, b_mod.reshape(1, n))


def _mod_spec(chunk, nbatch):
    return pl.BlockSpec((None, 1, D_MODEL),
                        lambda b, i: (jnp.where(i == 0, nbatch, b), 0, chunk))


def _inproj_kernel(x_ref, g_ref, sh_ref, sc_ref, w_ref, c64_ref, s64_ref, c32_ref, s32_ref,
                   qng_ref, wuq_ref, kvng_ref, wukk_ref, wukv_ref,
                   swaq, swak, swav, dq, dk, dv, rq, rk, rv, rg, mq, mk, mv):
    x = x_ref[0]
    h = _rms(x, g_ref[...]) * (1.0 + sc_ref[...]) + sh_ref[...]
    hb = h.astype(BF16)
    c64, s64, c32, s32 = c64_ref[...], s64_ref[...], c32_ref[...], s32_ref[...]

    def proj(c0, width):
        return _dot(hb, w_ref[:, c0:c0 + width])

    def rope_blocks(z, out, nf, scale):
        cos, sin = (c64, s64) if nf == 16 else (c32, s32)
        for j in range(z.shape[1] // LANES):
            zz = _rope(z[:, j * LANES:(j + 1) * LANES], cos, sin, nf)
            if scale != 1.0:
                zz = zz * scale
            out[0, :, j * LANES:(j + 1) * LANES] = zz.astype(out.dtype)

    rope_blocks(proj(C_SWA_Q, 256), swaq, 16, 0.125)
    rope_blocks(proj(C_SWA_K, 128), swak, 16, 1.0)
    swav[0] = proj(C_SWA_V, 128).astype(BF16)
    def values_transposed(z, out):
        zt = z.T.astype(BF16)
        for hh in range(4):
            out[0, hh * VT_ROWS:hh * VT_ROWS + 64, :] = zt[hh * 64:(hh + 1) * 64]
            out[0, hh * VT_ROWS + 64:(hh + 1) * VT_ROWS, :] = jnp.ones((VT_ROWS - 64, zt.shape[1]), BF16)

    rope_blocks(proj(C_DIF_Q, 256), dq, 8, 32 ** -0.5 * LOG2E)
    rope_blocks(proj(C_DIF_K, 256), dk, 8, 1.0)
    values_transposed(proj(C_DIF_V, 256), dv)
    rope_blocks(proj(C_RET_Q, 256), rq, 16, 0.125)
    rope_blocks(proj(C_RET_K, 256), rk, 16, 1.0)
    rv[0] = proj(C_RET_V, 256).astype(BF16)
    rg[0] = proj(C_RET_G, 256).astype(BF16)

    ql = _rms(proj(C_MLA_Q, 256), qng_ref[...]).astype(BF16)
    qa = _dot(ql, wuq_ref[...]) * (MLA_SCALE * LOG2E)
    for hh in range(4):
        mq[0, :, hh * 256:hh * 256 + LANES] = qa[:, hh * 256:hh * 256 + LANES].astype(BF16)
        zr = _rope(qa[:, hh * 256 + LANES:(hh + 1) * 256], c32, s32, 8)
        mq[0, :, hh * 256 + LANES:(hh + 1) * 256] = zr.astype(BF16)
    kvl = _rms(proj(C_MLA_KV, 128), kvng_ref[...]).astype(BF16)
    kn = _dot(kvl, wukk_ref[...])
    kr = _rope(proj(C_MLA_KR, 128), c32, s32, 8).astype(BF16)
    for p in range(2):
        mk[0, :, p * 256:p * 256 + LANES] = kn[:, p * LANES:(p + 1) * LANES].astype(BF16)
        mk[0, :, p * 256 + LANES:(p + 1) * 256] = kr
    values_transposed(_dot(kvl, wukv_ref[...]), mv)


def _inproj(x, mod, nbatch, norm_g, w_in_p, tabs, qng, wuq, kvng, wukk, wukv):
    B, S, D = x.shape
    nt = S // TM
    tok = lambda w: pl.BlockSpec((1, TM, w), lambda b, i: (b, i, 0))
    full = lambda a: pl.BlockSpec(a.shape, lambda b, i: (0,) * a.ndim)
    tab = pl.BlockSpec((TM, LANES), lambda b, i: (i, 0))
    widths = [256, 128, 128, 256, 256, None, 256, 256, 256, 256, 1024, 512, None]
    vt_spec = pl.BlockSpec((1, 4 * VT_ROWS, TM), lambda b, i: (b, 0, i))
    vt_shape = jax.ShapeDtypeStruct((B, 4 * VT_ROWS, S), BF16)
    return pl.pallas_call(
        _inproj_kernel,
        grid=(B, nt),
        in_specs=[tok(D), full(norm_g), _mod_spec(0, nbatch), _mod_spec(1, nbatch), full(w_in_p),
                  tab, tab, tab, tab, full(qng), full(wuq), full(kvng), full(wukk), full(wukv)],
        out_specs=[vt_spec if w is None else tok(w) for w in widths],
        out_shape=[vt_shape if w is None else jax.ShapeDtypeStruct((B, S, w), BF16) for w in widths],
        compiler_params=_cparams(("parallel", "parallel")),
        name="inproj",
    )(x, norm_g, mod, mod, w_in_p, *tabs, qng, wuq, kvng, wukk, wukv)


def _swa_kernel(sink_ref, q_ref, kp_ref, ko_ref, kn_ref, vp_ref, vo_ref, vn_ref, kc_ref, vc_ref,
                o_ref, *, n_ctx_tiles, seq):
    i = pl.program_id(1)
    blk = RET_CHUNK
    n = i - n_ctx_tiles
    k_all = jnp.concatenate([kp_ref[0], ko_ref[0], kn_ref[0], kc_ref[0]], axis=0)
    v_all = jnp.concatenate([vp_ref[0], vo_ref[0], vn_ref[0], vc_ref[0]], axis=0)
    nk = 3 * blk + CTX_LEN
    r_io = lax.broadcasted_iota(jnp.int32, (blk, nk), 0)
    c_io = lax.broadcasted_iota(jnp.int32, (blk, nk), 1)
    kpos = (n - 1) * blk + c_io
    in_band = ((c_io >= r_io) & (c_io <= r_io + 2 * SWA_WINDOW) & (kpos >= 0) & (kpos < seq)
               & (n >= 0))
    valid = in_band | (c_io >= 3 * blk)
    lane = _lane((blk, LANES))
    heads = [(r, g) for r in range(2) for g in range(2)]
    qm = [jnp.where((lane >= 64 * g) & (lane < 64 * (g + 1)), q_ref[0, :, r * LANES:(r + 1) * LANES],
                    jnp.zeros((blk, LANES), BF16)) for r, g in heads]
    s = [jnp.where(valid, _dot_nt(q, k_all), NEG_INF) for q in qm]
    sink = [sink_ref[2 * g + r] for r, g in heads]
    m = [jnp.maximum(jnp.max(s[h], axis=-1, keepdims=True), sink[h]) for h in range(4)]
    p = [jnp.exp(s[h] - m[h]) for h in range(4)]
    l = [jnp.sum(p[h], axis=-1, keepdims=True) + jnp.exp(sink[h] - m[h]) for h in range(4)]
    o = [_dot(p[h].astype(BF16), v_all) / l[h] for h in range(4)]
    for r in range(2):
        o_ref[0, :, r * LANES:(r + 1) * LANES] = jnp.where(lane < 64, o[2 * r], o[2 * r + 1]).astype(BF16)


def _swa(sink, q, k, v):
    B, S, _ = q.shape
    blk = RET_CHUNK
    nt = S // blk
    nct = CTX_LEN // blk
    kvs = lambda f: pl.BlockSpec((1, blk, LANES), f)
    prev = lambda b, i: (b, jnp.maximum(i - 1, 0), 0)
    own = lambda b, i: (b, i, 0)
    nxt = lambda b, i: (b, jnp.minimum(i + 1, nt - 1), 0)
    ctx = pl.BlockSpec((1, CTX_LEN, LANES), lambda b, i: (b, 0, 0))
    return pl.pallas_call(
        functools.partial(_swa_kernel, n_ctx_tiles=nct, seq=S - CTX_LEN),
        grid=(B, nt),
        in_specs=[pl.BlockSpec(memory_space=pltpu.SMEM),
                  pl.BlockSpec((1, blk, 256), own),
                  kvs(prev), kvs(own), kvs(nxt), kvs(prev), kvs(own), kvs(nxt), ctx, ctx],
        out_specs=pl.BlockSpec((1, blk, 256), own),
        out_shape=jax.ShapeDtypeStruct((B, S, 256), BF16),
        compiler_params=_cparams(("parallel", "parallel")),
        name="swa",
    )(sink, q, k, k, k, v, v, v, k, v)


def _flash(q_maps, v_blk, k_ref, v_ref, m_scr, acc_scr, gap_scr, *, n_lat, tk):
    nm = len(q_maps)

    def chunk(start, size, mode):
        kc = k_ref[0, pl.ds(start, size), :]
        for a in range(nm):
            vt = v_ref[0, v_blk[a] * VT_ROWS:(v_blk[a] + 1) * VT_ROWS, pl.ds(start, size)]
            st = _dot_nt(kc, q_maps[a])
            m_prev = None if mode == "first" else m_scr[a]
            if mode == "fast":
                pf = jnp.exp2(st - m_prev)
                rise = jnp.maximum(jnp.max(pf, axis=0, keepdims=True), 1.0)
                acc_scr[a] = (acc_scr[a] + _dot(vt, pf.astype(BF16))) * (1.0 / rise)
                m_scr[a] = m_prev + jnp.log2(rise)
                gap_scr[a] = jnp.maximum(gap_scr[a], rise)
                continue
            cm = jnp.max(st, axis=0, keepdims=True)
            if mode == "first":
                acc_scr[a] = _dot(vt, jnp.exp2(st - cm).astype(BF16))
                m_scr[a] = cm
                gap_scr[a] = jnp.ones(cm.shape, F32)
            else:
                m_new = jnp.maximum(m_prev, cm)
                p = jnp.exp2(st - m_new).astype(BF16)
                acc_scr[a] = acc_scr[a] * jnp.exp2(m_prev - m_new) + _dot(vt, p)
                m_scr[a] = m_new

    def sweep(mode):
        chunk(0, CTX_LEN, "first")
        if n_lat:

            def body(c, carry):
                chunk(pl.multiple_of(CTX_LEN + c * tk, LANES), tk, mode)
                return carry

            lax.fori_loop(0, n_lat, body, 0)

    sweep("fast")
    if n_lat:
        worst = jnp.max(jnp.concatenate([gap_scr[a] for a in range(nm)], axis=0))

        @pl.when(worst > 2.0 ** GAP_LIMIT)
        def _():
            sweep("exact")

    return [acc_scr[a, 0:64, :] / acc_scr[a, 64:65, :] for a in range(nm)]


def _attention_calls(kern, name, nm, q, k, vt, qw, kw, tk, extra_in=(), extra_specs=()):
    B, S, _ = q.shape
    T = S - CTX_LEN

    def call(tq, n_tiles, n_lat, q_spec, kv_rows, call_name):
        return pl.pallas_call(
            functools.partial(kern, n_lat=n_lat, tk=tk),
            grid=(B, 2, n_tiles),
            in_specs=list(extra_specs) + [
                q_spec,
                pl.BlockSpec((1, kv_rows, kw), lambda b, p, i: (b, 0, p)),
                pl.BlockSpec((1, 2 * VT_ROWS, kv_rows), lambda b, p, i: (b, p, 0))],
            out_specs=pl.BlockSpec((1, tq, LANES), lambda b, p, i: (b, i, p)),
            out_shape=jax.ShapeDtypeStruct((B, tq * n_tiles, 256), BF16),
            scratch_shapes=[pltpu.VMEM((nm, 1, tq), F32), pltpu.VMEM((nm, VT_ROWS, tq), F32),
                            pltpu.VMEM((nm, 1, tq), F32)],
            compiler_params=_cparams(("parallel", "parallel", "parallel")),
            name=call_name,
        )(*extra_in, q, k, vt)

    y_ctx = call(CTX_LEN, 1, 0, pl.BlockSpec((1, CTX_LEN, qw), lambda b, p, i: (b, 0, p)),
                 CTX_LEN, name + "_ctx")
    tq = min(TQ, T)
    y_lat = call(tq, T // tq, T // tk,
                 pl.BlockSpec((pl.Element(1), pl.Element(tq), pl.Element(qw)),
                              lambda b, p, i: (b, (i * (tq // CTX_LEN) + 1) * CTX_LEN, p * qw)),
                 S, name)
    return jnp.concatenate([y_ctx, y_lat], axis=1)


def _diff_attn(lam_ref, g_ref, q_ref, k_ref, v_ref, o_ref, m_scr, acc_scr, gap_scr, *,
               n_lat, tk, lambda_init):
    qb = q_ref[0]
    lane = _lane(qb.shape)
    q_maps = []
    for a in range(4):
        msk = (lane >= 32 * a) & (lane < 32 * (a + 1))
        q_maps.append(jnp.where(msk, qb, jnp.zeros_like(qb)))
    n = _flash(q_maps, (0, 0, 1, 1), k_ref, v_ref, m_scr, acc_scr, gap_scr, n_lat=n_lat, tk=tk)
    lp = lam_ref[...]
    lam = (jnp.exp(jnp.sum(lp[0:1] * lp[1:2], axis=-1, keepdims=True))
           - jnp.exp(jnp.sum(lp[2:3] * lp[3:4], axis=-1, keepdims=True)) + lambda_init)
    ys = []
    for hh in range(2):
        o = n[2 * hh] - lam * n[2 * hh + 1]
        ms = jnp.mean(o * o, axis=0, keepdims=True)
        ys.append((o * lax.rsqrt(ms + NORM_EPS) * g_ref[...] * (1.0 - lambda_init)).T)
    o_ref[0] = jnp.concatenate(ys, axis=1).astype(BF16)


def _diff(lam_p, subln_g2, q, k, v, lambda_init, tk):
    return _attention_calls(
        functools.partial(_diff_attn, lambda_init=lambda_init), "diff_attn", 4, q, k, v, LANES, LANES,
        tk, extra_in=(lam_p, subln_g2),
        extra_specs=(pl.BlockSpec(lam_p.shape, lambda b, p, i: (0, 0)),
                     pl.BlockSpec((64, 1), lambda b, p, i: (0, 0))))


def _mla_attn(q_ref, k_ref, v_ref, o_ref, m_scr, acc_scr, gap_scr, *, n_lat, tk):
    q_maps = [q_ref[0, :, 0:256], q_ref[0, :, 256:512]]
    n = _flash(q_maps, (0, 1), k_ref, v_ref, m_scr, acc_scr, gap_scr, n_lat=n_lat, tk=tk)
    o_ref[0] = jnp.concatenate([n[0].T, n[1].T], axis=1).astype(BF16)


def _mla(q, k, v, tk):
    return _attention_calls(_mla_attn, "mla_attn", 2, q, k, v, 512, 256, tk)


def _ret_tables(dl, backward):
    L = RET_CHUNK
    lg = -(jnp.maximum(-dl, 0.0) + jnp.log(1.0 + jnp.exp(-jnp.abs(dl))))
    row = lax.broadcasted_iota(jnp.int32, (L, LANES), 0).astype(F32)
    col = lax.broadcasted_iota(jnp.int32, (L, LANES), 1)
    colf = col.astype(F32)
    if backward:
        qd, kd, rel = jnp.exp((L - row) * lg), jnp.exp(row * lg), colf - row
    else:
        qd, kd, rel = jnp.exp((row + 1.0) * lg), jnp.exp((L - 1.0 - row) * lg), row - colf
    decs = []
    for hh in range(2):
        lg_h = jnp.sum(jnp.where(col[0:1] == 64 * hh, lg, 0.0), axis=-1, keepdims=True)
        decs.append(jnp.where(rel >= 0, jnp.exp(jnp.maximum(rel, 0.0) * lg_h), 0.0))
    return qd, kd, decs[0], decs[1], jnp.exp(float(L) * lg)


def _ret_chunk(q, k, v, tab, cd, s_ref):
    col = _lane((RET_CHUNK, LANES))
    qf, kf = q.astype(F32), k.astype(F32)
    o = _dot((qf * tab[0]).astype(BF16), s_ref[...].astype(BF16))
    outs = []
    for hh in range(2):
        hmask = (col >= 64 * hh) & (col < 64 * (hh + 1))
        a = _dot_nt(jnp.where(hmask, q, jnp.zeros_like(q)), k) * tab[2 + hh]
        outs.append(_dot(a.astype(BF16), v))
    o = o + jnp.where(col < 64, outs[0], outs[1])
    rowi = lax.broadcasted_iota(jnp.int32, (LANES, LANES), 0)
    coli = lax.broadcasted_iota(jnp.int32, (LANES, LANES), 1)
    s_new = s_ref[...] * cd + _dot_tn((kf * tab[1]).astype(BF16), v)
    s_ref[...] = jnp.where((rowi < 64) == (coli < 64), s_new, 0.0)
    return o


def _ret_kernel(dl_ref, qf_ref, kf_ref, vf_ref, qb_ref, kb_ref, vb_ref, of_ref, ob_ref,
                s_scr, tab_scr, cd_scr):
    nb = qf_ref.shape[0]

    @pl.when(pl.program_id(0) == 0)
    def _():
        s_scr[...] = jnp.zeros(s_scr.shape, F32)
        for d in range(2):
            for p in range(2):
                tabs = _ret_tables(dl_ref[d, p], backward=(d == 1))
                for t in range(4):
                    tab_scr[d, p, t] = tabs[t]
                cd_scr[d, p] = tabs[4]

    for d, (q_ref, k_ref, v_ref, o_ref) in enumerate(((qf_ref, kf_ref, vf_ref, of_ref),
                                                      (qb_ref, kb_ref, vb_ref, ob_ref))):
        for p in range(2):
            tab = [tab_scr[d, p, t] for t in range(4)]
            cd = cd_scr[d, p]
            ln = slice(p * LANES, (p + 1) * LANES)
            for b in range(nb):
                o_ref[b, :, ln] = _ret_chunk(q_ref[b, :, ln], k_ref[b, :, ln], v_ref[b, :, ln],
                                             tab, cd, s_scr.at[d, b, p])


def _retention(dl_lane, q, k, v):
    B, S, _ = q.shape
    L = RET_CHUNK
    nc = S // L
    ncc = CTX_LEN // L
    fwd = pl.BlockSpec((B, L, 256), lambda s: (0, s, 0))
    bwd = pl.BlockSpec((B, L, 256), lambda s: (0, jnp.where(s < ncc, ncc - 1 - s, nc - 1 + ncc - s), 0))
    return pl.pallas_call(
        _ret_kernel,
        grid=(nc,),
        in_specs=[pl.BlockSpec(dl_lane.shape, lambda s: (0, 0, 0, 0)), fwd, fwd, fwd, bwd, bwd, bwd],
        out_specs=[fwd, bwd],
        out_shape=[jax.ShapeDtypeStruct((B, S, 256), F32)] * 2,
        scratch_shapes=[pltpu.VMEM((2, B, 2, LANES, LANES), F32),
                        pltpu.VMEM((2, 2, 4, L, LANES), F32), pltpu.VMEM((2, 2, 1, LANES), F32)],
        compiler_params=_cparams(("arbitrary",)),
        name="retention",
    )(dl_lane, q, k, v, q, k, v)


def _gated_group_norm(o, g, gn):
    lo = _lane(o.shape) < 64

    def halves(t):
        a = jnp.sum(jnp.where(lo, t, 0.0), axis=-1, keepdims=True) * (1.0 / 64)
        b = jnp.sum(jnp.where(lo, 0.0, t), axis=-1, keepdims=True) * (1.0 / 64)
        return jnp.where(lo, a, b)

    d = o - halves(o)
    var = halves(d * d)
    return d * lax.rsqrt(var + NORM_EPS) * gn * (g * jax.nn.sigmoid(g))


def _outproj_kernel(x_ref, ya_ref, yb_ref, of_ref, ob_ref, rg_ref, gn_ref, ym_ref, w_ref, g1_ref,
                    sh_ref, sc_ref, ng_ref, wr_ref, br_ref, xo_ref, h_ref, rt_ref):
    mix = (_dot(ya_ref[0], w_ref[0:256]) + _dot(yb_ref[0], w_ref[256:512])
           + _dot(ym_ref[0], w_ref[768:1024]))
    for j in range(2):
        ln = slice(j * LANES, (j + 1) * LANES)
        yr = _gated_group_norm(of_ref[0, :, ln] + ob_ref[0, :, ln], rg_ref[0, :, ln].astype(F32),
                               gn_ref[:, ln])
        mix = mix + _dot(yr.astype(BF16), w_ref[512 + j * LANES:512 + (j + 1) * LANES])
    x = x_ref[0] + g1_ref[...] * mix
    xo_ref[0] = x
    h = _rms(x, ng_ref[...]) * (1.0 + sc_ref[...]) + sh_ref[...]
    h_ref[0] = h
    logits = jnp.dot(h, wr_ref[...], preferred_element_type=F32,
                     precision=lax.Precision.HIGHEST) + br_ref[...]
    lane = _lane(logits.shape).astype(F32)

    def first_argmax(vals, mask):
        mx = jnp.max(jnp.where(mask, vals, NEG_INF), axis=-1, keepdims=True)
        idx = jnp.min(jnp.where(mask & (vals == mx), lane, float(LANES)), axis=-1, keepdims=True)
        return mx, idx

    gmask = lane < MOE_GROUPS
    g_max, g_idx = first_argmax(logits, gmask)
    g_w = 1.0 / jnp.sum(jnp.exp(jnp.where(gmask, logits - g_max, NEG_INF)), axis=-1, keepdims=True)
    e_lo = MOE_GROUPS + g_idx * MOE_PER_GROUP
    emask = (lane >= e_lo) & (lane < e_lo + MOE_PER_GROUP)
    v1, i1 = first_argmax(logits, emask)
    v2, i2 = first_argmax(logits, emask & (lane != i1))
    t = jnp.exp(v2 - v1)
    w1 = g_w / (1.0 + t)
    w2 = w1 * t
    rt = jnp.where(lane == 0, i1 - MOE_GROUPS,
                   jnp.where(lane == 1, i2 - MOE_GROUPS,
                             jnp.where(lane == 2, w1, jnp.where(lane == 3, w2, 0.0))))
    rt_ref[0] = rt


def _outproj(x, ya, yb, o_f, o_b, rg, gn_g, ym, w_out_p, mod, nbatch, norm_g, w_route, b_route):
    B, S, D = x.shape
    nt = S // TM
    tok = lambda w: pl.BlockSpec((1, TM, w), lambda b, i: (b, i, 0))
    full = lambda a: pl.BlockSpec(a.shape, lambda b, i: (0,) * a.ndim)
    return pl.pallas_call(
        _outproj_kernel,
        grid=(B, nt),
        in_specs=[tok(D), tok(256), tok(256), tok(256), tok(256), tok(256), full(gn_g), tok(256),
                  full(w_out_p),
                  _mod_spec(2, nbatch), _mod_spec(3, nbatch), _mod_spec(4, nbatch), full(norm_g),
                  full(w_route), full(b_route)],
        out_specs=[tok(D), tok(D), tok(LANES)],
        out_shape=[jax.ShapeDtypeStruct((B, S, D), F32), jax.ShapeDtypeStruct((B, S, D), F32),
                   jax.ShapeDtypeStruct((B, S, LANES), F32)],
        compiler_params=_cparams(("parallel", "parallel")),
        name="outproj_router",
    )(x, ya, yb, o_f, o_b, rg, gn_g, ym, w_out_p, mod, mod, mod, norm_g, w_route, b_route)


def _rank_kernel(rt_ref, rank_ref, cnt_ref, carry):
    @pl.when(pl.program_id(0) == 0)
    def _():
        carry[...] = jnp.zeros(carry.shape, F32)

    rt = rt_ref[...]
    lane = _lane(rt.shape)
    lanef = lane.astype(F32)
    oh0 = jnp.where(lanef == rt[:, 0:1], 1.0, 0.0)
    oh1 = jnp.where(lanef == rt[:, 1:2], 1.0, 0.0)
    cnt = oh0 + oh1
    r_io = lax.broadcasted_iota(jnp.int32, (TM, TM), 0)
    c_io = lax.broadcasted_iota(jnp.int32, (TM, TM), 1)
    tri = jnp.where(r_io > c_io, 1.0, 0.0).astype(BF16)
    before = _dot(tri, cnt.astype(BF16)) + carry[...]
    r0 = jnp.sum(oh0 * before, axis=-1, keepdims=True)
    r1 = jnp.sum(oh1 * before, axis=-1, keepdims=True)
    rank_ref[...] = jnp.where(lane == 0, r0, jnp.where(lane == 1, r1, 0.0))
    carry[...] = carry[...] + jnp.sum(cnt, axis=0, keepdims=True)
    cnt_ref[...] = carry[...]


def _ranks(route):
    n = route.shape[0]
    return pl.pallas_call(
        _rank_kernel,
        grid=(n // TM,),
        in_specs=[pl.BlockSpec((TM, LANES), lambda i: (i, 0))],
        out_specs=[pl.BlockSpec((TM, LANES), lambda i: (i, 0)),
                   pl.BlockSpec((1, LANES), lambda i: (0, 0))],
        out_shape=[jax.ShapeDtypeStruct((n, LANES), F32), jax.ShapeDtypeStruct((1, LANES), F32)],
        scratch_shapes=[pltpu.VMEM((1, LANES), F32)],
        compiler_params=_cparams(("arbitrary",)),
        name="moe_rank",
    )(route)


def _dispatch_kernel(dest_ref, h_ref, xb_in_ref, xb_ref, sem):
    del xb_in_ref
    base = pl.program_id(0) * TM

    def copy(r, k):
        d = dest_ref[(base + r) * 2 + k]
        return pltpu.make_async_copy(h_ref.at[pl.ds(r, 1)], xb_ref.at[pl.ds(d, 1)], sem)

    def issue(r, c):
        copy(r, 0).start()
        copy(r, 1).start()
        return c

    def drain(r, c):
        copy(r, 0).wait()
        copy(r, 1).wait()
        return c

    lax.fori_loop(0, TM, issue, 0, unroll=DMA_UNROLL)
    lax.fori_loop(0, TM, drain, 0, unroll=DMA_UNROLL)


def _dispatch(dest, h, cap):
    n, d = h.shape
    return pl.pallas_call(
        _dispatch_kernel,
        grid_spec=pltpu.PrefetchScalarGridSpec(
            num_scalar_prefetch=1,
            grid=(n // TM,),
            in_specs=[pl.BlockSpec((TM, d), lambda i, dest: (i, 0)),
                      pl.BlockSpec(memory_space=pl.ANY)],
            out_specs=pl.BlockSpec(memory_space=pl.ANY),
            scratch_shapes=[pltpu.SemaphoreType.DMA(())]),
        out_shape=jax.ShapeDtypeStruct((cap, d), h.dtype),
        input_output_aliases={2: 0},
        compiler_params=_cparams(("arbitrary",)),
        name="moe_dispatch",
    )(dest, h, jnp.zeros((cap, d), h.dtype))


def _expert_kernel(be_ref, nu_ref, x_ref, wg_ref, wu_ref, wd_ref, y_ref, wg_s, wu_s, wd_s):
    i = pl.program_id(0)

    @pl.when((i == 0) | (be_ref[i] != be_ref[jnp.maximum(i - 1, 0)]))
    def _():
        wg_s[...] = wg_ref[0].astype(BF16)
        wu_s[...] = wu_ref[0].astype(BF16)
        wd_s[...] = wd_ref[0].astype(BF16)

    @pl.when(i < nu_ref[0])
    def _():
        xb = x_ref[...].astype(BF16)
        gate = _dot(xb, wg_s[...])
        up = _dot(xb, wu_s[...])
        hid = (gate * jax.nn.sigmoid(gate) * up).astype(BF16)
        y_ref[...] = _dot(hid, wd_s[...])

    @pl.when(i >= nu_ref[0])
    def _():
        y_ref[...] = jnp.zeros(y_ref.shape, y_ref.dtype)


def _experts(block_e, n_used, xb, w_gate, w_up, w_down, layer):
    cap, d = xb.shape
    hdim = w_gate.shape[-1]
    return pl.pallas_call(
        _expert_kernel,
        grid_spec=pltpu.PrefetchScalarGridSpec(
            num_scalar_prefetch=2,
            grid=(cap // MOE_ROWS,),
            in_specs=[pl.BlockSpec((MOE_ROWS, d), lambda i, be, nu: (i, 0)),
                      pl.BlockSpec((None, 1, d, hdim), lambda i, be, nu: (layer, be[i], 0, 0)),
                      pl.BlockSpec((None, 1, d, hdim), lambda i, be, nu: (layer, be[i], 0, 0)),
                      pl.BlockSpec((None, 1, hdim, d), lambda i, be, nu: (layer, be[i], 0, 0))],
            out_specs=pl.BlockSpec((MOE_ROWS, d), lambda i, be, nu: (i, 0)),
            scratch_shapes=[pltpu.VMEM((d, hdim), BF16), pltpu.VMEM((d, hdim), BF16),
                            pltpu.VMEM((hdim, d), BF16)]),
        out_shape=jax.ShapeDtypeStruct((cap, d), F32),
        compiler_params=_cparams(("arbitrary",)),
        name="moe_experts",
    )(block_e, n_used, xb, w_gate, w_up, w_down)


def _combine_kernel(dest_ref, x_ref, rt_ref, g2_ref, fg_ref, yb_ref, o_ref, gath, sem, *,
                    tile_off, tiles_per_batch, final):
    b, i = pl.program_id(0), pl.program_id(1)
    base = (b * tiles_per_batch + i + tile_off) * TM

    def copy(r, k):
        d = dest_ref[(base + r) * 2 + k]
        return pltpu.make_async_copy(yb_ref.at[pl.ds(d, 1)], gath.at[k, pl.ds(r, 1)], sem)

    def issue(r, c):
        copy(r, 0).start()
        copy(r, 1).start()
        return c

    def drain(r, c):
        copy(r, 0).wait()
        copy(r, 1).wait()
        return c

    lax.fori_loop(0, TM, issue, 0, unroll=DMA_UNROLL)
    lax.fori_loop(0, TM, drain, 0, unroll=DMA_UNROLL)
    rt = rt_ref[0]
    m = rt[:, 2:3] * gath[0] + rt[:, 3:4] * gath[1]
    x = x_ref[0] + g2_ref[...] * m
    if final:
        x = _rms(x, fg_ref[...])
    o_ref[0] = x


def _combine(dest, x, route, mod, nbatch, final_g, yb, final):
    B, S, D = x.shape
    nt = S // TM
    off = 1 if final else 0
    g2_spec = pl.BlockSpec((None, 1, D_MODEL),
                           lambda b, i, dest: (jnp.where(i + off == 0, nbatch, b), 0, 5))
    return pl.pallas_call(
        functools.partial(_combine_kernel, tile_off=off, tiles_per_batch=nt, final=final),
        grid_spec=pltpu.PrefetchScalarGridSpec(
            num_scalar_prefetch=1,
            grid=(B, nt - off),
            in_specs=[pl.BlockSpec((1, TM, D), lambda b, i, dest: (b, i + off, 0)),
                      pl.BlockSpec((1, TM, LANES), lambda b, i, dest: (b, i + off, 0)),
                      g2_spec,
                      pl.BlockSpec((1, D), lambda b, i, dest: (0, 0)),
                      pl.BlockSpec(memory_space=pl.ANY)],
            out_specs=pl.BlockSpec((1, TM, D), lambda b, i, dest: (b, i, 0)),
            scratch_shapes=[pltpu.VMEM((2, TM, D), F32), pltpu.SemaphoreType.DMA(())]),
        out_shape=jax.ShapeDtypeStruct((B, S - off * TM, D), F32),
        compiler_params=_cparams(("arbitrary", "arbitrary")),
        name="moe_combine",
    )(dest, x, route, mod, final_g, yb)


def _rope_tables(seq):
    n_rows = seq // GRID_W
    pos = jnp.arange(max(n_rows, GRID_W), dtype=F32)[:, None]
    lane = jnp.arange(LANES, dtype=jnp.int32)
    tabs = []
    for r in (64, 32):
        nf = r // 4
        inv = 1.0 / (ROPE_BASE ** (jnp.arange(nf, dtype=F32) / nf))
        ang = pos * inv[lane % nf][None, :]
        first = ((lane % (2 * nf)) < nf)[None, :]
        by_row = ((lane % r) < 2 * nf)[None, None, :]
        for small in (jnp.cos(ang), jnp.where(first, -jnp.sin(ang), jnp.sin(ang))):
            full = jnp.where(by_row, small[:n_rows, None, :], small[None, :GRID_W, :])
            tabs.append(full.reshape(seq, LANES))
    ident = (jnp.ones((CTX_LEN, LANES), F32), jnp.zeros((CTX_LEN, LANES), F32))
    return [jnp.concatenate([ident[j % 2], tab], axis=0) for j, tab in enumerate(tabs)]


def _layer_weights(w_in, w_out, w_uq, w_ukv, w_group, b_group, w_expert, b_expert):
    perm = jnp.array([0, 2, 1, 3])
    wq = w_in[:, :256].reshape(D_MODEL, 4, 64)[:, perm].reshape(D_MODEL, 256)
    w_in_p = jnp.concatenate([wq, w_in[:, 256:], jnp.zeros((D_MODEL, D_IN_PAD - w_in.shape[1]), F32)],
                             axis=1).astype(BF16)
    wo = jnp.concatenate([w_out[:256].reshape(4, 64, D_MODEL)[perm].reshape(256, D_MODEL), w_out[256:]],
                         axis=0).astype(BF16)
    uq = w_uq.reshape(256, 4, 96)
    blocks = []
    for h in range(4):
        blk = jnp.zeros((256, 256), F32)
        blk = blk.at[:, 64 * (h % 2):64 * (h % 2) + 64].set(uq[:, h, :64])
        blk = blk.at[:, LANES:LANES + 32].set(uq[:, h, 64:])
        blocks.append(blk)
    wuq = jnp.concatenate(blocks, axis=1).astype(BF16)
    ukv = w_ukv.reshape(128, 4, 128)
    wukk = ukv[:, :, :64].reshape(128, 256).astype(BF16)
    wukv = ukv[:, :, 64:].reshape(128, 256).astype(BF16)
    pad = LANES - MOE_GROUPS - MOE_EXPERTS
    w_route = jnp.concatenate([w_group, w_expert, jnp.zeros((D_MODEL, pad), F32)], axis=1)
    b_route = jnp.concatenate([b_group, b_expert, jnp.zeros((pad,), F32)]).reshape(1, LANES)
    return w_in_p, wo, wuq, wukk, wukv, w_route, b_route


def _moe_plan(route, ranks, counts, cap):
    e = route[:, :2].astype(jnp.int32)
    rank = ranks[:, :2].astype(jnp.int32)
    cnt = counts[0, :MOE_EXPERTS].astype(jnp.int32)
    padded = (cnt + MOE_ROWS - 1) // MOE_ROWS * MOE_ROWS
    pend = jnp.cumsum(padded)
    pstart = pend - padded
    dest = (pstart[e] + rank).reshape(-1)
    nblk = cap // MOE_ROWS
    blk_start = jnp.arange(nblk, dtype=jnp.int32) * MOE_ROWS
    block_e = jnp.minimum(jnp.sum((pend[None, :] <= blk_start[:, None]).astype(jnp.int32), axis=1),
                          MOE_EXPERTS - 1)
    n_used = (pend[-1:] // MOE_ROWS).astype(jnp.int32)
    return dest, block_e, n_used


def kernel(x, c, ctx, c_ctx, w_mod, b_mod, norm1_g, norm2_g, w_in, w_out, swa_sink, diff_lambda, diff_subln_g, ret_decay_logit, ret_gn_g, mla_q_norm_g, mla_w_uq, mla_kv_norm_g, mla_w_ukv, moe_w_group, moe_b_group, moe_w_expert, moe_b_expert, moe_w_gate, moe_w_up, moe_w_down, final_norm_g):
    B, T, D = x.shape
    S = CTX_LEN + T
    n_tok = B * S
    tk = min(FLASH_TK, T)
    cap = -(-(2 * n_tok + MOE_EXPERTS * (MOE_ROWS - 1)) // MOE_ROWS) * MOE_ROWS
    xs = jnp.concatenate([ctx, x], axis=1)
    cvec = jnp.zeros((8, D), F32).at[:B].set(c).at[B].set(c_ctx)
    tabs = _rope_tables(T)
    row = lambda v: v.reshape(1, -1)

    for l in range(DEPTH):
        lambda_init = 0.8 - 0.6 * math.exp(-0.3 * l)
        w_in_p, wo, wuq, wukk, wukv, w_route, b_route = _layer_weights(
            w_in[l], w_out[l], mla_w_uq[l], mla_w_ukv[l], moe_w_group[l], moe_b_group[l],
            moe_w_expert[l], moe_b_expert[l])
        mod = _modulation(cvec, w_mod[l], b_mod[l]).reshape(8, 1, 6 * D)
        (swaq, swak, swav, dq, dk, dv, rq, rk, rv, rg, mq, mk, mv) = _inproj(
            xs, mod, B, row(norm1_g[l]), w_in_p, tabs, row(mla_q_norm_g[l]), wuq,
            row(mla_kv_norm_g[l]), wukk, wukv)
        sink_p = swa_sink[l]
        ya = _swa(sink_p, swaq, swak, swav)
        yb = _diff(diff_lambda[l], diff_subln_g[l].reshape(64, 1), dq, dk, dv, lambda_init, tk)
        dl_lane = jnp.repeat(ret_decay_logit[l], 64, axis=-1).reshape(2, 2, 1, LANES)
        o_f, o_b = _retention(dl_lane, rq, rk, rv)
        ym = _mla(mq, mk, mv, tk)
        xs, h2, route = _outproj(xs, ya, yb, o_f, o_b, rg, row(ret_gn_g[l]), ym, wo, mod, B,
                                 row(norm2_g[l]), w_route, b_route)
        route2 = route.reshape(n_tok, LANES)
        ranks, counts = _ranks(route2)
        dest, block_e, n_used = _moe_plan(route2, ranks, counts, cap)
        xb = _dispatch(dest, h2.reshape(n_tok, D), cap)
        yb_e = _experts(block_e, n_used, xb, moe_w_gate, moe_w_up, moe_w_down, l)
        final = l == DEPTH - 1
        xs = _combine(dest, xs, route, mod, B, row(final_norm_g), yb_e, final)
    return xs
```

```python
import functools
import math

import jax
import jax.numpy as jnp
from jax import lax
from jax.experimental import pallas as pl
from jax.experimental.pallas import tpu as pltpu

F32 = jnp.float32
BF16 = jnp.bfloat16

D_MODEL = 1024
CTX_LEN = 256
GRID_W = 64
ROPE_BASE = 10000.0
NORM_EPS = 1e-6
NEG_INF = -1e30
DEPTH = 2

SWA_WINDOW = 128
RET_CHUNK = 128
MLA_SCALE = (64 + 32) ** -0.5
LOG2E = math.log2(math.e)
MOE_GROUPS = 4
MOE_PER_GROUP = 8
MOE_EXPERTS = 32
MOE_HIDDEN = 512

LANES = 128
TM = 256
TQ = 2048
DIFF_TK = 1024
MLA_TK = 2048
VT_ROWS = 80
GAP_LIMIT = 64.0
MOE_ROWS = 256
DMA_UNROLL = 8
VMEM_LIMIT = 56 * 1024 * 1024

C_SWA_Q, C_SWA_K, C_SWA_V = 0, 256, 384
C_DIF_Q, C_DIF_K, C_DIF_V = 512, 768, 1024
C_RET_Q, C_RET_K, C_RET_V, C_RET_G = 1280, 1536, 1792, 2048
C_MLA_Q, C_MLA_KV, C_MLA_KR = 2304, 2560, 2688
D_IN_PAD = 2816


def _cparams(sem):
    return pltpu.CompilerParams(dimension_semantics=sem, vmem_limit_bytes=VMEM_LIMIT)


def _dot(a, b):
    return jnp.dot(a, b, preferred_element_type=F32)


def _dot_nt(a, b):
    return lax.dot_general(a, b, (((1,), (1,)), ((), ())), preferred_element_type=F32)


def _dot_tn(a, b):
    return lax.dot_general(a, b, (((0,), (0,)), ((), ())), preferred_element_type=F32)


def _rms(x, g):
    ms = jnp.mean(x * x, axis=-1, keepdims=True)
    return x * lax.rsqrt(ms + NORM_EPS) * g


def _lane(shape):
    return lax.broadcasted_iota(jnp.int32, shape, len(shape) - 1)


def _rope(z, cos, sin, nf):
    first = (_lane(z.shape) % (2 * nf)) < nf
    partner = jnp.where(first, pltpu.roll(z, LANES - nf, 1), pltpu.roll(z, nf, 1))
    return z * cos + partner * sin


def _mod_kernel(a_ref, w_ref, b_ref, o_ref):
    a = a_ref[...]
    act = a * jax.nn.sigmoid(a)
    o_ref[...] = jnp.dot(act, w_ref[...], preferred_element_type=F32,
                         precision=lax.Precision.HIGHEST) + b_ref[...]


def _modulation(cvec, w_mod, b_mod):
    n = w_mod.shape[1]
    bn = 512
    return pl.pallas_call(
        _mod_kernel,
        grid=(n // bn,),
        in_specs=[pl.BlockSpec((8, D_MODEL), lambda j: (0, 0)),
                  pl.BlockSpec((D_MODEL, bn), lambda j: (0, j)),
                  pl.BlockSpec((1, bn), lambda j: (0, j))],
        out_specs=pl.BlockSpec((8, bn), lambda j: (0, j)),
        out_shape=jax.ShapeDtypeStruct((8, n), F32),
        compiler_params=_cparams(("parallel",)),
        name="modulation",
    )(cvec, w_mod, b_mod.reshape(1, n))


def _mod_spec(chunk, nbatch):
    return pl.BlockSpec((None, 1, D_MODEL),
                        lambda b, i: (jnp.where(i == 0, nbatch, b), 0, chunk))


def _inproj_kernel(x_ref, g_ref, sh_ref, sc_ref, w_ref, c64_ref, s64_ref, c32_ref, s32_ref,
                   qng_ref, wuq_ref, kvng_ref, wukk_ref, wukv_ref,
                   swaq, swak, swav, dq, dk, dv, rq, rk, rv, rg, mq, mk, mv):
    x = x_ref[0]
    h = _rms(x, g_ref[...]) * (1.0 + sc_ref[...]) + sh_ref[...]
    hb = h.astype(BF16)
    c64, s64, c32, s32 = c64_ref[...], s64_ref[...], c32_ref[...], s32_ref[...]

    def proj(c0, width):
        return _dot(hb, w_ref[:, c0:c0 + width])

    def rope_blocks(z, out, nf, scale):
        cos, sin = (c64, s64) if nf == 16 else (c32, s32)
        for j in range(z.shape[1] // LANES):
            zz = _rope(z[:, j * LANES:(j + 1) * LANES], cos, sin, nf)
            if scale != 1.0:
                zz = zz * scale
            out[0, :, j * LANES:(j + 1) * LANES] = zz.astype(out.dtype)

    rope_blocks(proj(C_SWA_Q, 256), swaq, 16, 0.125)
    rope_blocks(proj(C_SWA_K, 128), swak, 16, 1.0)
    swav[0] = proj(C_SWA_V, 128).astype(BF16)
    def values_transposed(z, out):
        zt = z.T.astype(BF16)
        for hh in range(4):
            out[0, hh * VT_ROWS:hh * VT_ROWS + 64, :] = zt[hh * 64:(hh + 1) * 64]
            out[0, hh * VT_ROWS + 64:(hh + 1) * VT_ROWS, :] = jnp.ones((VT_ROWS - 64, zt.shape[1]), BF16)

    rope_blocks(proj(C_DIF_Q, 256), dq, 8, 32 ** -0.5 * LOG2E)
    rope_blocks(proj(C_DIF_K, 256), dk, 8, 1.0)
    values_transposed(proj(C_DIF_V, 256), dv)
    rope_blocks(proj(C_RET_Q, 256), rq, 16, 0.125)
    rope_blocks(proj(C_RET_K, 256), rk, 16, 1.0)
    rv[0] = proj(C_RET_V, 256).astype(BF16)
    rg[0] = proj(C_RET_G, 256).astype(BF16)

    ql = _rms(proj(C_MLA_Q, 256), qng_ref[...]).astype(BF16)
    qa = _dot(ql, wuq_ref[...]) * (MLA_SCALE * LOG2E)
    for hh in range(4):
        mq[0, :, hh * 256:hh * 256 + LANES] = qa[:, hh * 256:hh * 256 + LANES].astype(BF16)
        zr = _rope(qa[:, hh * 256 + LANES:(hh + 1) * 256], c32, s32, 8)
        mq[0, :, hh * 256 + LANES:(hh + 1) * 256] = zr.astype(BF16)
    kvl = _rms(proj(C_MLA_KV, 128), kvng_ref[...]).astype(BF16)
    kn = _dot(kvl, wukk_ref[...])
    kr = _rope(proj(C_MLA_KR, 128), c32, s32, 8).astype(BF16)
    for p in range(2):
        mk[0, :, p * 256:p * 256 + LANES] = kn[:, p * LANES:(p + 1) * LANES].astype(BF16)
        mk[0, :, p * 256 + LANES:(p + 1) * 256] = kr
    values_transposed(_dot(kvl, wukv_ref[...]), mv)


def _inproj(x, mod, nbatch, norm_g, w_in_p, tabs, qng, wuq, kvng, wukk, wukv):
    B, S, D = x.shape
    nt = S // TM
    tok = lambda w: pl.BlockSpec((1, TM, w), lambda b, i: (b, i, 0))
    full = lambda a: pl.BlockSpec(a.shape, lambda b, i: (0,) * a.ndim)
    tab = pl.BlockSpec((TM, LANES), lambda b, i: (i, 0))
    widths = [256, 128, 128, 256, 256, None, 256, 256, 256, 256, 1024, 512, None]
    vt_spec = pl.BlockSpec((1, 4 * VT_ROWS, TM), lambda b, i: (b, 0, i))
    vt_shape = jax.ShapeDtypeStruct((B, 4 * VT_ROWS, S), BF16)
    return pl.pallas_call(
        _inproj_kernel,
        grid=(B, nt),
        in_specs=[tok(D), full(norm_g), _mod_spec(0, nbatch), _mod_spec(1, nbatch), full(w_in_p),
                  tab, tab, tab, tab, full(qng), full(wuq), full(kvng), full(wukk), full(wukv)],
        out_specs=[vt_spec if w is None else tok(w) for w in widths],
        out_shape=[vt_shape if w is None else jax.ShapeDtypeStruct((B, S, w), BF16) for w in widths],
        compiler_params=_cparams(("parallel", "parallel")),
        name="inproj",
    )(x, norm_g, mod, mod, w_in_p, *tabs, qng, wuq, kvng, wukk, wukv)


def _swa_kernel(sink_ref, q_ref, kp_ref, ko_ref, kn_ref, vp_ref, vo_ref, vn_ref, kc_ref, vc_ref,
                o_ref, *, n_ctx_tiles, seq):
    i = pl.program_id(1)
    blk = RET_CHUNK
    n = i - n_ctx_tiles
    k_all = jnp.concatenate([kp_ref[0], ko_ref[0], kn_ref[0], kc_ref[0]], axis=0)
    v_all = jnp.concatenate([vp_ref[0], vo_ref[0], vn_ref[0], vc_ref[0]], axis=0)
    nk = 3 * blk + CTX_LEN
    r_io = lax.broadcasted_iota(jnp.int32, (blk, nk), 0)
    c_io = lax.broadcasted_iota(jnp.int32, (blk, nk), 1)
    kpos = (n - 1) * blk + c_io
    in_band = ((c_io >= r_io) & (c_io <= r_io + 2 * SWA_WINDOW) & (kpos >= 0) & (kpos < seq)
               & (n >= 0))
    valid = in_band | (c_io >= 3 * blk)
    lane = _lane((blk, LANES))
    heads = [(r, g) for r in range(2) for g in range(2)]
    qm = [jnp.where((lane >= 64 * g) & (lane < 64 * (g + 1)), q_ref[0, :, r * LANES:(r + 1) * LANES],
                    jnp.zeros((blk, LANES), BF16)) for r, g in heads]
    s = [jnp.where(valid, _dot_nt(q, k_all), NEG_INF) for q in qm]
    sink = [sink_ref[2 * g + r] for r, g in heads]
    m = [jnp.maximum(jnp.max(s[h], axis=-1, keepdims=True), sink[h]) for h in range(4)]
    p = [jnp.exp(s[h] - m[h]) for h in range(4)]
    l = [jnp.sum(p[h], axis=-1, keepdims=True) + jnp.exp(sink[h] - m[h]) for h in range(4)]
    o = [_dot(p[h].astype(BF16), v_all) / l[h] for h in range(4)]
    for r in range(2):
        o_ref[0, :, r * LANES:(r + 1) * LANES] = jnp.where(lane < 64, o[2 * r], o[2 * r + 1]).astype(BF16)


def _swa(sink, q, k, v):
    B, S, _ = q.shape
    blk = RET_CHUNK
    nt = S // blk
    nct = CTX_LEN // blk
    kvs = lambda f: pl.BlockSpec((1, blk, LANES), f)
    prev = lambda b, i: (b, jnp.maximum(i - 1, 0), 0)
    own = lambda b, i: (b, i, 0)
    nxt = lambda b, i: (b, jnp.minimum(i + 1, nt - 1), 0)
    ctx = pl.BlockSpec((1, CTX_LEN, LANES), lambda b, i: (b, 0, 0))
    return pl.pallas_call(
        functools.partial(_swa_kernel, n_ctx_tiles=nct, seq=S - CTX_LEN),
        grid=(B, nt),
        in_specs=[pl.BlockSpec(memory_space=pltpu.SMEM),
                  pl.BlockSpec((1, blk, 256), own),
                  kvs(prev), kvs(own), kvs(nxt), kvs(prev), kvs(own), kvs(nxt), ctx, ctx],
        out_specs=pl.BlockSpec((1, blk, 256), own),
        out_shape=jax.ShapeDtypeStruct((B, S, 256), BF16),
        compiler_params=_cparams(("parallel", "parallel")),
        name="swa",
    )(sink, q, k, k, k, v, v, v, k, v)


def _flash(q_maps, v_blk, k_ref, v_ref, m_scr, acc_scr, gap_scr, *, n_lat, tk):
    nm = len(q_maps)
    qt_maps = [q.astype(F32).T.astype(BF16) for q in q_maps]

    def chunk(start, size, mode):
        kc = k_ref[0, pl.ds(start, size), :]
        for a in range(nm):
            vt = v_ref[0, v_blk[a] * VT_ROWS:(v_blk[a] + 1) * VT_ROWS, pl.ds(start, size)]
            st = _dot(kc, qt_maps[a])
            m_prev = None if mode == "first" else m_scr[a]
            if mode == "fast":
                pf = jnp.exp2(st - m_prev)
                rise = jnp.maximum(jnp.max(pf, axis=0, keepdims=True), 1.0)
                acc_scr[a] = (acc_scr[a] + _dot(vt, pf.astype(BF16))) * (1.0 / rise)
                m_scr[a] = m_prev + jnp.log2(rise)
                gap_scr[a] = jnp.maximum(gap_scr[a], rise)
                continue
            cm = jnp.max(st, axis=0, keepdims=True)
            if mode == "first":
                acc_scr[a] = _dot(vt, jnp.exp2(st - cm).astype(BF16))
                m_scr[a] = cm
                gap_scr[a] = jnp.ones(cm.shape, F32)
            else:
                m_new = jnp.maximum(m_prev, cm)
                p = jnp.exp2(st - m_new).astype(BF16)
                acc_scr[a] = acc_scr[a] * jnp.exp2(m_prev - m_new) + _dot(vt, p)
                m_scr[a] = m_new

    def sweep(mode):
        chunk(0, CTX_LEN, "first")
        if n_lat:

            def body(c, carry):
                chunk(pl.multiple_of(CTX_LEN + c * tk, LANES), tk, mode)
                return carry

            lax.fori_loop(0, n_lat, body, 0)

    sweep("fast")
    if n_lat:
        worst = jnp.max(jnp.concatenate([gap_scr[a] for a in range(nm)], axis=0))

        @pl.when(worst > 2.0 ** GAP_LIMIT)
        def _():
            sweep("exact")

    return [acc_scr[a, 0:64, :] / acc_scr[a, 64:65, :] for a in range(nm)]


def _attention_calls(kern, name, nm, q, k, vt, qw, kw, tk, extra_in=(), extra_specs=()):
    B, S, _ = q.shape
    T = S - CTX_LEN

    def call(tq, n_tiles, n_lat, q_spec, kv_rows, call_name):
        return pl.pallas_call(
            functools.partial(kern, n_lat=n_lat, tk=tk),
            grid=(B, 2, n_tiles),
            in_specs=list(extra_specs) + [
                q_spec,
                pl.BlockSpec((1, kv_rows, kw), lambda b, p, i: (b, 0, p)),
                pl.BlockSpec((1, 2 * VT_ROWS, kv_rows), lambda b, p, i: (b, p, 0))],
            out_specs=pl.BlockSpec((1, tq, LANES), lambda b, p, i: (b, i, p)),
            out_shape=jax.ShapeDtypeStruct((B, tq * n_tiles, 256), BF16),
            scratch_shapes=[pltpu.VMEM((nm, 1, tq), F32), pltpu.VMEM((nm, VT_ROWS, tq), F32),
                            pltpu.VMEM((nm, 1, tq), F32)],
            compiler_params=_cparams(("parallel", "parallel", "parallel")),
            name=call_name,
        )(*extra_in, q, k, vt)

    y_ctx = call(CTX_LEN, 1, 0, pl.BlockSpec((1, CTX_LEN, qw), lambda b, p, i: (b, 0, p)),
                 CTX_LEN, name + "_ctx")
    tq = min(TQ, T)
    y_lat = call(tq, T // tq, T // tk,
                 pl.BlockSpec((pl.Element(1), pl.Element(tq), pl.Element(qw)),
                              lambda b, p, i: (b, (i * (tq // CTX_LEN) + 1) * CTX_LEN, p * qw)),
                 S, name)
    return jnp.concatenate([y_ctx, y_lat], axis=1)


def _diff_attn(lam_ref, g_ref, q_ref, k_ref, v_ref, o_ref, m_scr, acc_scr, gap_scr, *,
               n_lat, tk, lambda_init):
    qb = q_ref[0]
    lane = _lane(qb.shape)
    q_maps = []
    for a in range(4):
        msk = (lane >= 32 * a) & (lane < 32 * (a + 1))
        q_maps.append(jnp.where(msk, qb, jnp.zeros_like(qb)))
    n = _flash(q_maps, (0, 0, 1, 1), k_ref, v_ref, m_scr, acc_scr, gap_scr, n_lat=n_lat, tk=tk)
    lp = lam_ref[...]
    lam = (jnp.exp(jnp.sum(lp[0:1] * lp[1:2], axis=-1, keepdims=True))
           - jnp.exp(jnp.sum(lp[2:3] * lp[3:4], axis=-1, keepdims=True)) + lambda_init)
    ys = []
    for hh in range(2):
        o = n[2 * hh] - lam * n[2 * hh + 1]
        ms = jnp.mean(o * o, axis=0, keepdims=True)
        ys.append((o * lax.rsqrt(ms + NORM_EPS) * g_ref[...] * (1.0 - lambda_init)).T)
    o_ref[0] = jnp.concatenate(ys, axis=1).astype(BF16)


def _diff(lam_p, subln_g2, q, k, v, lambda_init, tk):
    return _attention_calls(
        functools.partial(_diff_attn, lambda_init=lambda_init), "diff_attn", 4, q, k, v, LANES, LANES,
        tk, extra_in=(lam_p, subln_g2),
        extra_specs=(pl.BlockSpec(lam_p.shape, lambda b, p, i: (0, 0)),
                     pl.BlockSpec((64, 1), lambda b, p, i: (0, 0))))


def _mla_attn(q_ref, k_ref, v_ref, o_ref, m_scr, acc_scr, gap_scr, *, n_lat, tk):
    q_maps = [q_ref[0, :, 0:256], q_ref[0, :, 256:512]]
    n = _flash(q_maps, (0, 1), k_ref, v_ref, m_scr, acc_scr, gap_scr, n_lat=n_lat, tk=tk)
    o_ref[0] = jnp.concatenate([n[0].T, n[1].T], axis=1).astype(BF16)


def _mla(q, k, v, tk):
    return _attention_calls(_mla_attn, "mla_attn", 2, q, k, v, 512, 256, tk)


def _ret_tables(dl, backward):
    L = RET_CHUNK
    lg = -(jnp.maximum(-dl, 0.0) + jnp.log(1.0 + jnp.exp(-jnp.abs(dl))))
    row = lax.broadcasted_iota(jnp.int32, (L, LANES), 0).astype(F32)
    col = lax.broadcasted_iota(jnp.int32, (L, LANES), 1)
    colf = col.astype(F32)
    if backward:
        qd, kd, rel = jnp.exp((L - row) * lg), jnp.exp(row * lg), colf - row
    else:
        qd, kd, rel = jnp.exp((row + 1.0) * lg), jnp.exp((L - 1.0 - row) * lg), row - colf
    decs = []
    for hh in range(2):
        lg_h = jnp.sum(jnp.where(col[0:1] == 64 * hh, lg, 0.0), axis=-1, keepdims=True)
        decs.append(jnp.where(rel >= 0, jnp.exp(jnp.maximum(rel, 0.0) * lg_h), 0.0))
    return qd, kd, decs[0], decs[1], jnp.exp(float(L) * lg)


def _ret_chunk(q, k, v, tab, cd, s_ref):
    col = _lane((RET_CHUNK, LANES))
    qf, kf = q.astype(F32), k.astype(F32)
    o = _dot((qf * tab[0]).astype(BF16), s_ref[...].astype(BF16))
    outs = []
    for hh in range(2):
        hmask = (col >= 64 * hh) & (col < 64 * (hh + 1))
        a = _dot_nt(jnp.where(hmask, q, jnp.zeros_like(q)), k) * tab[2 + hh]
        outs.append(_dot(a.astype(BF16), v))
    o = o + jnp.where(col < 64, outs[0], outs[1])
    rowi = lax.broadcasted_iota(jnp.int32, (LANES, LANES), 0)
    coli = lax.broadcasted_iota(jnp.int32, (LANES, LANES), 1)
    s_new = s_ref[...] * cd + _dot_tn((kf * tab[1]).astype(BF16), v)
    s_ref[...] = jnp.where((rowi < 64) == (coli < 64), s_new, 0.0)
    return o


def _ret_kernel(dl_ref, qf_ref, kf_ref, vf_ref, qb_ref, kb_ref, vb_ref, of_ref, ob_ref,
                s_scr, tab_scr, cd_scr):
    nb = qf_ref.shape[0]

    @pl.when(pl.program_id(0) == 0)
    def _():
        s_scr[...] = jnp.zeros(s_scr.shape, F32)
        for d in range(2):
            for p in range(2):
                tabs = _ret_tables(dl_ref[d, p], backward=(d == 1))
                for t in range(4):
                    tab_scr[d, p, t] = tabs[t]
                cd_scr[d, p] = tabs[4]

    for d, (q_ref, k_ref, v_ref, o_ref) in enumerate(((qf_ref, kf_ref, vf_ref, of_ref),
                                                      (qb_ref, kb_ref, vb_ref, ob_ref))):
        for p in range(2):
            tab = [tab_scr[d, p, t] for t in range(4)]
            cd = cd_scr[d, p]
            ln = slice(p * LANES, (p + 1) * LANES)
            for b in range(nb):
                o_ref[b, :, ln] = _ret_chunk(q_ref[b, :, ln], k_ref[b, :, ln], v_ref[b, :, ln],
                                             tab, cd, s_scr.at[d, b, p])


def _retention(dl_lane, q, k, v):
    B, S, _ = q.shape
    L = RET_CHUNK
    nc = S // L
    ncc = CTX_LEN // L
    fwd = pl.BlockSpec((B, L, 256), lambda s: (0, s, 0))
    bwd = pl.BlockSpec((B, L, 256), lambda s: (0, jnp.where(s < ncc, ncc - 1 - s, nc - 1 + ncc - s), 0))
    return pl.pallas_call(
        _ret_kernel,
        grid=(nc,),
        in_specs=[pl.BlockSpec(dl_lane.shape, lambda s: (0, 0, 0, 0)), fwd, fwd, fwd, bwd, bwd, bwd],
        out_specs=[fwd, bwd],
        out_shape=[jax.ShapeDtypeStruct((B, S, 256), F32)] * 2,
        scratch_shapes=[pltpu.VMEM((2, B, 2, LANES, LANES), F32),
                        pltpu.VMEM((2, 2, 4, L, LANES), F32), pltpu.VMEM((2, 2, 1, LANES), F32)],
        compiler_params=_cparams(("arbitrary",)),
        name="retention",
    )(dl_lane, q, k, v, q, k, v)


def _gated_group_norm(o, g, gn):
    lo = _lane(o.shape) < 64

    def halves(t):
        a = jnp.sum(jnp.where(lo, t, 0.0), axis=-1, keepdims=True) * (1.0 / 64)
        b = jnp.sum(jnp.where(lo, 0.0, t), axis=-1, keepdims=True) * (1.0 / 64)
        return jnp.where(lo, a, b)

    d = o - halves(o)
    var = halves(d * d)
    return d * lax.rsqrt(var + NORM_EPS) * gn * (g * jax.nn.sigmoid(g))


def _outproj_kernel(x_ref, ya_ref, yb_ref, of_ref, ob_ref, rg_ref, gn_ref, ym_ref, w_ref, g1_ref,
                    sh_ref, sc_ref, ng_ref, wr_ref, br_ref, xo_ref, h_ref, rt_ref):
    mix = (_dot(ya_ref[0], w_ref[0:256]) + _dot(yb_ref[0], w_ref[256:512])
           + _dot(ym_ref[0], w_ref[768:1024]))
    for j in range(2):
        ln = slice(j * LANES, (j + 1) * LANES)
        yr = _gated_group_norm(of_ref[0, :, ln] + ob_ref[0, :, ln], rg_ref[0, :, ln].astype(F32),
                               gn_ref[:, ln])
        mix = mix + _dot(yr.astype(BF16), w_ref[512 + j * LANES:512 + (j + 1) * LANES])
    x = x_ref[0] + g1_ref[...] * mix
    xo_ref[0] = x
    h = _rms(x, ng_ref[...]) * (1.0 + sc_ref[...]) + sh_ref[...]
    h_ref[0] = h
    h_hi = h.astype(BF16)
    h_lo = (h - h_hi.astype(F32)).astype(BF16)
    logits = (_dot(h_hi, wr_ref[0]) + _dot(h_lo, wr_ref[0]) + _dot(h_hi, wr_ref[1])
              + br_ref[...])
    lane = _lane(logits.shape).astype(F32)

    def first_argmax(vals, mask):
        mx = jnp.max(jnp.where(mask, vals, NEG_INF), axis=-1, keepdims=True)
        idx = jnp.min(jnp.where(mask & (vals == mx), lane, float(LANES)), axis=-1, keepdims=True)
        return mx, idx

    gmask = lane < MOE_GROUPS
    g_max, g_idx = first_argmax(logits, gmask)
    g_w = 1.0 / jnp.sum(jnp.exp(jnp.where(gmask, logits - g_max, NEG_INF)), axis=-1, keepdims=True)
    e_lo = MOE_GROUPS + g_idx * MOE_PER_GROUP
    emask = (lane >= e_lo) & (lane < e_lo + MOE_PER_GROUP)
    v1, i1 = first_argmax(logits, emask)
    v2, i2 = first_argmax(logits, emask & (lane != i1))
    t = jnp.exp(v2 - v1)
    w1 = g_w / (1.0 + t)
    w2 = w1 * t
    rt = jnp.where(lane == 0, i1 - MOE_GROUPS,
                   jnp.where(lane == 1, i2 - MOE_GROUPS,
                             jnp.where(lane == 2, w1, jnp.where(lane == 3, w2, 0.0))))
    rt_ref[0] = rt


def _outproj(x, ya, yb, o_f, o_b, rg, gn_g, ym, w_out_p, mod, nbatch, norm_g, w_route, b_route):
    B, S, D = x.shape
    nt = S // TM
    tok = lambda w: pl.BlockSpec((1, TM, w), lambda b, i: (b, i, 0))
    full = lambda a: pl.BlockSpec(a.shape, lambda b, i: (0,) * a.ndim)
    return pl.pallas_call(
        _outproj_kernel,
        grid=(B, nt),
        in_specs=[tok(D), tok(256), tok(256), tok(256), tok(256), tok(256), full(gn_g), tok(256),
                  full(w_out_p),
                  _mod_spec(2, nbatch), _mod_spec(3, nbatch), _mod_spec(4, nbatch), full(norm_g),
                  full(w_route), full(b_route)],
        out_specs=[tok(D), tok(D), tok(LANES)],
        out_shape=[jax.ShapeDtypeStruct((B, S, D), F32), jax.ShapeDtypeStruct((B, S, D), F32),
                   jax.ShapeDtypeStruct((B, S, LANES), F32)],
        compiler_params=_cparams(("parallel", "parallel")),
        name="outproj_router",
    )(x, ya, yb, o_f, o_b, rg, gn_g, ym, w_out_p, mod, mod, mod, norm_g, w_route, b_route)


def _rank_kernel(rt_ref, rank_ref, cnt_ref, carry):
    @pl.when(pl.program_id(0) == 0)
    def _():
        carry[...] = jnp.zeros(carry.shape, F32)

    rt = rt_ref[...]
    lane = _lane(rt.shape)
    lanef = lane.astype(F32)
    oh0 = jnp.where(lanef == rt[:, 0:1], 1.0, 0.0)
    oh1 = jnp.where(lanef == rt[:, 1:2], 1.0, 0.0)
    cnt = oh0 + oh1
    r_io = lax.broadcasted_iota(jnp.int32, (TM, TM), 0)
    c_io = lax.broadcasted_iota(jnp.int32, (TM, TM), 1)
    tri = jnp.where(r_io > c_io, 1.0, 0.0).astype(BF16)
    before = _dot(tri, cnt.astype(BF16)) + carry[...]
    r0 = jnp.sum(oh0 * before, axis=-1, keepdims=True)
    r1 = jnp.sum(oh1 * before, axis=-1, keepdims=True)
    rank_ref[...] = jnp.where(lane == 0, r0, jnp.where(lane == 1, r1, 0.0))
    carry[...] = carry[...] + jnp.sum(cnt, axis=0, keepdims=True)
    cnt_ref[...] = carry[...]


def _ranks(route):
    n = route.shape[0]
    return pl.pallas_call(
        _rank_kernel,
        grid=(n // TM,),
        in_specs=[pl.BlockSpec((TM, LANES), lambda i: (i, 0))],
        out_specs=[pl.BlockSpec((TM, LANES), lambda i: (i, 0)),
                   pl.BlockSpec((1, LANES), lambda i: (0, 0))],
        out_shape=[jax.ShapeDtypeStruct((n, LANES), F32), jax.ShapeDtypeStruct((1, LANES), F32)],
        scratch_shapes=[pltpu.VMEM((1, LANES), F32)],
        compiler_params=_cparams(("arbitrary",)),
        name="moe_rank",
    )(route)


def _dispatch_kernel(dest_ref, h_ref, xb_in_ref, xb_ref, sem):
    del xb_in_ref
    base = pl.program_id(0) * TM

    def copy(r, k):
        d = dest_ref[(base + r) * 2 + k]
        return pltpu.make_async_copy(h_ref.at[pl.ds(r, 1)], xb_ref.at[pl.ds(d, 1)], sem)

    def issue(r, c):
        copy(r, 0).start()
        copy(r, 1).start()
        return c

    def drain(r, c):
        copy(r, 0).wait()
        copy(r, 1).wait()
        return c

    lax.fori_loop(0, TM, issue, 0, unroll=DMA_UNROLL)
    lax.fori_loop(0, TM, drain, 0, unroll=DMA_UNROLL)


def _dispatch(dest, h, cap):
    n, d = h.shape
    return pl.pallas_call(
        _dispatch_kernel,
        grid_spec=pltpu.PrefetchScalarGridSpec(
            num_scalar_prefetch=1,
            grid=(n // TM,),
            in_specs=[pl.BlockSpec((TM, d), lambda i, dest: (i, 0)),
                      pl.BlockSpec(memory_space=pl.ANY)],
            out_specs=pl.BlockSpec(memory_space=pl.ANY),
            scratch_shapes=[pltpu.SemaphoreType.DMA(())]),
        out_shape=jax.ShapeDtypeStruct((cap, d), h.dtype),
        input_output_aliases={2: 0},
        compiler_params=_cparams(("arbitrary",)),
        name="moe_dispatch",
    )(dest, h, jnp.zeros((cap, d), h.dtype))


def _expert_kernel(be_ref, nu_ref, x_ref, wg_ref, wu_ref, wd_ref, y_ref, wg_s, wu_s, wd_s):
    i = pl.program_id(0)

    @pl.when((i == 0) | (be_ref[i] != be_ref[jnp.maximum(i - 1, 0)]))
    def _():
        wg_s[...] = wg_ref[0].astype(BF16)
        wu_s[...] = wu_ref[0].astype(BF16)
        wd_s[...] = wd_ref[0].astype(BF16)

    @pl.when(i < nu_ref[0])
    def _():
        xb = x_ref[...].astype(BF16)
        gate = _dot(xb, wg_s[...])
        up = _dot(xb, wu_s[...])
        hid = (gate * jax.nn.sigmoid(gate) * up).astype(BF16)
        y_ref[...] = _dot(hid, wd_s[...])

    @pl.when(i >= nu_ref[0])
    def _():
        y_ref[...] = jnp.zeros(y_ref.shape, y_ref.dtype)


def _experts(block_e, n_used, xb, w_gate, w_up, w_down, layer):
    cap, d = xb.shape
    hdim = w_gate.shape[-1]
    return pl.pallas_call(
        _expert_kernel,
        grid_spec=pltpu.PrefetchScalarGridSpec(
            num_scalar_prefetch=2,
            grid=(cap // MOE_ROWS,),
            in_specs=[pl.BlockSpec((MOE_ROWS, d), lambda i, be, nu: (i, 0)),
                      pl.BlockSpec((None, 1, d, hdim), lambda i, be, nu: (layer, be[i], 0, 0)),
                      pl.BlockSpec((None, 1, d, hdim), lambda i, be, nu: (layer, be[i], 0, 0)),
                      pl.BlockSpec((None, 1, hdim, d), lambda i, be, nu: (layer, be[i], 0, 0))],
            out_specs=pl.BlockSpec((MOE_ROWS, d), lambda i, be, nu: (i, 0)),
            scratch_shapes=[pltpu.VMEM((d, hdim), BF16), pltpu.VMEM((d, hdim), BF16),
                            pltpu.VMEM((hdim, d), BF16)]),
        out_shape=jax.ShapeDtypeStruct((cap, d), F32),
        compiler_params=_cparams(("arbitrary",)),
        name="moe_experts",
    )(block_e, n_used, xb, w_gate, w_up, w_down)


def _combine_kernel(dest_ref, x_ref, rt_ref, g2_ref, fg_ref, yb_ref, o_ref, gath, sem, *,
                    tile_off, tiles_per_batch, final):
    b, i = pl.program_id(0), pl.program_id(1)
    base = (b * tiles_per_batch + i + tile_off) * TM

    def copy(r, k):
        d = dest_ref[(base + r) * 2 + k]
        return pltpu.make_async_copy(yb_ref.at[pl.ds(d, 1)], gath.at[k, pl.ds(r, 1)], sem)

    def issue(r, c):
        copy(r, 0).start()
        copy(r, 1).start()
        return c

    def drain(r, c):
        copy(r, 0).wait()
        copy(r, 1).wait()
        return c

    lax.fori_loop(0, TM, issue, 0, unroll=DMA_UNROLL)
    lax.fori_loop(0, TM, drain, 0, unroll=DMA_UNROLL)
    rt = rt_ref[0]
    m = rt[:, 2:3] * gath[0] + rt[:, 3:4] * gath[1]
    x = x_ref[0] + g2_ref[...] * m
    if final:
        x = _rms(x, fg_ref[...])
    o_ref[0] = x


def _combine(dest, x, route, mod, nbatch, final_g, yb, final):
    B, S, D = x.shape
    nt = S // TM
    off = 1 if final else 0
    g2_spec = pl.BlockSpec((None, 1, D_MODEL),
                           lambda b, i, dest: (jnp.where(i + off == 0, nbatch, b), 0, 5))
    return pl.pallas_call(
        functools.partial(_combine_kernel, tile_off=off, tiles_per_batch=nt, final=final),
        grid_spec=pltpu.PrefetchScalarGridSpec(
            num_scalar_prefetch=1,
            grid=(B, nt - off),
            in_specs=[pl.BlockSpec((1, TM, D), lambda b, i, dest: (b, i + off, 0)),
                      pl.BlockSpec((1, TM, LANES), lambda b, i, dest: (b, i + off, 0)),
                      g2_spec,
                      pl.BlockSpec((1, D), lambda b, i, dest: (0, 0)),
                      pl.BlockSpec(memory_space=pl.ANY)],
            out_specs=pl.BlockSpec((1, TM, D), lambda b, i, dest: (b, i, 0)),
            scratch_shapes=[pltpu.VMEM((2, TM, D), F32), pltpu.SemaphoreType.DMA(())]),
        out_shape=jax.ShapeDtypeStruct((B, S - off * TM, D), F32),
        compiler_params=_cparams(("arbitrary", "arbitrary")),
        name="moe_combine",
    )(dest, x, route, mod, final_g, yb)


def _rope_tables(seq):
    n_rows = seq // GRID_W
    pos = jnp.arange(max(n_rows, GRID_W), dtype=F32)[:, None]
    lane = jnp.arange(LANES, dtype=jnp.int32)
    tabs = []
    for r in (64, 32):
        nf = r // 4
        inv = 1.0 / (ROPE_BASE ** (jnp.arange(nf, dtype=F32) / nf))
        ang = pos * inv[lane % nf][None, :]
        first = ((lane % (2 * nf)) < nf)[None, :]
        by_row = ((lane % r) < 2 * nf)[None, None, :]
        for small in (jnp.cos(ang), jnp.where(first, -jnp.sin(ang), jnp.sin(ang))):
            full = jnp.where(by_row, small[:n_rows, None, :], small[None, :GRID_W, :])
            tabs.append(full.reshape(seq, LANES))
    ident = (jnp.ones((CTX_LEN, LANES), F32), jnp.zeros((CTX_LEN, LANES), F32))
    return [jnp.concatenate([ident[j % 2], tab], axis=0) for j, tab in enumerate(tabs)]


def _layer_weights(w_in, w_out, w_uq, w_ukv, w_group, b_group, w_expert, b_expert):
    perm = jnp.array([0, 2, 1, 3])
    wq = w_in[:, :256].reshape(D_MODEL, 4, 64)[:, perm].reshape(D_MODEL, 256)
    w_in_p = jnp.concatenate([wq, w_in[:, 256:], jnp.zeros((D_MODEL, D_IN_PAD - w_in.shape[1]), F32)],
                             axis=1).astype(BF16)
    wo = jnp.concatenate([w_out[:256].reshape(4, 64, D_MODEL)[perm].reshape(256, D_MODEL), w_out[256:]],
                         axis=0).astype(BF16)
    uq = w_uq.reshape(256, 4, 96)
    blocks = []
    for h in range(4):
        blk = jnp.zeros((256, 256), F32)
        blk = blk.at[:, 64 * (h % 2):64 * (h % 2) + 64].set(uq[:, h, :64])
        blk = blk.at[:, LANES:LANES + 32].set(uq[:, h, 64:])
        blocks.append(blk)
    wuq = jnp.concatenate(blocks, axis=1).astype(BF16)
    ukv = w_ukv.reshape(128, 4, 128)
    wukk = ukv[:, :, :64].reshape(128, 256).astype(BF16)
    wukv = ukv[:, :, 64:].reshape(128, 256).astype(BF16)
    pad = LANES - MOE_GROUPS - MOE_EXPERTS
    w_route = jnp.concatenate([w_group, w_expert, jnp.zeros((D_MODEL, pad), F32)], axis=1)
    w_hi = w_route.astype(BF16)
    w_route = jnp.stack([w_hi, (w_route - w_hi.astype(F32)).astype(BF16)])
    b_route = jnp.concatenate([b_group, b_expert, jnp.zeros((pad,), F32)]).reshape(1, LANES)
    return w_in_p, wo, wuq, wukk, wukv, w_route, b_route


def _moe_plan(route, ranks, counts, cap):
    e = route[:, :2].astype(jnp.int32)
    rank = ranks[:, :2].astype(jnp.int32)
    cnt = counts[0, :MOE_EXPERTS].astype(jnp.int32)
    padded = (cnt + MOE_ROWS - 1) // MOE_ROWS * MOE_ROWS
    pend = jnp.cumsum(padded)
    pstart = pend - padded
    dest = (pstart[e] + rank).reshape(-1)
    nblk = cap // MOE_ROWS
    blk_start = jnp.arange(nblk, dtype=jnp.int32) * MOE_ROWS
    block_e = jnp.minimum(jnp.sum((pend[None, :] <= blk_start[:, None]).astype(jnp.int32), axis=1),
                          MOE_EXPERTS - 1)
    n_used = (pend[-1:] // MOE_ROWS).astype(jnp.int32)
    return dest, block_e, n_used


def kernel(x, c, ctx, c_ctx, w_mod, b_mod, norm1_g, norm2_g, w_in, w_out, swa_sink, diff_lambda, diff_subln_g, ret_decay_logit, ret_gn_g, mla_q_norm_g, mla_w_uq, mla_kv_norm_g, mla_w_ukv, moe_w_group, moe_b_group, moe_w_expert, moe_b_expert, moe_w_gate, moe_w_up, moe_w_down, final_norm_g):
    B, T, D = x.shape
    S = CTX_LEN + T
    n_tok = B * S
    diff_tk, mla_tk = min(DIFF_TK, T), min(MLA_TK, T)
    cap = -(-(2 * n_tok + MOE_EXPERTS * (MOE_ROWS - 1)) // MOE_ROWS) * MOE_ROWS
    xs = jnp.concatenate([ctx, x], axis=1)
    cvec = jnp.zeros((8, D), F32).at[:B].set(c).at[B].set(c_ctx)
    tabs = _rope_tables(T)
    row = lambda v: v.reshape(1, -1)

    for l in range(DEPTH):
        lambda_init = 0.8 - 0.6 * math.exp(-0.3 * l)
        w_in_p, wo, wuq, wukk, wukv, w_route, b_route = _layer_weights(
            w_in[l], w_out[l], mla_w_uq[l], mla_w_ukv[l], moe_w_group[l], moe_b_group[l],
            moe_w_expert[l], moe_b_expert[l])
        mod = _modulation(cvec, w_mod[l], b_mod[l]).reshape(8, 1, 6 * D)
        (swaq, swak, swav, dq, dk, dv, rq, rk, rv, rg, mq, mk, mv) = _inproj(
            xs, mod, B, row(norm1_g[l]), w_in_p, tabs, row(mla_q_norm_g[l]), wuq,
            row(mla_kv_norm_g[l]), wukk, wukv)
        sink_p = swa_sink[l]
        ya = _swa(sink_p, swaq, swak, swav)
        yb = _diff(diff_lambda[l], diff_subln_g[l].reshape(64, 1), dq, dk, dv, lambda_init,
                   diff_tk)
        dl_lane = jnp.repeat(ret_decay_logit[l], 64, axis=-1).reshape(2, 2, 1, LANES)
        o_f, o_b = _retention(dl_lane, rq, rk, rv)
        ym = _mla(mq, mk, mv, mla_tk)
        xs, h2, route = _outproj(xs, ya, yb, o_f, o_b, rg, row(ret_gn_g[l]), ym, wo, mod, B,
                                 row(norm2_g[l]), w_route, b_route)
        route2 = route.reshape(n_tok, LANES)
        ranks, counts = _ranks(route2)
        dest, block_e, n_used = _moe_plan(route2, ranks, counts, cap)
        xb = _dispatch(dest, h2.reshape(n_tok, D), cap)
        yb_e = _experts(block_e, n_used, xb, moe_w_gate, moe_w_up, moe_w_down, l)
        final = l == DEPTH - 1
        xs = _combine(dest, xs, route, mod, B, row(final_norm_g), yb_e, final)
    return xs
```

```python
import functools
import math

import jax
import jax.numpy as jnp
from jax import lax
from jax.experimental import pallas as pl
from jax.experimental.pallas import tpu as pltpu

F32 = jnp.float32
BF16 = jnp.bfloat16

D_MODEL = 1024
CTX_LEN = 256
GRID_W = 64
ROPE_BASE = 10000.0
NORM_EPS = 1e-6
NEG_INF = -1e30
DEPTH = 2

SWA_WINDOW = 128
RET_CHUNK = 128
MLA_SCALE = (64 + 32) ** -0.5
LOG2E = math.log2(math.e)
MOE_GROUPS = 4
MOE_PER_GROUP = 8
MOE_EXPERTS = 32
MOE_HIDDEN = 512

LANES = 128
TM = 256
DIFF_TQ = 1024
MLA_TQ = 2048
DIFF_TK = 2048
MLA_TK = 2048
VT_ROWS = 80
GAP_LIMIT = 64.0
MOE_ROWS = 256
DMA_UNROLL = 8
VMEM_LIMIT = 56 * 1024 * 1024

C_SWA_Q, C_SWA_K, C_SWA_V = 0, 256, 384
C_DIF_Q, C_DIF_K, C_DIF_V = 512, 768, 1024
C_RET_Q, C_RET_K, C_RET_V, C_RET_G = 1280, 1536, 1792, 2048
C_MLA_Q, C_MLA_KV, C_MLA_KR = 2304, 2560, 2688
D_IN_PAD = 2816


def _cparams(sem):
    return pltpu.CompilerParams(dimension_semantics=sem, vmem_limit_bytes=VMEM_LIMIT)


def _dot(a, b):
    return jnp.dot(a, b, preferred_element_type=F32)


def _dot_nt(a, b):
    return lax.dot_general(a, b, (((1,), (1,)), ((), ())), preferred_element_type=F32)


def _dot_tn(a, b):
    return lax.dot_general(a, b, (((0,), (0,)), ((), ())), preferred_element_type=F32)


def _rms(x, g):
    ms = jnp.mean(x * x, axis=-1, keepdims=True)
    return x * lax.rsqrt(ms + NORM_EPS) * g


def _lane(shape):
    return lax.broadcasted_iota(jnp.int32, shape, len(shape) - 1)


def _rope(z, cos, sin, nf):
    first = (_lane(z.shape) % (2 * nf)) < nf
    partner = jnp.where(first, pltpu.roll(z, LANES - nf, 1), pltpu.roll(z, nf, 1))
    return z * cos + partner * sin


def _mod_kernel(a_ref, w_ref, b_ref, o_ref):
    a = a_ref[...]
    act = a * jax.nn.sigmoid(a)
    o_ref[...] = jnp.dot(act, w_ref[...], preferred_element_type=F32,
                         precision=lax.Precision.HIGHEST) + b_ref[...]


def _modulation(cvec, w_mod, b_mod):
    n = w_mod.shape[1]
    bn = 512
    return pl.pallas_call(
        _mod_kernel,
        grid=(n // bn,),
        in_specs=[pl.BlockSpec((8, D_MODEL), lambda j: (0, 0)),
                  pl.BlockSpec((D_MODEL, bn), lambda j: (0, j)),
                  pl.BlockSpec((1, bn), lambda j: (0, j))],
        out_specs=pl.BlockSpec((8, bn), lambda j: (0, j)),
        out_shape=jax.ShapeDtypeStruct((8, n), F32),
        compiler_params=_cparams(("parallel",)),
        name="modulation",
    )(cvec, w_mod, b_mod.reshape(1, n))


def _mod_spec(chunk, nbatch):
    return pl.BlockSpec((None, 1, D_MODEL),
                        lambda b, i: (jnp.where(i == 0, nbatch, b), 0, chunk))


def _inproj_kernel(x_ref, g_ref, sh_ref, sc_ref, w_ref, c64_ref, s64_ref, c32_ref, s32_ref,
                   qng_ref, wuq_ref, kvng_ref, wukk_ref, wukv_ref,
                   swaq, swak, swav, dq, dk, dv, rq, rk, rv, rg, mq, mk, mv):
    x = x_ref[0]
    h = _rms(x, g_ref[...]) * (1.0 + sc_ref[...]) + sh_ref[...]
    hb = h.astype(BF16)
    c64, s64, c32, s32 = c64_ref[...], s64_ref[...], c32_ref[...], s32_ref[...]

    def proj(c0, width):
        return _dot(hb, w_ref[:, c0:c0 + width])

    def rope_blocks(z, out, nf, scale):
        cos, sin = (c64, s64) if nf == 16 else (c32, s32)
        for j in range(z.shape[1] // LANES):
            zz = _rope(z[:, j * LANES:(j + 1) * LANES], cos, sin, nf)
            if scale != 1.0:
                zz = zz * scale
            out[0, :, j * LANES:(j + 1) * LANES] = zz.astype(out.dtype)

    rope_blocks(proj(C_SWA_Q, 256), swaq, 16, 0.125)
    rope_blocks(proj(C_SWA_K, 128), swak, 16, 1.0)
    swav[0] = proj(C_SWA_V, 128).astype(BF16)
    def values_transposed(z, out):
        zt = z.T.astype(BF16)
        for hh in range(4):
            out[0, hh * VT_ROWS:hh * VT_ROWS + 64, :] = zt[hh * 64:(hh + 1) * 64]
            out[0, hh * VT_ROWS + 64:(hh + 1) * VT_ROWS, :] = jnp.ones((VT_ROWS - 64, zt.shape[1]), BF16)

    rope_blocks(proj(C_DIF_Q, 256), dq, 8, 32 ** -0.5 * LOG2E)
    rope_blocks(proj(C_DIF_K, 256), dk, 8, 1.0)
    values_transposed(proj(C_DIF_V, 256), dv)
    rope_blocks(proj(C_RET_Q, 256), rq, 16, 0.125)
    rope_blocks(proj(C_RET_K, 256), rk, 16, 1.0)
    rv[0] = proj(C_RET_V, 256).astype(BF16)
    rg[0] = proj(C_RET_G, 256).astype(BF16)

    ql = _rms(proj(C_MLA_Q, 256), qng_ref[...]).astype(BF16)
    qa = _dot(ql, wuq_ref[...]) * (MLA_SCALE * LOG2E)
    for hh in range(4):
        mq[0, :, hh * 256:hh * 256 + LANES] = qa[:, hh * 256:hh * 256 + LANES].astype(BF16)
        zr = _rope(qa[:, hh * 256 + LANES:(hh + 1) * 256], c32, s32, 8)
        mq[0, :, hh * 256 + LANES:(hh + 1) * 256] = zr.astype(BF16)
    kvl = _rms(proj(C_MLA_KV, 128), kvng_ref[...]).astype(BF16)
    kn = _dot(kvl, wukk_ref[...])
    kr = _rope(proj(C_MLA_KR, 128), c32, s32, 8).astype(BF16)
    for p in range(2):
        mk[0, :, p * 256:p * 256 + LANES] = kn[:, p * LANES:(p + 1) * LANES].astype(BF16)
        mk[0, :, p * 256 + LANES:(p + 1) * 256] = kr
    values_transposed(_dot(kvl, wukv_ref[...]), mv)


def _inproj(x, mod, nbatch, norm_g, w_in_p, tabs, qng, wuq, kvng, wukk, wukv):
    B, S, D = x.shape
    nt = S // TM
    tok = lambda w: pl.BlockSpec((1, TM, w), lambda b, i: (b, i, 0))
    full = lambda a: pl.BlockSpec(a.shape, lambda b, i: (0,) * a.ndim)
    tab = pl.BlockSpec((TM, LANES), lambda b, i: (i, 0))
    widths = [256, 128, 128, 256, 256, None, 256, 256, 256, 256, 1024, 512, None]
    vt_spec = pl.BlockSpec((1, 4 * VT_ROWS, TM), lambda b, i: (b, 0, i))
    vt_shape = jax.ShapeDtypeStruct((B, 4 * VT_ROWS, S), BF16)
    return pl.pallas_call(
        _inproj_kernel,
        grid=(B, nt),
        in_specs=[tok(D), full(norm_g), _mod_spec(0, nbatch), _mod_spec(1, nbatch), full(w_in_p),
                  tab, tab, tab, tab, full(qng), full(wuq), full(kvng), full(wukk), full(wukv)],
        out_specs=[vt_spec if w is None else tok(w) for w in widths],
        out_shape=[vt_shape if w is None else jax.ShapeDtypeStruct((B, S, w), BF16) for w in widths],
        compiler_params=_cparams(("parallel", "parallel")),
        name="inproj",
    )(x, norm_g, mod, mod, w_in_p, *tabs, qng, wuq, kvng, wukk, wukv)


def _swa_kernel(sink_ref, q_ref, kp_ref, ko_ref, kn_ref, vp_ref, vo_ref, vn_ref, kc_ref, vc_ref,
                o_ref, *, n_ctx_tiles, seq):
    i = pl.program_id(1)
    blk = RET_CHUNK
    n = i - n_ctx_tiles
    k_all = jnp.concatenate([kp_ref[0], ko_ref[0], kn_ref[0], kc_ref[0]], axis=0)
    v_all = jnp.concatenate([vp_ref[0], vo_ref[0], vn_ref[0], vc_ref[0]], axis=0)
    nk = 3 * blk + CTX_LEN
    r_io = lax.broadcasted_iota(jnp.int32, (blk, nk), 0)
    c_io = lax.broadcasted_iota(jnp.int32, (blk, nk), 1)
    kpos = (n - 1) * blk + c_io
    in_band = ((c_io >= r_io) & (c_io <= r_io + 2 * SWA_WINDOW) & (kpos >= 0) & (kpos < seq)
               & (n >= 0))
    valid = in_band | (c_io >= 3 * blk)
    lane = _lane((blk, LANES))
    heads = [(r, g) for r in range(2) for g in range(2)]
    qm = [jnp.where((lane >= 64 * g) & (lane < 64 * (g + 1)), q_ref[0, :, r * LANES:(r + 1) * LANES],
                    jnp.zeros((blk, LANES), BF16)) for r, g in heads]
    s = [jnp.where(valid, _dot_nt(q, k_all), NEG_INF) for q in qm]
    sink = [sink_ref[2 * g + r] for r, g in heads]
    m = [jnp.maximum(jnp.max(s[h], axis=-1, keepdims=True), sink[h]) for h in range(4)]
    p = [jnp.exp(s[h] - m[h]) for h in range(4)]
    l = [jnp.sum(p[h], axis=-1, keepdims=True) + jnp.exp(sink[h] - m[h]) for h in range(4)]
    o = [_dot(p[h].astype(BF16), v_all) / l[h] for h in range(4)]
    for r in range(2):
        o_ref[0, :, r * LANES:(r + 1) * LANES] = jnp.where(lane < 64, o[2 * r], o[2 * r + 1]).astype(BF16)


def _swa(sink, q, k, v):
    B, S, _ = q.shape
    blk = RET_CHUNK
    nt = S // blk
    nct = CTX_LEN // blk
    kvs = lambda f: pl.BlockSpec((1, blk, LANES), f)
    prev = lambda b, i: (b, jnp.maximum(i - 1, 0), 0)
    own = lambda b, i: (b, i, 0)
    nxt = lambda b, i: (b, jnp.minimum(i + 1, nt - 1), 0)
    ctx = pl.BlockSpec((1, CTX_LEN, LANES), lambda b, i: (b, 0, 0))
    return pl.pallas_call(
        functools.partial(_swa_kernel, n_ctx_tiles=nct, seq=S - CTX_LEN),
        grid=(B, nt),
        in_specs=[pl.BlockSpec(memory_space=pltpu.SMEM),
                  pl.BlockSpec((1, blk, 256), own),
                  kvs(prev), kvs(own), kvs(nxt), kvs(prev), kvs(own), kvs(nxt), ctx, ctx],
        out_specs=pl.BlockSpec((1, blk, 256), own),
        out_shape=jax.ShapeDtypeStruct((B, S, 256), BF16),
        compiler_params=_cparams(("parallel", "parallel")),
        name="swa",
    )(sink, q, k, k, k, v, v, v, k, v)


def _flash(q_maps, v_blk, k_ref, v_ref, m_scr, acc_scr, gap_scr, *, n_lat, tk):
    nm = len(q_maps)
    qt_maps = [q.astype(F32).T.astype(BF16) for q in q_maps]

    def chunk(start, size, mode):
        kc = k_ref[0, pl.ds(start, size), :]
        for a in range(nm):
            vt = v_ref[0, v_blk[a] * VT_ROWS:(v_blk[a] + 1) * VT_ROWS, pl.ds(start, size)]
            st = _dot(kc, qt_maps[a])
            m_prev = None if mode == "first" else m_scr[a]
            if mode == "fast":
                pf = jnp.exp2(st - m_prev)
                rise = jnp.maximum(jnp.max(pf, axis=0, keepdims=True), 1.0)
                acc_scr[a] = (acc_scr[a] + _dot(vt, pf.astype(BF16))) * (1.0 / rise)
                m_scr[a] = m_prev + jnp.log2(rise)
                gap_scr[a] = jnp.maximum(gap_scr[a], rise)
                continue
            cm = jnp.max(st, axis=0, keepdims=True)
            if mode == "first":
                acc_scr[a] = _dot(vt, jnp.exp2(st - cm).astype(BF16))
                m_scr[a] = cm
                gap_scr[a] = jnp.ones(cm.shape, F32)
            else:
                m_new = jnp.maximum(m_prev, cm)
                p = jnp.exp2(st - m_new).astype(BF16)
                acc_scr[a] = acc_scr[a] * jnp.exp2(m_prev - m_new) + _dot(vt, p)
                m_scr[a] = m_new

    def sweep(mode):
        chunk(0, CTX_LEN, "first")
        if n_lat:

            def body(c, carry):
                chunk(pl.multiple_of(CTX_LEN + c * tk, LANES), tk, mode)
                return carry

            lax.fori_loop(0, n_lat, body, 0)

    sweep("fast")
    if n_lat:
        worst = jnp.max(jnp.concatenate([gap_scr[a] for a in range(nm)], axis=0))

        @pl.when(worst > 2.0 ** GAP_LIMIT)
        def _():
            sweep("exact")

    return [acc_scr[a, 0:64, :] / acc_scr[a, 64:65, :] for a in range(nm)]


def _attention_calls(kern, name, nm, q, k, vt, qw, kw, tk, tq, extra_in=(), extra_specs=()):
    B, S, _ = q.shape
    T = S - CTX_LEN

    def call(tq, n_tiles, n_lat, q_spec, kv_rows, call_name):
        return pl.pallas_call(
            functools.partial(kern, n_lat=n_lat, tk=tk),
            grid=(B, 2, n_tiles),
            in_specs=list(extra_specs) + [
                q_spec,
                pl.BlockSpec((1, kv_rows, kw), lambda b, p, i: (b, 0, p)),
                pl.BlockSpec((1, 2 * VT_ROWS, kv_rows), lambda b, p, i: (b, p, 0))],
            out_specs=pl.BlockSpec((1, tq, LANES), lambda b, p, i: (b, i, p)),
            out_shape=jax.ShapeDtypeStruct((B, tq * n_tiles, 256), BF16),
            scratch_shapes=[pltpu.VMEM((nm, 1, tq), F32), pltpu.VMEM((nm, VT_ROWS, tq), F32),
                            pltpu.VMEM((nm, 1, tq), F32)],
            compiler_params=_cparams(("parallel", "parallel", "parallel")),
            name=call_name,
        )(*extra_in, q, k, vt)

    y_ctx = call(CTX_LEN, 1, 0, pl.BlockSpec((1, CTX_LEN, qw), lambda b, p, i: (b, 0, p)),
                 CTX_LEN, name + "_ctx")
    tq = min(tq, T)
    y_lat = call(tq, T // tq, T // tk,
                 pl.BlockSpec((pl.Element(1), pl.Element(tq), pl.Element(qw)),
                              lambda b, p, i: (b, (i * (tq // CTX_LEN) + 1) * CTX_LEN, p * qw)),
                 S, name)
    return jnp.concatenate([y_ctx, y_lat], axis=1)


def _diff_attn(lam_ref, g_ref, q_ref, k_ref, v_ref, o_ref, m_scr, acc_scr, gap_scr, *,
               n_lat, tk, lambda_init):
    qb = q_ref[0]
    lane = _lane(qb.shape)
    q_maps = []
    for a in range(4):
        msk = (lane >= 32 * a) & (lane < 32 * (a + 1))
        q_maps.append(jnp.where(msk, qb, jnp.zeros_like(qb)))
    n = _flash(q_maps, (0, 0, 1, 1), k_ref, v_ref, m_scr, acc_scr, gap_scr, n_lat=n_lat, tk=tk)
    lp = lam_ref[...]
    lam = (jnp.exp(jnp.sum(lp[0:1] * lp[1:2], axis=-1, keepdims=True))
           - jnp.exp(jnp.sum(lp[2:3] * lp[3:4], axis=-1, keepdims=True)) + lambda_init)
    ys = []
    for hh in range(2):
        o = n[2 * hh] - lam * n[2 * hh + 1]
        ms = jnp.mean(o * o, axis=0, keepdims=True)
        ys.append((o * lax.rsqrt(ms + NORM_EPS) * g_ref[...] * (1.0 - lambda_init)).T)
    o_ref[0] = jnp.concatenate(ys, axis=1).astype(BF16)


def _diff(lam_p, subln_g2, q, k, v, lambda_init, tk):
    return _attention_calls(
        functools.partial(_diff_attn, lambda_init=lambda_init), "diff_attn", 4, q, k, v, LANES, LANES,
        tk, DIFF_TQ, extra_in=(lam_p, subln_g2),
        extra_specs=(pl.BlockSpec(lam_p.shape, lambda b, p, i: (0, 0)),
                     pl.BlockSpec((64, 1), lambda b, p, i: (0, 0))))


def _mla_attn(q_ref, k_ref, v_ref, o_ref, m_scr, acc_scr, gap_scr, *, n_lat, tk):
    q_maps = [q_ref[0, :, 0:256], q_ref[0, :, 256:512]]
    n = _flash(q_maps, (0, 1), k_ref, v_ref, m_scr, acc_scr, gap_scr, n_lat=n_lat, tk=tk)
    o_ref[0] = jnp.concatenate([n[0].T, n[1].T], axis=1).astype(BF16)


def _mla(q, k, v, tk):
    return _attention_calls(_mla_attn, "mla_attn", 2, q, k, v, 512, 256, tk, MLA_TQ)


def _ret_tables(dl, backward):
    L = RET_CHUNK
    lg = -(jnp.maximum(-dl, 0.0) + jnp.log(1.0 + jnp.exp(-jnp.abs(dl))))
    row = lax.broadcasted_iota(jnp.int32, (L, LANES), 0).astype(F32)
    col = lax.broadcasted_iota(jnp.int32, (L, LANES), 1)
    colf = col.astype(F32)
    if backward:
        qd, kd, rel = jnp.exp((L - row) * lg), jnp.exp(row * lg), colf - row
    else:
        qd, kd, rel = jnp.exp((row + 1.0) * lg), jnp.exp((L - 1.0 - row) * lg), row - colf
    decs = []
    for hh in range(2):
        lg_h = jnp.sum(jnp.where(col[0:1] == 64 * hh, lg, 0.0), axis=-1, keepdims=True)
        decs.append(jnp.where(rel >= 0, jnp.exp(jnp.maximum(rel, 0.0) * lg_h), 0.0))
    return qd, kd, decs[0], decs[1], jnp.exp(float(L) * lg)


def _ret_chunk(q, k, v, tab, cd, s_ref):
    col = _lane((RET_CHUNK, LANES))
    qf, kf = q.astype(F32), k.astype(F32)
    o = _dot((qf * tab[0]).astype(BF16), s_ref[...].astype(BF16))
    outs = []
    for hh in range(2):
        hmask = (col >= 64 * hh) & (col < 64 * (hh + 1))
        a = _dot_nt(jnp.where(hmask, q, jnp.zeros_like(q)), k) * tab[2 + hh]
        outs.append(_dot(a.astype(BF16), v))
    o = o + jnp.where(col < 64, outs[0], outs[1])
    rowi = lax.broadcasted_iota(jnp.int32, (LANES, LANES), 0)
    coli = lax.broadcasted_iota(jnp.int32, (LANES, LANES), 1)
    s_new = s_ref[...] * cd + _dot_tn((kf * tab[1]).astype(BF16), v)
    s_ref[...] = jnp.where((rowi < 64) == (coli < 64), s_new, 0.0)
    return o


def _ret_kernel(dl_ref, qf_ref, kf_ref, vf_ref, qb_ref, kb_ref, vb_ref, of_ref, ob_ref,
                s_scr, tab_scr, cd_scr):
    nb = qf_ref.shape[0]

    @pl.when(pl.program_id(0) == 0)
    def _():
        s_scr[...] = jnp.zeros(s_scr.shape, F32)
        for d in range(2):
            for p in range(2):
                tabs = _ret_tables(dl_ref[d, p], backward=(d == 1))
                for t in range(4):
                    tab_scr[d, p, t] = tabs[t]
                cd_scr[d, p] = tabs[4]

    for d, (q_ref, k_ref, v_ref, o_ref) in enumerate(((qf_ref, kf_ref, vf_ref, of_ref),
                                                      (qb_ref, kb_ref, vb_ref, ob_ref))):
        for p in range(2):
            tab = [tab_scr[d, p, t] for t in range(4)]
            cd = cd_scr[d, p]
            ln = slice(p * LANES, (p + 1) * LANES)
            for b in range(nb):
                o_ref[b, :, ln] = _ret_chunk(q_ref[b, :, ln], k_ref[b, :, ln], v_ref[b, :, ln],
                                             tab, cd, s_scr.at[d, b, p])


def _retention(dl_lane, q, k, v):
    B, S, _ = q.shape
    L = RET_CHUNK
    nc = S // L
    ncc = CTX_LEN // L
    fwd = pl.BlockSpec((B, L, 256), lambda s: (0, s, 0))
    bwd = pl.BlockSpec((B, L, 256), lambda s: (0, jnp.where(s < ncc, ncc - 1 - s, nc - 1 + ncc - s), 0))
    return pl.pallas_call(
        _ret_kernel,
        grid=(nc,),
        in_specs=[pl.BlockSpec(dl_lane.shape, lambda s: (0, 0, 0, 0)), fwd, fwd, fwd, bwd, bwd, bwd],
        out_specs=[fwd, bwd],
        out_shape=[jax.ShapeDtypeStruct((B, S, 256), F32)] * 2,
        scratch_shapes=[pltpu.VMEM((2, B, 2, LANES, LANES), F32),
                        pltpu.VMEM((2, 2, 4, L, LANES), F32), pltpu.VMEM((2, 2, 1, LANES), F32)],
        compiler_params=_cparams(("arbitrary",)),
        name="retention",
    )(dl_lane, q, k, v, q, k, v)


def _gated_group_norm(o, g, gn):
    lo = _lane(o.shape) < 64

    def halves(t):
        a = jnp.sum(jnp.where(lo, t, 0.0), axis=-1, keepdims=True) * (1.0 / 64)
        b = jnp.sum(jnp.where(lo, 0.0, t), axis=-1, keepdims=True) * (1.0 / 64)
        return jnp.where(lo, a, b)

    d = o - halves(o)
    var = halves(d * d)
    return d * lax.rsqrt(var + NORM_EPS) * gn * (g * jax.nn.sigmoid(g))


def _outproj_kernel(x_ref, ya_ref, yb_ref, of_ref, ob_ref, rg_ref, gn_ref, ym_ref, w_ref, g1_ref,
                    sh_ref, sc_ref, ng_ref, wr_ref, br_ref, xo_ref, h_ref, rt_ref):
    mix = (_dot(ya_ref[0], w_ref[0:256]) + _dot(yb_ref[0], w_ref[256:512])
           + _dot(ym_ref[0], w_ref[768:1024]))
    for j in range(2):
        ln = slice(j * LANES, (j + 1) * LANES)
        yr = _gated_group_norm(of_ref[0, :, ln] + ob_ref[0, :, ln], rg_ref[0, :, ln].astype(F32),
                               gn_ref[:, ln])
        mix = mix + _dot(yr.astype(BF16), w_ref[512 + j * LANES:512 + (j + 1) * LANES])
    x = x_ref[0] + g1_ref[...] * mix
    xo_ref[0] = x
    h = _rms(x, ng_ref[...]) * (1.0 + sc_ref[...]) + sh_ref[...]
    h_ref[0] = h
    h_hi = h.astype(BF16)
    h_lo = (h - h_hi.astype(F32)).astype(BF16)
    logits = (_dot(h_hi, wr_ref[0]) + _dot(h_lo, wr_ref[0]) + _dot(h_hi, wr_ref[1])
              + br_ref[...])
    lane = _lane(logits.shape).astype(F32)

    def first_argmax(vals, mask):
        mx = jnp.max(jnp.where(mask, vals, NEG_INF), axis=-1, keepdims=True)
        idx = jnp.min(jnp.where(mask & (vals == mx), lane, float(LANES)), axis=-1, keepdims=True)
        return mx, idx

    gmask = lane < MOE_GROUPS
    g_max, g_idx = first_argmax(logits, gmask)
    g_w = 1.0 / jnp.sum(jnp.exp(jnp.where(gmask, logits - g_max, NEG_INF)), axis=-1, keepdims=True)
    e_lo = MOE_GROUPS + g_idx * MOE_PER_GROUP
    emask = (lane >= e_lo) & (lane < e_lo + MOE_PER_GROUP)
    v1, i1 = first_argmax(logits, emask)
    v2, i2 = first_argmax(logits, emask & (lane != i1))
    t = jnp.exp(v2 - v1)
    w1 = g_w / (1.0 + t)
    w2 = w1 * t
    rt = jnp.where(lane == 0, i1 - MOE_GROUPS,
                   jnp.where(lane == 1, i2 - MOE_GROUPS,
                             jnp.where(lane == 2, w1, jnp.where(lane == 3, w2, 0.0))))
    rt_ref[0] = rt


def _outproj(x, ya, yb, o_f, o_b, rg, gn_g, ym, w_out_p, mod, nbatch, norm_g, w_route, b_route):
    B, S, D = x.shape
    nt = S // TM
    tok = lambda w: pl.BlockSpec((1, TM, w), lambda b, i: (b, i, 0))
    full = lambda a: pl.BlockSpec(a.shape, lambda b, i: (0,) * a.ndim)
    return pl.pallas_call(
        _outproj_kernel,
        grid=(B, nt),
        in_specs=[tok(D), tok(256), tok(256), tok(256), tok(256), tok(256), full(gn_g), tok(256),
                  full(w_out_p),
                  _mod_spec(2, nbatch), _mod_spec(3, nbatch), _mod_spec(4, nbatch), full(norm_g),
                  full(w_route), full(b_route)],
        out_specs=[tok(D), tok(D), tok(LANES)],
        out_shape=[jax.ShapeDtypeStruct((B, S, D), F32), jax.ShapeDtypeStruct((B, S, D), F32),
                   jax.ShapeDtypeStruct((B, S, LANES), F32)],
        compiler_params=_cparams(("parallel", "parallel")),
        name="outproj_router",
    )(x, ya, yb, o_f, o_b, rg, gn_g, ym, w_out_p, mod, mod, mod, norm_g, w_route, b_route)


def _rank_kernel(rt_ref, rank_ref, cnt_ref, carry):
    @pl.when(pl.program_id(0) == 0)
    def _():
        carry[...] = jnp.zeros(carry.shape, F32)

    rt = rt_ref[...]
    lane = _lane(rt.shape)
    lanef = lane.astype(F32)
    oh0 = jnp.where(lanef == rt[:, 0:1], 1.0, 0.0)
    oh1 = jnp.where(lanef == rt[:, 1:2], 1.0, 0.0)
    cnt = oh0 + oh1
    r_io = lax.broadcasted_iota(jnp.int32, (TM, TM), 0)
    c_io = lax.broadcasted_iota(jnp.int32, (TM, TM), 1)
    tri = jnp.where(r_io > c_io, 1.0, 0.0).astype(BF16)
    before = _dot(tri, cnt.astype(BF16)) + carry[...]
    r0 = jnp.sum(oh0 * before, axis=-1, keepdims=True)
    r1 = jnp.sum(oh1 * before, axis=-1, keepdims=True)
    rank_ref[...] = jnp.where(lane == 0, r0, jnp.where(lane == 1, r1, 0.0))
    carry[...] = carry[...] + jnp.sum(cnt, axis=0, keepdims=True)
    cnt_ref[...] = carry[...]


def _ranks(route):
    n = route.shape[0]
    return pl.pallas_call(
        _rank_kernel,
        grid=(n // TM,),
        in_specs=[pl.BlockSpec((TM, LANES), lambda i: (i, 0))],
        out_specs=[pl.BlockSpec((TM, LANES), lambda i: (i, 0)),
                   pl.BlockSpec((1, LANES), lambda i: (0, 0))],
        out_shape=[jax.ShapeDtypeStruct((n, LANES), F32), jax.ShapeDtypeStruct((1, LANES), F32)],
        scratch_shapes=[pltpu.VMEM((1, LANES), F32)],
        compiler_params=_cparams(("arbitrary",)),
        name="moe_rank",
    )(route)


def _dispatch_kernel(dest_ref, h_ref, xb_in_ref, xb_ref, sem):
    del xb_in_ref
    base = pl.program_id(0) * TM

    def copy(r, k):
        d = dest_ref[(base + r) * 2 + k]
        return pltpu.make_async_copy(h_ref.at[pl.ds(r, 1)], xb_ref.at[pl.ds(d, 1)], sem)

    def issue(r, c):
        copy(r, 0).start()
        copy(r, 1).start()
        return c

    def drain(r, c):
        copy(r, 0).wait()
        copy(r, 1).wait()
        return c

    lax.fori_loop(0, TM, issue, 0, unroll=DMA_UNROLL)
    lax.fori_loop(0, TM, drain, 0, unroll=DMA_UNROLL)


def _dispatch(dest, h, cap):
    n, d = h.shape
    return pl.pallas_call(
        _dispatch_kernel,
        grid_spec=pltpu.PrefetchScalarGridSpec(
            num_scalar_prefetch=1,
            grid=(n // TM,),
            in_specs=[pl.BlockSpec((TM, d), lambda i, dest: (i, 0)),
                      pl.BlockSpec(memory_space=pl.ANY)],
            out_specs=pl.BlockSpec(memory_space=pl.ANY),
            scratch_shapes=[pltpu.SemaphoreType.DMA(())]),
        out_shape=jax.ShapeDtypeStruct((cap, d), h.dtype),
        input_output_aliases={2: 0},
        compiler_params=_cparams(("arbitrary",)),
        name="moe_dispatch",
    )(dest, h, jnp.zeros((cap, d), h.dtype))


def _expert_kernel(be_ref, nu_ref, x_ref, wg_ref, wu_ref, wd_ref, y_ref, wg_s, wu_s, wd_s):
    i = pl.program_id(0)

    @pl.when((i == 0) | (be_ref[i] != be_ref[jnp.maximum(i - 1, 0)]))
    def _():
        wg_s[...] = wg_ref[0].astype(BF16)
        wu_s[...] = wu_ref[0].astype(BF16)
        wd_s[...] = wd_ref[0].astype(BF16)

    @pl.when(i < nu_ref[0])
    def _():
        xb = x_ref[...].astype(BF16)
        gate = _dot(xb, wg_s[...])
        up = _dot(xb, wu_s[...])
        hid = (gate * jax.nn.sigmoid(gate) * up).astype(BF16)
        y_ref[...] = _dot(hid, wd_s[...])

    @pl.when(i >= nu_ref[0])
    def _():
        y_ref[...] = jnp.zeros(y_ref.shape, y_ref.dtype)


def _experts(block_e, n_used, xb, w_gate, w_up, w_down, layer):
    cap, d = xb.shape
    hdim = w_gate.shape[-1]
    return pl.pallas_call(
        _expert_kernel,
        grid_spec=pltpu.PrefetchScalarGridSpec(
            num_scalar_prefetch=2,
            grid=(cap // MOE_ROWS,),
            in_specs=[pl.BlockSpec((MOE_ROWS, d), lambda i, be, nu: (i, 0)),
                      pl.BlockSpec((None, 1, d, hdim), lambda i, be, nu: (layer, be[i], 0, 0)),
                      pl.BlockSpec((None, 1, d, hdim), lambda i, be, nu: (layer, be[i], 0, 0)),
                      pl.BlockSpec((None, 1, hdim, d), lambda i, be, nu: (layer, be[i], 0, 0))],
            out_specs=pl.BlockSpec((MOE_ROWS, d), lambda i, be, nu: (i, 0)),
            scratch_shapes=[pltpu.VMEM((d, hdim), BF16), pltpu.VMEM((d, hdim), BF16),
                            pltpu.VMEM((hdim, d), BF16)]),
        out_shape=jax.ShapeDtypeStruct((cap, d), F32),
        compiler_params=_cparams(("arbitrary",)),
        name="moe_experts",
    )(block_e, n_used, xb, w_gate, w_up, w_down)


def _combine_kernel(dest_ref, x_ref, rt_ref, g2_ref, fg_ref, yb_ref, o_ref, gath, sem, *,
                    tile_off, tiles_per_batch, final):
    b, i = pl.program_id(0), pl.program_id(1)
    base = (b * tiles_per_batch + i + tile_off) * TM

    def copy(r, k):
        d = dest_ref[(base + r) * 2 + k]
        return pltpu.make_async_copy(yb_ref.at[pl.ds(d, 1)], gath.at[k, pl.ds(r, 1)], sem)

    def issue(r, c):
        copy(r, 0).start()
        copy(r, 1).start()
        return c

    def drain(r, c):
        copy(r, 0).wait()
        copy(r, 1).wait()
        return c

    lax.fori_loop(0, TM, issue, 0, unroll=DMA_UNROLL)
    lax.fori_loop(0, TM, drain, 0, unroll=DMA_UNROLL)
    rt = rt_ref[0]
    m = rt[:, 2:3] * gath[0] + rt[:, 3:4] * gath[1]
    x = x_ref[0] + g2_ref[...] * m
    if final:
        x = _rms(x, fg_ref[...])
    o_ref[0] = x


def _combine(dest, x, route, mod, nbatch, final_g, yb, final):
    B, S, D = x.shape
    nt = S // TM
    off = 1 if final else 0
    g2_spec = pl.BlockSpec((None, 1, D_MODEL),
                           lambda b, i, dest: (jnp.where(i + off == 0, nbatch, b), 0, 5))
    return pl.pallas_call(
        functools.partial(_combine_kernel, tile_off=off, tiles_per_batch=nt, final=final),
        grid_spec=pltpu.PrefetchScalarGridSpec(
            num_scalar_prefetch=1,
            grid=(B, nt - off),
            in_specs=[pl.BlockSpec((1, TM, D), lambda b, i, dest: (b, i + off, 0)),
                      pl.BlockSpec((1, TM, LANES), lambda b, i, dest: (b, i + off, 0)),
                      g2_spec,
                      pl.BlockSpec((1, D), lambda b, i, dest: (0, 0)),
                      pl.BlockSpec(memory_space=pl.ANY)],
            out_specs=pl.BlockSpec((1, TM, D), lambda b, i, dest: (b, i, 0)),
            scratch_shapes=[pltpu.VMEM((2, TM, D), F32), pltpu.SemaphoreType.DMA(())]),
        out_shape=jax.ShapeDtypeStruct((B, S - off * TM, D), F32),
        compiler_params=_cparams(("arbitrary", "arbitrary")),
        name="moe_combine",
    )(dest, x, route, mod, final_g, yb)


def _rope_tables(seq):
    n_rows = seq // GRID_W
    pos = jnp.arange(max(n_rows, GRID_W), dtype=F32)[:, None]
    lane = jnp.arange(LANES, dtype=jnp.int32)
    tabs = []
    for r in (64, 32):
        nf = r // 4
        inv = 1.0 / (ROPE_BASE ** (jnp.arange(nf, dtype=F32) / nf))
        ang = pos * inv[lane % nf][None, :]
        first = ((lane % (2 * nf)) < nf)[None, :]
        by_row = ((lane % r) < 2 * nf)[None, None, :]
        for small in (jnp.cos(ang), jnp.where(first, -jnp.sin(ang), jnp.sin(ang))):
            full = jnp.where(by_row, small[:n_rows, None, :], small[None, :GRID_W, :])
            tabs.append(full.reshape(seq, LANES))
    ident = (jnp.ones((CTX_LEN, LANES), F32), jnp.zeros((CTX_LEN, LANES), F32))
    return [jnp.concatenate([ident[j % 2], tab], axis=0) for j, tab in enumerate(tabs)]


def _layer_weights(w_in, w_out, w_uq, w_ukv, w_group, b_group, w_expert, b_expert):
    perm = jnp.array([0, 2, 1, 3])
    wq = w_in[:, :256].reshape(D_MODEL, 4, 64)[:, perm].reshape(D_MODEL, 256)
    w_in_p = jnp.concatenate([wq, w_in[:, 256:], jnp.zeros((D_MODEL, D_IN_PAD - w_in.shape[1]), F32)],
                             axis=1).astype(BF16)
    wo = jnp.concatenate([w_out[:256].reshape(4, 64, D_MODEL)[perm].reshape(256, D_MODEL), w_out[256:]],
                         axis=0).astype(BF16)
    uq = w_uq.reshape(256, 4, 96)
    blocks = []
    for h in range(4):
        blk = jnp.zeros((256, 256), F32)
        blk = blk.at[:, 64 * (h % 2):64 * (h % 2) + 64].set(uq[:, h, :64])
        blk = blk.at[:, LANES:LANES + 32].set(uq[:, h, 64:])
        blocks.append(blk)
    wuq = jnp.concatenate(blocks, axis=1).astype(BF16)
    ukv = w_ukv.reshape(128, 4, 128)
    wukk = ukv[:, :, :64].reshape(128, 256).astype(BF16)
    wukv = ukv[:, :, 64:].reshape(128, 256).astype(BF16)
    pad = LANES - MOE_GROUPS - MOE_EXPERTS
    w_route = jnp.concatenate([w_group, w_expert, jnp.zeros((D_MODEL, pad), F32)], axis=1)
    w_hi = w_route.astype(BF16)
    w_route = jnp.stack([w_hi, (w_route - w_hi.astype(F32)).astype(BF16)])
    b_route = jnp.concatenate([b_group, b_expert, jnp.zeros((pad,), F32)]).reshape(1, LANES)
    return w_in_p, wo, wuq, wukk, wukv, w_route, b_route


def _moe_plan(route, ranks, counts, cap):
    e = route[:, :2].astype(jnp.int32)
    rank = ranks[:, :2].astype(jnp.int32)
    cnt = counts[0, :MOE_EXPERTS].astype(jnp.int32)
    padded = (cnt + MOE_ROWS - 1) // MOE_ROWS * MOE_ROWS
    pend = jnp.cumsum(padded)
    pstart = pend - padded
    dest = (pstart[e] + rank).reshape(-1)
    nblk = cap // MOE_ROWS
    blk_start = jnp.arange(nblk, dtype=jnp.int32) * MOE_ROWS
    block_e = jnp.minimum(jnp.sum((pend[None, :] <= blk_start[:, None]).astype(jnp.int32), axis=1),
                          MOE_EXPERTS - 1)
    n_used = (pend[-1:] // MOE_ROWS).astype(jnp.int32)
    return dest, block_e, n_used


def kernel(x, c, ctx, c_ctx, w_mod, b_mod, norm1_g, norm2_g, w_in, w_out, swa_sink, diff_lambda, diff_subln_g, ret_decay_logit, ret_gn_g, mla_q_norm_g, mla_w_uq, mla_kv_norm_g, mla_w_ukv, moe_w_group, moe_b_group, moe_w_expert, moe_b_expert, moe_w_gate, moe_w_up, moe_w_down, final_norm_g):
    B, T, D = x.shape
    S = CTX_LEN + T
    n_tok = B * S
    diff_tk, mla_tk = min(DIFF_TK, T), min(MLA_TK, T)
    cap = -(-(2 * n_tok + MOE_EXPERTS * (MOE_ROWS - 1)) // MOE_ROWS) * MOE_ROWS
    xs = jnp.concatenate([ctx, x], axis=1)
    cvec = jnp.zeros((8, D), F32).at[:B].set(c).at[B].set(c_ctx)
    tabs = _rope_tables(T)
    row = lambda v: v.reshape(1, -1)

    for l in range(DEPTH):
        lambda_init = 0.8 - 0.6 * math.exp(-0.3 * l)
        w_in_p, wo, wuq, wukk, wukv, w_route, b_route = _layer_weights(
            w_in[l], w_out[l], mla_w_uq[l], mla_w_ukv[l], moe_w_group[l], moe_b_group[l],
            moe_w_expert[l], moe_b_expert[l])
        mod = _modulation(cvec, w_mod[l], b_mod[l]).reshape(8, 1, 6 * D)
        (swaq, swak, swav, dq, dk, dv, rq, rk, rv, rg, mq, mk, mv) = _inproj(
            xs, mod, B, row(norm1_g[l]), w_in_p, tabs, row(mla_q_norm_g[l]), wuq,
            row(mla_kv_norm_g[l]), wukk, wukv)
        sink_p = swa_sink[l]
        ya = _swa(sink_p, swaq, swak, swav)
        yb = _diff(diff_lambda[l], diff_subln_g[l].reshape(64, 1), dq, dk, dv, lambda_init,
                   diff_tk)
        dl_lane = jnp.repeat(ret_decay_logit[l], 64, axis=-1).reshape(2, 2, 1, LANES)
        o_f, o_b = _retention(dl_lane, rq, rk, rv)
        ym = _mla(mq, mk, mv, mla_tk)
        xs, h2, route = _outproj(xs, ya, yb, o_f, o_b, rg, row(ret_gn_g[l]), ym, wo, mod, B,
                                 row(norm2_g[l]), w_route, b_route)
        route2 = route.reshape(n_tok, LANES)
        ranks, counts = _ranks(route2)
        dest, block_e, n_used = _moe_plan(route2, ranks, counts, cap)
        xb = _dispatch(dest, h2.reshape(n_tok, D), cap)
        yb_e = _experts(block_e, n_used, xb, moe_w_gate, moe_w_up, moe_w_down, l)
        final = l == DEPTH - 1
        xs = _combine(dest, xs, route, mod, B, row(final_norm_g), yb_e, final)
    return xs
```

```python
import functools
import math

import jax
import jax.numpy as jnp
from jax import lax
from jax.experimental import pallas as pl
from jax.experimental.pallas import tpu as pltpu

F32 = jnp.float32
BF16 = jnp.bfloat16

D_MODEL = 1024
CTX_LEN = 256
GRID_W = 64
ROPE_BASE = 10000.0
NORM_EPS = 1e-6
NEG_INF = -1e30
DEPTH = 2

SWA_WINDOW = 128
RET_CHUNK = 128
MLA_SCALE = (64 + 32) ** -0.5
LOG2E = math.log2(math.e)
MOE_GROUPS = 4
MOE_PER_GROUP = 8
MOE_EXPERTS = 32
MOE_HIDDEN = 512

LANES = 128
TM = 256
DIFF_TQ = 2048
MLA_TQ = 2048
DIFF_TK = 1024
MLA_TK = 2048
VT_ROWS = 80
GAP_LIMIT = 64.0
MOE_ROWS = 256
DMA_UNROLL = 8
VMEM_LIMIT = 56 * 1024 * 1024

C_SWA_Q, C_SWA_K, C_SWA_V = 0, 256, 384
C_DIF_Q, C_DIF_K, C_DIF_V = 512, 768, 1024
C_RET_Q, C_RET_K, C_RET_V, C_RET_G = 1280, 1536, 1792, 2048
C_MLA_Q, C_MLA_KV, C_MLA_KR = 2304, 2560, 2688
D_IN_PAD = 2816


def _cparams(sem):
    return pltpu.CompilerParams(dimension_semantics=sem, vmem_limit_bytes=VMEM_LIMIT)


def _dot(a, b):
    return jnp.dot(a, b, preferred_element_type=F32)


def _dot_nt(a, b):
    return lax.dot_general(a, b, (((1,), (1,)), ((), ())), preferred_element_type=F32)


def _dot_tn(a, b):
    return lax.dot_general(a, b, (((0,), (0,)), ((), ())), preferred_element_type=F32)


def _rms(x, g):
    ms = jnp.mean(x * x, axis=-1, keepdims=True)
    return x * lax.rsqrt(ms + NORM_EPS) * g


def _lane(shape):
    return lax.broadcasted_iota(jnp.int32, shape, len(shape) - 1)


def _rope(z, cos, sin, nf):
    first = (_lane(z.shape) % (2 * nf)) < nf
    partner = jnp.where(first, pltpu.roll(z, LANES - nf, 1), pltpu.roll(z, nf, 1))
    return z * cos + partner * sin


def _mod_kernel(a_ref, w_ref, b_ref, o_ref):
    a = a_ref[...]
    act = a * jax.nn.sigmoid(a)
    o_ref[...] = jnp.dot(act, w_ref[...], preferred_element_type=F32,
                         precision=lax.Precision.HIGHEST) + b_ref[...]


def _modulation(cvec, w_mod, b_mod):
    n = w_mod.shape[1]
    bn = 512
    return pl.pallas_call(
        _mod_kernel,
        grid=(n // bn,),
        in_specs=[pl.BlockSpec((8, D_MODEL), lambda j: (0, 0)),
                  pl.BlockSpec((D_MODEL, bn), lambda j: (0, j)),
                  pl.BlockSpec((1, bn), lambda j: (0, j))],
        out_specs=pl.BlockSpec((8, bn), lambda j: (0, j)),
        out_shape=jax.ShapeDtypeStruct((8, n), F32),
        compiler_params=_cparams(("parallel",)),
        name="modulation",
    )(cvec, w_mod, b_mod.reshape(1, n))


def _mod_spec(chunk, nbatch):
    return pl.BlockSpec((None, 1, D_MODEL),
                        lambda b, i: (jnp.where(i == 0, nbatch, b), 0, chunk))


def _inproj_kernel(x_ref, g_ref, sh_ref, sc_ref, w_ref, c64_ref, s64_ref, c32_ref, s32_ref,
                   qng_ref, wuq_ref, kvng_ref, wukk_ref, wukv_ref,
                   swaq, swak, swav, dq, dk, dv, rq, rk, rv, rg, mq, mk, mv):
    x = x_ref[0]
    h = _rms(x, g_ref[...]) * (1.0 + sc_ref[...]) + sh_ref[...]
    hb = h.astype(BF16)
    c64, s64, c32, s32 = c64_ref[...], s64_ref[...], c32_ref[...], s32_ref[...]

    def proj(c0, width):
        return _dot(hb, w_ref[:, c0:c0 + width])

    def rope_blocks(z, out, nf, scale):
        cos, sin = (c64, s64) if nf == 16 else (c32, s32)
        for j in range(z.shape[1] // LANES):
            zz = _rope(z[:, j * LANES:(j + 1) * LANES], cos, sin, nf)
            if scale != 1.0:
                zz = zz * scale
            out[0, :, j * LANES:(j + 1) * LANES] = zz.astype(out.dtype)

    rope_blocks(proj(C_SWA_Q, 256), swaq, 16, 0.125)
    rope_blocks(proj(C_SWA_K, 128), swak, 16, 1.0)
    swav[0] = proj(C_SWA_V, 128).astype(BF16)
    def values_transposed(z, out):
        zt = z.T.astype(BF16)
        for hh in range(4):
            out[0, hh * VT_ROWS:hh * VT_ROWS + 64, :] = zt[hh * 64:(hh + 1) * 64]
            out[0, hh * VT_ROWS + 64:(hh + 1) * VT_ROWS, :] = jnp.ones((VT_ROWS - 64, zt.shape[1]), BF16)

    rope_blocks(proj(C_DIF_Q, 256), dq, 8, 32 ** -0.5 * LOG2E)
    rope_blocks(proj(C_DIF_K, 256), dk, 8, 1.0)
    values_transposed(proj(C_DIF_V, 256), dv)
    rope_blocks(proj(C_RET_Q, 256), rq, 16, 0.125)
    rope_blocks(proj(C_RET_K, 256), rk, 16, 1.0)
    rv[0] = proj(C_RET_V, 256).astype(BF16)
    rg[0] = proj(C_RET_G, 256).astype(BF16)

    ql = _rms(proj(C_MLA_Q, 256), qng_ref[...]).astype(BF16)
    qa = _dot(ql, wuq_ref[...]) * (MLA_SCALE * LOG2E)
    for hh in range(4):
        mq[0, :, hh * 256:hh * 256 + LANES] = qa[:, hh * 256:hh * 256 + LANES].astype(BF16)
        zr = _rope(qa[:, hh * 256 + LANES:(hh + 1) * 256], c32, s32, 8)
        mq[0, :, hh * 256 + LANES:(hh + 1) * 256] = zr.astype(BF16)
    kvl = _rms(proj(C_MLA_KV, 128), kvng_ref[...]).astype(BF16)
    kn = _dot(kvl, wukk_ref[...])
    kr = _rope(proj(C_MLA_KR, 128), c32, s32, 8).astype(BF16)
    for p in range(2):
        mk[0, :, p * 256:p * 256 + LANES] = kn[:, p * LANES:(p + 1) * LANES].astype(BF16)
        mk[0, :, p * 256 + LANES:(p + 1) * 256] = kr
    values_transposed(_dot(kvl, wukv_ref[...]), mv)


def _inproj(x, mod, nbatch, norm_g, w_in_p, tabs, qng, wuq, kvng, wukk, wukv):
    B, S, D = x.shape
    nt = S // TM
    tok = lambda w: pl.BlockSpec((1, TM, w), lambda b, i: (b, i, 0))
    full = lambda a: pl.BlockSpec(a.shape, lambda b, i: (0,) * a.ndim)
    tab = pl.BlockSpec((TM, LANES), lambda b, i: (i, 0))
    widths = [256, 128, 128, 256, 256, None, 256, 256, 256, 256, 1024, 512, None]
    vt_spec = pl.BlockSpec((1, 4 * VT_ROWS, TM), lambda b, i: (b, 0, i))
    vt_shape = jax.ShapeDtypeStruct((B, 4 * VT_ROWS, S), BF16)
    return pl.pallas_call(
        _inproj_kernel,
        grid=(B, nt),
        in_specs=[tok(D), full(norm_g), _mod_spec(0, nbatch), _mod_spec(1, nbatch), full(w_in_p),
                  tab, tab, tab, tab, full(qng), full(wuq), full(kvng), full(wukk), full(wukv)],
        out_specs=[vt_spec if w is None else tok(w) for w in widths],
        out_shape=[vt_shape if w is None else jax.ShapeDtypeStruct((B, S, w), BF16) for w in widths],
        compiler_params=_cparams(("parallel", "parallel")),
        name="inproj",
    )(x, norm_g, mod, mod, w_in_p, *tabs, qng, wuq, kvng, wukk, wukv)


def _swa_kernel(sink_ref, q_ref, kp_ref, ko_ref, kn_ref, vp_ref, vo_ref, vn_ref, kc_ref, vc_ref,
                o_ref, *, n_ctx_tiles, seq):
    i = pl.program_id(1)
    blk = RET_CHUNK
    n = i - n_ctx_tiles
    k_all = jnp.concatenate([kp_ref[0], ko_ref[0], kn_ref[0], kc_ref[0]], axis=0)
    v_all = jnp.concatenate([vp_ref[0], vo_ref[0], vn_ref[0], vc_ref[0]], axis=0)
    nk = 3 * blk + CTX_LEN
    r_io = lax.broadcasted_iota(jnp.int32, (blk, nk), 0)
    c_io = lax.broadcasted_iota(jnp.int32, (blk, nk), 1)
    kpos = (n - 1) * blk + c_io
    in_band = ((c_io >= r_io) & (c_io <= r_io + 2 * SWA_WINDOW) & (kpos >= 0) & (kpos < seq)
               & (n >= 0))
    valid = in_band | (c_io >= 3 * blk)
    lane = _lane((blk, LANES))
    heads = [(r, g) for r in range(2) for g in range(2)]
    qm = [jnp.where((lane >= 64 * g) & (lane < 64 * (g + 1)), q_ref[0, :, r * LANES:(r + 1) * LANES],
                    jnp.zeros((blk, LANES), BF16)) for r, g in heads]
    s = [jnp.where(valid, _dot_nt(q, k_all), NEG_INF) for q in qm]
    sink = [sink_ref[2 * g + r] for r, g in heads]
    m = [jnp.maximum(jnp.max(s[h], axis=-1, keepdims=True), sink[h]) for h in range(4)]
    p = [jnp.exp(s[h] - m[h]) for h in range(4)]
    l = [jnp.sum(p[h], axis=-1, keepdims=True) + jnp.exp(sink[h] - m[h]) for h in range(4)]
    o = [_dot(p[h].astype(BF16), v_all) / l[h] for h in range(4)]
    for r in range(2):
        o_ref[0, :, r * LANES:(r + 1) * LANES] = jnp.where(lane < 64, o[2 * r], o[2 * r + 1]).astype(BF16)


def _swa(sink, q, k, v):
    B, S, _ = q.shape
    blk = RET_CHUNK
    nt = S // blk
    nct = CTX_LEN // blk
    kvs = lambda f: pl.BlockSpec((1, blk, LANES), f)
    prev = lambda b, i: (b, jnp.maximum(i - 1, 0), 0)
    own = lambda b, i: (b, i, 0)
    nxt = lambda b, i: (b, jnp.minimum(i + 1, nt - 1), 0)
    ctx = pl.BlockSpec((1, CTX_LEN, LANES), lambda b, i: (b, 0, 0))
    return pl.pallas_call(
        functools.partial(_swa_kernel, n_ctx_tiles=nct, seq=S - CTX_LEN),
        grid=(B, nt),
        in_specs=[pl.BlockSpec(memory_space=pltpu.SMEM),
                  pl.BlockSpec((1, blk, 256), own),
                  kvs(prev), kvs(own), kvs(nxt), kvs(prev), kvs(own), kvs(nxt), ctx, ctx],
        out_specs=pl.BlockSpec((1, blk, 256), own),
        out_shape=jax.ShapeDtypeStruct((B, S, 256), BF16),
        compiler_params=_cparams(("parallel", "parallel")),
        name="swa",
    )(sink, q, k, k, k, v, v, v, k, v)


def _flash(q_maps, v_blk, k_ref, v_ref, m_scr, acc_scr, gap_scr, *, n_lat, tk):
    nm = len(q_maps)
    qt_maps = [q.astype(F32).T.astype(BF16) for q in q_maps]

    def chunk(start, size, mode):
        kc = k_ref[0, pl.ds(start, size), :]
        for a in range(nm):
            vt = v_ref[0, v_blk[a] * VT_ROWS:(v_blk[a] + 1) * VT_ROWS, pl.ds(start, size)]
            st = _dot(kc, qt_maps[a])
            m_prev = None if mode == "first" else m_scr[a]
            if mode == "fast":
                pf = jnp.exp2(st - m_prev)
                rise = jnp.maximum(jnp.max(pf, axis=0, keepdims=True), 1.0)
                acc_scr[a] = (acc_scr[a] + _dot(vt, pf.astype(BF16))) * (1.0 / rise)
                m_scr[a] = m_prev + jnp.log2(rise)
                gap_scr[a] = jnp.maximum(gap_scr[a], rise)
                continue
            cm = jnp.max(st, axis=0, keepdims=True)
            if mode == "first":
                acc_scr[a] = _dot(vt, jnp.exp2(st - cm).astype(BF16))
                m_scr[a] = cm
                gap_scr[a] = jnp.ones(cm.shape, F32)
            else:
                m_new = jnp.maximum(m_prev, cm)
                p = jnp.exp2(st - m_new).astype(BF16)
                acc_scr[a] = acc_scr[a] * jnp.exp2(m_prev - m_new) + _dot(vt, p)
                m_scr[a] = m_new

    def sweep(mode):
        chunk(0, CTX_LEN, "first")
        if n_lat:

            def body(c, carry):
                chunk(pl.multiple_of(CTX_LEN + c * tk, LANES), tk, mode)
                return carry

            lax.fori_loop(0, n_lat, body, 0)

    sweep("fast")
    if n_lat:
        worst = jnp.max(jnp.concatenate([gap_scr[a] for a in range(nm)], axis=0))

        @pl.when(worst > 2.0 ** GAP_LIMIT)
        def _():
            sweep("exact")

    return [acc_scr[a, 0:64, :] / acc_scr[a, 64:65, :] for a in range(nm)]


def _attention_calls(kern, name, nm, q, k, vt, qw, kw, tk, tq, extra_in=(), extra_specs=()):
    B, S, _ = q.shape
    T = S - CTX_LEN

    def call(tq, n_tiles, n_lat, q_spec, kv_rows, call_name):
        return pl.pallas_call(
            functools.partial(kern, n_lat=n_lat, tk=tk),
            grid=(B, 2, n_tiles),
            in_specs=list(extra_specs) + [
                q_spec,
                pl.BlockSpec((1, kv_rows, kw), lambda b, p, i: (b, 0, p)),
                pl.BlockSpec((1, 2 * VT_ROWS, kv_rows), lambda b, p, i: (b, p, 0))],
            out_specs=pl.BlockSpec((1, tq, LANES), lambda b, p, i: (b, i, p)),
            out_shape=jax.ShapeDtypeStruct((B, tq * n_tiles, 256), BF16),
            scratch_shapes=[pltpu.VMEM((nm, 1, tq), F32), pltpu.VMEM((nm, VT_ROWS, tq), F32),
                            pltpu.VMEM((nm, 1, tq), F32)],
            compiler_params=_cparams(("parallel", "parallel", "parallel")),
            name=call_name,
        )(*extra_in, q, k, vt)

    y_ctx = call(CTX_LEN, 1, 0, pl.BlockSpec((1, CTX_LEN, qw), lambda b, p, i: (b, 0, p)),
                 CTX_LEN, name + "_ctx")
    tq = min(tq, T)
    y_lat = call(tq, T // tq, T // tk,
                 pl.BlockSpec((pl.Element(1), pl.Element(tq), pl.Element(qw)),
                              lambda b, p, i: (b, (i * (tq // CTX_LEN) + 1) * CTX_LEN, p * qw)),
                 S, name)
    return jnp.concatenate([y_ctx, y_lat], axis=1)


def _diff_attn(lam_ref, g_ref, q_ref, k_ref, v_ref, o_ref, m_scr, acc_scr, gap_scr, *,
               n_lat, tk, lambda_init):
    qb = q_ref[0]
    lane = _lane(qb.shape)
    q_maps = []
    for a in range(4):
        msk = (lane >= 32 * a) & (lane < 32 * (a + 1))
        q_maps.append(jnp.where(msk, qb, jnp.zeros_like(qb)))
    n = _flash(q_maps, (0, 0, 1, 1), k_ref, v_ref, m_scr, acc_scr, gap_scr, n_lat=n_lat, tk=tk)
    lp = lam_ref[...]
    lam = (jnp.exp(jnp.sum(lp[0:1] * lp[1:2], axis=-1, keepdims=True))
           - jnp.exp(jnp.sum(lp[2:3] * lp[3:4], axis=-1, keepdims=True)) + lambda_init)
    ys = []
    for hh in range(2):
        o = n[2 * hh] - lam * n[2 * hh + 1]
        ms = jnp.mean(o * o, axis=0, keepdims=True)
        ys.append((o * lax.rsqrt(ms + NORM_EPS) * g_ref[...] * (1.0 - lambda_init)).T)
    o_ref[0] = jnp.concatenate(ys, axis=1).astype(BF16)


def _diff(lam_p, subln_g2, q, k, v, lambda_init, tk):
    return _attention_calls(
        functools.partial(_diff_attn, lambda_init=lambda_init), "diff_attn", 4, q, k, v, LANES, LANES,
        tk, DIFF_TQ, extra_in=(lam_p, subln_g2),
        extra_specs=(pl.BlockSpec(lam_p.shape, lambda b, p, i: (0, 0)),
                     pl.BlockSpec((64, 1), lambda b, p, i: (0, 0))))


def _mla_attn(q_ref, k_ref, v_ref, o_ref, m_scr, acc_scr, gap_scr, *, n_lat, tk):
    q_maps = [q_ref[0, :, 0:256], q_ref[0, :, 256:512]]
    n = _flash(q_maps, (0, 1), k_ref, v_ref, m_scr, acc_scr, gap_scr, n_lat=n_lat, tk=tk)
    o_ref[0] = jnp.concatenate([n[0].T, n[1].T], axis=1).astype(BF16)


def _mla(q, k, v, tk):
    return _attention_calls(_mla_attn, "mla_attn", 2, q, k, v, 512, 256, tk, MLA_TQ)


def _ret_tables(dl, backward):
    L = RET_CHUNK
    lg = -(jnp.maximum(-dl, 0.0) + jnp.log(1.0 + jnp.exp(-jnp.abs(dl))))
    row = lax.broadcasted_iota(jnp.int32, (L, LANES), 0).astype(F32)
    col = lax.broadcasted_iota(jnp.int32, (L, LANES), 1)
    colf = col.astype(F32)
    if backward:
        qd, kd, rel = jnp.exp((L - row) * lg), jnp.exp(row * lg), colf - row
    else:
        qd, kd, rel = jnp.exp((row + 1.0) * lg), jnp.exp((L - 1.0 - row) * lg), row - colf
    decs = []
    for hh in range(2):
        lg_h = jnp.sum(jnp.where(col[0:1] == 64 * hh, lg, 0.0), axis=-1, keepdims=True)
        decs.append(jnp.where(rel >= 0, jnp.exp(jnp.maximum(rel, 0.0) * lg_h), 0.0))
    return qd, kd, decs[0], decs[1], jnp.exp(float(L) * lg)


def _ret_chunks(chains):
    col = _lane((RET_CHUNK, LANES))
    lo = col < 64
    rowi = lax.broadcasted_iota(jnp.int32, (LANES, LANES), 0)
    coli = lax.broadcasted_iota(jnp.int32, (LANES, LANES), 1)
    diag = (rowi < 64) == (coli < 64)
    inter = [_dot((q.astype(F32) * tab[0]).astype(BF16), s_ref[...].astype(BF16))
             for q, k, v, tab, cd, s_ref, store in chains]
    inc = [_dot_tn((k.astype(F32) * tab[1]).astype(BF16), v) for q, k, v, tab, cd, s_ref, store in chains]
    att = [[_dot_nt(jnp.where(lo == (hh == 0), q, jnp.zeros_like(q)), k) * tab[2 + hh] for hh in range(2)]
           for q, k, v, tab, cd, s_ref, store in chains]
    intra = [[_dot(a.astype(BF16), c[2]) for a in pair] for pair, c in zip(att, chains)]
    for c, o_inter, o_intra, s_inc in zip(chains, inter, intra, inc):
        c[6](o_inter + jnp.where(lo, o_intra[0], o_intra[1]))
        c[5][...] = jnp.where(diag, c[5][...] * c[4] + s_inc, 0.0)


def _ret_kernel(dl_ref, qf_ref, kf_ref, vf_ref, qb_ref, kb_ref, vb_ref, of_ref, ob_ref,
                s_scr, tab_scr, cd_scr):
    nb = qf_ref.shape[0]

    @pl.when(pl.program_id(0) == 0)
    def _():
        s_scr[...] = jnp.zeros(s_scr.shape, F32)
        for d in range(2):
            for p in range(2):
                tabs = _ret_tables(dl_ref[d, p], backward=(d == 1))
                for t in range(4):
                    tab_scr[d, p, t] = tabs[t]
                cd_scr[d, p] = tabs[4]

    chains = []
    for d, (q_ref, k_ref, v_ref, o_ref) in enumerate(((qf_ref, kf_ref, vf_ref, of_ref),
                                                      (qb_ref, kb_ref, vb_ref, ob_ref))):
        for p in range(2):
            tab = [tab_scr[d, p, t] for t in range(4)]
            cd = cd_scr[d, p]
            ln = slice(p * LANES, (p + 1) * LANES)
            for b in range(nb):
                def store(o, o_ref=o_ref, b=b, ln=ln):
                    o_ref[b, :, ln] = o
                chains.append((q_ref[b, :, ln], k_ref[b, :, ln], v_ref[b, :, ln], tab, cd,
                               s_scr.at[d, b, p], store))
    _ret_chunks(chains)


def _retention(dl_lane, q, k, v):
    B, S, _ = q.shape
    L = RET_CHUNK
    nc = S // L
    ncc = CTX_LEN // L
    fwd = pl.BlockSpec((B, L, 256), lambda s: (0, s, 0))
    bwd = pl.BlockSpec((B, L, 256), lambda s: (0, jnp.where(s < ncc, ncc - 1 - s, nc - 1 + ncc - s), 0))
    return pl.pallas_call(
        _ret_kernel,
        grid=(nc,),
        in_specs=[pl.BlockSpec(dl_lane.shape, lambda s: (0, 0, 0, 0)), fwd, fwd, fwd, bwd, bwd, bwd],
        out_specs=[fwd, bwd],
        out_shape=[jax.ShapeDtypeStruct((B, S, 256), F32)] * 2,
        scratch_shapes=[pltpu.VMEM((2, B, 2, LANES, LANES), F32),
                        pltpu.VMEM((2, 2, 4, L, LANES), F32), pltpu.VMEM((2, 2, 1, LANES), F32)],
        compiler_params=_cparams(("arbitrary",)),
        name="retention",
    )(dl_lane, q, k, v, q, k, v)


def _gated_group_norm(o, g, gn):
    lo = _lane(o.shape) < 64

    def halves(t):
        a = jnp.sum(jnp.where(lo, t, 0.0), axis=-1, keepdims=True) * (1.0 / 64)
        b = jnp.sum(jnp.where(lo, 0.0, t), axis=-1, keepdims=True) * (1.0 / 64)
        return jnp.where(lo, a, b)

    d = o - halves(o)
    var = halves(d * d)
    return d * lax.rsqrt(var + NORM_EPS) * gn * (g * jax.nn.sigmoid(g))


def _outproj_kernel(x_ref, ya_ref, yb_ref, of_ref, ob_ref, rg_ref, gn_ref, ym_ref, w_ref, g1_ref,
                    sh_ref, sc_ref, ng_ref, wr_ref, br_ref, xo_ref, h_ref, rt_ref):
    mix = (_dot(ya_ref[0], w_ref[0:256]) + _dot(yb_ref[0], w_ref[256:512])
           + _dot(ym_ref[0], w_ref[768:1024]))
    for j in range(2):
        ln = slice(j * LANES, (j + 1) * LANES)
        yr = _gated_group_norm(of_ref[0, :, ln] + ob_ref[0, :, ln], rg_ref[0, :, ln].astype(F32),
                               gn_ref[:, ln])
        mix = mix + _dot(yr.astype(BF16), w_ref[512 + j * LANES:512 + (j + 1) * LANES])
    x = x_ref[0] + g1_ref[...] * mix
    xo_ref[0] = x
    h = _rms(x, ng_ref[...]) * (1.0 + sc_ref[...]) + sh_ref[...]
    h_ref[0] = h
    h_hi = h.astype(BF16)
    h_lo = (h - h_hi.astype(F32)).astype(BF16)
    logits = (_dot(h_hi, wr_ref[0]) + _dot(h_lo, wr_ref[0]) + _dot(h_hi, wr_ref[1])
              + br_ref[...])
    lane = _lane(logits.shape).astype(F32)

    def first_argmax(vals, mask):
        mx = jnp.max(jnp.where(mask, vals, NEG_INF), axis=-1, keepdims=True)
        idx = jnp.min(jnp.where(mask & (vals == mx), lane, float(LANES)), axis=-1, keepdims=True)
        return mx, idx

    gmask = lane < MOE_GROUPS
    g_max, g_idx = first_argmax(logits, gmask)
    g_w = 1.0 / jnp.sum(jnp.exp(jnp.where(gmask, logits - g_max, NEG_INF)), axis=-1, keepdims=True)
    e_lo = MOE_GROUPS + g_idx * MOE_PER_GROUP
    emask = (lane >= e_lo) & (lane < e_lo + MOE_PER_GROUP)
    v1, i1 = first_argmax(logits, emask)
    v2, i2 = first_argmax(logits, emask & (lane != i1))
    t = jnp.exp(v2 - v1)
    w1 = g_w / (1.0 + t)
    w2 = w1 * t
    rt = jnp.where(lane == 0, i1 - MOE_GROUPS,
                   jnp.where(lane == 1, i2 - MOE_GROUPS,
                             jnp.where(lane == 2, w1, jnp.where(lane == 3, w2, 0.0))))
    rt_ref[0] = rt


def _outproj(x, ya, yb, o_f, o_b, rg, gn_g, ym, w_out_p, mod, nbatch, norm_g, w_route, b_route):
    B, S, D = x.shape
    nt = S // TM
    tok = lambda w: pl.BlockSpec((1, TM, w), lambda b, i: (b, i, 0))
    full = lambda a: pl.BlockSpec(a.shape, lambda b, i: (0,) * a.ndim)
    return pl.pallas_call(
        _outproj_kernel,
        grid=(B, nt),
        in_specs=[tok(D), tok(256), tok(256), tok(256), tok(256), tok(256), full(gn_g), tok(256),
                  full(w_out_p),
                  _mod_spec(2, nbatch), _mod_spec(3, nbatch), _mod_spec(4, nbatch), full(norm_g),
                  full(w_route), full(b_route)],
        out_specs=[tok(D), tok(D), tok(LANES)],
        out_shape=[jax.ShapeDtypeStruct((B, S, D), F32), jax.ShapeDtypeStruct((B, S, D), F32),
                   jax.ShapeDtypeStruct((B, S, LANES), F32)],
        compiler_params=_cparams(("parallel", "parallel")),
        name="outproj_router",
    )(x, ya, yb, o_f, o_b, rg, gn_g, ym, w_out_p, mod, mod, mod, norm_g, w_route, b_route)


def _rank_kernel(rt_ref, rank_ref, cnt_ref, carry):
    @pl.when(pl.program_id(0) == 0)
    def _():
        carry[...] = jnp.zeros(carry.shape, F32)

    rt = rt_ref[...]
    lane = _lane(rt.shape)
    lanef = lane.astype(F32)
    oh0 = jnp.where(lanef == rt[:, 0:1], 1.0, 0.0)
    oh1 = jnp.where(lanef == rt[:, 1:2], 1.0, 0.0)
    cnt = oh0 + oh1
    r_io = lax.broadcasted_iota(jnp.int32, (TM, TM), 0)
    c_io = lax.broadcasted_iota(jnp.int32, (TM, TM), 1)
    tri = jnp.where(r_io > c_io, 1.0, 0.0).astype(BF16)
    before = _dot(tri, cnt.astype(BF16)) + carry[...]
    r0 = jnp.sum(oh0 * before, axis=-1, keepdims=True)
    r1 = jnp.sum(oh1 * before, axis=-1, keepdims=True)
    rank_ref[...] = jnp.where(lane == 0, r0, jnp.where(lane == 1, r1, 0.0))
    carry[...] = carry[...] + jnp.sum(cnt, axis=0, keepdims=True)
    cnt_ref[...] = carry[...]


def _ranks(route):
    n = route.shape[0]
    return pl.pallas_call(
        _rank_kernel,
        grid=(n // TM,),
        in_specs=[pl.BlockSpec((TM, LANES), lambda i: (i, 0))],
        out_specs=[pl.BlockSpec((TM, LANES), lambda i: (i, 0)),
                   pl.BlockSpec((1, LANES), lambda i: (0, 0))],
        out_shape=[jax.ShapeDtypeStruct((n, LANES), F32), jax.ShapeDtypeStruct((1, LANES), F32)],
        scratch_shapes=[pltpu.VMEM((1, LANES), F32)],
        compiler_params=_cparams(("arbitrary",)),
        name="moe_rank",
    )(route)


def _dispatch_kernel(dest_ref, h_ref, xb_in_ref, xb_ref, sem):
    del xb_in_ref
    base = pl.program_id(0) * TM

    def copy(r, k):
        d = dest_ref[(base + r) * 2 + k]
        return pltpu.make_async_copy(h_ref.at[pl.ds(r, 1)], xb_ref.at[pl.ds(d, 1)], sem)

    def issue(r, c):
        copy(r, 0).start()
        copy(r, 1).start()
        return c

    def drain(r, c):
        copy(r, 0).wait()
        copy(r, 1).wait()
        return c

    lax.fori_loop(0, TM, issue, 0, unroll=DMA_UNROLL)
    lax.fori_loop(0, TM, drain, 0, unroll=DMA_UNROLL)


def _dispatch(dest, h, cap):
    n, d = h.shape
    return pl.pallas_call(
        _dispatch_kernel,
        grid_spec=pltpu.PrefetchScalarGridSpec(
            num_scalar_prefetch=1,
            grid=(n // TM,),
            in_specs=[pl.BlockSpec((TM, d), lambda i, dest: (i, 0)),
                      pl.BlockSpec(memory_space=pl.ANY)],
            out_specs=pl.BlockSpec(memory_space=pl.ANY),
            scratch_shapes=[pltpu.SemaphoreType.DMA(())]),
        out_shape=jax.ShapeDtypeStruct((cap, d), h.dtype),
        input_output_aliases={2: 0},
        compiler_params=_cparams(("arbitrary",)),
        name="moe_dispatch",
    )(dest, h, jnp.zeros((cap, d), h.dtype))


def _expert_kernel(be_ref, nu_ref, x_ref, wg_ref, wu_ref, wd_ref, y_ref, wg_s, wu_s, wd_s):
    i = pl.program_id(0)

    @pl.when((i == 0) | (be_ref[i] != be_ref[jnp.maximum(i - 1, 0)]))
    def _():
        wg_s[...] = wg_ref[0].astype(BF16)
        wu_s[...] = wu_ref[0].astype(BF16)
        wd_s[...] = wd_ref[0].astype(BF16)

    @pl.when(i < nu_ref[0])
    def _():
        xb = x_ref[...].astype(BF16)
        gate = _dot(xb, wg_s[...])
        up = _dot(xb, wu_s[...])
        hid = (gate * jax.nn.sigmoid(gate) * up).astype(BF16)
        y_ref[...] = _dot(hid, wd_s[...])

    @pl.when(i >= nu_ref[0])
    def _():
        y_ref[...] = jnp.zeros(y_ref.shape, y_ref.dtype)


def _experts(block_e, n_used, xb, w_gate, w_up, w_down, layer):
    cap, d = xb.shape
    hdim = w_gate.shape[-1]
    return pl.pallas_call(
        _expert_kernel,
        grid_spec=pltpu.PrefetchScalarGridSpec(
            num_scalar_prefetch=2,
            grid=(cap // MOE_ROWS,),
            in_specs=[pl.BlockSpec((MOE_ROWS, d), lambda i, be, nu: (i, 0)),
                      pl.BlockSpec((None, 1, d, hdim), lambda i, be, nu: (layer, be[i], 0, 0)),
                      pl.BlockSpec((None, 1, d, hdim), lambda i, be, nu: (layer, be[i], 0, 0)),
                      pl.BlockSpec((None, 1, hdim, d), lambda i, be, nu: (layer, be[i], 0, 0))],
            out_specs=pl.BlockSpec((MOE_ROWS, d), lambda i, be, nu: (i, 0)),
            scratch_shapes=[pltpu.VMEM((d, hdim), BF16), pltpu.VMEM((d, hdim), BF16),
                            pltpu.VMEM((hdim, d), BF16)]),
        out_shape=jax.ShapeDtypeStruct((cap, d), F32),
        compiler_params=_cparams(("arbitrary",)),
        name="moe_experts",
    )(block_e, n_used, xb, w_gate, w_up, w_down)


def _combine_kernel(dest_ref, x_ref, rt_ref, g2_ref, fg_ref, yb_ref, o_ref, gath, sem, *,
                    tile_off, tiles_per_batch, final):
    b, i = pl.program_id(0), pl.program_id(1)
    base = (b * tiles_per_batch + i + tile_off) * TM

    def copy(r, k):
        d = dest_ref[(base + r) * 2 + k]
        return pltpu.make_async_copy(yb_ref.at[pl.ds(d, 1)], gath.at[k, pl.ds(r, 1)], sem)

    def issue(r, c):
        copy(r, 0).start()
        copy(r, 1).start()
        return c

    def drain(r, c):
        copy(r, 0).wait()
        copy(r, 1).wait()
        return c

    lax.fori_loop(0, TM, issue, 0, unroll=DMA_UNROLL)
    lax.fori_loop(0, TM, drain, 0, unroll=DMA_UNROLL)
    rt = rt_ref[0]
    m = rt[:, 2:3] * gath[0] + rt[:, 3:4] * gath[1]
    x = x_ref[0] + g2_ref[...] * m
    if final:
        x = _rms(x, fg_ref[...])
    o_ref[0] = x


def _combine(dest, x, route, mod, nbatch, final_g, yb, final):
    B, S, D = x.shape
    nt = S // TM
    off = 1 if final else 0
    g2_spec = pl.BlockSpec((None, 1, D_MODEL),
                           lambda b, i, dest: (jnp.where(i + off == 0, nbatch, b), 0, 5))
    return pl.pallas_call(
        functools.partial(_combine_kernel, tile_off=off, tiles_per_batch=nt, final=final),
        grid_spec=pltpu.PrefetchScalarGridSpec(
            num_scalar_prefetch=1,
            grid=(B, nt - off),
            in_specs=[pl.BlockSpec((1, TM, D), lambda b, i, dest: (b, i + off, 0)),
                      pl.BlockSpec((1, TM, LANES), lambda b, i, dest: (b, i + off, 0)),
                      g2_spec,
                      pl.BlockSpec((1, D), lambda b, i, dest: (0, 0)),
                      pl.BlockSpec(memory_space=pl.ANY)],
            out_specs=pl.BlockSpec((1, TM, D), lambda b, i, dest: (b, i, 0)),
            scratch_shapes=[pltpu.VMEM((2, TM, D), F32), pltpu.SemaphoreType.DMA(())]),
        out_shape=jax.ShapeDtypeStruct((B, S - off * TM, D), F32),
        compiler_params=_cparams(("arbitrary", "arbitrary")),
        name="moe_combine",
    )(dest, x, route, mod, final_g, yb)


def _rope_tables(seq):
    n_rows = seq // GRID_W
    pos = jnp.arange(max(n_rows, GRID_W), dtype=F32)[:, None]
    lane = jnp.arange(LANES, dtype=jnp.int32)
    tabs = []
    for r in (64, 32):
        nf = r // 4
        inv = 1.0 / (ROPE_BASE ** (jnp.arange(nf, dtype=F32) / nf))
        ang = pos * inv[lane % nf][None, :]
        first = ((lane % (2 * nf)) < nf)[None, :]
        by_row = ((lane % r) < 2 * nf)[None, None, :]
        for small in (jnp.cos(ang), jnp.where(first, -jnp.sin(ang), jnp.sin(ang))):
            full = jnp.where(by_row, small[:n_rows, None, :], small[None, :GRID_W, :])
            tabs.append(full.reshape(seq, LANES))
    ident = (jnp.ones((CTX_LEN, LANES), F32), jnp.zeros((CTX_LEN, LANES), F32))
    return [jnp.concatenate([ident[j % 2], tab], axis=0) for j, tab in enumerate(tabs)]


def _layer_weights(w_in, w_out, w_uq, w_ukv, w_group, b_group, w_expert, b_expert):
    perm = jnp.array([0, 2, 1, 3])
    wq = w_in[:, :256].reshape(D_MODEL, 4, 64)[:, perm].reshape(D_MODEL, 256)
    w_in_p = jnp.concatenate([wq, w_in[:, 256:], jnp.zeros((D_MODEL, D_IN_PAD - w_in.shape[1]), F32)],
                             axis=1).astype(BF16)
    wo = jnp.concatenate([w_out[:256].reshape(4, 64, D_MODEL)[perm].reshape(256, D_MODEL), w_out[256:]],
                         axis=0).astype(BF16)
    uq = w_uq.reshape(256, 4, 96)
    blocks = []
    for h in range(4):
        blk = jnp.zeros((256, 256), F32)
        blk = blk.at[:, 64 * (h % 2):64 * (h % 2) + 64].set(uq[:, h, :64])
        blk = blk.at[:, LANES:LANES + 32].set(uq[:, h, 64:])
        blocks.append(blk)
    wuq = jnp.concatenate(blocks, axis=1).astype(BF16)
    ukv = w_ukv.reshape(128, 4, 128)
    wukk = ukv[:, :, :64].reshape(128, 256).astype(BF16)
    wukv = ukv[:, :, 64:].reshape(128, 256).astype(BF16)
    pad = LANES - MOE_GROUPS - MOE_EXPERTS
    w_route = jnp.concatenate([w_group, w_expert, jnp.zeros((D_MODEL, pad), F32)], axis=1)
    w_hi = w_route.astype(BF16)
    w_route = jnp.stack([w_hi, (w_route - w_hi.astype(F32)).astype(BF16)])
    b_route = jnp.concatenate([b_group, b_expert, jnp.zeros((pad,), F32)]).reshape(1, LANES)
    return w_in_p, wo, wuq, wukk, wukv, w_route, b_route


def _moe_plan(route, ranks, counts, cap):
    e = route[:, :2].astype(jnp.int32)
    rank = ranks[:, :2].astype(jnp.int32)
    cnt = counts[0, :MOE_EXPERTS].astype(jnp.int32)
    padded = (cnt + MOE_ROWS - 1) // MOE_ROWS * MOE_ROWS
    pend = jnp.cumsum(padded)
    pstart = pend - padded
    dest = (pstart[e] + rank).reshape(-1)
    nblk = cap // MOE_ROWS
    blk_start = jnp.arange(nblk, dtype=jnp.int32) * MOE_ROWS
    block_e = jnp.minimum(jnp.sum((pend[None, :] <= blk_start[:, None]).astype(jnp.int32), axis=1),
                          MOE_EXPERTS - 1)
    n_used = (pend[-1:] // MOE_ROWS).astype(jnp.int32)
    return dest, block_e, n_used


def kernel(x, c, ctx, c_ctx, w_mod, b_mod, norm1_g, norm2_g, w_in, w_out, swa_sink, diff_lambda, diff_subln_g, ret_decay_logit, ret_gn_g, mla_q_norm_g, mla_w_uq, mla_kv_norm_g, mla_w_ukv, moe_w_group, moe_b_group, moe_w_expert, moe_b_expert, moe_w_gate, moe_w_up, moe_w_down, final_norm_g):
    B, T, D = x.shape
    S = CTX_LEN + T
    n_tok = B * S
    diff_tk, mla_tk = min(DIFF_TK, T), min(MLA_TK, T)
    cap = -(-(2 * n_tok + MOE_EXPERTS * (MOE_ROWS - 1)) // MOE_ROWS) * MOE_ROWS
    xs = jnp.concatenate([ctx, x], axis=1)
    cvec = jnp.zeros((8, D), F32).at[:B].set(c).at[B].set(c_ctx)
    tabs = _rope_tables(T)
    row = lambda v: v.reshape(1, -1)

    for l in range(DEPTH):
        lambda_init = 0.8 - 0.6 * math.exp(-0.3 * l)
        w_in_p, wo, wuq, wukk, wukv, w_route, b_route = _layer_weights(
            w_in[l], w_out[l], mla_w_uq[l], mla_w_ukv[l], moe_w_group[l], moe_b_group[l],
            moe_w_expert[l], moe_b_expert[l])
        mod = _modulation(cvec, w_mod[l], b_mod[l]).reshape(8, 1, 6 * D)
        (swaq, swak, swav, dq, dk, dv, rq, rk, rv, rg, mq, mk, mv) = _inproj(
            xs, mod, B, row(norm1_g[l]), w_in_p, tabs, row(mla_q_norm_g[l]), wuq,
            row(mla_kv_norm_g[l]), wukk, wukv)
        sink_p = swa_sink[l]
        ya = _swa(sink_p, swaq, swak, swav)
        yb = _diff(diff_lambda[l], diff_subln_g[l].reshape(64, 1), dq, dk, dv, lambda_init,
                   diff_tk)
        dl_lane = jnp.repeat(ret_decay_logit[l], 64, axis=-1).reshape(2, 2, 1, LANES)
        o_f, o_b = _retention(dl_lane, rq, rk, rv)
        ym = _mla(mq, mk, mv, mla_tk)
        xs, h2, route = _outproj(xs, ya, yb, o_f, o_b, rg, row(ret_gn_g[l]), ym, wo, mod, B,
                                 row(norm2_g[l]), w_route, b_route)
        route2 = route.reshape(n_tok, LANES)
        ranks, counts = _ranks(route2)
        dest, block_e, n_used = _moe_plan(route2, ranks, counts, cap)
        xb = _dispatch(dest, h2.reshape(n_tok, D), cap)
        yb_e = _experts(block_e, n_used, xb, moe_w_gate, moe_w_up, moe_w_down, l)
        final = l == DEPTH - 1
        xs = _combine(dest, xs, route, mod, B, row(final_norm_g), yb_e, final)
    return xs
```

```python
import functools
import math

import jax
import jax.numpy as jnp
from jax import lax
from jax.experimental import pallas as pl
from jax.experimental.pallas import tpu as pltpu

F32 = jnp.float32
BF16 = jnp.bfloat16

D_MODEL = 1024
CTX_LEN = 256
GRID_W = 64
ROPE_BASE = 10000.0
NORM_EPS = 1e-6
NEG_INF = -1e30
DEPTH = 2

SWA_WINDOW = 128
RET_CHUNK = 128
MLA_SCALE = (64 + 32) ** -0.5
LOG2E = math.log2(math.e)
MOE_GROUPS = 4
MOE_PER_GROUP = 8
MOE_EXPERTS = 32
MOE_HIDDEN = 512

LANES = 128
TM = 256
DIFF_TQ = 2048
MLA_TQ = 2048
DIFF_TK = 1024
MLA_TK = 2048
VT_ROWS = 80
GAP_LIMIT = 64.0
MOE_ROWS = 256
DMA_UNROLL = 8
VMEM_LIMIT = 56 * 1024 * 1024

C_SWA_Q, C_SWA_K, C_SWA_V = 0, 256, 384
C_DIF_Q, C_DIF_K, C_DIF_V = 512, 768, 1024
C_RET_Q, C_RET_K, C_RET_V, C_RET_G = 1280, 1536, 1792, 2048
C_MLA_Q, C_MLA_KV, C_MLA_KR = 2304, 2560, 2688
D_IN_PAD = 2816


def _cparams(sem):
    return pltpu.CompilerParams(dimension_semantics=sem, vmem_limit_bytes=VMEM_LIMIT)


def _dot(a, b):
    return jnp.dot(a, b, preferred_element_type=F32)


def _dot_nt(a, b):
    return lax.dot_general(a, b, (((1,), (1,)), ((), ())), preferred_element_type=F32)


def _dot_tn(a, b):
    return lax.dot_general(a, b, (((0,), (0,)), ((), ())), preferred_element_type=F32)


def _rms(x, g):
    ms = jnp.mean(x * x, axis=-1, keepdims=True)
    return x * lax.rsqrt(ms + NORM_EPS) * g


def _lane(shape):
    return lax.broadcasted_iota(jnp.int32, shape, len(shape) - 1)


def _rope(z, cos, sin, nf):
    first = (_lane(z.shape) % (2 * nf)) < nf
    partner = jnp.where(first, pltpu.roll(z, LANES - nf, 1), pltpu.roll(z, nf, 1))
    return z * cos + partner * sin


def _mod_kernel(a_ref, w_ref, b_ref, o_ref):
    a = a_ref[...]
    act = a * jax.nn.sigmoid(a)
    o_ref[...] = jnp.dot(act, w_ref[...], preferred_element_type=F32,
                         precision=lax.Precision.HIGHEST) + b_ref[...]


def _modulation(cvec, w_mod, b_mod):
    n = w_mod.shape[1]
    bn = 512
    return pl.pallas_call(
        _mod_kernel,
        grid=(n // bn,),
        in_specs=[pl.BlockSpec((8, D_MODEL), lambda j: (0, 0)),
                  pl.BlockSpec((D_MODEL, bn), lambda j: (0, j)),
                  pl.BlockSpec((1, bn), lambda j: (0, j))],
        out_specs=pl.BlockSpec((8, bn), lambda j: (0, j)),
        out_shape=jax.ShapeDtypeStruct((8, n), F32),
        compiler_params=_cparams(("parallel",)),
        name="modulation",
    )(cvec, w_mod, b_mod.reshape(1, n))


def _mod_spec(chunk, nbatch):
    return pl.BlockSpec((None, 1, D_MODEL),
                        lambda b, i: (jnp.where(i == 0, nbatch, b), 0, chunk))


def _inproj_kernel(x_ref, g_ref, sh_ref, sc_ref, w_ref, c64_ref, s64_ref, c32_ref, s32_ref,
                   qng_ref, wuq_ref, kvng_ref, wukk_ref, wukv_ref,
                   swaq, swak, swav, dq, dk, dv, rq, rk, rv, rg, mq, mk, mv):
    x = x_ref[0]
    h = _rms(x, g_ref[...]) * (1.0 + sc_ref[...]) + sh_ref[...]
    hb = h.astype(BF16)
    c64, s64, c32, s32 = c64_ref[...], s64_ref[...], c32_ref[...], s32_ref[...]

    def proj(c0, width):
        return _dot(hb, w_ref[:, c0:c0 + width])

    def rope_blocks(z, out, nf, scale):
        cos, sin = (c64, s64) if nf == 16 else (c32, s32)
        for j in range(z.shape[1] // LANES):
            zz = _rope(z[:, j * LANES:(j + 1) * LANES], cos, sin, nf)
            if scale != 1.0:
                zz = zz * scale
            out[0, :, j * LANES:(j + 1) * LANES] = zz.astype(out.dtype)

    rope_blocks(proj(C_SWA_Q, 256), swaq, 16, 0.125)
    rope_blocks(proj(C_SWA_K, 128), swak, 16, 1.0)
    swav[0] = proj(C_SWA_V, 128).astype(BF16)
    def values_transposed(z, out):
        zt = z.T.astype(BF16)
        for hh in range(4):
            out[0, hh * VT_ROWS:hh * VT_ROWS + 64, :] = zt[hh * 64:(hh + 1) * 64]
            out[0, hh * VT_ROWS + 64:(hh + 1) * VT_ROWS, :] = jnp.ones((VT_ROWS - 64, zt.shape[1]), BF16)

    rope_blocks(proj(C_DIF_Q, 256), dq, 8, 32 ** -0.5 * LOG2E)
    rope_blocks(proj(C_DIF_K, 256), dk, 8, 1.0)
    values_transposed(proj(C_DIF_V, 256), dv)
    rope_blocks(proj(C_RET_Q, 256), rq, 16, 0.125)
    rope_blocks(proj(C_RET_K, 256), rk, 16, 1.0)
    rv[0] = proj(C_RET_V, 256).astype(BF16)
    rg[0] = proj(C_RET_G, 256).astype(BF16)

    ql = _rms(proj(C_MLA_Q, 256), qng_ref[...]).astype(BF16)
    qa = _dot(ql, wuq_ref[...]) * (MLA_SCALE * LOG2E)
    for hh in range(4):
        mq[0, :, hh * 256:hh * 256 + LANES] = qa[:, hh * 256:hh * 256 + LANES].astype(BF16)
        zr = _rope(qa[:, hh * 256 + LANES:(hh + 1) * 256], c32, s32, 8)
        mq[0, :, hh * 256 + LANES:(hh + 1) * 256] = zr.astype(BF16)
    kvl = _rms(proj(C_MLA_KV, 128), kvng_ref[...]).astype(BF16)
    kn = _dot(kvl, wukk_ref[...])
    kr = _rope(proj(C_MLA_KR, 128), c32, s32, 8).astype(BF16)
    for p in range(2):
        mk[0, :, p * 256:p * 256 + LANES] = kn[:, p * LANES:(p + 1) * LANES].astype(BF16)
        mk[0, :, p * 256 + LANES:(p + 1) * 256] = kr
    values_transposed(_dot(kvl, wukv_ref[...]), mv)


def _inproj(x, mod, nbatch, norm_g, w_in_p, tabs, qng, wuq, kvng, wukk, wukv):
    B, S, D = x.shape
    nt = S // TM
    tok = lambda w: pl.BlockSpec((1, TM, w), lambda b, i: (b, i, 0))
    full = lambda a: pl.BlockSpec(a.shape, lambda b, i: (0,) * a.ndim)
    tab = pl.BlockSpec((TM, LANES), lambda b, i: (i, 0))
    widths = [256, 128, 128, 256, 256, None, 256, 256, 256, 256, 1024, 512, None]
    vt_spec = pl.BlockSpec((1, 4 * VT_ROWS, TM), lambda b, i: (b, 0, i))
    vt_shape = jax.ShapeDtypeStruct((B, 4 * VT_ROWS, S), BF16)
    return pl.pallas_call(
        _inproj_kernel,
        grid=(B, nt),
        in_specs=[tok(D), full(norm_g), _mod_spec(0, nbatch), _mod_spec(1, nbatch), full(w_in_p),
                  tab, tab, tab, tab, full(qng), full(wuq), full(kvng), full(wukk), full(wukv)],
        out_specs=[vt_spec if w is None else tok(w) for w in widths],
        out_shape=[vt_shape if w is None else jax.ShapeDtypeStruct((B, S, w), BF16) for w in widths],
        compiler_params=_cparams(("parallel", "parallel")),
        name="inproj",
    )(x, norm_g, mod, mod, w_in_p, *tabs, qng, wuq, kvng, wukk, wukv)


def _swa_kernel(sink_ref, q_ref, kp_ref, ko_ref, kn_ref, vp_ref, vo_ref, vn_ref, kc_ref, vc_ref,
                o_ref, *, n_ctx_tiles, seq):
    i = pl.program_id(1)
    blk = RET_CHUNK
    n = i - n_ctx_tiles
    k_all = jnp.concatenate([kp_ref[0], ko_ref[0], kn_ref[0], kc_ref[0]], axis=0)
    v_all = jnp.concatenate([vp_ref[0], vo_ref[0], vn_ref[0], vc_ref[0]], axis=0)
    nk = 3 * blk + CTX_LEN
    r_io = lax.broadcasted_iota(jnp.int32, (blk, nk), 0)
    c_io = lax.broadcasted_iota(jnp.int32, (blk, nk), 1)
    kpos = (n - 1) * blk + c_io
    in_band = ((c_io >= r_io) & (c_io <= r_io + 2 * SWA_WINDOW) & (kpos >= 0) & (kpos < seq)
               & (n >= 0))
    valid = in_band | (c_io >= 3 * blk)
    lane = _lane((blk, LANES))
    heads = [(r, g) for r in range(2) for g in range(2)]
    qm = [jnp.where((lane >= 64 * g) & (lane < 64 * (g + 1)), q_ref[0, :, r * LANES:(r + 1) * LANES],
                    jnp.zeros((blk, LANES), BF16)) for r, g in heads]
    s = [jnp.where(valid, _dot_nt(q, k_all), NEG_INF) for q in qm]
    sink = [sink_ref[2 * g + r] for r, g in heads]
    m = [jnp.maximum(jnp.max(s[h], axis=-1, keepdims=True), sink[h]) for h in range(4)]
    p = [jnp.exp(s[h] - m[h]) for h in range(4)]
    l = [jnp.sum(p[h], axis=-1, keepdims=True) + jnp.exp(sink[h] - m[h]) for h in range(4)]
    o = [_dot(p[h].astype(BF16), v_all) / l[h] for h in range(4)]
    for r in range(2):
        o_ref[0, :, r * LANES:(r + 1) * LANES] = jnp.where(lane < 64, o[2 * r], o[2 * r + 1]).astype(BF16)


def _swa(sink, q, k, v):
    B, S, _ = q.shape
    blk = RET_CHUNK
    nt = S // blk
    nct = CTX_LEN // blk
    kvs = lambda f: pl.BlockSpec((1, blk, LANES), f)
    prev = lambda b, i: (b, jnp.maximum(i - 1, 0), 0)
    own = lambda b, i: (b, i, 0)
    nxt = lambda b, i: (b, jnp.minimum(i + 1, nt - 1), 0)
    ctx = pl.BlockSpec((1, CTX_LEN, LANES), lambda b, i: (b, 0, 0))
    return pl.pallas_call(
        functools.partial(_swa_kernel, n_ctx_tiles=nct, seq=S - CTX_LEN),
        grid=(B, nt),
        in_specs=[pl.BlockSpec(memory_space=pltpu.SMEM),
                  pl.BlockSpec((1, blk, 256), own),
                  kvs(prev), kvs(own), kvs(nxt), kvs(prev), kvs(own), kvs(nxt), ctx, ctx],
        out_specs=pl.BlockSpec((1, blk, 256), own),
        out_shape=jax.ShapeDtypeStruct((B, S, 256), BF16),
        compiler_params=_cparams(("parallel", "parallel")),
        name="swa",
    )(sink, q, k, k, k, v, v, v, k, v)


def _flash(q_maps, v_blk, k_ref, v_ref, m_scr, acc_scr, gap_scr, *, n_lat, tk):
    nm = len(q_maps)
    qt_maps = [q.astype(F32).T.astype(BF16) for q in q_maps]

    def chunk(start, size, mode):
        kc = k_ref[0, pl.ds(start, size), :]
        for a in range(nm):
            vt = v_ref[0, v_blk[a] * VT_ROWS:(v_blk[a] + 1) * VT_ROWS, pl.ds(start, size)]
            st = _dot(kc, qt_maps[a])
            m_prev = None if mode == "first" else m_scr[a]
            if mode == "fast":
                pf = jnp.exp2(st - m_prev)
                rise = jnp.maximum(jnp.max(pf, axis=0, keepdims=True), 1.0)
                acc_scr[a] = (acc_scr[a] + _dot(vt, pf.astype(BF16))) * (1.0 / rise)
                m_scr[a] = m_prev + jnp.log2(rise)
                gap_scr[a] = jnp.maximum(gap_scr[a], rise)
                continue
            cm = jnp.max(st, axis=0, keepdims=True)
            if mode == "first":
                acc_scr[a] = _dot(vt, jnp.exp2(st - cm).astype(BF16))
                m_scr[a] = cm
                gap_scr[a] = jnp.ones(cm.shape, F32)
            else:
                m_new = jnp.maximum(m_prev, cm)
                p = jnp.exp2(st - m_new).astype(BF16)
                acc_scr[a] = acc_scr[a] * jnp.exp2(m_prev - m_new) + _dot(vt, p)
                m_scr[a] = m_new

    def sweep(mode):
        chunk(0, CTX_LEN, "first")
        if n_lat:

            def body(c, carry):
                chunk(pl.multiple_of(CTX_LEN + c * tk, LANES), tk, mode)
                return carry

            lax.fori_loop(0, n_lat, body, 0)

    sweep("fast")
    if n_lat:
        worst = jnp.max(jnp.concatenate([gap_scr[a] for a in range(nm)], axis=0))

        @pl.when(worst > 2.0 ** GAP_LIMIT)
        def _():
            sweep("exact")

    return [acc_scr[a, 0:64, :] / acc_scr[a, 64:65, :] for a in range(nm)]


def _attention_calls(kern, name, nm, q, k, vt, qw, kw, tk, tq, extra_in=(), extra_specs=()):
    B, S, _ = q.shape
    T = S - CTX_LEN

    def call(tq, n_tiles, n_lat, q_spec, kv_rows, call_name):
        return pl.pallas_call(
            functools.partial(kern, n_lat=n_lat, tk=tk),
            grid=(B, 2, n_tiles),
            in_specs=list(extra_specs) + [
                q_spec,
                pl.BlockSpec((1, kv_rows, kw), lambda b, p, i: (b, 0, p)),
                pl.BlockSpec((1, 2 * VT_ROWS, kv_rows), lambda b, p, i: (b, p, 0))],
            out_specs=pl.BlockSpec((1, tq, LANES), lambda b, p, i: (b, i, p)),
            out_shape=jax.ShapeDtypeStruct((B, tq * n_tiles, 256), BF16),
            scratch_shapes=[pltpu.VMEM((nm, 1, tq), F32), pltpu.VMEM((nm, VT_ROWS, tq), F32),
                            pltpu.VMEM((nm, 1, tq), F32)],
            compiler_params=_cparams(("parallel", "parallel", "parallel")),
            name=call_name,
        )(*extra_in, q, k, vt)

    y_ctx = call(CTX_LEN, 1, 0, pl.BlockSpec((1, CTX_LEN, qw), lambda b, p, i: (b, 0, p)),
                 CTX_LEN, name + "_ctx")
    tq = min(tq, T)
    y_lat = call(tq, T // tq, T // tk,
                 pl.BlockSpec((pl.Element(1), pl.Element(tq), pl.Element(qw)),
                              lambda b, p, i: (b, (i * (tq // CTX_LEN) + 1) * CTX_LEN, p * qw)),
                 S, name)
    return jnp.concatenate([y_ctx, y_lat], axis=1)


def _diff_attn(lam_ref, g_ref, q_ref, k_ref, v_ref, o_ref, m_scr, acc_scr, gap_scr, *,
               n_lat, tk, lambda_init):
    qb = q_ref[0]
    lane = _lane(qb.shape)
    q_maps = []
    for a in range(4):
        msk = (lane >= 32 * a) & (lane < 32 * (a + 1))
        q_maps.append(jnp.where(msk, qb, jnp.zeros_like(qb)))
    n = _flash(q_maps, (0, 0, 1, 1), k_ref, v_ref, m_scr, acc_scr, gap_scr, n_lat=n_lat, tk=tk)
    lp = lam_ref[...]
    lam = (jnp.exp(jnp.sum(lp[0:1] * lp[1:2], axis=-1, keepdims=True))
           - jnp.exp(jnp.sum(lp[2:3] * lp[3:4], axis=-1, keepdims=True)) + lambda_init)
    ys = []
    for hh in range(2):
        o = n[2 * hh] - lam * n[2 * hh + 1]
        ms = jnp.mean(o * o, axis=0, keepdims=True)
        ys.append((o * lax.rsqrt(ms + NORM_EPS) * g_ref[...] * (1.0 - lambda_init)).T)
    o_ref[0] = jnp.concatenate(ys, axis=1).astype(BF16)


def _diff(lam_p, subln_g2, q, k, v, lambda_init, tk):
    return _attention_calls(
        functools.partial(_diff_attn, lambda_init=lambda_init), "diff_attn", 4, q, k, v, LANES, LANES,
        tk, DIFF_TQ, extra_in=(lam_p, subln_g2),
        extra_specs=(pl.BlockSpec(lam_p.shape, lambda b, p, i: (0, 0)),
                     pl.BlockSpec((64, 1), lambda b, p, i: (0, 0))))


def _mla_attn(q_ref, k_ref, v_ref, o_ref, m_scr, acc_scr, gap_scr, *, n_lat, tk):
    q_maps = [q_ref[0, :, 0:256], q_ref[0, :, 256:512]]
    n = _flash(q_maps, (0, 1), k_ref, v_ref, m_scr, acc_scr, gap_scr, n_lat=n_lat, tk=tk)
    o_ref[0] = jnp.concatenate([n[0].T, n[1].T], axis=1).astype(BF16)


def _mla(q, k, v, tk):
    return _attention_calls(_mla_attn, "mla_attn", 2, q, k, v, 512, 256, tk, MLA_TQ)


def _ret_tables(dl, backward):
    L = RET_CHUNK
    lg = -(jnp.maximum(-dl, 0.0) + jnp.log(1.0 + jnp.exp(-jnp.abs(dl))))
    row = lax.broadcasted_iota(jnp.int32, (L, LANES), 0).astype(F32)
    col = lax.broadcasted_iota(jnp.int32, (L, LANES), 1)
    colf = col.astype(F32)
    if backward:
        qd, kd, rel = jnp.exp((L - row) * lg), jnp.exp(row * lg), colf - row
    else:
        qd, kd, rel = jnp.exp((row + 1.0) * lg), jnp.exp((L - 1.0 - row) * lg), row - colf
    decs = []
    for hh in range(2):
        lg_h = jnp.sum(jnp.where(col[0:1] == 64 * hh, lg, 0.0), axis=-1, keepdims=True)
        decs.append(jnp.where(rel >= 0, jnp.exp(jnp.maximum(rel, 0.0) * lg_h), 0.0))
    return qd, kd, decs[0], decs[1], jnp.exp(float(L) * lg)


def _ret_chunks(chains):
    col = _lane((RET_CHUNK, LANES))
    lo = col < 64
    rowi = lax.broadcasted_iota(jnp.int32, (LANES, LANES), 0)
    coli = lax.broadcasted_iota(jnp.int32, (LANES, LANES), 1)
    diag = (rowi < 64) == (coli < 64)
    inter = [_dot((q.astype(F32) * tab[0]).astype(BF16), s_ref[...].astype(BF16))
             for q, k, v, tab, cd, s_ref, store in chains]
    inc = [_dot_tn((k.astype(F32) * tab[1]).astype(BF16), v) for q, k, v, tab, cd, s_ref, store in chains]
    att = [[_dot_nt(jnp.where(lo == (hh == 0), q, jnp.zeros_like(q)), k) * tab[2 + hh] for hh in range(2)]
           for q, k, v, tab, cd, s_ref, store in chains]
    intra = [[_dot(a.astype(BF16), c[2]) for a in pair] for pair, c in zip(att, chains)]
    for c, o_inter, o_intra, s_inc in zip(chains, inter, intra, inc):
        c[6](o_inter + jnp.where(lo, o_intra[0], o_intra[1]))
        c[5][...] = jnp.where(diag, c[5][...] * c[4] + s_inc, 0.0)


def _ret_kernel(dl_ref, qf_ref, kf_ref, vf_ref, qb_ref, kb_ref, vb_ref, of_ref, ob_ref,
                s_scr, tab_scr, cd_scr):
    nb = qf_ref.shape[0]

    @pl.when(pl.program_id(0) == 0)
    def _():
        s_scr[...] = jnp.zeros(s_scr.shape, F32)
        for d in range(2):
            for p in range(2):
                tabs = _ret_tables(dl_ref[d, p], backward=(d == 1))
                for t in range(4):
                    tab_scr[d, p, t] = tabs[t]
                cd_scr[d, p] = tabs[4]

    chains = []
    for d, (q_ref, k_ref, v_ref, o_ref) in enumerate(((qf_ref, kf_ref, vf_ref, of_ref),
                                                      (qb_ref, kb_ref, vb_ref, ob_ref))):
        for p in range(2):
            tab = [tab_scr[d, p, t] for t in range(4)]
            cd = cd_scr[d, p]
            ln = slice(p * LANES, (p + 1) * LANES)
            for b in range(nb):
                def store(o, o_ref=o_ref, b=b, ln=ln):
                    o_ref[b, :, ln] = o
                chains.append((q_ref[b, :, ln], k_ref[b, :, ln], v_ref[b, :, ln], tab, cd,
                               s_scr.at[d, b, p], store))
    _ret_chunks(chains)


def _retention(dl_lane, q, k, v):
    B, S, _ = q.shape
    L = RET_CHUNK
    nc = S // L
    ncc = CTX_LEN // L
    fwd = pl.BlockSpec((B, L, 256), lambda s: (0, s, 0))
    bwd = pl.BlockSpec((B, L, 256), lambda s: (0, jnp.where(s < ncc, ncc - 1 - s, nc - 1 + ncc - s), 0))
    return pl.pallas_call(
        _ret_kernel,
        grid=(nc,),
        in_specs=[pl.BlockSpec(dl_lane.shape, lambda s: (0, 0, 0, 0)), fwd, fwd, fwd, bwd, bwd, bwd],
        out_specs=[fwd, bwd],
        out_shape=[jax.ShapeDtypeStruct((B, S, 256), F32)] * 2,
        scratch_shapes=[pltpu.VMEM((2, B, 2, LANES, LANES), F32),
                        pltpu.VMEM((2, 2, 4, L, LANES), F32), pltpu.VMEM((2, 2, 1, LANES), F32)],
        compiler_params=_cparams(("arbitrary",)),
        name="retention",
    )(dl_lane, q, k, v, q, k, v)


def _gated_group_norm(o, g, gn):
    lo = _lane(o.shape) < 64

    def halves(t):
        a = jnp.sum(jnp.where(lo, t, 0.0), axis=-1, keepdims=True) * (1.0 / 64)
        b = jnp.sum(jnp.where(lo, 0.0, t), axis=-1, keepdims=True) * (1.0 / 64)
        return jnp.where(lo, a, b)

    d = o - halves(o)
    var = halves(d * d)
    return d * lax.rsqrt(var + NORM_EPS) * gn * (g * jax.nn.sigmoid(g))


def _outproj_kernel(x_ref, ya_ref, yb_ref, of_ref, ob_ref, rg_ref, gn_ref, ym_ref, w_ref, g1_ref,
                    sh_ref, sc_ref, ng_ref, wr_ref, br_ref, xo_ref, h_ref, rt_ref, cnt_ref, carry):
    mix = (_dot(ya_ref[0], w_ref[0:256]) + _dot(yb_ref[0], w_ref[256:512])
           + _dot(ym_ref[0], w_ref[768:1024]))
    for j in range(2):
        ln = slice(j * LANES, (j + 1) * LANES)
        yr = _gated_group_norm(of_ref[0, :, ln] + ob_ref[0, :, ln], rg_ref[0, :, ln].astype(F32),
                               gn_ref[:, ln])
        mix = mix + _dot(yr.astype(BF16), w_ref[512 + j * LANES:512 + (j + 1) * LANES])
    x = x_ref[0] + g1_ref[...] * mix
    xo_ref[0] = x
    h = _rms(x, ng_ref[...]) * (1.0 + sc_ref[...]) + sh_ref[...]
    h_ref[0] = h
    h_hi = h.astype(BF16)
    h_lo = (h - h_hi.astype(F32)).astype(BF16)
    logits = (_dot(h_hi, wr_ref[0]) + _dot(h_lo, wr_ref[0]) + _dot(h_hi, wr_ref[1])
              + br_ref[...])
    lane = _lane(logits.shape).astype(F32)

    def first_argmax(vals, mask):
        mx = jnp.max(jnp.where(mask, vals, NEG_INF), axis=-1, keepdims=True)
        idx = jnp.min(jnp.where(mask & (vals == mx), lane, float(LANES)), axis=-1, keepdims=True)
        return mx, idx

    gmask = lane < MOE_GROUPS
    g_max, g_idx = first_argmax(logits, gmask)
    g_w = 1.0 / jnp.sum(jnp.exp(jnp.where(gmask, logits - g_max, NEG_INF)), axis=-1, keepdims=True)
    e_lo = MOE_GROUPS + g_idx * MOE_PER_GROUP
    emask = (lane >= e_lo) & (lane < e_lo + MOE_PER_GROUP)
    v1, i1 = first_argmax(logits, emask)
    v2, i2 = first_argmax(logits, emask & (lane != i1))
    t = jnp.exp(v2 - v1)
    w1 = g_w / (1.0 + t)
    w2 = w1 * t
    e1, e2 = i1 - MOE_GROUPS, i2 - MOE_GROUPS

    @pl.when((pl.program_id(0) == 0) & (pl.program_id(1) == 0))
    def _():
        carry[...] = jnp.zeros(carry.shape, F32)

    oh1 = jnp.where(lane == e1, 1.0, 0.0)
    oh2 = jnp.where(lane == e2, 1.0, 0.0)
    cnt = oh1 + oh2
    tm = cnt.shape[0]
    tri = jnp.where(lax.broadcasted_iota(jnp.int32, (tm, tm), 0)
                    > lax.broadcasted_iota(jnp.int32, (tm, tm), 1), 1.0, 0.0).astype(BF16)
    before = _dot(tri, cnt.astype(BF16)) + carry[...]
    r1 = jnp.sum(oh1 * before, axis=-1, keepdims=True)
    r2 = jnp.sum(oh2 * before, axis=-1, keepdims=True)
    carry[...] = carry[...] + jnp.sum(cnt, axis=0, keepdims=True)
    cnt_ref[...] = carry[...]
    vals = (e1, e2, w1, w2, r1, r2)
    rt = jnp.zeros(logits.shape, F32)
    for j, val in enumerate(vals):
        rt = jnp.where(lane == j, val, rt)
    rt_ref[0] = rt


def _outproj(x, ya, yb, o_f, o_b, rg, gn_g, ym, w_out_p, mod, nbatch, norm_g, w_route, b_route):
    B, S, D = x.shape
    nt = S // TM
    tok = lambda w: pl.BlockSpec((1, TM, w), lambda b, i: (b, i, 0))
    full = lambda a: pl.BlockSpec(a.shape, lambda b, i: (0,) * a.ndim)
    return pl.pallas_call(
        _outproj_kernel,
        grid=(B, nt),
        in_specs=[tok(D), tok(256), tok(256), tok(256), tok(256), tok(256), full(gn_g), tok(256),
                  full(w_out_p),
                  _mod_spec(2, nbatch), _mod_spec(3, nbatch), _mod_spec(4, nbatch), full(norm_g),
                  full(w_route), full(b_route)],
        out_specs=[tok(D), tok(D), tok(LANES), pl.BlockSpec((1, LANES), lambda b, i: (0, 0))],
        out_shape=[jax.ShapeDtypeStruct((B, S, D), F32), jax.ShapeDtypeStruct((B, S, D), F32),
                   jax.ShapeDtypeStruct((B, S, LANES), F32), jax.ShapeDtypeStruct((1, LANES), F32)],
        scratch_shapes=[pltpu.VMEM((1, LANES), F32)],
        compiler_params=_cparams(("arbitrary", "arbitrary")),
        name="outproj_router",
    )(x, ya, yb, o_f, o_b, rg, gn_g, ym, w_out_p, mod, mod, mod, norm_g, w_route, b_route)


def _dispatch_kernel(dest_ref, h_ref, xb_in_ref, xb_ref, sem):
    del xb_in_ref
    base = pl.program_id(0) * TM

    def copy(r, k):
        d = dest_ref[(base + r) * 2 + k]
        return pltpu.make_async_copy(h_ref.at[pl.ds(r, 1)], xb_ref.at[pl.ds(d, 1)], sem)

    def issue(r, c):
        copy(r, 0).start()
        copy(r, 1).start()
        return c

    def drain(r, c):
        copy(r, 0).wait()
        copy(r, 1).wait()
        return c

    lax.fori_loop(0, TM, issue, 0, unroll=DMA_UNROLL)
    lax.fori_loop(0, TM, drain, 0, unroll=DMA_UNROLL)


def _dispatch(dest, h, slots):
    n, d = h.shape
    cap = slots.shape[0]
    return pl.pallas_call(
        _dispatch_kernel,
        grid_spec=pltpu.PrefetchScalarGridSpec(
            num_scalar_prefetch=1,
            grid=(n // TM,),
            in_specs=[pl.BlockSpec((TM, d), lambda i, dest: (i, 0)),
                      pl.BlockSpec(memory_space=pl.ANY)],
            out_specs=pl.BlockSpec(memory_space=pl.ANY),
            scratch_shapes=[pltpu.SemaphoreType.DMA(())]),
        out_shape=jax.ShapeDtypeStruct((cap, d), h.dtype),
        input_output_aliases={2: 0},
        compiler_params=_cparams(("arbitrary",)),
        name="moe_dispatch",
    )(dest, h, slots)


def _expert_kernel(be_ref, nu_ref, x_ref, wg_ref, wu_ref, wd_ref, y_ref, wg_s, wu_s, wd_s):
    i = pl.program_id(0)

    @pl.when((i == 0) | (be_ref[i] != be_ref[jnp.maximum(i - 1, 0)]))
    def _():
        wg_s[...] = wg_ref[0].astype(BF16)
        wu_s[...] = wu_ref[0].astype(BF16)
        wd_s[...] = wd_ref[0].astype(BF16)

    @pl.when(i < nu_ref[0])
    def _():
        xb = x_ref[...].astype(BF16)
        gate = _dot(xb, wg_s[...])
        up = _dot(xb, wu_s[...])
        hid = (gate * jax.nn.sigmoid(gate) * up).astype(BF16)
        y_ref[...] = _dot(hid, wd_s[...])

    @pl.when(i >= nu_ref[0])
    def _():
        y_ref[...] = jnp.zeros(y_ref.shape, y_ref.dtype)


def _experts(block_e, n_used, xb, w_gate, w_up, w_down, layer):
    cap, d = xb.shape
    hdim = w_gate.shape[-1]
    return pl.pallas_call(
        _expert_kernel,
        grid_spec=pltpu.PrefetchScalarGridSpec(
            num_scalar_prefetch=2,
            grid=(cap // MOE_ROWS,),
            in_specs=[pl.BlockSpec((MOE_ROWS, d), lambda i, be, nu: (i, 0)),
                      pl.BlockSpec((None, 1, d, hdim), lambda i, be, nu: (layer, be[i], 0, 0)),
                      pl.BlockSpec((None, 1, d, hdim), lambda i, be, nu: (layer, be[i], 0, 0)),
                      pl.BlockSpec((None, 1, hdim, d), lambda i, be, nu: (layer, be[i], 0, 0))],
            out_specs=pl.BlockSpec((MOE_ROWS, d), lambda i, be, nu: (i, 0)),
            scratch_shapes=[pltpu.VMEM((d, hdim), BF16), pltpu.VMEM((d, hdim), BF16),
                            pltpu.VMEM((hdim, d), BF16)]),
        out_shape=jax.ShapeDtypeStruct((cap, d), F32),
        compiler_params=_cparams(("arbitrary",)),
        name="moe_experts",
    )(block_e, n_used, xb, w_gate, w_up, w_down)


def _combine_kernel(dest_ref, x_ref, rt_ref, g2_ref, fg_ref, yb_ref, o_ref, gath, sem, *,
                    tile_off, tiles_per_batch, final):
    b, i = pl.program_id(0), pl.program_id(1)
    base = (b * tiles_per_batch + i + tile_off) * TM

    def copy(r, k):
        d = dest_ref[(base + r) * 2 + k]
        return pltpu.make_async_copy(yb_ref.at[pl.ds(d, 1)], gath.at[k, pl.ds(r, 1)], sem)

    def issue(r, c):
        copy(r, 0).start()
        copy(r, 1).start()
        return c

    def drain(r, c):
        copy(r, 0).wait()
        copy(r, 1).wait()
        return c

    lax.fori_loop(0, TM, issue, 0, unroll=DMA_UNROLL)
    lax.fori_loop(0, TM, drain, 0, unroll=DMA_UNROLL)
    rt = rt_ref[0]
    m = rt[:, 2:3] * gath[0] + rt[:, 3:4] * gath[1]
    x = x_ref[0] + g2_ref[...] * m
    if final:
        x = _rms(x, fg_ref[...])
    o_ref[0] = x


def _combine(dest, x, route, mod, nbatch, final_g, yb, final):
    B, S, D = x.shape
    nt = S // TM
    off = 1 if final else 0
    g2_spec = pl.BlockSpec((None, 1, D_MODEL),
                           lambda b, i, dest: (jnp.where(i + off == 0, nbatch, b), 0, 5))
    return pl.pallas_call(
        functools.partial(_combine_kernel, tile_off=off, tiles_per_batch=nt, final=final),
        grid_spec=pltpu.PrefetchScalarGridSpec(
            num_scalar_prefetch=1,
            grid=(B, nt - off),
            in_specs=[pl.BlockSpec((1, TM, D), lambda b, i, dest: (b, i + off, 0)),
                      pl.BlockSpec((1, TM, LANES), lambda b, i, dest: (b, i + off, 0)),
                      g2_spec,
                      pl.BlockSpec((1, D), lambda b, i, dest: (0, 0)),
                      pl.BlockSpec(memory_space=pl.ANY)],
            out_specs=pl.BlockSpec((1, TM, D), lambda b, i, dest: (b, i, 0)),
            scratch_shapes=[pltpu.VMEM((2, TM, D), F32), pltpu.SemaphoreType.DMA(())]),
        out_shape=jax.ShapeDtypeStruct((B, S - off * TM, D), F32),
        compiler_params=_cparams(("arbitrary", "arbitrary")),
        name="moe_combine",
    )(dest, x, route, mod, final_g, yb)


def _rope_tables(seq):
    n_rows = seq // GRID_W
    pos = jnp.arange(max(n_rows, GRID_W), dtype=F32)[:, None]
    lane = jnp.arange(LANES, dtype=jnp.int32)
    tabs = []
    for r in (64, 32):
        nf = r // 4
        inv = 1.0 / (ROPE_BASE ** (jnp.arange(nf, dtype=F32) / nf))
        ang = pos * inv[lane % nf][None, :]
        first = ((lane % (2 * nf)) < nf)[None, :]
        by_row = ((lane % r) < 2 * nf)[None, None, :]
        for small in (jnp.cos(ang), jnp.where(first, -jnp.sin(ang), jnp.sin(ang))):
            full = jnp.where(by_row, small[:n_rows, None, :], small[None, :GRID_W, :])
            tabs.append(full.reshape(seq, LANES))
    ident = (jnp.ones((CTX_LEN, LANES), F32), jnp.zeros((CTX_LEN, LANES), F32))
    return [jnp.concatenate([ident[j % 2], tab], axis=0) for j, tab in enumerate(tabs)]


def _layer_weights(w_in, w_out, w_uq, w_ukv, w_group, b_group, w_expert, b_expert):
    perm = jnp.array([0, 2, 1, 3])
    wq = w_in[:, :256].reshape(D_MODEL, 4, 64)[:, perm].reshape(D_MODEL, 256)
    w_in_p = jnp.concatenate([wq, w_in[:, 256:], jnp.zeros((D_MODEL, D_IN_PAD - w_in.shape[1]), F32)],
                             axis=1).astype(BF16)
    wo = jnp.concatenate([w_out[:256].reshape(4, 64, D_MODEL)[perm].reshape(256, D_MODEL), w_out[256:]],
                         axis=0).astype(BF16)
    uq = w_uq.reshape(256, 4, 96)
    blocks = []
    for h in range(4):
        blk = jnp.zeros((256, 256), F32)
        blk = blk.at[:, 64 * (h % 2):64 * (h % 2) + 64].set(uq[:, h, :64])
        blk = blk.at[:, LANES:LANES + 32].set(uq[:, h, 64:])
        blocks.append(blk)
    wuq = jnp.concatenate(blocks, axis=1).astype(BF16)
    ukv = w_ukv.reshape(128, 4, 128)
    wukk = ukv[:, :, :64].reshape(128, 256).astype(BF16)
    wukv = ukv[:, :, 64:].reshape(128, 256).astype(BF16)
    pad = LANES - MOE_GROUPS - MOE_EXPERTS
    w_route = jnp.concatenate([w_group, w_expert, jnp.zeros((D_MODEL, pad), F32)], axis=1)
    w_hi = w_route.astype(BF16)
    w_route = jnp.stack([w_hi, (w_route - w_hi.astype(F32)).astype(BF16)])
    b_route = jnp.concatenate([b_group, b_expert, jnp.zeros((pad,), F32)]).reshape(1, LANES)
    return w_in_p, wo, wuq, wukk, wukv, w_route, b_route


def _moe_plan(route, counts, cap):
    e = route[:, :2].astype(jnp.int32)
    rank = route[:, 4:6].astype(jnp.int32)
    cnt = counts[0, :MOE_EXPERTS].astype(jnp.int32)
    padded = (cnt + MOE_ROWS - 1) // MOE_ROWS * MOE_ROWS
    pend = jnp.cumsum(padded)
    pstart = pend - padded
    dest = (pstart[e] + rank).reshape(-1)
    nblk = cap // MOE_ROWS
    blk_start = jnp.arange(nblk, dtype=jnp.int32) * MOE_ROWS
    block_e = jnp.minimum(jnp.sum((pend[None, :] <= blk_start[:, None]).astype(jnp.int32), axis=1),
                          MOE_EXPERTS - 1)
    n_used = (pend[-1:] // MOE_ROWS).astype(jnp.int32)
    return dest, block_e, n_used


def kernel(x, c, ctx, c_ctx, w_mod, b_mod, norm1_g, norm2_g, w_in, w_out, swa_sink, diff_lambda, diff_subln_g, ret_decay_logit, ret_gn_g, mla_q_norm_g, mla_w_uq, mla_kv_norm_g, mla_w_ukv, moe_w_group, moe_b_group, moe_w_expert, moe_b_expert, moe_w_gate, moe_w_up, moe_w_down, final_norm_g):
    B, T, D = x.shape
    S = CTX_LEN + T
    n_tok = B * S
    diff_tk, mla_tk = min(DIFF_TK, T), min(MLA_TK, T)
    cap = -(-(2 * n_tok + MOE_EXPERTS * (MOE_ROWS - 1)) // MOE_ROWS) * MOE_ROWS
    xs = jnp.concatenate([ctx, x], axis=1)
    cvec = jnp.zeros((8, D), F32).at[:B].set(c).at[B].set(c_ctx)
    tabs = _rope_tables(T)
    row = lambda v: v.reshape(1, -1)

    for l in range(DEPTH):
        lambda_init = 0.8 - 0.6 * math.exp(-0.3 * l)
        w_in_p, wo, wuq, wukk, wukv, w_route, b_route = _layer_weights(
            w_in[l], w_out[l], mla_w_uq[l], mla_w_ukv[l], moe_w_group[l], moe_b_group[l],
            moe_w_expert[l], moe_b_expert[l])
        mod = _modulation(cvec, w_mod[l], b_mod[l]).reshape(8, 1, 6 * D)
        (swaq, swak, swav, dq, dk, dv, rq, rk, rv, rg, mq, mk, mv) = _inproj(
            xs, mod, B, row(norm1_g[l]), w_in_p, tabs, row(mla_q_norm_g[l]), wuq,
            row(mla_kv_norm_g[l]), wukk, wukv)
        sink_p = swa_sink[l]
        ya = _swa(sink_p, swaq, swak, swav)
        yb = _diff(diff_lambda[l], diff_subln_g[l].reshape(64, 1), dq, dk, dv, lambda_init,
                   diff_tk)
        dl_lane = jnp.repeat(ret_decay_logit[l], 64, axis=-1).reshape(2, 2, 1, LANES)
        o_f, o_b = _retention(dl_lane, rq, rk, rv)
        ym = _mla(mq, mk, mv, mla_tk)
        xs, h2, route, counts = _outproj(xs, ya, yb, o_f, o_b, rg, row(ret_gn_g[l]), ym, wo, mod, B,
                                         row(norm2_g[l]), w_route, b_route)
        dest, block_e, n_used = _moe_plan(route.reshape(n_tok, LANES), counts, cap)
        xb = _dispatch(dest, h2.reshape(n_tok, D), jnp.zeros((cap, D), F32) if l == 0 else xb)
        yb_e = _experts(block_e, n_used, xb, moe_w_gate, moe_w_up, moe_w_down, l)
        final = l == DEPTH - 1
        xs = _combine(dest, xs, route, mod, B, row(final_norm_g), yb_e, final)
    return xs
```

```python
import functools
import math

import jax
import jax.numpy as jnp
from jax import lax
from jax.experimental import pallas as pl
from jax.experimental.pallas import tpu as pltpu

F32 = jnp.float32
BF16 = jnp.bfloat16

D_MODEL = 1024
CTX_LEN = 256
GRID_W = 64
ROPE_BASE = 10000.0
NORM_EPS = 1e-6
NEG_INF = -1e30
DEPTH = 2

SWA_WINDOW = 128
RET_CHUNK = 128
MLA_SCALE = (64 + 32) ** -0.5
LOG2E = math.log2(math.e)
MOE_GROUPS = 4
MOE_PER_GROUP = 8
MOE_EXPERTS = 32
MOE_HIDDEN = 512

LANES = 128
TM = 256
DIFF_TQ = 2048
MLA_TQ = 2048
DIFF_TK = 1024
MLA_TK = 2048
VT_ROWS = 80
GAP_LIMIT = 64.0
MOE_ROWS = 256
DMA_UNROLL = 8
VMEM_LIMIT = 56 * 1024 * 1024

C_SWA_Q, C_SWA_K, C_SWA_V = 0, 256, 384
C_DIF_Q, C_DIF_K, C_DIF_V = 512, 768, 1024
C_RET_Q, C_RET_K, C_RET_V, C_RET_G = 1280, 1536, 1792, 2048
C_MLA_Q, C_MLA_KV, C_MLA_KR = 2304, 2560, 2688
D_IN_PAD = 2816


def _cparams(sem):
    return pltpu.CompilerParams(dimension_semantics=sem, vmem_limit_bytes=VMEM_LIMIT)


def _dot(a, b):
    return jnp.dot(a, b, preferred_element_type=F32)


def _dot_nt(a, b):
    return lax.dot_general(a, b, (((1,), (1,)), ((), ())), preferred_element_type=F32)


def _dot_tn(a, b):
    return lax.dot_general(a, b, (((0,), (0,)), ((), ())), preferred_element_type=F32)


def _rms(x, g):
    ms = jnp.mean(x * x, axis=-1, keepdims=True)
    return x * lax.rsqrt(ms + NORM_EPS) * g


def _lane(shape):
    return lax.broadcasted_iota(jnp.int32, shape, len(shape) - 1)


def _rope(z, cos, sin, nf):
    first = (_lane(z.shape) % (2 * nf)) < nf
    partner = jnp.where(first, pltpu.roll(z, LANES - nf, 1), pltpu.roll(z, nf, 1))
    return z * cos + partner * sin


def _mod_kernel(a_ref, w_ref, b_ref, o_ref):
    a = a_ref[...]
    act = a * jax.nn.sigmoid(a)
    o_ref[...] = jnp.dot(act, w_ref[...], preferred_element_type=F32,
                         precision=lax.Precision.HIGHEST) + b_ref[...]


def _modulation(cvec, w_mod, b_mod):
    n = w_mod.shape[1]
    bn = 512
    return pl.pallas_call(
        _mod_kernel,
        grid=(n // bn,),
        in_specs=[pl.BlockSpec((8, D_MODEL), lambda j: (0, 0)),
                  pl.BlockSpec((D_MODEL, bn), lambda j: (0, j)),
                  pl.BlockSpec((1, bn), lambda j: (0, j))],
        out_specs=pl.BlockSpec((8, bn), lambda j: (0, j)),
        out_shape=jax.ShapeDtypeStruct((8, n), F32),
        compiler_params=_cparams(("parallel",)),
        name="modulation",
    )(cvec, w_mod, b_mod.reshape(1, n))


def _mod_spec(chunk, nbatch):
    return pl.BlockSpec((None, 1, D_MODEL),
                        lambda b, i: (jnp.where(i == 0, nbatch, b), 0, chunk))


def _inproj_kernel(x_ref, g_ref, sh_ref, sc_ref, w_ref, c64_ref, s64_ref, c32_ref, s32_ref,
                   qng_ref, wuq_ref, kvng_ref, wukk_ref, wukv_ref,
                   swaq, swak, swav, dq, dk, dv, rq, rk, rv, rg, mq, mk, mv):
    x = x_ref[0]
    h = _rms(x, g_ref[...]) * (1.0 + sc_ref[...]) + sh_ref[...]
    hb = h.astype(BF16)
    c64, s64, c32, s32 = c64_ref[...], s64_ref[...], c32_ref[...], s32_ref[...]

    def proj(c0, width):
        return _dot(hb, w_ref[:, c0:c0 + width])

    def rope_blocks(z, out, nf, scale):
        cos, sin = (c64, s64) if nf == 16 else (c32, s32)
        for j in range(z.shape[1] // LANES):
            zz = _rope(z[:, j * LANES:(j + 1) * LANES], cos, sin, nf)
            if scale != 1.0:
                zz = zz * scale
            out[0, :, j * LANES:(j + 1) * LANES] = zz.astype(out.dtype)

    rope_blocks(proj(C_SWA_Q, 256), swaq, 16, 0.125)
    rope_blocks(proj(C_SWA_K, 128), swak, 16, 1.0)
    swav[0] = proj(C_SWA_V, 128).astype(BF16)
    def values_transposed(z, out):
        zt = z.T.astype(BF16)
        for hh in range(4):
            out[0, hh * VT_ROWS:hh * VT_ROWS + 64, :] = zt[hh * 64:(hh + 1) * 64]
            out[0, hh * VT_ROWS + 64:(hh + 1) * VT_ROWS, :] = jnp.ones((VT_ROWS - 64, zt.shape[1]), BF16)

    rope_blocks(proj(C_DIF_Q, 256), dq, 8, 32 ** -0.5 * LOG2E)
    rope_blocks(proj(C_DIF_K, 256), dk, 8, 1.0)
    values_transposed(proj(C_DIF_V, 256), dv)
    rope_blocks(proj(C_RET_Q, 256), rq, 16, 0.125)
    rope_blocks(proj(C_RET_K, 256), rk, 16, 1.0)
    rv[0] = proj(C_RET_V, 256).astype(BF16)
    rg[0] = proj(C_RET_G, 256).astype(BF16)

    ql = _rms(proj(C_MLA_Q, 256), qng_ref[...]).astype(BF16)
    qa = _dot(ql, wuq_ref[...]) * (MLA_SCALE * LOG2E)
    for hh in range(4):
        mq[0, :, hh * 256:hh * 256 + LANES] = qa[:, hh * 256:hh * 256 + LANES].astype(BF16)
        zr = _rope(qa[:, hh * 256 + LANES:(hh + 1) * 256], c32, s32, 8)
        mq[0, :, hh * 256 + LANES:(hh + 1) * 256] = zr.astype(BF16)
    kvl = _rms(proj(C_MLA_KV, 128), kvng_ref[...]).astype(BF16)
    kn = _dot(kvl, wukk_ref[...])
    kr = _rope(proj(C_MLA_KR, 128), c32, s32, 8).astype(BF16)
    for p in range(2):
        mk[0, :, p * 256:p * 256 + LANES] = kn[:, p * LANES:(p + 1) * LANES].astype(BF16)
        mk[0, :, p * 256 + LANES:(p + 1) * 256] = kr
    values_transposed(_dot(kvl, wukv_ref[...]), mv)


def _inproj(x, mod, nbatch, norm_g, w_in_p, tabs, qng, wuq, kvng, wukk, wukv):
    B, S, D = x.shape
    nt = S // TM
    tok = lambda w: pl.BlockSpec((1, TM, w), lambda b, i: (b, i, 0))
    full = lambda a: pl.BlockSpec(a.shape, lambda b, i: (0,) * a.ndim)
    tab = pl.BlockSpec((TM, LANES), lambda b, i: (i, 0))
    widths = [256, 128, 128, 256, 256, None, 256, 256, 256, 256, 1024, 512, None]
    vt_spec = pl.BlockSpec((1, 4 * VT_ROWS, TM), lambda b, i: (b, 0, i))
    vt_shape = jax.ShapeDtypeStruct((B, 4 * VT_ROWS, S), BF16)
    return pl.pallas_call(
        _inproj_kernel,
        grid=(B, nt),
        in_specs=[tok(D), full(norm_g), _mod_spec(0, nbatch), _mod_spec(1, nbatch), full(w_in_p),
                  tab, tab, tab, tab, full(qng), full(wuq), full(kvng), full(wukk), full(wukv)],
        out_specs=[vt_spec if w is None else tok(w) for w in widths],
        out_shape=[vt_shape if w is None else jax.ShapeDtypeStruct((B, S, w), BF16) for w in widths],
        compiler_params=_cparams(("parallel", "parallel")),
        name="inproj",
    )(x, norm_g, mod, mod, w_in_p, *tabs, qng, wuq, kvng, wukk, wukv)


def _swa_kernel(sink_ref, q_ref, kp_ref, ko_ref, kn_ref, vp_ref, vo_ref, vn_ref, kc_ref, vc_ref,
                o_ref, *, n_ctx_tiles, seq):
    i = pl.program_id(1)
    blk = RET_CHUNK
    kb = [kp_ref[0], ko_ref[0, 0:blk], ko_ref[0, blk:2 * blk], kn_ref[0]]
    vb = [vp_ref[0], vo_ref[0, 0:blk], vo_ref[0, blk:2 * blk], vn_ref[0]]
    nk = 3 * blk + CTX_LEN
    r_io = lax.broadcasted_iota(jnp.int32, (blk, nk), 0)
    c_io = lax.broadcasted_iota(jnp.int32, (blk, nk), 1)
    lane = _lane((blk, LANES))
    chains = [(s, r, g) for s in range(2) for r in range(2) for g in range(2)]
    k_all, v_all, valid = [], [], []
    for s in range(2):
        n = 2 * (i - n_ctx_tiles) + s
        kpos = (n - 1) * blk + c_io
        in_band = ((c_io >= r_io) & (c_io <= r_io + 2 * SWA_WINDOW) & (kpos >= 0) & (kpos < seq)
                   & (n >= 0))
        valid.append(in_band | (c_io >= 3 * blk))
        k_all.append(jnp.concatenate(kb[s:s + 3] + [kc_ref[0]], axis=0))
        v_all.append(jnp.concatenate(vb[s:s + 3] + [vc_ref[0]], axis=0))
    qm = [jnp.where((lane >= 64 * g) & (lane < 64 * (g + 1)),
                    q_ref[0, s * blk:(s + 1) * blk, r * LANES:(r + 1) * LANES],
                    jnp.zeros((blk, LANES), BF16)) for s, r, g in chains]
    sc = [jnp.where(valid[c[0]], _dot_nt(q, k_all[c[0]]), NEG_INF) for q, c in zip(qm, chains)]
    sink = [sink_ref[2 * g + r] for s, r, g in chains]
    m = [jnp.maximum(jnp.max(x, axis=-1, keepdims=True), sk) for x, sk in zip(sc, sink)]
    p = [jnp.exp(x - mm) for x, mm in zip(sc, m)]
    l = [jnp.sum(pp, axis=-1, keepdims=True) + jnp.exp(sk - mm) for pp, sk, mm in zip(p, sink, m)]
    o = [_dot(pp.astype(BF16), v_all[c[0]]) / ll for pp, ll, c in zip(p, l, chains)]
    for s in range(2):
        for r in range(2):
            j = 4 * s + 2 * r
            o_ref[0, s * blk:(s + 1) * blk, r * LANES:(r + 1) * LANES] = (
                jnp.where(lane < 64, o[j], o[j + 1]).astype(BF16))


def _swa(sink, q, k, v):
    B, S, _ = q.shape
    blk = RET_CHUNK
    nb = S // blk
    nt = S // (2 * blk)
    edge = lambda f: pl.BlockSpec((1, blk, LANES), f)
    prev = lambda b, i: (b, jnp.maximum(2 * i - 1, 0), 0)
    own = lambda b, i: (b, i, 0)
    nxt = lambda b, i: (b, jnp.minimum(2 * i + 2, nb - 1), 0)
    own_kv = pl.BlockSpec((1, 2 * blk, LANES), own)
    ctx = pl.BlockSpec((1, CTX_LEN, LANES), lambda b, i: (b, 0, 0))
    return pl.pallas_call(
        functools.partial(_swa_kernel, n_ctx_tiles=CTX_LEN // (2 * blk), seq=S - CTX_LEN),
        grid=(B, nt),
        in_specs=[pl.BlockSpec(memory_space=pltpu.SMEM),
                  pl.BlockSpec((1, 2 * blk, 256), own),
                  edge(prev), own_kv, edge(nxt), edge(prev), own_kv, edge(nxt), ctx, ctx],
        out_specs=pl.BlockSpec((1, 2 * blk, 256), own),
        out_shape=jax.ShapeDtypeStruct((B, S, 256), BF16),
        compiler_params=_cparams(("parallel", "parallel")),
        name="swa",
    )(sink, q, k, k, k, v, v, v, k, v)


def _flash(q_maps, v_blk, k_ref, v_ref, m_scr, acc_scr, gap_scr, *, n_lat, tk):
    nm = len(q_maps)
    qt_maps = [q.astype(F32).T.astype(BF16) for q in q_maps]

    def chunk(start, size, mode):
        kc = k_ref[0, pl.ds(start, size), :]
        for a in range(nm):
            vt = v_ref[0, v_blk[a] * VT_ROWS:(v_blk[a] + 1) * VT_ROWS, pl.ds(start, size)]
            st = _dot(kc, qt_maps[a])
            m_prev = None if mode == "first" else m_scr[a]
            if mode == "fast":
                pf = jnp.exp2(st - m_prev)
                rise = jnp.maximum(jnp.max(pf, axis=0, keepdims=True), 1.0)
                acc_scr[a] = (acc_scr[a] + _dot(vt, pf.astype(BF16))) * (1.0 / rise)
                m_scr[a] = m_prev + jnp.log2(rise)
                gap_scr[a] = jnp.maximum(gap_scr[a], rise)
                continue
            cm = jnp.max(st, axis=0, keepdims=True)
            if mode == "first":
                acc_scr[a] = _dot(vt, jnp.exp2(st - cm).astype(BF16))
                m_scr[a] = cm
                gap_scr[a] = jnp.ones(cm.shape, F32)
            else:
                m_new = jnp.maximum(m_prev, cm)
                p = jnp.exp2(st - m_new).astype(BF16)
                acc_scr[a] = acc_scr[a] * jnp.exp2(m_prev - m_new) + _dot(vt, p)
                m_scr[a] = m_new

    def sweep(mode):
        chunk(0, CTX_LEN, "first")
        if n_lat:

            def body(c, carry):
                chunk(pl.multiple_of(CTX_LEN + c * tk, LANES), tk, mode)
                return carry

            lax.fori_loop(0, n_lat, body, 0)

    sweep("fast")
    if n_lat:
        worst = jnp.max(jnp.concatenate([gap_scr[a] for a in range(nm)], axis=0))

        @pl.when(worst > 2.0 ** GAP_LIMIT)
        def _():
            sweep("exact")

    return [acc_scr[a, 0:64, :] / acc_scr[a, 64:65, :] for a in range(nm)]


def _attention_calls(kern, name, nm, q, k, vt, qw, kw, tk, tq, extra_in=(), extra_specs=()):
    B, S, _ = q.shape
    T = S - CTX_LEN

    def call(tq, n_tiles, n_lat, q_spec, kv_rows, call_name):
        return pl.pallas_call(
            functools.partial(kern, n_lat=n_lat, tk=tk),
            grid=(B, 2, n_tiles),
            in_specs=list(extra_specs) + [
                q_spec,
                pl.BlockSpec((1, kv_rows, kw), lambda b, p, i: (b, 0, p)),
                pl.BlockSpec((1, 2 * VT_ROWS, kv_rows), lambda b, p, i: (b, p, 0))],
            out_specs=pl.BlockSpec((1, tq, LANES), lambda b, p, i: (b, i, p)),
            out_shape=jax.ShapeDtypeStruct((B, tq * n_tiles, 256), BF16),
            scratch_shapes=[pltpu.VMEM((nm, 1, tq), F32), pltpu.VMEM((nm, VT_ROWS, tq), F32),
                            pltpu.VMEM((nm, 1, tq), F32)],
            compiler_params=_cparams(("parallel", "parallel", "parallel")),
            name=call_name,
        )(*extra_in, q, k, vt)

    y_ctx = call(CTX_LEN, 1, 0, pl.BlockSpec((1, CTX_LEN, qw), lambda b, p, i: (b, 0, p)),
                 CTX_LEN, name + "_ctx")
    tq = min(tq, T)
    y_lat = call(tq, T // tq, T // tk,
                 pl.BlockSpec((pl.Element(1), pl.Element(tq), pl.Element(qw)),
                              lambda b, p, i: (b, (i * (tq // CTX_LEN) + 1) * CTX_LEN, p * qw)),
                 S, name)
    return jnp.concatenate([y_ctx, y_lat], axis=1)


def _diff_attn(lam_ref, g_ref, q_ref, k_ref, v_ref, o_ref, m_scr, acc_scr, gap_scr, *,
               n_lat, tk, lambda_init):
    qb = q_ref[0]
    lane = _lane(qb.shape)
    q_maps = []
    for a in range(4):
        msk = (lane >= 32 * a) & (lane < 32 * (a + 1))
        q_maps.append(jnp.where(msk, qb, jnp.zeros_like(qb)))
    n = _flash(q_maps, (0, 0, 1, 1), k_ref, v_ref, m_scr, acc_scr, gap_scr, n_lat=n_lat, tk=tk)
    lp = lam_ref[...]
    lam = (jnp.exp(jnp.sum(lp[0:1] * lp[1:2], axis=-1, keepdims=True))
           - jnp.exp(jnp.sum(lp[2:3] * lp[3:4], axis=-1, keepdims=True)) + lambda_init)
    ys = []
    for hh in range(2):
        o = n[2 * hh] - lam * n[2 * hh + 1]
        ms = jnp.mean(o * o, axis=0, keepdims=True)
        ys.append((o * lax.rsqrt(ms + NORM_EPS) * g_ref[...] * (1.0 - lambda_init)).T)
    o_ref[0] = jnp.concatenate(ys, axis=1).astype(BF16)


def _diff(lam_p, subln_g2, q, k, v, lambda_init, tk):
    return _attention_calls(
        functools.partial(_diff_attn, lambda_init=lambda_init), "diff_attn", 4, q, k, v, LANES, LANES,
        tk, DIFF_TQ, extra_in=(lam_p, subln_g2),
        extra_specs=(pl.BlockSpec(lam_p.shape, lambda b, p, i: (0, 0)),
                     pl.BlockSpec((64, 1), lambda b, p, i: (0, 0))))


def _mla_attn(q_ref, k_ref, v_ref, o_ref, m_scr, acc_scr, gap_scr, *, n_lat, tk):
    q_maps = [q_ref[0, :, 0:256], q_ref[0, :, 256:512]]
    n = _flash(q_maps, (0, 1), k_ref, v_ref, m_scr, acc_scr, gap_scr, n_lat=n_lat, tk=tk)
    o_ref[0] = jnp.concatenate([n[0].T, n[1].T], axis=1).astype(BF16)


def _mla(q, k, v, tk):
    return _attention_calls(_mla_attn, "mla_attn", 2, q, k, v, 512, 256, tk, MLA_TQ)


def _ret_tables(dl, backward):
    L = RET_CHUNK
    lg = -(jnp.maximum(-dl, 0.0) + jnp.log(1.0 + jnp.exp(-jnp.abs(dl))))
    row = lax.broadcasted_iota(jnp.int32, (L, LANES), 0).astype(F32)
    col = lax.broadcasted_iota(jnp.int32, (L, LANES), 1)
    colf = col.astype(F32)
    if backward:
        qd, kd, rel = jnp.exp((L - row) * lg), jnp.exp(row * lg), colf - row
    else:
        qd, kd, rel = jnp.exp((row + 1.0) * lg), jnp.exp((L - 1.0 - row) * lg), row - colf
    decs = []
    for hh in range(2):
        lg_h = jnp.sum(jnp.where(col[0:1] == 64 * hh, lg, 0.0), axis=-1, keepdims=True)
        decs.append(jnp.where(rel >= 0, jnp.exp(jnp.maximum(rel, 0.0) * lg_h), 0.0))
    return qd, kd, decs[0], decs[1], jnp.exp(float(L) * lg)


def _ret_chunks(chains):
    col = _lane((RET_CHUNK, LANES))
    lo = col < 64
    rowi = lax.broadcasted_iota(jnp.int32, (LANES, LANES), 0)
    coli = lax.broadcasted_iota(jnp.int32, (LANES, LANES), 1)
    diag = (rowi < 64) == (coli < 64)
    inter = [_dot((q.astype(F32) * tab[0]).astype(BF16), s_ref[...].astype(BF16))
             for q, k, v, tab, cd, s_ref, store in chains]
    inc = [_dot_tn((k.astype(F32) * tab[1]).astype(BF16), v) for q, k, v, tab, cd, s_ref, store in chains]
    att = [[_dot_nt(jnp.where(lo == (hh == 0), q, jnp.zeros_like(q)), k) * tab[2 + hh] for hh in range(2)]
           for q, k, v, tab, cd, s_ref, store in chains]
    intra = [[_dot(a.astype(BF16), c[2]) for a in pair] for pair, c in zip(att, chains)]
    for c, o_inter, o_intra, s_inc in zip(chains, inter, intra, inc):
        c[6](o_inter + jnp.where(lo, o_intra[0], o_intra[1]))
        c[5][...] = jnp.where(diag, c[5][...] * c[4] + s_inc, 0.0)


def _ret_kernel(dl_ref, qf_ref, kf_ref, vf_ref, qb_ref, kb_ref, vb_ref, of_ref, ob_ref,
                s_scr, tab_scr, cd_scr):
    nb = qf_ref.shape[0]

    @pl.when(pl.program_id(0) == 0)
    def _():
        s_scr[...] = jnp.zeros(s_scr.shape, F32)
        for d in range(2):
            for p in range(2):
                tabs = _ret_tables(dl_ref[d, p], backward=(d == 1))
                for t in range(4):
                    tab_scr[d, p, t] = tabs[t]
                cd_scr[d, p] = tabs[4]

    chains = []
    for d, (q_ref, k_ref, v_ref, o_ref) in enumerate(((qf_ref, kf_ref, vf_ref, of_ref),
                                                      (qb_ref, kb_ref, vb_ref, ob_ref))):
        for p in range(2):
            tab = [tab_scr[d, p, t] for t in range(4)]
            cd = cd_scr[d, p]
            ln = slice(p * LANES, (p + 1) * LANES)
            for b in range(nb):
                def store(o, o_ref=o_ref, b=b, ln=ln):
                    o_ref[b, :, ln] = o
                chains.append((q_ref[b, :, ln], k_ref[b, :, ln], v_ref[b, :, ln], tab, cd,
                               s_scr.at[d, b, p], store))
    _ret_chunks(chains)


def _retention(dl_lane, q, k, v):
    B, S, _ = q.shape
    L = RET_CHUNK
    nc = S // L
    ncc = CTX_LEN // L
    fwd = pl.BlockSpec((B, L, 256), lambda s: (0, s, 0))
    bwd = pl.BlockSpec((B, L, 256), lambda s: (0, jnp.where(s < ncc, ncc - 1 - s, nc - 1 + ncc - s), 0))
    return pl.pallas_call(
        _ret_kernel,
        grid=(nc,),
        in_specs=[pl.BlockSpec(dl_lane.shape, lambda s: (0, 0, 0, 0)), fwd, fwd, fwd, bwd, bwd, bwd],
        out_specs=[fwd, bwd],
        out_shape=[jax.ShapeDtypeStruct((B, S, 256), F32)] * 2,
        scratch_shapes=[pltpu.VMEM((2, B, 2, LANES, LANES), F32),
                        pltpu.VMEM((2, 2, 4, L, LANES), F32), pltpu.VMEM((2, 2, 1, LANES), F32)],
        compiler_params=_cparams(("arbitrary",)),
        name="retention",
    )(dl_lane, q, k, v, q, k, v)


def _gated_group_norm(o, g, gn):
    lo = _lane(o.shape) < 64

    def halves(t):
        a = jnp.sum(jnp.where(lo, t, 0.0), axis=-1, keepdims=True) * (1.0 / 64)
        b = jnp.sum(jnp.where(lo, 0.0, t), axis=-1, keepdims=True) * (1.0 / 64)
        return jnp.where(lo, a, b)

    d = o - halves(o)
    var = halves(d * d)
    return d * lax.rsqrt(var + NORM_EPS) * gn * (g * jax.nn.sigmoid(g))


def _outproj_kernel(x_ref, ya_ref, yb_ref, of_ref, ob_ref, rg_ref, gn_ref, ym_ref, w_ref, g1_ref,
                    sh_ref, sc_ref, ng_ref, wr_ref, br_ref, xo_ref, h_ref, rt_ref, cnt_ref, carry):
    mix = (_dot(ya_ref[0], w_ref[0:256]) + _dot(yb_ref[0], w_ref[256:512])
           + _dot(ym_ref[0], w_ref[768:1024]))
    for j in range(2):
        ln = slice(j * LANES, (j + 1) * LANES)
        yr = _gated_group_norm(of_ref[0, :, ln] + ob_ref[0, :, ln], rg_ref[0, :, ln].astype(F32),
                               gn_ref[:, ln])
        mix = mix + _dot(yr.astype(BF16), w_ref[512 + j * LANES:512 + (j + 1) * LANES])
    x = x_ref[0] + g1_ref[...] * mix
    xo_ref[0] = x
    h = _rms(x, ng_ref[...]) * (1.0 + sc_ref[...]) + sh_ref[...]
    h_ref[0] = h
    h_hi = h.astype(BF16)
    h_lo = (h - h_hi.astype(F32)).astype(BF16)
    logits = (_dot(h_hi, wr_ref[0]) + _dot(h_lo, wr_ref[0]) + _dot(h_hi, wr_ref[1])
              + br_ref[...])
    lane = _lane(logits.shape).astype(F32)

    def first_argmax(vals, mask):
        mx = jnp.max(jnp.where(mask, vals, NEG_INF), axis=-1, keepdims=True)
        idx = jnp.min(jnp.where(mask & (vals == mx), lane, float(LANES)), axis=-1, keepdims=True)
        return mx, idx

    gmask = lane < MOE_GROUPS
    g_max, g_idx = first_argmax(logits, gmask)
    g_w = 1.0 / jnp.sum(jnp.exp(jnp.where(gmask, logits - g_max, NEG_INF)), axis=-1, keepdims=True)
    e_lo = MOE_GROUPS + g_idx * MOE_PER_GROUP
    emask = (lane >= e_lo) & (lane < e_lo + MOE_PER_GROUP)
    v1, i1 = first_argmax(logits, emask)
    v2, i2 = first_argmax(logits, emask & (lane != i1))
    t = jnp.exp(v2 - v1)
    w1 = g_w / (1.0 + t)
    w2 = w1 * t
    e1, e2 = i1 - MOE_GROUPS, i2 - MOE_GROUPS

    @pl.when((pl.program_id(0) == 0) & (pl.program_id(1) == 0))
    def _():
        carry[...] = jnp.zeros(carry.shape, F32)

    oh1 = jnp.where(lane == e1, 1.0, 0.0)
    oh2 = jnp.where(lane == e2, 1.0, 0.0)
    cnt = oh1 + oh2
    tm = cnt.shape[0]
    tri = jnp.where(lax.broadcasted_iota(jnp.int32, (tm, tm), 0)
                    > lax.broadcasted_iota(jnp.int32, (tm, tm), 1), 1.0, 0.0).astype(BF16)
    before = _dot(tri, cnt.astype(BF16)) + carry[...]
    r1 = jnp.sum(oh1 * before, axis=-1, keepdims=True)
    r2 = jnp.sum(oh2 * before, axis=-1, keepdims=True)
    carry[...] = carry[...] + jnp.sum(cnt, axis=0, keepdims=True)
    cnt_ref[...] = carry[...]
    vals = (e1, e2, w1, w2, r1, r2)
    rt = jnp.zeros(logits.shape, F32)
    for j, val in enumerate(vals):
        rt = jnp.where(lane == j, val, rt)
    rt_ref[0] = rt


def _outproj(x, ya, yb, o_f, o_b, rg, gn_g, ym, w_out_p, mod, nbatch, norm_g, w_route, b_route):
    B, S, D = x.shape
    nt = S // TM
    tok = lambda w: pl.BlockSpec((1, TM, w), lambda b, i: (b, i, 0))
    full = lambda a: pl.BlockSpec(a.shape, lambda b, i: (0,) * a.ndim)
    return pl.pallas_call(
        _outproj_kernel,
        grid=(B, nt),
        in_specs=[tok(D), tok(256), tok(256), tok(256), tok(256), tok(256), full(gn_g), tok(256),
                  full(w_out_p),
                  _mod_spec(2, nbatch), _mod_spec(3, nbatch), _mod_spec(4, nbatch), full(norm_g),
                  full(w_route), full(b_route)],
        out_specs=[tok(D), tok(D), tok(LANES), pl.BlockSpec((1, LANES), lambda b, i: (0, 0))],
        out_shape=[jax.ShapeDtypeStruct((B, S, D), F32), jax.ShapeDtypeStruct((B, S, D), F32),
                   jax.ShapeDtypeStruct((B, S, LANES), F32), jax.ShapeDtypeStruct((1, LANES), F32)],
        scratch_shapes=[pltpu.VMEM((1, LANES), F32)],
        compiler_params=_cparams(("arbitrary", "arbitrary")),
        name="outproj_router",
    )(x, ya, yb, o_f, o_b, rg, gn_g, ym, w_out_p, mod, mod, mod, norm_g, w_route, b_route)


def _dispatch_kernel(dest_ref, h_ref, xb_in_ref, xb_ref, sem):
    del xb_in_ref
    base = pl.program_id(0) * TM

    def copy(r, k):
        d = dest_ref[(base + r) * 2 + k]
        return pltpu.make_async_copy(h_ref.at[pl.ds(r, 1)], xb_ref.at[pl.ds(d, 1)], sem)

    def issue(r, c):
        copy(r, 0).start()
        copy(r, 1).start()
        return c

    def drain(r, c):
        copy(r, 0).wait()
        copy(r, 1).wait()
        return c

    lax.fori_loop(0, TM, issue, 0, unroll=DMA_UNROLL)
    lax.fori_loop(0, TM, drain, 0, unroll=DMA_UNROLL)


def _dispatch(dest, h, slots):
    n, d = h.shape
    cap = slots.shape[0]
    return pl.pallas_call(
        _dispatch_kernel,
        grid_spec=pltpu.PrefetchScalarGridSpec(
            num_scalar_prefetch=1,
            grid=(n // TM,),
            in_specs=[pl.BlockSpec((TM, d), lambda i, dest: (i, 0)),
                      pl.BlockSpec(memory_space=pl.ANY)],
            out_specs=pl.BlockSpec(memory_space=pl.ANY),
            scratch_shapes=[pltpu.SemaphoreType.DMA(())]),
        out_shape=jax.ShapeDtypeStruct((cap, d), h.dtype),
        input_output_aliases={2: 0},
        compiler_params=_cparams(("arbitrary",)),
        name="moe_dispatch",
    )(dest, h, slots)


def _expert_kernel(be_ref, nu_ref, x_ref, wg_ref, wu_ref, wd_ref, y_ref, wg_s, wu_s, wd_s):
    i = pl.program_id(0)

    @pl.when((i == 0) | (be_ref[i] != be_ref[jnp.maximum(i - 1, 0)]))
    def _():
        wg_s[...] = wg_ref[0].astype(BF16)
        wu_s[...] = wu_ref[0].astype(BF16)
        wd_s[...] = wd_ref[0].astype(BF16)

    @pl.when(i < nu_ref[0])
    def _():
        xb = x_ref[...].astype(BF16)
        gate = _dot(xb, wg_s[...])
        up = _dot(xb, wu_s[...])
        hid = (gate * jax.nn.sigmoid(gate) * up).astype(BF16)
        y_ref[...] = _dot(hid, wd_s[...])

    @pl.when(i >= nu_ref[0])
    def _():
        y_ref[...] = jnp.zeros(y_ref.shape, y_ref.dtype)


def _experts(block_e, n_used, xb, w_gate, w_up, w_down, layer):
    cap, d = xb.shape
    hdim = w_gate.shape[-1]
    return pl.pallas_call(
        _expert_kernel,
        grid_spec=pltpu.PrefetchScalarGridSpec(
            num_scalar_prefetch=2,
            grid=(cap // MOE_ROWS,),
            in_specs=[pl.BlockSpec((MOE_ROWS, d), lambda i, be, nu: (i, 0)),
                      pl.BlockSpec((None, 1, d, hdim), lambda i, be, nu: (layer, be[i], 0, 0)),
                      pl.BlockSpec((None, 1, d, hdim), lambda i, be, nu: (layer, be[i], 0, 0)),
                      pl.BlockSpec((None, 1, hdim, d), lambda i, be, nu: (layer, be[i], 0, 0))],
            out_specs=pl.BlockSpec((MOE_ROWS, d), lambda i, be, nu: (i, 0)),
            scratch_shapes=[pltpu.VMEM((d, hdim), BF16), pltpu.VMEM((d, hdim), BF16),
                            pltpu.VMEM((hdim, d), BF16)]),
        out_shape=jax.ShapeDtypeStruct((cap, d), F32),
        compiler_params=_cparams(("arbitrary",)),
        name="moe_experts",
    )(block_e, n_used, xb, w_gate, w_up, w_down)


def _combine_kernel(dest_ref, x_ref, rt_ref, g2_ref, fg_ref, yb_ref, o_ref, gath, sem, *,
                    tile_off, tiles_per_batch, final):
    b, i = pl.program_id(0), pl.program_id(1)
    base = (b * tiles_per_batch + i + tile_off) * TM

    def copy(r, k):
        d = dest_ref[(base + r) * 2 + k]
        return pltpu.make_async_copy(yb_ref.at[pl.ds(d, 1)], gath.at[k, pl.ds(r, 1)], sem)

    def issue(r, c):
        copy(r, 0).start()
        copy(r, 1).start()
        return c

    def drain(r, c):
        copy(r, 0).wait()
        copy(r, 1).wait()
        return c

    lax.fori_loop(0, TM, issue, 0, unroll=DMA_UNROLL)
    lax.fori_loop(0, TM, drain, 0, unroll=DMA_UNROLL)
    rt = rt_ref[0]
    m = rt[:, 2:3] * gath[0] + rt[:, 3:4] * gath[1]
    x = x_ref[0] + g2_ref[...] * m
    if final:
        x = _rms(x, fg_ref[...])
    o_ref[0] = x


def _combine(dest, x, route, mod, nbatch, final_g, yb, final):
    B, S, D = x.shape
    nt = S // TM
    off = 1 if final else 0
    g2_spec = pl.BlockSpec((None, 1, D_MODEL),
                           lambda b, i, dest: (jnp.where(i + off == 0, nbatch, b), 0, 5))
    return pl.pallas_call(
        functools.partial(_combine_kernel, tile_off=off, tiles_per_batch=nt, final=final),
        grid_spec=pltpu.PrefetchScalarGridSpec(
            num_scalar_prefetch=1,
            grid=(B, nt - off),
            in_specs=[pl.BlockSpec((1, TM, D), lambda b, i, dest: (b, i + off, 0)),
                      pl.BlockSpec((1, TM, LANES), lambda b, i, dest: (b, i + off, 0)),
                      g2_spec,
                      pl.BlockSpec((1, D), lambda b, i, dest: (0, 0)),
                      pl.BlockSpec(memory_space=pl.ANY)],
            out_specs=pl.BlockSpec((1, TM, D), lambda b, i, dest: (b, i, 0)),
            scratch_shapes=[pltpu.VMEM((2, TM, D), F32), pltpu.SemaphoreType.DMA(())]),
        out_shape=jax.ShapeDtypeStruct((B, S - off * TM, D), F32),
        compiler_params=_cparams(("arbitrary", "arbitrary")),
        name="moe_combine",
    )(dest, x, route, mod, final_g, yb)


def _rope_tables(seq):
    n_rows = seq // GRID_W
    pos = jnp.arange(max(n_rows, GRID_W), dtype=F32)[:, None]
    lane = jnp.arange(LANES, dtype=jnp.int32)
    tabs = []
    for r in (64, 32):
        nf = r // 4
        inv = 1.0 / (ROPE_BASE ** (jnp.arange(nf, dtype=F32) / nf))
        ang = pos * inv[lane % nf][None, :]
        first = ((lane % (2 * nf)) < nf)[None, :]
        by_row = ((lane % r) < 2 * nf)[None, None, :]
        for small in (jnp.cos(ang), jnp.where(first, -jnp.sin(ang), jnp.sin(ang))):
            full = jnp.where(by_row, small[:n_rows, None, :], small[None, :GRID_W, :])
            tabs.append(full.reshape(seq, LANES))
    ident = (jnp.ones((CTX_LEN, LANES), F32), jnp.zeros((CTX_LEN, LANES), F32))
    return [jnp.concatenate([ident[j % 2], tab], axis=0) for j, tab in enumerate(tabs)]


def _layer_weights(w_in, w_out, w_uq, w_ukv, w_group, b_group, w_expert, b_expert):
    perm = jnp.array([0, 2, 1, 3])
    wq = w_in[:, :256].reshape(D_MODEL, 4, 64)[:, perm].reshape(D_MODEL, 256)
    w_in_p = jnp.concatenate([wq, w_in[:, 256:], jnp.zeros((D_MODEL, D_IN_PAD - w_in.shape[1]), F32)],
                             axis=1).astype(BF16)
    wo = jnp.concatenate([w_out[:256].reshape(4, 64, D_MODEL)[perm].reshape(256, D_MODEL), w_out[256:]],
                         axis=0).astype(BF16)
    uq = w_uq.reshape(256, 4, 96)
    blocks = []
    for h in range(4):
        blk = jnp.zeros((256, 256), F32)
        blk = blk.at[:, 64 * (h % 2):64 * (h % 2) + 64].set(uq[:, h, :64])
        blk = blk.at[:, LANES:LANES + 32].set(uq[:, h, 64:])
        blocks.append(blk)
    wuq = jnp.concatenate(blocks, axis=1).astype(BF16)
    ukv = w_ukv.reshape(128, 4, 128)
    wukk = ukv[:, :, :64].reshape(128, 256).astype(BF16)
    wukv = ukv[:, :, 64:].reshape(128, 256).astype(BF16)
    pad = LANES - MOE_GROUPS - MOE_EXPERTS
    w_route = jnp.concatenate([w_group, w_expert, jnp.zeros((D_MODEL, pad), F32)], axis=1)
    w_hi = w_route.astype(BF16)
    w_route = jnp.stack([w_hi, (w_route - w_hi.astype(F32)).astype(BF16)])
    b_route = jnp.concatenate([b_group, b_expert, jnp.zeros((pad,), F32)]).reshape(1, LANES)
    return w_in_p, wo, wuq, wukk, wukv, w_route, b_route


def _moe_plan(route, counts, cap):
    e = route[:, :2].astype(jnp.int32).reshape(-1)
    cnt = counts[0, :MOE_EXPERTS].astype(jnp.int32)
    padded = (cnt + MOE_ROWS - 1) // MOE_ROWS * MOE_ROWS
    pend = jnp.cumsum(padded)
    pstart = pend - padded
    dest = route[:, 4:6].astype(jnp.int32).reshape(-1)
    for j in range(MOE_EXPERTS):
        dest = dest + jnp.where(e == j, pstart[j], 0)
    nblk = cap // MOE_ROWS
    blk_start = jnp.arange(nblk, dtype=jnp.int32) * MOE_ROWS
    block_e = jnp.minimum(jnp.sum((pend[None, :] <= blk_start[:, None]).astype(jnp.int32), axis=1),
                          MOE_EXPERTS - 1)
    n_used = (pend[-1:] // MOE_ROWS).astype(jnp.int32)
    return dest, block_e, n_used


def kernel(x, c, ctx, c_ctx, w_mod, b_mod, norm1_g, norm2_g, w_in, w_out, swa_sink, diff_lambda, diff_subln_g, ret_decay_logit, ret_gn_g, mla_q_norm_g, mla_w_uq, mla_kv_norm_g, mla_w_ukv, moe_w_group, moe_b_group, moe_w_expert, moe_b_expert, moe_w_gate, moe_w_up, moe_w_down, final_norm_g):
    B, T, D = x.shape
    S = CTX_LEN + T
    n_tok = B * S
    diff_tk, mla_tk = min(DIFF_TK, T), min(MLA_TK, T)
    cap = -(-(2 * n_tok + MOE_EXPERTS * (MOE_ROWS - 1)) // MOE_ROWS) * MOE_ROWS
    xs = jnp.concatenate([ctx, x], axis=1)
    cvec = jnp.zeros((8, D), F32).at[:B].set(c).at[B].set(c_ctx)
    tabs = _rope_tables(T)
    row = lambda v: v.reshape(1, -1)

    for l in range(DEPTH):
        lambda_init = 0.8 - 0.6 * math.exp(-0.3 * l)
        w_in_p, wo, wuq, wukk, wukv, w_route, b_route = _layer_weights(
            w_in[l], w_out[l], mla_w_uq[l], mla_w_ukv[l], moe_w_group[l], moe_b_group[l],
            moe_w_expert[l], moe_b_expert[l])
        mod = _modulation(cvec, w_mod[l], b_mod[l]).reshape(8, 1, 6 * D)
        (swaq, swak, swav, dq, dk, dv, rq, rk, rv, rg, mq, mk, mv) = _inproj(
            xs, mod, B, row(norm1_g[l]), w_in_p, tabs, row(mla_q_norm_g[l]), wuq,
            row(mla_kv_norm_g[l]), wukk, wukv)
        sink_p = swa_sink[l]
        ya = _swa(sink_p, swaq, swak, swav)
        yb = _diff(diff_lambda[l], diff_subln_g[l].reshape(64, 1), dq, dk, dv, lambda_init,
                   diff_tk)
        dl_lane = jnp.repeat(ret_decay_logit[l], 64, axis=-1).reshape(2, 2, 1, LANES)
        o_f, o_b = _retention(dl_lane, rq, rk, rv)
        ym = _mla(mq, mk, mv, mla_tk)
        xs, h2, route, counts = _outproj(xs, ya, yb, o_f, o_b, rg, row(ret_gn_g[l]), ym, wo, mod, B,
                                         row(norm2_g[l]), w_route, b_route)
        dest, block_e, n_used = _moe_plan(route.reshape(n_tok, LANES), counts, cap)
        xb = _dispatch(dest, h2.reshape(n_tok, D), jnp.zeros((cap, D), F32) if l == 0 else xb)
        yb_e = _experts(block_e, n_used, xb, moe_w_gate, moe_w_up, moe_w_down, l)
        final = l == DEPTH - 1
        xs = _combine(dest, xs, route, mod, B, row(final_norm_g), yb_e, final)
    return xs
```

```python
import functools
import math

import jax
import jax.numpy as jnp
from jax import lax
from jax.experimental import pallas as pl
from jax.experimental.pallas import tpu as pltpu

F32 = jnp.float32
BF16 = jnp.bfloat16

D_MODEL = 1024
CTX_LEN = 256
GRID_W = 64
ROPE_BASE = 10000.0
NORM_EPS = 1e-6
NEG_INF = -1e30
DEPTH = 2

SWA_WINDOW = 128
RET_CHUNK = 128
MLA_SCALE = (64 + 32) ** -0.5
LOG2E = math.log2(math.e)
MOE_GROUPS = 4
MOE_PER_GROUP = 8
MOE_EXPERTS = 32
MOE_HIDDEN = 512

LANES = 128
TM = 256
DIFF_TQ = 2048
MLA_TQ = 2048
DIFF_TK = 1024
MLA_TK = 2048
VT_ROWS = 80
GAP_LIMIT = 64.0
MOE_ROWS = 256
DMA_UNROLL = 8
VMEM_LIMIT = 56 * 1024 * 1024

C_SWA_Q, C_SWA_K, C_SWA_V = 0, 256, 384
C_DIF_Q, C_DIF_K, C_DIF_V = 512, 768, 1024
C_RET_Q, C_RET_K, C_RET_V, C_RET_G = 1280, 1536, 1792, 2048
C_MLA_Q, C_MLA_KV, C_MLA_KR = 2304, 2560, 2688
D_IN_PAD = 2816


def _cparams(sem):
    return pltpu.CompilerParams(dimension_semantics=sem, vmem_limit_bytes=VMEM_LIMIT)


def _dot(a, b):
    return jnp.dot(a, b, preferred_element_type=F32)


def _dot_nt(a, b):
    return lax.dot_general(a, b, (((1,), (1,)), ((), ())), preferred_element_type=F32)


def _dot_tn(a, b):
    return lax.dot_general(a, b, (((0,), (0,)), ((), ())), preferred_element_type=F32)


def _rms(x, g):
    ms = jnp.mean(x * x, axis=-1, keepdims=True)
    return x * lax.rsqrt(ms + NORM_EPS) * g


def _lane(shape):
    return lax.broadcasted_iota(jnp.int32, shape, len(shape) - 1)


def _rope(z, cos, sin, nf):
    first = (_lane(z.shape) % (2 * nf)) < nf
    partner = jnp.where(first, pltpu.roll(z, LANES - nf, 1), pltpu.roll(z, nf, 1))
    return z * cos + partner * sin


def _mod_kernel(a_ref, w_ref, b_ref, o_ref):
    a = a_ref[...]
    act = a * jax.nn.sigmoid(a)
    o_ref[...] = jnp.dot(act, w_ref[...], preferred_element_type=F32,
                         precision=lax.Precision.HIGHEST) + b_ref[...]


def _modulation(cvec, w_mod, b_mod):
    n = w_mod.shape[1]
    bn = 512
    return pl.pallas_call(
        _mod_kernel,
        grid=(n // bn,),
        in_specs=[pl.BlockSpec((8, D_MODEL), lambda j: (0, 0)),
                  pl.BlockSpec((D_MODEL, bn), lambda j: (0, j)),
                  pl.BlockSpec((1, bn), lambda j: (0, j))],
        out_specs=pl.BlockSpec((8, bn), lambda j: (0, j)),
        out_shape=jax.ShapeDtypeStruct((8, n), F32),
        compiler_params=_cparams(("parallel",)),
        name="modulation",
    )(cvec, w_mod, b_mod.reshape(1, n))


def _mod_spec(chunk, nbatch):
    return pl.BlockSpec((None, 1, D_MODEL),
                        lambda b, i: (jnp.where(i == 0, nbatch, b), 0, chunk))


def _inproj_kernel(x_ref, g_ref, sh_ref, sc_ref, w_ref, c64_ref, s64_ref, c32_ref, s32_ref,
                   qng_ref, wuq_ref, kvng_ref, wukk_ref, wukv_ref,
                   swaq, swak, swav, dq, dk, dv, rq, rk, rv, rg, mq, mk, mv):
    x = x_ref[0]
    h = _rms(x, g_ref[...]) * (1.0 + sc_ref[...]) + sh_ref[...]
    hb = h.astype(BF16)
    c64, s64, c32, s32 = c64_ref[...], s64_ref[...], c32_ref[...], s32_ref[...]

    def proj(c0, width):
        return _dot(hb, w_ref[:, c0:c0 + width])

    def rope_blocks(z, out, nf, scale):
        cos, sin = (c64, s64) if nf == 16 else (c32, s32)
        for j in range(z.shape[1] // LANES):
            zz = _rope(z[:, j * LANES:(j + 1) * LANES], cos, sin, nf)
            if scale != 1.0:
                zz = zz * scale
            out[0, :, j * LANES:(j + 1) * LANES] = zz.astype(out.dtype)

    rope_blocks(proj(C_SWA_Q, 256), swaq, 16, 0.125)
    rope_blocks(proj(C_SWA_K, 128), swak, 16, 1.0)
    swav[0] = proj(C_SWA_V, 128).astype(BF16)
    def values_transposed(z, out):
        zt = z.T.astype(BF16)
        for hh in range(4):
            out[0, hh * VT_ROWS:hh * VT_ROWS + 64, :] = zt[hh * 64:(hh + 1) * 64]
            out[0, hh * VT_ROWS + 64:(hh + 1) * VT_ROWS, :] = jnp.ones((VT_ROWS - 64, zt.shape[1]), BF16)

    rope_blocks(proj(C_DIF_Q, 256), dq, 8, 32 ** -0.5 * LOG2E)
    rope_blocks(proj(C_DIF_K, 256), dk, 8, 1.0)
    values_transposed(proj(C_DIF_V, 256), dv)
    rope_blocks(proj(C_RET_Q, 256), rq, 16, 0.125)
    rope_blocks(proj(C_RET_K, 256), rk, 16, 1.0)
    rv[0] = proj(C_RET_V, 256).astype(BF16)
    rg[0] = proj(C_RET_G, 256).astype(BF16)

    ql = _rms(proj(C_MLA_Q, 256), qng_ref[...]).astype(BF16)
    qa = _dot(ql, wuq_ref[...]) * (MLA_SCALE * LOG2E)
    for hh in range(4):
        mq[0, :, hh * 256:hh * 256 + LANES] = qa[:, hh * 256:hh * 256 + LANES].astype(BF16)
        zr = _rope(qa[:, hh * 256 + LANES:(hh + 1) * 256], c32, s32, 8)
        mq[0, :, hh * 256 + LANES:(hh + 1) * 256] = zr.astype(BF16)
    kvl = _rms(proj(C_MLA_KV, 128), kvng_ref[...]).astype(BF16)
    kn = _dot(kvl, wukk_ref[...])
    kr = _rope(proj(C_MLA_KR, 128), c32, s32, 8).astype(BF16)
    for p in range(2):
        mk[0, :, p * 256:p * 256 + LANES] = kn[:, p * LANES:(p + 1) * LANES].astype(BF16)
        mk[0, :, p * 256 + LANES:(p + 1) * 256] = kr
    values_transposed(_dot(kvl, wukv_ref[...]), mv)


def _inproj(x, mod, nbatch, norm_g, w_in_p, tabs, qng, wuq, kvng, wukk, wukv):
    B, S, D = x.shape
    nt = S // TM
    tok = lambda w: pl.BlockSpec((1, TM, w), lambda b, i: (b, i, 0))
    full = lambda a: pl.BlockSpec(a.shape, lambda b, i: (0,) * a.ndim)
    tab = pl.BlockSpec((TM, LANES), lambda b, i: (i, 0))
    widths = [256, 128, 128, 256, 256, None, 256, 256, 256, 256, 1024, 512, None]
    vt_spec = pl.BlockSpec((1, 4 * VT_ROWS, TM), lambda b, i: (b, 0, i))
    vt_shape = jax.ShapeDtypeStruct((B, 4 * VT_ROWS, S), BF16)
    return pl.pallas_call(
        _inproj_kernel,
        grid=(B, nt),
        in_specs=[tok(D), full(norm_g), _mod_spec(0, nbatch), _mod_spec(1, nbatch), full(w_in_p),
                  tab, tab, tab, tab, full(qng), full(wuq), full(kvng), full(wukk), full(wukv)],
        out_specs=[vt_spec if w is None else tok(w) for w in widths],
        out_shape=[vt_shape if w is None else jax.ShapeDtypeStruct((B, S, w), BF16) for w in widths],
        compiler_params=_cparams(("parallel", "parallel")),
        name="inproj",
    )(x, norm_g, mod, mod, w_in_p, *tabs, qng, wuq, kvng, wukk, wukv)


def _swa_kernel(sink_ref, q_ref, kp_ref, ko_ref, kn_ref, vp_ref, vo_ref, vn_ref, kc_ref, vc_ref,
                o_ref, *, n_ctx_tiles, seq):
    i = pl.program_id(1)
    blk = RET_CHUNK
    kb = [kp_ref[0], ko_ref[0, 0:blk], ko_ref[0, blk:2 * blk], kn_ref[0]]
    vb = [vp_ref[0], vo_ref[0, 0:blk], vo_ref[0, blk:2 * blk], vn_ref[0]]
    nk = 3 * blk + CTX_LEN
    r_io = lax.broadcasted_iota(jnp.int32, (blk, nk), 0)
    c_io = lax.broadcasted_iota(jnp.int32, (blk, nk), 1)
    lane = _lane((blk, LANES))
    chains = [(s, r, g) for s in range(2) for r in range(2) for g in range(2)]
    k_all, v_all, valid = [], [], []
    for s in range(2):
        n = 2 * (i - n_ctx_tiles) + s
        kpos = (n - 1) * blk + c_io
        in_band = ((c_io >= r_io) & (c_io <= r_io + 2 * SWA_WINDOW) & (kpos >= 0) & (kpos < seq)
                   & (n >= 0))
        valid.append(in_band | (c_io >= 3 * blk))
        k_all.append(jnp.concatenate(kb[s:s + 3] + [kc_ref[0]], axis=0))
        v_all.append(jnp.concatenate(vb[s:s + 3] + [vc_ref[0]], axis=0))
    qm = [jnp.where((lane >= 64 * g) & (lane < 64 * (g + 1)),
                    q_ref[0, s * blk:(s + 1) * blk, r * LANES:(r + 1) * LANES],
                    jnp.zeros((blk, LANES), BF16)) for s, r, g in chains]
    sc = [jnp.where(valid[c[0]], _dot_nt(q, k_all[c[0]]), NEG_INF) for q, c in zip(qm, chains)]
    sink = [sink_ref[2 * g + r] for s, r, g in chains]
    m = [jnp.maximum(jnp.max(x, axis=-1, keepdims=True), sk) for x, sk in zip(sc, sink)]
    p = [jnp.exp(x - mm) for x, mm in zip(sc, m)]
    l = [jnp.sum(pp, axis=-1, keepdims=True) + jnp.exp(sk - mm) for pp, sk, mm in zip(p, sink, m)]
    o = [_dot(pp.astype(BF16), v_all[c[0]]) / ll for pp, ll, c in zip(p, l, chains)]
    for s in range(2):
        for r in range(2):
            j = 4 * s + 2 * r
            o_ref[0, s * blk:(s + 1) * blk, r * LANES:(r + 1) * LANES] = (
                jnp.where(lane < 64, o[j], o[j + 1]).astype(BF16))


def _swa(sink, q, k, v):
    B, S, _ = q.shape
    blk = RET_CHUNK
    nb = S // blk
    nt = S // (2 * blk)
    edge = lambda f: pl.BlockSpec((1, blk, LANES), f)
    prev = lambda b, i: (b, jnp.maximum(2 * i - 1, 0), 0)
    own = lambda b, i: (b, i, 0)
    nxt = lambda b, i: (b, jnp.minimum(2 * i + 2, nb - 1), 0)
    own_kv = pl.BlockSpec((1, 2 * blk, LANES), own)
    ctx = pl.BlockSpec((1, CTX_LEN, LANES), lambda b, i: (b, 0, 0))
    return pl.pallas_call(
        functools.partial(_swa_kernel, n_ctx_tiles=CTX_LEN // (2 * blk), seq=S - CTX_LEN),
        grid=(B, nt),
        in_specs=[pl.BlockSpec(memory_space=pltpu.SMEM),
                  pl.BlockSpec((1, 2 * blk, 256), own),
                  edge(prev), own_kv, edge(nxt), edge(prev), own_kv, edge(nxt), ctx, ctx],
        out_specs=pl.BlockSpec((1, 2 * blk, 256), own),
        out_shape=jax.ShapeDtypeStruct((B, S, 256), BF16),
        compiler_params=_cparams(("parallel", "parallel")),
        name="swa",
    )(sink, q, k, k, k, v, v, v, k, v)


def _flash(q_maps, v_blk, k_ref, v_ref, m_scr, acc_scr, gap_scr, *, n_lat, tk):
    nm = len(q_maps)
    qt_maps = [q.astype(F32).T.astype(BF16) for q in q_maps]

    def chunk(start, size, mode):
        kc = k_ref[0, pl.ds(start, size), :]
        for a in range(nm):
            vt = v_ref[0, v_blk[a] * VT_ROWS:(v_blk[a] + 1) * VT_ROWS, pl.ds(start, size)]
            st = _dot(kc, qt_maps[a])
            m_prev = None if mode == "first" else m_scr[a]
            if mode == "fast":
                pf = jnp.exp2(st - m_prev)
                rise = jnp.maximum(jnp.max(pf, axis=0, keepdims=True), 1.0)
                acc_scr[a] = (acc_scr[a] + _dot(vt, pf.astype(BF16))) * (1.0 / rise)
                m_scr[a] = m_prev + jnp.log2(rise)
                gap_scr[a] = jnp.maximum(gap_scr[a], rise)
                continue
            cm = jnp.max(st, axis=0, keepdims=True)
            if mode == "first":
                acc_scr[a] = _dot(vt, jnp.exp2(st - cm).astype(BF16))
                m_scr[a] = cm
                gap_scr[a] = jnp.ones(cm.shape, F32)
            else:
                m_new = jnp.maximum(m_prev, cm)
                p = jnp.exp2(st - m_new).astype(BF16)
                acc_scr[a] = acc_scr[a] * jnp.exp2(m_prev - m_new) + _dot(vt, p)
                m_scr[a] = m_new

    def sweep(mode):
        chunk(0, CTX_LEN, "first")
        if n_lat:

            def body(c, carry):
                chunk(pl.multiple_of(CTX_LEN + c * tk, LANES), tk, mode)
                return carry

            lax.fori_loop(0, n_lat, body, 0)

    sweep("fast")
    if n_lat:
        worst = jnp.max(jnp.concatenate([gap_scr[a] for a in range(nm)], axis=0))

        @pl.when(worst > 2.0 ** GAP_LIMIT)
        def _():
            sweep("exact")

    return [acc_scr[a, 0:64, :] / acc_scr[a, 64:65, :] for a in range(nm)]


def _attention_calls(kern, name, nm, q, k, vt, qw, kw, tk, tq, extra_in=(), extra_specs=()):
    B, S, _ = q.shape
    T = S - CTX_LEN

    def call(tq, n_tiles, n_lat, q_spec, kv_rows, call_name):
        return pl.pallas_call(
            functools.partial(kern, n_lat=n_lat, tk=tk),
            grid=(B, 2, n_tiles),
            in_specs=list(extra_specs) + [
                q_spec,
                pl.BlockSpec((1, kv_rows, kw), lambda b, p, i: (b, 0, p)),
                pl.BlockSpec((1, 2 * VT_ROWS, kv_rows), lambda b, p, i: (b, p, 0))],
            out_specs=pl.BlockSpec((1, tq, LANES), lambda b, p, i: (b, i, p)),
            out_shape=jax.ShapeDtypeStruct((B, tq * n_tiles, 256), BF16),
            scratch_shapes=[pltpu.VMEM((nm, 1, tq), F32), pltpu.VMEM((nm, VT_ROWS, tq), F32),
                            pltpu.VMEM((nm, 1, tq), F32)],
            compiler_params=_cparams(("parallel", "parallel", "parallel")),
            name=call_name,
        )(*extra_in, q, k, vt)

    y_ctx = call(CTX_LEN, 1, 0, pl.BlockSpec((1, CTX_LEN, qw), lambda b, p, i: (b, 0, p)),
                 CTX_LEN, name + "_ctx")
    tq = min(tq, T)
    y_lat = call(tq, T // tq, T // tk,
                 pl.BlockSpec((pl.Element(1), pl.Element(tq), pl.Element(qw)),
                              lambda b, p, i: (b, (i * (tq // CTX_LEN) + 1) * CTX_LEN, p * qw)),
                 S, name)
    return jnp.concatenate([y_ctx, y_lat], axis=1)


def _diff_attn(lam_ref, g_ref, q_ref, k_ref, v_ref, o_ref, m_scr, acc_scr, gap_scr, *,
               n_lat, tk, lambda_init):
    qb = q_ref[0]
    lane = _lane(qb.shape)
    q_maps = []
    for a in range(4):
        msk = (lane >= 32 * a) & (lane < 32 * (a + 1))
        q_maps.append(jnp.where(msk, qb, jnp.zeros_like(qb)))
    n = _flash(q_maps, (0, 0, 1, 1), k_ref, v_ref, m_scr, acc_scr, gap_scr, n_lat=n_lat, tk=tk)
    lp = lam_ref[...]
    lam = (jnp.exp(jnp.sum(lp[0:1] * lp[1:2], axis=-1, keepdims=True))
           - jnp.exp(jnp.sum(lp[2:3] * lp[3:4], axis=-1, keepdims=True)) + lambda_init)
    ys = []
    for hh in range(2):
        o = n[2 * hh] - lam * n[2 * hh + 1]
        ms = jnp.mean(o * o, axis=0, keepdims=True)
        ys.append((o * lax.rsqrt(ms + NORM_EPS) * g_ref[...] * (1.0 - lambda_init)).T)
    o_ref[0] = jnp.concatenate(ys, axis=1).astype(BF16)


def _diff(lam_p, subln_g2, q, k, v, lambda_init, tk):
    return _attention_calls(
        functools.partial(_diff_attn, lambda_init=lambda_init), "diff_attn", 4, q, k, v, LANES, LANES,
        tk, DIFF_TQ, extra_in=(lam_p, subln_g2),
        extra_specs=(pl.BlockSpec(lam_p.shape, lambda b, p, i: (0, 0)),
                     pl.BlockSpec((64, 1), lambda b, p, i: (0, 0))))


def _mla_attn(q_ref, k_ref, v_ref, o_ref, m_scr, acc_scr, gap_scr, *, n_lat, tk):
    q_maps = [q_ref[0, :, 0:256], q_ref[0, :, 256:512]]
    n = _flash(q_maps, (0, 1), k_ref, v_ref, m_scr, acc_scr, gap_scr, n_lat=n_lat, tk=tk)
    o_ref[0] = jnp.concatenate([n[0].T, n[1].T], axis=1).astype(BF16)


def _mla(q, k, v, tk):
    return _attention_calls(_mla_attn, "mla_attn", 2, q, k, v, 512, 256, tk, MLA_TQ)


def _ret_tables(dl, backward):
    L = RET_CHUNK
    lg = -(jnp.maximum(-dl, 0.0) + jnp.log(1.0 + jnp.exp(-jnp.abs(dl))))
    row = lax.broadcasted_iota(jnp.int32, (L, LANES), 0).astype(F32)
    col = lax.broadcasted_iota(jnp.int32, (L, LANES), 1)
    colf = col.astype(F32)
    if backward:
        qd, kd, rel = jnp.exp((L - row) * lg), jnp.exp(row * lg), colf - row
    else:
        qd, kd, rel = jnp.exp((row + 1.0) * lg), jnp.exp((L - 1.0 - row) * lg), row - colf
    decs = []
    for hh in range(2):
        lg_h = jnp.sum(jnp.where(col[0:1] == 64 * hh, lg, 0.0), axis=-1, keepdims=True)
        decs.append(jnp.where(rel >= 0, jnp.exp(jnp.maximum(rel, 0.0) * lg_h), 0.0))
    return qd, kd, decs[0], decs[1], jnp.exp(float(L) * lg)


def _ret_chunks(chains):
    col = _lane((RET_CHUNK, LANES))
    lo = col < 64
    rowi = lax.broadcasted_iota(jnp.int32, (LANES, LANES), 0)
    coli = lax.broadcasted_iota(jnp.int32, (LANES, LANES), 1)
    diag = (rowi < 64) == (coli < 64)
    inter = [_dot((q.astype(F32) * tab[0]).astype(BF16), s_ref[...].astype(BF16))
             for q, k, v, tab, cd, s_ref, store in chains]
    inc = [_dot_tn((k.astype(F32) * tab[1]).astype(BF16), v) for q, k, v, tab, cd, s_ref, store in chains]
    att = [[_dot_nt(jnp.where(lo == (hh == 0), q, jnp.zeros_like(q)), k) * tab[2 + hh] for hh in range(2)]
           for q, k, v, tab, cd, s_ref, store in chains]
    intra = [[_dot(a.astype(BF16), c[2]) for a in pair] for pair, c in zip(att, chains)]
    for c, o_inter, o_intra, s_inc in zip(chains, inter, intra, inc):
        c[6](o_inter + jnp.where(lo, o_intra[0], o_intra[1]))
        c[5][...] = jnp.where(diag, c[5][...] * c[4] + s_inc, 0.0)


def _ret_kernel(dl_ref, qf_ref, kf_ref, vf_ref, qb_ref, kb_ref, vb_ref, of_ref, ob_ref,
                s_scr, tab_scr, cd_scr):
    nb = qf_ref.shape[0]

    @pl.when(pl.program_id(0) == 0)
    def _():
        s_scr[...] = jnp.zeros(s_scr.shape, F32)
        for d in range(2):
            for p in range(2):
                tabs = _ret_tables(dl_ref[d, p], backward=(d == 1))
                for t in range(4):
                    tab_scr[d, p, t] = tabs[t]
                cd_scr[d, p] = tabs[4]

    chains = []
    for d, (q_ref, k_ref, v_ref, o_ref) in enumerate(((qf_ref, kf_ref, vf_ref, of_ref),
                                                      (qb_ref, kb_ref, vb_ref, ob_ref))):
        for p in range(2):
            tab = [tab_scr[d, p, t] for t in range(4)]
            cd = cd_scr[d, p]
            ln = slice(p * LANES, (p + 1) * LANES)
            for b in range(nb):
                def store(o, o_ref=o_ref, b=b, ln=ln):
                    o_ref[b, :, ln] = o
                chains.append((q_ref[b, :, ln], k_ref[b, :, ln], v_ref[b, :, ln], tab, cd,
                               s_scr.at[d, b, p], store))
    _ret_chunks(chains)


def _retention(dl_lane, q, k, v):
    B, S, _ = q.shape
    L = RET_CHUNK
    nc = S // L
    ncc = CTX_LEN // L
    fwd = pl.BlockSpec((B, L, 256), lambda s: (0, s, 0))
    bwd = pl.BlockSpec((B, L, 256), lambda s: (0, jnp.where(s < ncc, ncc - 1 - s, nc - 1 + ncc - s), 0))
    return pl.pallas_call(
        _ret_kernel,
        grid=(nc,),
        in_specs=[pl.BlockSpec(dl_lane.shape, lambda s: (0, 0, 0, 0)), fwd, fwd, fwd, bwd, bwd, bwd],
        out_specs=[fwd, bwd],
        out_shape=[jax.ShapeDtypeStruct((B, S, 256), F32)] * 2,
        scratch_shapes=[pltpu.VMEM((2, B, 2, LANES, LANES), F32),
                        pltpu.VMEM((2, 2, 4, L, LANES), F32), pltpu.VMEM((2, 2, 1, LANES), F32)],
        compiler_params=_cparams(("arbitrary",)),
        name="retention",
    )(dl_lane, q, k, v, q, k, v)


def _gated_group_norm(o, g, gn):
    lo = _lane(o.shape) < 64

    def halves(t):
        a = jnp.sum(jnp.where(lo, t, 0.0), axis=-1, keepdims=True) * (1.0 / 64)
        b = jnp.sum(jnp.where(lo, 0.0, t), axis=-1, keepdims=True) * (1.0 / 64)
        return jnp.where(lo, a, b)

    d = o - halves(o)
    var = halves(d * d)
    return d * lax.rsqrt(var + NORM_EPS) * gn * (g * jax.nn.sigmoid(g))


def _outproj_kernel(x_ref, ya_ref, yb_ref, of_ref, ob_ref, rg_ref, gn_ref, ym_ref, w_ref, g1_ref,
                    sh_ref, sc_ref, ng_ref, wr_ref, br_ref, xo_ref, h_ref, rt_ref, cnt_ref, carry):
    mix = (_dot(ya_ref[0], w_ref[0:256]) + _dot(yb_ref[0], w_ref[256:512])
           + _dot(ym_ref[0], w_ref[768:1024]))
    for j in range(2):
        ln = slice(j * LANES, (j + 1) * LANES)
        yr = _gated_group_norm(of_ref[0, :, ln] + ob_ref[0, :, ln], rg_ref[0, :, ln].astype(F32),
                               gn_ref[:, ln])
        mix = mix + _dot(yr.astype(BF16), w_ref[512 + j * LANES:512 + (j + 1) * LANES])
    x = x_ref[0] + g1_ref[...] * mix
    xo_ref[0] = x
    h = _rms(x, ng_ref[...]) * (1.0 + sc_ref[...]) + sh_ref[...]
    h_ref[0] = h
    h_hi = h.astype(BF16)
    h_lo = (h - h_hi.astype(F32)).astype(BF16)
    logits = (_dot(h_hi, wr_ref[0]) + _dot(h_lo, wr_ref[0]) + _dot(h_hi, wr_ref[1])
              + br_ref[...])
    lane = _lane(logits.shape).astype(F32)

    def first_argmax(vals, mask):
        mx = jnp.max(jnp.where(mask, vals, NEG_INF), axis=-1, keepdims=True)
        idx = jnp.min(jnp.where(mask & (vals == mx), lane, float(LANES)), axis=-1, keepdims=True)
        return mx, idx

    gmask = lane < MOE_GROUPS
    g_max, g_idx = first_argmax(logits, gmask)
    g_w = 1.0 / jnp.sum(jnp.exp(jnp.where(gmask, logits - g_max, NEG_INF)), axis=-1, keepdims=True)
    e_lo = MOE_GROUPS + g_idx * MOE_PER_GROUP
    emask = (lane >= e_lo) & (lane < e_lo + MOE_PER_GROUP)
    v1, i1 = first_argmax(logits, emask)
    v2, i2 = first_argmax(logits, emask & (lane != i1))
    t = jnp.exp(v2 - v1)
    w1 = g_w / (1.0 + t)
    w2 = w1 * t
    e1, e2 = i1 - MOE_GROUPS, i2 - MOE_GROUPS

    @pl.when((pl.program_id(0) == 0) & (pl.program_id(1) == 0))
    def _():
        carry[...] = jnp.zeros(carry.shape, F32)

    oh1 = jnp.where(lane == e1, 1.0, 0.0)
    oh2 = jnp.where(lane == e2, 1.0, 0.0)
    cnt = oh1 + oh2
    tm = cnt.shape[0]
    tri = jnp.where(lax.broadcasted_iota(jnp.int32, (tm, tm), 0)
                    > lax.broadcasted_iota(jnp.int32, (tm, tm), 1), 1.0, 0.0).astype(BF16)
    before = _dot(tri, cnt.astype(BF16)) + carry[...]
    r1 = jnp.sum(oh1 * before, axis=-1, keepdims=True)
    r2 = jnp.sum(oh2 * before, axis=-1, keepdims=True)
    carry[...] = carry[...] + jnp.sum(cnt, axis=0, keepdims=True)
    cnt_ref[...] = carry[...]
    vals = (e1, e2, w1, w2, r1, r2)
    rt = jnp.zeros(logits.shape, F32)
    for j, val in enumerate(vals):
        rt = jnp.where(lane == j, val, rt)
    rt_ref[0] = rt


def _outproj(x, ya, yb, o_f, o_b, rg, gn_g, ym, w_out_p, mod, nbatch, norm_g, w_route, b_route):
    B, S, D = x.shape
    nt = S // TM
    tok = lambda w: pl.BlockSpec((1, TM, w), lambda b, i: (b, i, 0))
    full = lambda a: pl.BlockSpec(a.shape, lambda b, i: (0,) * a.ndim)
    return pl.pallas_call(
        _outproj_kernel,
        grid=(B, nt),
        in_specs=[tok(D), tok(256), tok(256), tok(256), tok(256), tok(256), full(gn_g), tok(256),
                  full(w_out_p),
                  _mod_spec(2, nbatch), _mod_spec(3, nbatch), _mod_spec(4, nbatch), full(norm_g),
                  full(w_route), full(b_route)],
        out_specs=[tok(D), tok(D), tok(LANES), pl.BlockSpec((1, LANES), lambda b, i: (0, 0))],
        out_shape=[jax.ShapeDtypeStruct((B, S, D), F32), jax.ShapeDtypeStruct((B, S, D), F32),
                   jax.ShapeDtypeStruct((B, S, LANES), F32), jax.ShapeDtypeStruct((1, LANES), F32)],
        scratch_shapes=[pltpu.VMEM((1, LANES), F32)],
        compiler_params=_cparams(("arbitrary", "arbitrary")),
        name="outproj_router",
    )(x, ya, yb, o_f, o_b, rg, gn_g, ym, w_out_p, mod, mod, mod, norm_g, w_route, b_route)


def _dispatch_kernel(dest_ref, h_ref, xb_in_ref, xb_ref, sem):
    del xb_in_ref
    base = pl.program_id(0) * TM

    def copy(r, k):
        d = dest_ref[(base + r) * 2 + k]
        return pltpu.make_async_copy(h_ref.at[pl.ds(r, 1)], xb_ref.at[pl.ds(d, 1)], sem)

    def issue(r, c):
        copy(r, 0).start()
        copy(r, 1).start()
        return c

    def drain(r, c):
        copy(r, 0).wait()
        copy(r, 1).wait()
        return c

    lax.fori_loop(0, TM, issue, 0, unroll=DMA_UNROLL)
    lax.fori_loop(0, TM, drain, 0, unroll=DMA_UNROLL)


def _dispatch(dest, h, slots):
    n, d = h.shape
    cap = slots.shape[0]
    return pl.pallas_call(
        _dispatch_kernel,
        grid_spec=pltpu.PrefetchScalarGridSpec(
            num_scalar_prefetch=1,
            grid=(n // TM,),
            in_specs=[pl.BlockSpec((TM, d), lambda i, dest: (i, 0)),
                      pl.BlockSpec(memory_space=pl.ANY)],
            out_specs=pl.BlockSpec(memory_space=pl.ANY),
            scratch_shapes=[pltpu.SemaphoreType.DMA(())]),
        out_shape=jax.ShapeDtypeStruct((cap, d), h.dtype),
        input_output_aliases={2: 0},
        compiler_params=_cparams(("arbitrary",)),
        name="moe_dispatch",
    )(dest, h, slots)


def _expert_kernel(be_ref, nu_ref, x_ref, wg_ref, wu_ref, wd_ref, y_ref, wg_s, wu_s, wd_s):
    i = pl.program_id(0)

    @pl.when((i == 0) | (be_ref[i] != be_ref[jnp.maximum(i - 1, 0)]))
    def _():
        wg_s[...] = wg_ref[0].astype(BF16)
        wu_s[...] = wu_ref[0].astype(BF16)
        wd_s[...] = wd_ref[0].astype(BF16)

    @pl.when(i < nu_ref[0])
    def _():
        xb = x_ref[...].astype(BF16)
        gate = _dot(xb, wg_s[...])
        up = _dot(xb, wu_s[...])
        hid = (gate * jax.nn.sigmoid(gate) * up).astype(BF16)
        y_ref[...] = _dot(hid, wd_s[...])

    @pl.when(i >= nu_ref[0])
    def _():
        y_ref[...] = jnp.zeros(y_ref.shape, y_ref.dtype)


def _experts(block_e, n_used, xb, w_gate, w_up, w_down, layer):
    cap, d = xb.shape
    hdim = w_gate.shape[-1]
    return pl.pallas_call(
        _expert_kernel,
        grid_spec=pltpu.PrefetchScalarGridSpec(
            num_scalar_prefetch=2,
            grid=(cap // MOE_ROWS,),
            in_specs=[pl.BlockSpec((MOE_ROWS, d), lambda i, be, nu: (i, 0)),
                      pl.BlockSpec((None, 1, d, hdim), lambda i, be, nu: (layer, be[i], 0, 0)),
                      pl.BlockSpec((None, 1, d, hdim), lambda i, be, nu: (layer, be[i], 0, 0)),
                      pl.BlockSpec((None, 1, hdim, d), lambda i, be, nu: (layer, be[i], 0, 0))],
            out_specs=pl.BlockSpec((MOE_ROWS, d), lambda i, be, nu: (i, 0)),
            scratch_shapes=[pltpu.VMEM((d, hdim), BF16), pltpu.VMEM((d, hdim), BF16),
                            pltpu.VMEM((hdim, d), BF16)]),
        out_shape=jax.ShapeDtypeStruct((cap, d), F32),
        compiler_params=_cparams(("arbitrary",)),
        name="moe_experts",
    )(block_e, n_used, xb, w_gate, w_up, w_down)


def _combine_kernel(dest_ref, x_ref, rt_ref, g2_ref, fg_ref, yb_ref, o_ref, gath, sem, *,
                    tile_off, tiles_per_batch, final):
    b, i = pl.program_id(0), pl.program_id(1)
    base = (b * tiles_per_batch + i + tile_off) * TM

    def copy(r, k):
        d = dest_ref[(base + r) * 2 + k]
        return pltpu.make_async_copy(yb_ref.at[pl.ds(d, 1)], gath.at[k, pl.ds(r, 1)], sem)

    def issue(r, c):
        copy(r, 0).start()
        copy(r, 1).start()
        return c

    def drain(r, c):
        copy(r, 0).wait()
        copy(r, 1).wait()
        return c

    lax.fori_loop(0, TM, issue, 0, unroll=DMA_UNROLL)
    lax.fori_loop(0, TM, drain, 0, unroll=DMA_UNROLL)
    rt = rt_ref[0]
    m = rt[:, 2:3] * gath[0] + rt[:, 3:4] * gath[1]
    x = x_ref[0] + g2_ref[...] * m
    if final:
        x = _rms(x, fg_ref[...])
    o_ref[0] = x


def _combine(dest, x, route, mod, nbatch, final_g, yb, final):
    B, S, D = x.shape
    nt = S // TM
    off = 1 if final else 0
    g2_spec = pl.BlockSpec((None, 1, D_MODEL),
                           lambda b, i, dest: (jnp.where(i + off == 0, nbatch, b), 0, 5))
    return pl.pallas_call(
        functools.partial(_combine_kernel, tile_off=off, tiles_per_batch=nt, final=final),
        grid_spec=pltpu.PrefetchScalarGridSpec(
            num_scalar_prefetch=1,
            grid=(B, nt - off),
            in_specs=[pl.BlockSpec((1, TM, D), lambda b, i, dest: (b, i + off, 0)),
                      pl.BlockSpec((1, TM, LANES), lambda b, i, dest: (b, i + off, 0)),
                      g2_spec,
                      pl.BlockSpec((1, D), lambda b, i, dest: (0, 0)),
                      pl.BlockSpec(memory_space=pl.ANY)],
            out_specs=pl.BlockSpec((1, TM, D), lambda b, i, dest: (b, i, 0)),
            scratch_shapes=[pltpu.VMEM((2, TM, D), F32), pltpu.SemaphoreType.DMA(())]),
        out_shape=jax.ShapeDtypeStruct((B, S - off * TM, D), F32),
        compiler_params=_cparams(("arbitrary", "arbitrary")),
        name="moe_combine",
    )(dest, x, route, mod, final_g, yb)


def _rope_tables(seq):
    n_rows = seq // GRID_W
    pos = jnp.arange(max(n_rows, GRID_W), dtype=F32)[:, None]
    lane = jnp.arange(LANES, dtype=jnp.int32)
    tabs = []
    for r in (64, 32):
        nf = r // 4
        inv = 1.0 / (ROPE_BASE ** (jnp.arange(nf, dtype=F32) / nf))
        ang = pos * inv[lane % nf][None, :]
        first = ((lane % (2 * nf)) < nf)[None, :]
        by_row = ((lane % r) < 2 * nf)[None, None, :]
        for small in (jnp.cos(ang), jnp.where(first, -jnp.sin(ang), jnp.sin(ang))):
            full = jnp.where(by_row, small[:n_rows, None, :], small[None, :GRID_W, :])
            tabs.append(full.reshape(seq, LANES))
    ident = (jnp.ones((CTX_LEN, LANES), F32), jnp.zeros((CTX_LEN, LANES), F32))
    return [jnp.concatenate([ident[j % 2], tab], axis=0) for j, tab in enumerate(tabs)]


def _layer_weights(w_in, w_out, w_uq, w_ukv, w_group, b_group, w_expert, b_expert):
    perm = jnp.array([0, 2, 1, 3])
    wq = w_in[:, :256].reshape(D_MODEL, 4, 64)[:, perm].reshape(D_MODEL, 256)
    w_in_p = jnp.concatenate([wq, w_in[:, 256:], jnp.zeros((D_MODEL, D_IN_PAD - w_in.shape[1]), F32)],
                             axis=1).astype(BF16)
    wo = jnp.concatenate([w_out[:256].reshape(4, 64, D_MODEL)[perm].reshape(256, D_MODEL), w_out[256:]],
                         axis=0).astype(BF16)
    uq = w_uq.reshape(256, 4, 96)
    blocks = []
    for h in range(4):
        blk = jnp.zeros((256, 256), F32)
        blk = blk.at[:, 64 * (h % 2):64 * (h % 2) + 64].set(uq[:, h, :64])
        blk = blk.at[:, LANES:LANES + 32].set(uq[:, h, 64:])
        blocks.append(blk)
    wuq = jnp.concatenate(blocks, axis=1).astype(BF16)
    ukv = w_ukv.reshape(128, 4, 128)
    wukk = ukv[:, :, :64].reshape(128, 256).astype(BF16)
    wukv = ukv[:, :, 64:].reshape(128, 256).astype(BF16)
    pad = LANES - MOE_GROUPS - MOE_EXPERTS
    w_route = jnp.concatenate([w_group, w_expert, jnp.zeros((D_MODEL, pad), F32)], axis=1)
    w_hi = w_route.astype(BF16)
    w_route = jnp.stack([w_hi, (w_route - w_hi.astype(F32)).astype(BF16)])
    b_route = jnp.concatenate([b_group, b_expert, jnp.zeros((pad,), F32)]).reshape(1, LANES)
    return w_in_p, wo, wuq, wukk, wukv, w_route, b_route


def _moe_plan(route, counts, cap):
    e = route[:, :2].astype(jnp.int32)
    rank = route[:, 4:6].astype(jnp.int32)
    cnt = counts[0, :MOE_EXPERTS].astype(jnp.int32)
    padded = (cnt + MOE_ROWS - 1) // MOE_ROWS * MOE_ROWS
    pend = jnp.cumsum(padded)
    pstart = pend - padded
    dest = (pstart[e] + rank).reshape(-1)
    nblk = cap // MOE_ROWS
    blk_start = jnp.arange(nblk, dtype=jnp.int32) * MOE_ROWS
    block_e = jnp.minimum(jnp.sum((pend[None, :] <= blk_start[:, None]).astype(jnp.int32), axis=1),
                          MOE_EXPERTS - 1)
    n_used = (pend[-1:] // MOE_ROWS).astype(jnp.int32)
    return dest, block_e, n_used


def kernel(x, c, ctx, c_ctx, w_mod, b_mod, norm1_g, norm2_g, w_in, w_out, swa_sink, diff_lambda, diff_subln_g, ret_decay_logit, ret_gn_g, mla_q_norm_g, mla_w_uq, mla_kv_norm_g, mla_w_ukv, moe_w_group, moe_b_group, moe_w_expert, moe_b_expert, moe_w_gate, moe_w_up, moe_w_down, final_norm_g):
    B, T, D = x.shape
    S = CTX_LEN + T
    n_tok = B * S
    diff_tk, mla_tk = min(DIFF_TK, T), min(MLA_TK, T)
    cap = -(-(2 * n_tok + MOE_EXPERTS * (MOE_ROWS - 1)) // MOE_ROWS) * MOE_ROWS
    xs = jnp.concatenate([ctx, x], axis=1)
    cvec = jnp.zeros((8, D), F32).at[:B].set(c).at[B].set(c_ctx)
    tabs = _rope_tables(T)
    row = lambda v: v.reshape(1, -1)

    for l in range(DEPTH):
        lambda_init = 0.8 - 0.6 * math.exp(-0.3 * l)
        w_in_p, wo, wuq, wukk, wukv, w_route, b_route = _layer_weights(
            w_in[l], w_out[l], mla_w_uq[l], mla_w_ukv[l], moe_w_group[l], moe_b_group[l],
            moe_w_expert[l], moe_b_expert[l])
        mod = _modulation(cvec, w_mod[l], b_mod[l]).reshape(8, 1, 6 * D)
        (swaq, swak, swav, dq, dk, dv, rq, rk, rv, rg, mq, mk, mv) = _inproj(
            xs, mod, B, row(norm1_g[l]), w_in_p, tabs, row(mla_q_norm_g[l]), wuq,
            row(mla_kv_norm_g[l]), wukk, wukv)
        sink_p = swa_sink[l]
        ya = _swa(sink_p, swaq, swak, swav)
        yb = _diff(diff_lambda[l], diff_subln_g[l].reshape(64, 1), dq, dk, dv, lambda_init,
                   diff_tk)
        dl_lane = jnp.repeat(ret_decay_logit[l], 64, axis=-1).reshape(2, 2, 1, LANES)
        o_f, o_b = _retention(dl_lane, rq, rk, rv)
        ym = _mla(mq, mk, mv, mla_tk)
        xs, h2, route, counts = _outproj(xs, ya, yb, o_f, o_b, rg, row(ret_gn_g[l]), ym, wo, mod, B,
                                         row(norm2_g[l]), w_route, b_route)
        dest, block_e, n_used = _moe_plan(route.reshape(n_tok, LANES), counts, cap)
        xb = _dispatch(dest, h2.reshape(n_tok, D), jnp.zeros((cap, D), F32) if l == 0 else xb)
        yb_e = _experts(block_e, n_used, xb, moe_w_gate, moe_w_up, moe_w_down, l)
        final = l == DEPTH - 1
        xs = _combine(dest, xs, route, mod, B, row(final_norm_g), yb_e, final)
    return xs
```

```python
import functools
import math

import jax
import jax.numpy as jnp
from jax import lax
from jax.experimental import pallas as pl
from jax.experimental.pallas import tpu as pltpu

F32 = jnp.float32
BF16 = jnp.bfloat16

D_MODEL = 1024
CTX_LEN = 256
GRID_W = 64
ROPE_BASE = 10000.0
NORM_EPS = 1e-6
NEG_INF = -1e30
DEPTH = 2

SWA_WINDOW = 128
RET_CHUNK = 128
MLA_SCALE = (64 + 32) ** -0.5
LOG2E = math.log2(math.e)
MOE_GROUPS = 4
MOE_PER_GROUP = 8
MOE_EXPERTS = 32
MOE_HIDDEN = 512

LANES = 128
TM = 256
DIFF_TQ = 2048
MLA_TQ = 2048
DIFF_TK = 1024
MLA_TK = 2048
VT_ROWS = 80
GAP_LIMIT = 64.0
MOE_ROWS = 512
DMA_UNROLL = 8
VMEM_LIMIT = 56 * 1024 * 1024

C_SWA_Q, C_SWA_K, C_SWA_V = 0, 256, 384
C_DIF_Q, C_DIF_K, C_DIF_V = 512, 768, 1024
C_RET_Q, C_RET_K, C_RET_V, C_RET_G = 1280, 1536, 1792, 2048
C_MLA_Q, C_MLA_KV, C_MLA_KR = 2304, 2560, 2688
D_IN_PAD = 2816


def _cparams(sem):
    return pltpu.CompilerParams(dimension_semantics=sem, vmem_limit_bytes=VMEM_LIMIT)


def _dot(a, b):
    return jnp.dot(a, b, preferred_element_type=F32)


def _dot_nt(a, b):
    return lax.dot_general(a, b, (((1,), (1,)), ((), ())), preferred_element_type=F32)


def _dot_tn(a, b):
    return lax.dot_general(a, b, (((0,), (0,)), ((), ())), preferred_element_type=F32)


def _rms(x, g):
    ms = jnp.mean(x * x, axis=-1, keepdims=True)
    return x * lax.rsqrt(ms + NORM_EPS) * g


def _lane(shape):
    return lax.broadcasted_iota(jnp.int32, shape, len(shape) - 1)


def _rope(z, cos, sin, nf):
    first = (_lane(z.shape) % (2 * nf)) < nf
    partner = jnp.where(first, pltpu.roll(z, LANES - nf, 1), pltpu.roll(z, nf, 1))
    return z * cos + partner * sin


def _mod_kernel(a_ref, w_ref, b_ref, o_ref):
    a = a_ref[...]
    act = a * jax.nn.sigmoid(a)
    o_ref[...] = jnp.dot(act, w_ref[...], preferred_element_type=F32,
                         precision=lax.Precision.HIGHEST) + b_ref[...]


def _modulation(cvec, w_mod, b_mod):
    n = w_mod.shape[1]
    bn = 512
    return pl.pallas_call(
        _mod_kernel,
        grid=(n // bn,),
        in_specs=[pl.BlockSpec((8, D_MODEL), lambda j: (0, 0)),
                  pl.BlockSpec((D_MODEL, bn), lambda j: (0, j)),
                  pl.BlockSpec((1, bn), lambda j: (0, j))],
        out_specs=pl.BlockSpec((8, bn), lambda j: (0, j)),
        out_shape=jax.ShapeDtypeStruct((8, n), F32),
        compiler_params=_cparams(("parallel",)),
        name="modulation",
    )(cvec, w_mod, b_mod.reshape(1, n))


def _mod_spec(chunk, nbatch):
    return pl.BlockSpec((None, 1, D_MODEL),
                        lambda b, i: (jnp.where(i == 0, nbatch, b), 0, chunk))


def _inproj_kernel(x_ref, g_ref, sh_ref, sc_ref, w_ref, c64_ref, s64_ref, c32_ref, s32_ref,
                   qng_ref, wuq_ref, kvng_ref, wukk_ref, wukv_ref,
                   swaq, swak, swav, dq, dk, dv, rq, rk, rv, rg, mq, mk, mv):
    x = x_ref[0]
    h = _rms(x, g_ref[...]) * (1.0 + sc_ref[...]) + sh_ref[...]
    hb = h.astype(BF16)
    c64, s64, c32, s32 = c64_ref[...], s64_ref[...], c32_ref[...], s32_ref[...]

    def proj(c0, width):
        return _dot(hb, w_ref[:, c0:c0 + width])

    def rope_blocks(z, out, nf, scale):
        cos, sin = (c64, s64) if nf == 16 else (c32, s32)
        for j in range(z.shape[1] // LANES):
            zz = _rope(z[:, j * LANES:(j + 1) * LANES], cos, sin, nf)
            if scale != 1.0:
                zz = zz * scale
            out[0, :, j * LANES:(j + 1) * LANES] = zz.astype(out.dtype)

    rope_blocks(proj(C_SWA_Q, 256), swaq, 16, 0.125)
    rope_blocks(proj(C_SWA_K, 128), swak, 16, 1.0)
    swav[0] = proj(C_SWA_V, 128).astype(BF16)
    def values_transposed(z, out):
        zt = z.T.astype(BF16)
        for hh in range(4):
            out[0, hh * VT_ROWS:hh * VT_ROWS + 64, :] = zt[hh * 64:(hh + 1) * 64]
            out[0, hh * VT_ROWS + 64:(hh + 1) * VT_ROWS, :] = jnp.ones((VT_ROWS - 64, zt.shape[1]), BF16)

    rope_blocks(proj(C_DIF_Q, 256), dq, 8, 32 ** -0.5 * LOG2E)
    rope_blocks(proj(C_DIF_K, 256), dk, 8, 1.0)
    values_transposed(proj(C_DIF_V, 256), dv)
    rope_blocks(proj(C_RET_Q, 256), rq, 16, 0.125)
    rope_blocks(proj(C_RET_K, 256), rk, 16, 1.0)
    rv[0] = proj(C_RET_V, 256).astype(BF16)
    rg[0] = proj(C_RET_G, 256).astype(BF16)

    ql = _rms(proj(C_MLA_Q, 256), qng_ref[...]).astype(BF16)
    qa = _dot(ql, wuq_ref[...]) * (MLA_SCALE * LOG2E)
    for hh in range(4):
        mq[0, :, hh * 256:hh * 256 + LANES] = qa[:, hh * 256:hh * 256 + LANES].astype(BF16)
        zr = _rope(qa[:, hh * 256 + LANES:(hh + 1) * 256], c32, s32, 8)
        mq[0, :, hh * 256 + LANES:(hh + 1) * 256] = zr.astype(BF16)
    kvl = _rms(proj(C_MLA_KV, 128), kvng_ref[...]).astype(BF16)
    kn = _dot(kvl, wukk_ref[...])
    kr = _rope(proj(C_MLA_KR, 128), c32, s32, 8).astype(BF16)
    for p in range(2):
        mk[0, :, p * 256:p * 256 + LANES] = kn[:, p * LANES:(p + 1) * LANES].astype(BF16)
        mk[0, :, p * 256 + LANES:(p + 1) * 256] = kr
    values_transposed(_dot(kvl, wukv_ref[...]), mv)


def _inproj(x, mod, nbatch, norm_g, w_in_p, tabs, qng, wuq, kvng, wukk, wukv):
    B, S, D = x.shape
    nt = S // TM
    tok = lambda w: pl.BlockSpec((1, TM, w), lambda b, i: (b, i, 0))
    full = lambda a: pl.BlockSpec(a.shape, lambda b, i: (0,) * a.ndim)
    tab = pl.BlockSpec((TM, LANES), lambda b, i: (i, 0))
    widths = [256, 128, 128, 256, 256, None, 256, 256, 256, 256, 1024, 512, None]
    vt_spec = pl.BlockSpec((1, 4 * VT_ROWS, TM), lambda b, i: (b, 0, i))
    vt_shape = jax.ShapeDtypeStruct((B, 4 * VT_ROWS, S), BF16)
    return pl.pallas_call(
        _inproj_kernel,
        grid=(B, nt),
        in_specs=[tok(D), full(norm_g), _mod_spec(0, nbatch), _mod_spec(1, nbatch), full(w_in_p),
                  tab, tab, tab, tab, full(qng), full(wuq), full(kvng), full(wukk), full(wukv)],
        out_specs=[vt_spec if w is None else tok(w) for w in widths],
        out_shape=[vt_shape if w is None else jax.ShapeDtypeStruct((B, S, w), BF16) for w in widths],
        compiler_params=_cparams(("parallel", "parallel")),
        name="inproj",
    )(x, norm_g, mod, mod, w_in_p, *tabs, qng, wuq, kvng, wukk, wukv)


def _swa_kernel(sink_ref, q_ref, kp_ref, ko_ref, kn_ref, vp_ref, vo_ref, vn_ref, kc_ref, vc_ref,
                o_ref, *, n_ctx_tiles, seq):
    i = pl.program_id(1)
    blk = RET_CHUNK
    kb = [kp_ref[0], ko_ref[0, 0:blk], ko_ref[0, blk:2 * blk], kn_ref[0]]
    vb = [vp_ref[0], vo_ref[0, 0:blk], vo_ref[0, blk:2 * blk], vn_ref[0]]
    nk = 3 * blk + CTX_LEN
    r_io = lax.broadcasted_iota(jnp.int32, (blk, nk), 0)
    c_io = lax.broadcasted_iota(jnp.int32, (blk, nk), 1)
    lane = _lane((blk, LANES))
    chains = [(s, r, g) for s in range(2) for r in range(2) for g in range(2)]
    k_all, v_all, valid = [], [], []
    for s in range(2):
        n = 2 * (i - n_ctx_tiles) + s
        kpos = (n - 1) * blk + c_io
        in_band = ((c_io >= r_io) & (c_io <= r_io + 2 * SWA_WINDOW) & (kpos >= 0) & (kpos < seq)
                   & (n >= 0))
        valid.append(in_band | (c_io >= 3 * blk))
        k_all.append(jnp.concatenate(kb[s:s + 3] + [kc_ref[0]], axis=0))
        v_all.append(jnp.concatenate(vb[s:s + 3] + [vc_ref[0]], axis=0))
    qm = [jnp.where((lane >= 64 * g) & (lane < 64 * (g + 1)),
                    q_ref[0, s * blk:(s + 1) * blk, r * LANES:(r + 1) * LANES],
                    jnp.zeros((blk, LANES), BF16)) for s, r, g in chains]
    sc = [jnp.where(valid[c[0]], _dot_nt(q, k_all[c[0]]), NEG_INF) for q, c in zip(qm, chains)]
    sink = [sink_ref[2 * g + r] for s, r, g in chains]
    m = [jnp.maximum(jnp.max(x, axis=-1, keepdims=True), sk) for x, sk in zip(sc, sink)]
    p = [jnp.exp(x - mm) for x, mm in zip(sc, m)]
    l = [jnp.sum(pp, axis=-1, keepdims=True) + jnp.exp(sk - mm) for pp, sk, mm in zip(p, sink, m)]
    o = [_dot(pp.astype(BF16), v_all[c[0]]) / ll for pp, ll, c in zip(p, l, chains)]
    for s in range(2):
        for r in range(2):
            j = 4 * s + 2 * r
            o_ref[0, s * blk:(s + 1) * blk, r * LANES:(r + 1) * LANES] = (
                jnp.where(lane < 64, o[j], o[j + 1]).astype(BF16))


def _swa(sink, q, k, v):
    B, S, _ = q.shape
    blk = RET_CHUNK
    nb = S // blk
    nt = S // (2 * blk)
    edge = lambda f: pl.BlockSpec((1, blk, LANES), f)
    prev = lambda b, i: (b, jnp.maximum(2 * i - 1, 0), 0)
    own = lambda b, i: (b, i, 0)
    nxt = lambda b, i: (b, jnp.minimum(2 * i + 2, nb - 1), 0)
    own_kv = pl.BlockSpec((1, 2 * blk, LANES), own)
    ctx = pl.BlockSpec((1, CTX_LEN, LANES), lambda b, i: (b, 0, 0))
    return pl.pallas_call(
        functools.partial(_swa_kernel, n_ctx_tiles=CTX_LEN // (2 * blk), seq=S - CTX_LEN),
        grid=(B, nt),
        in_specs=[pl.BlockSpec(memory_space=pltpu.SMEM),
                  pl.BlockSpec((1, 2 * blk, 256), own),
                  edge(prev), own_kv, edge(nxt), edge(prev), own_kv, edge(nxt), ctx, ctx],
        out_specs=pl.BlockSpec((1, 2 * blk, 256), own),
        out_shape=jax.ShapeDtypeStruct((B, S, 256), BF16),
        compiler_params=_cparams(("parallel", "parallel")),
        name="swa",
    )(sink, q, k, k, k, v, v, v, k, v)


def _flash(q_maps, v_blk, k_ref, v_ref, m_scr, acc_scr, gap_scr, *, n_lat, tk):
    nm = len(q_maps)
    qt_maps = [q.astype(F32).T.astype(BF16) for q in q_maps]

    def chunk(start, size, mode):
        kc = k_ref[0, pl.ds(start, size), :]
        for a in range(nm):
            vt = v_ref[0, v_blk[a] * VT_ROWS:(v_blk[a] + 1) * VT_ROWS, pl.ds(start, size)]
            st = _dot(kc, qt_maps[a])
            m_prev = None if mode == "first" else m_scr[a]
            if mode == "fast":
                pf = jnp.exp2(st - m_prev)
                rise = jnp.maximum(jnp.max(pf, axis=0, keepdims=True), 1.0)
                acc_scr[a] = (acc_scr[a] + _dot(vt, pf.astype(BF16))) * (1.0 / rise)
                m_scr[a] = m_prev + jnp.log2(rise)
                gap_scr[a] = jnp.maximum(gap_scr[a], rise)
                continue
            cm = jnp.max(st, axis=0, keepdims=True)
            if mode == "first":
                acc_scr[a] = _dot(vt, jnp.exp2(st - cm).astype(BF16))
                m_scr[a] = cm
                gap_scr[a] = jnp.ones(cm.shape, F32)
            else:
                m_new = jnp.maximum(m_prev, cm)
                p = jnp.exp2(st - m_new).astype(BF16)
                acc_scr[a] = acc_scr[a] * jnp.exp2(m_prev - m_new) + _dot(vt, p)
                m_scr[a] = m_new

    def sweep(mode):
        chunk(0, CTX_LEN, "first")
        if n_lat:

            def body(c, carry):
                chunk(pl.multiple_of(CTX_LEN + c * tk, LANES), tk, mode)
                return carry

            lax.fori_loop(0, n_lat, body, 0)

    sweep("fast")
    if n_lat:
        worst = jnp.max(jnp.concatenate([gap_scr[a] for a in range(nm)], axis=0))

        @pl.when(worst > 2.0 ** GAP_LIMIT)
        def _():
            sweep("exact")

    return [acc_scr[a, 0:64, :] / acc_scr[a, 64:65, :] for a in range(nm)]


def _attention_calls(kern, name, nm, q, k, vt, qw, kw, tk, tq, extra_in=(), extra_specs=()):
    B, S, _ = q.shape
    T = S - CTX_LEN

    def call(tq, n_tiles, n_lat, q_spec, kv_rows, call_name):
        return pl.pallas_call(
            functools.partial(kern, n_lat=n_lat, tk=tk),
            grid=(B, 2, n_tiles),
            in_specs=list(extra_specs) + [
                q_spec,
                pl.BlockSpec((1, kv_rows, kw), lambda b, p, i: (b, 0, p)),
                pl.BlockSpec((1, 2 * VT_ROWS, kv_rows), lambda b, p, i: (b, p, 0))],
            out_specs=pl.BlockSpec((1, tq, LANES), lambda b, p, i: (b, i, p)),
            out_shape=jax.ShapeDtypeStruct((B, tq * n_tiles, 256), BF16),
            scratch_shapes=[pltpu.VMEM((nm, 1, tq), F32), pltpu.VMEM((nm, VT_ROWS, tq), F32),
                            pltpu.VMEM((nm, 1, tq), F32)],
            compiler_params=_cparams(("parallel", "parallel", "parallel")),
            name=call_name,
        )(*extra_in, q, k, vt)

    y_ctx = call(CTX_LEN, 1, 0, pl.BlockSpec((1, CTX_LEN, qw), lambda b, p, i: (b, 0, p)),
                 CTX_LEN, name + "_ctx")
    tq = min(tq, T)
    y_lat = call(tq, T // tq, T // tk,
                 pl.BlockSpec((pl.Element(1), pl.Element(tq), pl.Element(qw)),
                              lambda b, p, i: (b, (i * (tq // CTX_LEN) + 1) * CTX_LEN, p * qw)),
                 S, name)
    return jnp.concatenate([y_ctx, y_lat], axis=1)


def _diff_attn(lam_ref, g_ref, q_ref, k_ref, v_ref, o_ref, m_scr, acc_scr, gap_scr, *,
               n_lat, tk, lambda_init):
    qb = q_ref[0]
    lane = _lane(qb.shape)
    q_maps = []
    for a in range(4):
        msk = (lane >= 32 * a) & (lane < 32 * (a + 1))
        q_maps.append(jnp.where(msk, qb, jnp.zeros_like(qb)))
    n = _flash(q_maps, (0, 0, 1, 1), k_ref, v_ref, m_scr, acc_scr, gap_scr, n_lat=n_lat, tk=tk)
    lp = lam_ref[...]
    lam = (jnp.exp(jnp.sum(lp[0:1] * lp[1:2], axis=-1, keepdims=True))
           - jnp.exp(jnp.sum(lp[2:3] * lp[3:4], axis=-1, keepdims=True)) + lambda_init)
    ys = []
    for hh in range(2):
        o = n[2 * hh] - lam * n[2 * hh + 1]
        ms = jnp.mean(o * o, axis=0, keepdims=True)
        ys.append((o * lax.rsqrt(ms + NORM_EPS) * g_ref[...] * (1.0 - lambda_init)).T)
    o_ref[0] = jnp.concatenate(ys, axis=1).astype(BF16)


def _diff(lam_p, subln_g2, q, k, v, lambda_init, tk):
    return _attention_calls(
        functools.partial(_diff_attn, lambda_init=lambda_init), "diff_attn", 4, q, k, v, LANES, LANES,
        tk, DIFF_TQ, extra_in=(lam_p, subln_g2),
        extra_specs=(pl.BlockSpec(lam_p.shape, lambda b, p, i: (0, 0)),
                     pl.BlockSpec((64, 1), lambda b, p, i: (0, 0))))


def _mla_attn(q_ref, k_ref, v_ref, o_ref, m_scr, acc_scr, gap_scr, *, n_lat, tk):
    q_maps = [q_ref[0, :, 0:256], q_ref[0, :, 256:512]]
    n = _flash(q_maps, (0, 1), k_ref, v_ref, m_scr, acc_scr, gap_scr, n_lat=n_lat, tk=tk)
    o_ref[0] = jnp.concatenate([n[0].T, n[1].T], axis=1).astype(BF16)


def _mla(q, k, v, tk):
    return _attention_calls(_mla_attn, "mla_attn", 2, q, k, v, 512, 256, tk, MLA_TQ)


def _ret_tables(dl, backward):
    L = RET_CHUNK
    lg = -(jnp.maximum(-dl, 0.0) + jnp.log(1.0 + jnp.exp(-jnp.abs(dl))))
    row = lax.broadcasted_iota(jnp.int32, (L, LANES), 0).astype(F32)
    col = lax.broadcasted_iota(jnp.int32, (L, LANES), 1)
    colf = col.astype(F32)
    if backward:
        qd, kd, rel = jnp.exp((L - row) * lg), jnp.exp(row * lg), colf - row
    else:
        qd, kd, rel = jnp.exp((row + 1.0) * lg), jnp.exp((L - 1.0 - row) * lg), row - colf
    decs = []
    for hh in range(2):
        lg_h = jnp.sum(jnp.where(col[0:1] == 64 * hh, lg, 0.0), axis=-1, keepdims=True)
        decs.append(jnp.where(rel >= 0, jnp.exp(jnp.maximum(rel, 0.0) * lg_h), 0.0))
    return qd, kd, decs[0], decs[1], jnp.exp(float(L) * lg)


def _ret_chunks(chains):
    col = _lane((RET_CHUNK, LANES))
    lo = col < 64
    rowi = lax.broadcasted_iota(jnp.int32, (LANES, LANES), 0)
    coli = lax.broadcasted_iota(jnp.int32, (LANES, LANES), 1)
    diag = (rowi < 64) == (coli < 64)
    inter = [_dot((q.astype(F32) * tab[0]).astype(BF16), s_ref[...].astype(BF16))
             for q, k, v, tab, cd, s_ref, store in chains]
    inc = [_dot_tn((k.astype(F32) * tab[1]).astype(BF16), v) for q, k, v, tab, cd, s_ref, store in chains]
    att = [[_dot_nt(jnp.where(lo == (hh == 0), q, jnp.zeros_like(q)), k) * tab[2 + hh] for hh in range(2)]
           for q, k, v, tab, cd, s_ref, store in chains]
    intra = [[_dot(a.astype(BF16), c[2]) for a in pair] for pair, c in zip(att, chains)]
    for c, o_inter, o_intra, s_inc in zip(chains, inter, intra, inc):
        c[6](o_inter + jnp.where(lo, o_intra[0], o_intra[1]))
        c[5][...] = jnp.where(diag, c[5][...] * c[4] + s_inc, 0.0)


def _ret_kernel(dl_ref, qf_ref, kf_ref, vf_ref, qb_ref, kb_ref, vb_ref, of_ref, ob_ref,
                s_scr, tab_scr, cd_scr):
    nb = qf_ref.shape[0]

    @pl.when(pl.program_id(0) == 0)
    def _():
        s_scr[...] = jnp.zeros(s_scr.shape, F32)
        for d in range(2):
            for p in range(2):
                tabs = _ret_tables(dl_ref[d, p], backward=(d == 1))
                for t in range(4):
                    tab_scr[d, p, t] = tabs[t]
                cd_scr[d, p] = tabs[4]

    chains = []
    for d, (q_ref, k_ref, v_ref, o_ref) in enumerate(((qf_ref, kf_ref, vf_ref, of_ref),
                                                      (qb_ref, kb_ref, vb_ref, ob_ref))):
        for p in range(2):
            tab = [tab_scr[d, p, t] for t in range(4)]
            cd = cd_scr[d, p]
            ln = slice(p * LANES, (p + 1) * LANES)
            for b in range(nb):
                def store(o, o_ref=o_ref, b=b, ln=ln):
                    o_ref[b, :, ln] = o
                chains.append((q_ref[b, :, ln], k_ref[b, :, ln], v_ref[b, :, ln], tab, cd,
                               s_scr.at[d, b, p], store))
    _ret_chunks(chains)


def _retention(dl_lane, q, k, v):
    B, S, _ = q.shape
    L = RET_CHUNK
    nc = S // L
    ncc = CTX_LEN // L
    fwd = pl.BlockSpec((B, L, 256), lambda s: (0, s, 0))
    bwd = pl.BlockSpec((B, L, 256), lambda s: (0, jnp.where(s < ncc, ncc - 1 - s, nc - 1 + ncc - s), 0))
    return pl.pallas_call(
        _ret_kernel,
        grid=(nc,),
        in_specs=[pl.BlockSpec(dl_lane.shape, lambda s: (0, 0, 0, 0)), fwd, fwd, fwd, bwd, bwd, bwd],
        out_specs=[fwd, bwd],
        out_shape=[jax.ShapeDtypeStruct((B, S, 256), F32)] * 2,
        scratch_shapes=[pltpu.VMEM((2, B, 2, LANES, LANES), F32),
                        pltpu.VMEM((2, 2, 4, L, LANES), F32), pltpu.VMEM((2, 2, 1, LANES), F32)],
        compiler_params=_cparams(("arbitrary",)),
        name="retention",
    )(dl_lane, q, k, v, q, k, v)


def _gated_group_norm(o, g, gn):
    lo = _lane(o.shape) < 64

    def halves(t):
        a = jnp.sum(jnp.where(lo, t, 0.0), axis=-1, keepdims=True) * (1.0 / 64)
        b = jnp.sum(jnp.where(lo, 0.0, t), axis=-1, keepdims=True) * (1.0 / 64)
        return jnp.where(lo, a, b)

    d = o - halves(o)
    var = halves(d * d)
    return d * lax.rsqrt(var + NORM_EPS) * gn * (g * jax.nn.sigmoid(g))


def _outproj_kernel(x_ref, ya_ref, yb_ref, of_ref, ob_ref, rg_ref, gn_ref, ym_ref, w_ref, g1_ref,
                    sh_ref, sc_ref, ng_ref, wr_ref, br_ref, xo_ref, h_ref, rt_ref, cnt_ref, carry):
    mix = (_dot(ya_ref[0], w_ref[0:256]) + _dot(yb_ref[0], w_ref[256:512])
           + _dot(ym_ref[0], w_ref[768:1024]))
    for j in range(2):
        ln = slice(j * LANES, (j + 1) * LANES)
        yr = _gated_group_norm(of_ref[0, :, ln] + ob_ref[0, :, ln], rg_ref[0, :, ln].astype(F32),
                               gn_ref[:, ln])
        mix = mix + _dot(yr.astype(BF16), w_ref[512 + j * LANES:512 + (j + 1) * LANES])
    x = x_ref[0] + g1_ref[...] * mix
    xo_ref[0] = x
    h = _rms(x, ng_ref[...]) * (1.0 + sc_ref[...]) + sh_ref[...]
    h_ref[0] = h
    h_hi = h.astype(BF16)
    h_lo = (h - h_hi.astype(F32)).astype(BF16)
    logits = (_dot(h_hi, wr_ref[0]) + _dot(h_lo, wr_ref[0]) + _dot(h_hi, wr_ref[1])
              + br_ref[...])
    lane = _lane(logits.shape).astype(F32)

    def first_argmax(vals, mask):
        mx = jnp.max(jnp.where(mask, vals, NEG_INF), axis=-1, keepdims=True)
        idx = jnp.min(jnp.where(mask & (vals == mx), lane, float(LANES)), axis=-1, keepdims=True)
        return mx, idx

    gmask = lane < MOE_GROUPS
    g_max, g_idx = first_argmax(logits, gmask)
    g_w = 1.0 / jnp.sum(jnp.exp(jnp.where(gmask, logits - g_max, NEG_INF)), axis=-1, keepdims=True)
    e_lo = MOE_GROUPS + g_idx * MOE_PER_GROUP
    emask = (lane >= e_lo) & (lane < e_lo + MOE_PER_GROUP)
    v1, i1 = first_argmax(logits, emask)
    v2, i2 = first_argmax(logits, emask & (lane != i1))
    t = jnp.exp(v2 - v1)
    w1 = g_w / (1.0 + t)
    w2 = w1 * t
    e1, e2 = i1 - MOE_GROUPS, i2 - MOE_GROUPS

    @pl.when((pl.program_id(0) == 0) & (pl.program_id(1) == 0))
    def _():
        carry[...] = jnp.zeros(carry.shape, F32)

    oh1 = jnp.where(lane == e1, 1.0, 0.0)
    oh2 = jnp.where(lane == e2, 1.0, 0.0)
    cnt = oh1 + oh2
    tm = cnt.shape[0]
    tri = jnp.where(lax.broadcasted_iota(jnp.int32, (tm, tm), 0)
                    > lax.broadcasted_iota(jnp.int32, (tm, tm), 1), 1.0, 0.0).astype(BF16)
    before = _dot(tri, cnt.astype(BF16)) + carry[...]
    r1 = jnp.sum(oh1 * before, axis=-1, keepdims=True)
    r2 = jnp.sum(oh2 * before, axis=-1, keepdims=True)
    carry[...] = carry[...] + jnp.sum(cnt, axis=0, keepdims=True)
    cnt_ref[...] = carry[...]
    vals = (e1, e2, w1, w2, r1, r2)
    rt = jnp.zeros(logits.shape, F32)
    for j, val in enumerate(vals):
        rt = jnp.where(lane == j, val, rt)
    rt_ref[0] = rt


def _outproj(x, ya, yb, o_f, o_b, rg, gn_g, ym, w_out_p, mod, nbatch, norm_g, w_route, b_route):
    B, S, D = x.shape
    nt = S // TM
    tok = lambda w: pl.BlockSpec((1, TM, w), lambda b, i: (b, i, 0))
    full = lambda a: pl.BlockSpec(a.shape, lambda b, i: (0,) * a.ndim)
    return pl.pallas_call(
        _outproj_kernel,
        grid=(B, nt),
        in_specs=[tok(D), tok(256), tok(256), tok(256), tok(256), tok(256), full(gn_g), tok(256),
                  full(w_out_p),
                  _mod_spec(2, nbatch), _mod_spec(3, nbatch), _mod_spec(4, nbatch), full(norm_g),
                  full(w_route), full(b_route)],
        out_specs=[tok(D), tok(D), tok(LANES), pl.BlockSpec((1, LANES), lambda b, i: (0, 0))],
        out_shape=[jax.ShapeDtypeStruct((B, S, D), F32), jax.ShapeDtypeStruct((B, S, D), F32),
                   jax.ShapeDtypeStruct((B, S, LANES), F32), jax.ShapeDtypeStruct((1, LANES), F32)],
        scratch_shapes=[pltpu.VMEM((1, LANES), F32)],
        compiler_params=_cparams(("arbitrary", "arbitrary")),
        name="outproj_router",
    )(x, ya, yb, o_f, o_b, rg, gn_g, ym, w_out_p, mod, mod, mod, norm_g, w_route, b_route)


def _dispatch_kernel(dest_ref, h_ref, xb_in_ref, xb_ref, sem):
    del xb_in_ref
    base = pl.program_id(0) * TM

    def copy(r, k):
        d = dest_ref[(base + r) * 2 + k]
        return pltpu.make_async_copy(h_ref.at[pl.ds(r, 1)], xb_ref.at[pl.ds(d, 1)], sem)

    def issue(r, c):
        copy(r, 0).start()
        copy(r, 1).start()
        return c

    def drain(r, c):
        copy(r, 0).wait()
        copy(r, 1).wait()
        return c

    lax.fori_loop(0, TM, issue, 0, unroll=DMA_UNROLL)
    lax.fori_loop(0, TM, drain, 0, unroll=DMA_UNROLL)


def _dispatch(dest, h, slots):
    n, d = h.shape
    cap = slots.shape[0]
    return pl.pallas_call(
        _dispatch_kernel,
        grid_spec=pltpu.PrefetchScalarGridSpec(
            num_scalar_prefetch=1,
            grid=(n // TM,),
            in_specs=[pl.BlockSpec((TM, d), lambda i, dest: (i, 0)),
                      pl.BlockSpec(memory_space=pl.ANY)],
            out_specs=pl.BlockSpec(memory_space=pl.ANY),
            scratch_shapes=[pltpu.SemaphoreType.DMA(())]),
        out_shape=jax.ShapeDtypeStruct((cap, d), h.dtype),
        input_output_aliases={2: 0},
        compiler_params=_cparams(("arbitrary",)),
        name="moe_dispatch",
    )(dest, h, slots)


def _expert_kernel(be_ref, nu_ref, x_ref, wg_ref, wu_ref, wd_ref, y_ref, wg_s, wu_s, wd_s):
    i = pl.program_id(0)

    @pl.when((i == 0) | (be_ref[i] != be_ref[jnp.maximum(i - 1, 0)]))
    def _():
        wg_s[...] = wg_ref[0].astype(BF16)
        wu_s[...] = wu_ref[0].astype(BF16)
        wd_s[...] = wd_ref[0].astype(BF16)

    @pl.when(i < nu_ref[0])
    def _():
        xb = x_ref[...].astype(BF16)
        gate = _dot(xb, wg_s[...])
        up = _dot(xb, wu_s[...])
        hid = (gate * jax.nn.sigmoid(gate) * up).astype(BF16)
        y_ref[...] = _dot(hid, wd_s[...])

    @pl.when(i >= nu_ref[0])
    def _():
        y_ref[...] = jnp.zeros(y_ref.shape, y_ref.dtype)


def _experts(block_e, n_used, xb, w_gate, w_up, w_down, layer):
    cap, d = xb.shape
    hdim = w_gate.shape[-1]
    return pl.pallas_call(
        _expert_kernel,
        grid_spec=pltpu.PrefetchScalarGridSpec(
            num_scalar_prefetch=2,
            grid=(cap // MOE_ROWS,),
            in_specs=[pl.BlockSpec((MOE_ROWS, d), lambda i, be, nu: (i, 0)),
                      pl.BlockSpec((None, 1, d, hdim), lambda i, be, nu: (layer, be[i], 0, 0)),
                      pl.BlockSpec((None, 1, d, hdim), lambda i, be, nu: (layer, be[i], 0, 0)),
                      pl.BlockSpec((None, 1, hdim, d), lambda i, be, nu: (layer, be[i], 0, 0))],
            out_specs=pl.BlockSpec((MOE_ROWS, d), lambda i, be, nu: (i, 0)),
            scratch_shapes=[pltpu.VMEM((d, hdim), BF16), pltpu.VMEM((d, hdim), BF16),
                            pltpu.VMEM((hdim, d), BF16)]),
        out_shape=jax.ShapeDtypeStruct((cap, d), F32),
        compiler_params=_cparams(("arbitrary",)),
        name="moe_experts",
    )(block_e, n_used, xb, w_gate, w_up, w_down)


def _combine_kernel(dest_ref, x_ref, rt_ref, g2_ref, fg_ref, yb_ref, o_ref, gath, sem, *,
                    tile_off, tiles_per_batch, final):
    b, i = pl.program_id(0), pl.program_id(1)
    base = (b * tiles_per_batch + i + tile_off) * TM

    def copy(r, k):
        d = dest_ref[(base + r) * 2 + k]
        return pltpu.make_async_copy(yb_ref.at[pl.ds(d, 1)], gath.at[k, pl.ds(r, 1)], sem)

    def issue(r, c):
        copy(r, 0).start()
        copy(r, 1).start()
        return c

    def drain(r, c):
        copy(r, 0).wait()
        copy(r, 1).wait()
        return c

    lax.fori_loop(0, TM, issue, 0, unroll=DMA_UNROLL)
    lax.fori_loop(0, TM, drain, 0, unroll=DMA_UNROLL)
    rt = rt_ref[0]
    m = rt[:, 2:3] * gath[0] + rt[:, 3:4] * gath[1]
    x = x_ref[0] + g2_ref[...] * m
    if final:
        x = _rms(x, fg_ref[...])
    o_ref[0] = x


def _combine(dest, x, route, mod, nbatch, final_g, yb, final):
    B, S, D = x.shape
    nt = S // TM
    off = 1 if final else 0
    g2_spec = pl.BlockSpec((None, 1, D_MODEL),
                           lambda b, i, dest: (jnp.where(i + off == 0, nbatch, b), 0, 5))
    return pl.pallas_call(
        functools.partial(_combine_kernel, tile_off=off, tiles_per_batch=nt, final=final),
        grid_spec=pltpu.PrefetchScalarGridSpec(
            num_scalar_prefetch=1,
            grid=(B, nt - off),
            in_specs=[pl.BlockSpec((1, TM, D), lambda b, i, dest: (b, i + off, 0)),
                      pl.BlockSpec((1, TM, LANES), lambda b, i, dest: (b, i + off, 0)),
                      g2_spec,
                      pl.BlockSpec((1, D), lambda b, i, dest: (0, 0)),
                      pl.BlockSpec(memory_space=pl.ANY)],
            out_specs=pl.BlockSpec((1, TM, D), lambda b, i, dest: (b, i, 0)),
            scratch_shapes=[pltpu.VMEM((2, TM, D), F32), pltpu.SemaphoreType.DMA(())]),
        out_shape=jax.ShapeDtypeStruct((B, S - off * TM, D), F32),
        compiler_params=_cparams(("arbitrary", "arbitrary")),
        name="moe_combine",
    )(dest, x, route, mod, final_g, yb)


def _rope_tables(seq):
    n_rows = seq // GRID_W
    pos = jnp.arange(max(n_rows, GRID_W), dtype=F32)[:, None]
    lane = jnp.arange(LANES, dtype=jnp.int32)
    tabs = []
    for r in (64, 32):
        nf = r // 4
        inv = 1.0 / (ROPE_BASE ** (jnp.arange(nf, dtype=F32) / nf))
        ang = pos * inv[lane % nf][None, :]
        first = ((lane % (2 * nf)) < nf)[None, :]
        by_row = ((lane % r) < 2 * nf)[None, None, :]
        for small in (jnp.cos(ang), jnp.where(first, -jnp.sin(ang), jnp.sin(ang))):
            full = jnp.where(by_row, small[:n_rows, None, :], small[None, :GRID_W, :])
            tabs.append(full.reshape(seq, LANES))
    ident = (jnp.ones((CTX_LEN, LANES), F32), jnp.zeros((CTX_LEN, LANES), F32))
    return [jnp.concatenate([ident[j % 2], tab], axis=0) for j, tab in enumerate(tabs)]


def _layer_weights(w_in, w_out, w_uq, w_ukv, w_group, b_group, w_expert, b_expert):
    perm = jnp.array([0, 2, 1, 3])
    wq = w_in[:, :256].reshape(D_MODEL, 4, 64)[:, perm].reshape(D_MODEL, 256)
    w_in_p = jnp.concatenate([wq, w_in[:, 256:], jnp.zeros((D_MODEL, D_IN_PAD - w_in.shape[1]), F32)],
                             axis=1).astype(BF16)
    wo = jnp.concatenate([w_out[:256].reshape(4, 64, D_MODEL)[perm].reshape(256, D_MODEL), w_out[256:]],
                         axis=0).astype(BF16)
    uq = w_uq.reshape(256, 4, 96)
    blocks = []
    for h in range(4):
        blk = jnp.zeros((256, 256), F32)
        blk = blk.at[:, 64 * (h % 2):64 * (h % 2) + 64].set(uq[:, h, :64])
        blk = blk.at[:, LANES:LANES + 32].set(uq[:, h, 64:])
        blocks.append(blk)
    wuq = jnp.concatenate(blocks, axis=1).astype(BF16)
    ukv = w_ukv.reshape(128, 4, 128)
    wukk = ukv[:, :, :64].reshape(128, 256).astype(BF16)
    wukv = ukv[:, :, 64:].reshape(128, 256).astype(BF16)
    pad = LANES - MOE_GROUPS - MOE_EXPERTS
    w_route = jnp.concatenate([w_group, w_expert, jnp.zeros((D_MODEL, pad), F32)], axis=1)
    w_hi = w_route.astype(BF16)
    w_route = jnp.stack([w_hi, (w_route - w_hi.astype(F32)).astype(BF16)])
    b_route = jnp.concatenate([b_group, b_expert, jnp.zeros((pad,), F32)]).reshape(1, LANES)
    return w_in_p, wo, wuq, wukk, wukv, w_route, b_route


def _moe_plan(route, counts, cap):
    e = route[:, :2].astype(jnp.int32)
    rank = route[:, 4:6].astype(jnp.int32)
    cnt = counts[0, :MOE_EXPERTS].astype(jnp.int32)
    padded = (cnt + MOE_ROWS - 1) // MOE_ROWS * MOE_ROWS
    pend = jnp.cumsum(padded)
    pstart = pend - padded
    dest = (pstart[e] + rank).reshape(-1)
    nblk = cap // MOE_ROWS
    blk_start = jnp.arange(nblk, dtype=jnp.int32) * MOE_ROWS
    block_e = jnp.minimum(jnp.sum((pend[None, :] <= blk_start[:, None]).astype(jnp.int32), axis=1),
                          MOE_EXPERTS - 1)
    n_used = (pend[-1:] // MOE_ROWS).astype(jnp.int32)
    return dest, block_e, n_used


def kernel(x, c, ctx, c_ctx, w_mod, b_mod, norm1_g, norm2_g, w_in, w_out, swa_sink, diff_lambda, diff_subln_g, ret_decay_logit, ret_gn_g, mla_q_norm_g, mla_w_uq, mla_kv_norm_g, mla_w_ukv, moe_w_group, moe_b_group, moe_w_expert, moe_b_expert, moe_w_gate, moe_w_up, moe_w_down, final_norm_g):
    B, T, D = x.shape
    S = CTX_LEN + T
    n_tok = B * S
    diff_tk, mla_tk = min(DIFF_TK, T), min(MLA_TK, T)
    cap = -(-(2 * n_tok + MOE_EXPERTS * (MOE_ROWS - 1)) // MOE_ROWS) * MOE_ROWS
    xs = jnp.concatenate([ctx, x], axis=1)
    cvec = jnp.zeros((8, D), F32).at[:B].set(c).at[B].set(c_ctx)
    tabs = _rope_tables(T)
    row = lambda v: v.reshape(1, -1)

    for l in range(DEPTH):
        lambda_init = 0.8 - 0.6 * math.exp(-0.3 * l)
        w_in_p, wo, wuq, wukk, wukv, w_route, b_route = _layer_weights(
            w_in[l], w_out[l], mla_w_uq[l], mla_w_ukv[l], moe_w_group[l], moe_b_group[l],
            moe_w_expert[l], moe_b_expert[l])
        mod = _modulation(cvec, w_mod[l], b_mod[l]).reshape(8, 1, 6 * D)
        (swaq, swak, swav, dq, dk, dv, rq, rk, rv, rg, mq, mk, mv) = _inproj(
            xs, mod, B, row(norm1_g[l]), w_in_p, tabs, row(mla_q_norm_g[l]), wuq,
            row(mla_kv_norm_g[l]), wukk, wukv)
        sink_p = swa_sink[l]
        ya = _swa(sink_p, swaq, swak, swav)
        yb = _diff(diff_lambda[l], diff_subln_g[l].reshape(64, 1), dq, dk, dv, lambda_init,
                   diff_tk)
        dl_lane = jnp.repeat(ret_decay_logit[l], 64, axis=-1).reshape(2, 2, 1, LANES)
        o_f, o_b = _retention(dl_lane, rq, rk, rv)
        ym = _mla(mq, mk, mv, mla_tk)
        xs, h2, route, counts = _outproj(xs, ya, yb, o_f, o_b, rg, row(ret_gn_g[l]), ym, wo, mod, B,
                                         row(norm2_g[l]), w_route, b_route)
        dest, block_e, n_used = _moe_plan(route.reshape(n_tok, LANES), counts, cap)
        xb = _dispatch(dest, h2.reshape(n_tok, D), jnp.zeros((cap, D), F32) if l == 0 else xb)
        yb_e = _experts(block_e, n_used, xb, moe_w_gate, moe_w_up, moe_w_down, l)
        final = l == DEPTH - 1
        xs = _combine(dest, xs, route, mod, B, row(final_norm_g), yb_e, final)
    return xs
```

```python
import functools
import math

import jax
import jax.numpy as jnp
from jax import lax
from jax.experimental import pallas as pl
from jax.experimental.pallas import tpu as pltpu

F32 = jnp.float32
BF16 = jnp.bfloat16

D_MODEL = 1024
CTX_LEN = 256
GRID_W = 64
ROPE_BASE = 10000.0
NORM_EPS = 1e-6
NEG_INF = -1e30
DEPTH = 2

SWA_WINDOW = 128
RET_CHUNK = 128
MLA_SCALE = (64 + 32) ** -0.5
LOG2E = math.log2(math.e)
MOE_GROUPS = 4
MOE_PER_GROUP = 8
MOE_EXPERTS = 32
MOE_HIDDEN = 512

LANES = 128
TM = 256
DIFF_TQ = 2048
MLA_TQ = 2048
DIFF_TK = 1024
MLA_TK = 2048
VT_ROWS = 80
GAP_LIMIT = 64.0
MOE_ROWS = 512
DMA_UNROLL = 8
VMEM_LIMIT = 56 * 1024 * 1024

C_SWA_Q, C_SWA_K, C_SWA_V = 0, 256, 384
C_DIF_Q, C_DIF_K, C_DIF_V = 512, 768, 1024
C_RET_Q, C_RET_K, C_RET_V, C_RET_G = 1280, 1536, 1792, 2048
C_MLA_Q, C_MLA_KV, C_MLA_KR = 2304, 2560, 2688
D_IN_PAD = 2816


def _cparams(sem):
    return pltpu.CompilerParams(dimension_semantics=sem, vmem_limit_bytes=VMEM_LIMIT)


def _dot(a, b):
    return jnp.dot(a, b, preferred_element_type=F32)


def _dot_nt(a, b):
    return lax.dot_general(a, b, (((1,), (1,)), ((), ())), preferred_element_type=F32)


def _dot_tn(a, b):
    return lax.dot_general(a, b, (((0,), (0,)), ((), ())), preferred_element_type=F32)


def _rms(x, g):
    ms = jnp.mean(x * x, axis=-1, keepdims=True)
    return x * lax.rsqrt(ms + NORM_EPS) * g


def _lane(shape):
    return lax.broadcasted_iota(jnp.int32, shape, len(shape) - 1)


def _rope(z, cos, sin, nf):
    first = (_lane(z.shape) % (2 * nf)) < nf
    partner = jnp.where(first, pltpu.roll(z, LANES - nf, 1), pltpu.roll(z, nf, 1))
    return z * cos + partner * sin


def _mod_kernel(a_ref, w_ref, b_ref, o_ref):
    a = a_ref[...]
    act = a * jax.nn.sigmoid(a)
    o_ref[...] = jnp.dot(act, w_ref[...], preferred_element_type=F32,
                         precision=lax.Precision.HIGHEST) + b_ref[...]


def _modulation(cvec, w_mod, b_mod):
    n = w_mod.shape[1]
    bn = 512
    return pl.pallas_call(
        _mod_kernel,
        grid=(n // bn,),
        in_specs=[pl.BlockSpec((8, D_MODEL), lambda j: (0, 0)),
                  pl.BlockSpec((D_MODEL, bn), lambda j: (0, j)),
                  pl.BlockSpec((1, bn), lambda j: (0, j))],
        out_specs=pl.BlockSpec((8, bn), lambda j: (0, j)),
        out_shape=jax.ShapeDtypeStruct((8, n), F32),
        compiler_params=_cparams(("parallel",)),
        name="modulation",
    )(cvec, w_mod, b_mod.reshape(1, n))


def _mod_spec(chunk, nbatch):
    return pl.BlockSpec((None, 1, D_MODEL),
                        lambda b, i: (jnp.where(i == 0, nbatch, b), 0, chunk))


def _inproj_kernel(x_ref, g_ref, sh_ref, sc_ref, w_ref, c64_ref, s64_ref, c32_ref, s32_ref,
                   qng_ref, wuq_ref, kvng_ref, wukk_ref, wukv_ref,
                   swaq, swak, swav, dq, dk, dv, rq, rk, rv, rg, mq, mk, mv):
    x = x_ref[0]
    h = _rms(x, g_ref[...]) * (1.0 + sc_ref[...]) + sh_ref[...]
    hb = h.astype(BF16)
    c64, s64, c32, s32 = c64_ref[...], s64_ref[...], c32_ref[...], s32_ref[...]

    def proj(c0, width):
        return _dot(hb, w_ref[:, c0:c0 + width])

    def rope_blocks(z, out, nf, scale):
        cos, sin = (c64, s64) if nf == 16 else (c32, s32)
        for j in range(z.shape[1] // LANES):
            zz = _rope(z[:, j * LANES:(j + 1) * LANES], cos, sin, nf)
            if scale != 1.0:
                zz = zz * scale
            out[0, :, j * LANES:(j + 1) * LANES] = zz.astype(out.dtype)

    rope_blocks(proj(C_SWA_Q, 256), swaq, 16, 0.125)
    rope_blocks(proj(C_SWA_K, 128), swak, 16, 1.0)
    swav[0] = proj(C_SWA_V, 128).astype(BF16)
    def values_transposed(z, out):
        zt = z.T.astype(BF16)
        for hh in range(4):
            out[0, hh * VT_ROWS:hh * VT_ROWS + 64, :] = zt[hh * 64:(hh + 1) * 64]
            out[0, hh * VT_ROWS + 64:(hh + 1) * VT_ROWS, :] = jnp.ones((VT_ROWS - 64, zt.shape[1]), BF16)

    rope_blocks(proj(C_DIF_Q, 256), dq, 8, 32 ** -0.5 * LOG2E)
    rope_blocks(proj(C_DIF_K, 256), dk, 8, 1.0)
    values_transposed(proj(C_DIF_V, 256), dv)
    rope_blocks(proj(C_RET_Q, 256), rq, 16, 0.125)
    rope_blocks(proj(C_RET_K, 256), rk, 16, 1.0)
    rv[0] = proj(C_RET_V, 256).astype(BF16)
    rg[0] = proj(C_RET_G, 256).astype(BF16)

    ql = _rms(proj(C_MLA_Q, 256), qng_ref[...]).astype(BF16)
    qa = _dot(ql, wuq_ref[...]) * (MLA_SCALE * LOG2E)
    for hh in range(4):
        mq[0, :, hh * 256:hh * 256 + LANES] = qa[:, hh * 256:hh * 256 + LANES].astype(BF16)
        zr = _rope(qa[:, hh * 256 + LANES:(hh + 1) * 256], c32, s32, 8)
        mq[0, :, hh * 256 + LANES:(hh + 1) * 256] = zr.astype(BF16)
    kvl = _rms(proj(C_MLA_KV, 128), kvng_ref[...]).astype(BF16)
    kn = _dot(kvl, wukk_ref[...])
    kr = _rope(proj(C_MLA_KR, 128), c32, s32, 8).astype(BF16)
    for p in range(2):
        mk[0, :, p * 256:p * 256 + LANES] = kn[:, p * LANES:(p + 1) * LANES].astype(BF16)
        mk[0, :, p * 256 + LANES:(p + 1) * 256] = kr
    values_transposed(_dot(kvl, wukv_ref[...]), mv)


def _inproj(x, mod, nbatch, norm_g, w_in_p, tabs, qng, wuq, kvng, wukk, wukv):
    B, S, D = x.shape
    nt = S // TM
    tok = lambda w: pl.BlockSpec((1, TM, w), lambda b, i: (b, i, 0))
    full = lambda a: pl.BlockSpec(a.shape, lambda b, i: (0,) * a.ndim)
    tab = pl.BlockSpec((TM, LANES), lambda b, i: (i, 0))
    widths = [256, 128, 128, 256, 256, None, 256, 256, 256, 256, 1024, 512, None]
    vt_spec = pl.BlockSpec((1, 4 * VT_ROWS, TM), lambda b, i: (b, 0, i))
    vt_shape = jax.ShapeDtypeStruct((B, 4 * VT_ROWS, S), BF16)
    return pl.pallas_call(
        _inproj_kernel,
        grid=(B, nt),
        in_specs=[tok(D), full(norm_g), _mod_spec(0, nbatch), _mod_spec(1, nbatch), full(w_in_p),
                  tab, tab, tab, tab, full(qng), full(wuq), full(kvng), full(wukk), full(wukv)],
        out_specs=[vt_spec if w is None else tok(w) for w in widths],
        out_shape=[vt_shape if w is None else jax.ShapeDtypeStruct((B, S, w), BF16) for w in widths],
        compiler_params=_cparams(("parallel", "parallel")),
        name="inproj",
    )(x, norm_g, mod, mod, w_in_p, *tabs, qng, wuq, kvng, wukk, wukv)


def _swa_kernel(sink_ref, q_ref, kp_ref, ko_ref, kn_ref, vp_ref, vo_ref, vn_ref, kc_ref, vc_ref,
                o_ref, *, n_ctx_tiles, seq):
    i = pl.program_id(1)
    blk = RET_CHUNK
    kb = [kp_ref[0], ko_ref[0, 0:blk], ko_ref[0, blk:2 * blk], kn_ref[0]]
    vb = [vp_ref[0], vo_ref[0, 0:blk], vo_ref[0, blk:2 * blk], vn_ref[0]]
    nk = 3 * blk + CTX_LEN
    r_io = lax.broadcasted_iota(jnp.int32, (blk, nk), 0)
    c_io = lax.broadcasted_iota(jnp.int32, (blk, nk), 1)
    lane = _lane((blk, LANES))
    chains = [(s, r, g) for s in range(2) for r in range(2) for g in range(2)]
    k_all, v_all, valid = [], [], []
    for s in range(2):
        n = 2 * (i - n_ctx_tiles) + s
        kpos = (n - 1) * blk + c_io
        in_band = ((c_io >= r_io) & (c_io <= r_io + 2 * SWA_WINDOW) & (kpos >= 0) & (kpos < seq)
                   & (n >= 0))
        valid.append(in_band | (c_io >= 3 * blk))
        k_all.append(jnp.concatenate(kb[s:s + 3] + [kc_ref[0]], axis=0))
        v_all.append(jnp.concatenate(vb[s:s + 3] + [vc_ref[0]], axis=0))
    qm = [jnp.where((lane >= 64 * g) & (lane < 64 * (g + 1)),
                    q_ref[0, s * blk:(s + 1) * blk, r * LANES:(r + 1) * LANES],
                    jnp.zeros((blk, LANES), BF16)) for s, r, g in chains]
    sc = [jnp.where(valid[c[0]], _dot_nt(q, k_all[c[0]]), NEG_INF) for q, c in zip(qm, chains)]
    sink = [sink_ref[2 * g + r] for s, r, g in chains]
    m = [jnp.maximum(jnp.max(x, axis=-1, keepdims=True), sk) for x, sk in zip(sc, sink)]
    p = [jnp.exp(x - mm) for x, mm in zip(sc, m)]
    l = [jnp.sum(pp, axis=-1, keepdims=True) + jnp.exp(sk - mm) for pp, sk, mm in zip(p, sink, m)]
    o = [_dot(pp.astype(BF16), v_all[c[0]]) / ll for pp, ll, c in zip(p, l, chains)]
    for s in range(2):
        for r in range(2):
            j = 4 * s + 2 * r
            o_ref[0, s * blk:(s + 1) * blk, r * LANES:(r + 1) * LANES] = (
                jnp.where(lane < 64, o[j], o[j + 1]).astype(BF16))


def _swa(sink, q, k, v):
    B, S, _ = q.shape
    blk = RET_CHUNK
    nb = S // blk
    nt = S // (2 * blk)
    edge = lambda f: pl.BlockSpec((1, blk, LANES), f)
    prev = lambda b, i: (b, jnp.maximum(2 * i - 1, 0), 0)
    own = lambda b, i: (b, i, 0)
    nxt = lambda b, i: (b, jnp.minimum(2 * i + 2, nb - 1), 0)
    own_kv = pl.BlockSpec((1, 2 * blk, LANES), own)
    ctx = pl.BlockSpec((1, CTX_LEN, LANES), lambda b, i: (b, 0, 0))
    return pl.pallas_call(
        functools.partial(_swa_kernel, n_ctx_tiles=CTX_LEN // (2 * blk), seq=S - CTX_LEN),
        grid=(B, nt),
        in_specs=[pl.BlockSpec(memory_space=pltpu.SMEM),
                  pl.BlockSpec((1, 2 * blk, 256), own),
                  edge(prev), own_kv, edge(nxt), edge(prev), own_kv, edge(nxt), ctx, ctx],
        out_specs=pl.BlockSpec((1, 2 * blk, 256), own),
        out_shape=jax.ShapeDtypeStruct((B, S, 256), BF16),
        compiler_params=_cparams(("parallel", "parallel")),
        name="swa",
    )(sink, q, k, k, k, v, v, v, k, v)


def _flash(q_maps, v_blk, k_ref, v_ref, m_scr, acc_scr, gap_scr, *, n_lat, tk):
    nm = len(q_maps)
    qt_maps = [q.astype(F32).T.astype(BF16) for q in q_maps]

    def chunk(start, size, mode):
        kc = k_ref[0, pl.ds(start, size), :]
        for a in range(nm):
            vt = v_ref[0, v_blk[a] * VT_ROWS:(v_blk[a] + 1) * VT_ROWS, pl.ds(start, size)]
            st = _dot(kc, qt_maps[a])
            m_prev = None if mode == "first" else m_scr[a]
            if mode == "fast":
                pf = jnp.exp2(st - m_prev)
                rise = jnp.maximum(jnp.max(pf, axis=0, keepdims=True), 1.0)
                acc_scr[a] = (acc_scr[a] + _dot(vt, pf.astype(BF16))) * (1.0 / rise)
                m_scr[a] = m_prev + jnp.log2(rise)
                gap_scr[a] = jnp.maximum(gap_scr[a], rise)
                continue
            cm = jnp.max(st, axis=0, keepdims=True)
            if mode == "first":
                acc_scr[a] = _dot(vt, jnp.exp2(st - cm).astype(BF16))
                m_scr[a] = cm
                gap_scr[a] = jnp.ones(cm.shape, F32)
            else:
                m_new = jnp.maximum(m_prev, cm)
                p = jnp.exp2(st - m_new).astype(BF16)
                acc_scr[a] = acc_scr[a] * jnp.exp2(m_prev - m_new) + _dot(vt, p)
                m_scr[a] = m_new

    def sweep(mode):
        chunk(0, CTX_LEN, "first")
        if n_lat:

            def body(c, carry):
                chunk(pl.multiple_of(CTX_LEN + c * tk, LANES), tk, mode)
                return carry

            lax.fori_loop(0, n_lat, body, 0)

    sweep("fast")
    if n_lat:
        worst = jnp.max(jnp.concatenate([gap_scr[a] for a in range(nm)], axis=0))

        @pl.when(worst > 2.0 ** GAP_LIMIT)
        def _():
            sweep("exact")

    return [acc_scr[a, 0:64, :] / acc_scr[a, 64:65, :] for a in range(nm)]


def _attention_calls(kern, name, nm, q, k, vt, qw, kw, tk, tq, extra_in=(), extra_specs=()):
    B, S, _ = q.shape
    T = S - CTX_LEN

    def call(tq, n_tiles, n_lat, q_spec, kv_rows, call_name):
        return pl.pallas_call(
            functools.partial(kern, n_lat=n_lat, tk=tk),
            grid=(B, 2, n_tiles),
            in_specs=list(extra_specs) + [
                q_spec,
                pl.BlockSpec((1, kv_rows, kw), lambda b, p, i: (b, 0, p)),
                pl.BlockSpec((1, 2 * VT_ROWS, kv_rows), lambda b, p, i: (b, p, 0))],
            out_specs=pl.BlockSpec((1, tq, LANES), lambda b, p, i: (b, i, p)),
            out_shape=jax.ShapeDtypeStruct((B, tq * n_tiles, 256), BF16),
            scratch_shapes=[pltpu.VMEM((nm, 1, tq), F32), pltpu.VMEM((nm, VT_ROWS, tq), F32),
                            pltpu.VMEM((nm, 1, tq), F32)],
            compiler_params=_cparams(("parallel", "parallel", "parallel")),
            name=call_name,
        )(*extra_in, q, k, vt)

    y_ctx = call(CTX_LEN, 1, 0, pl.BlockSpec((1, CTX_LEN, qw), lambda b, p, i: (b, 0, p)),
                 CTX_LEN, name + "_ctx")
    tq = min(tq, T)
    y_lat = call(tq, T // tq, T // tk,
                 pl.BlockSpec((pl.Element(1), pl.Element(tq), pl.Element(qw)),
                              lambda b, p, i: (b, (i * (tq // CTX_LEN) + 1) * CTX_LEN, p * qw)),
                 S, name)
    return jnp.concatenate([y_ctx, y_lat], axis=1)


def _diff_attn(lam_ref, g_ref, q_ref, k_ref, v_ref, o_ref, m_scr, acc_scr, gap_scr, *,
               n_lat, tk, lambda_init):
    qb = q_ref[0]
    lane = _lane(qb.shape)
    q_maps = []
    for a in range(4):
        msk = (lane >= 32 * a) & (lane < 32 * (a + 1))
        q_maps.append(jnp.where(msk, qb, jnp.zeros_like(qb)))
    n = _flash(q_maps, (0, 0, 1, 1), k_ref, v_ref, m_scr, acc_scr, gap_scr, n_lat=n_lat, tk=tk)
    lp = lam_ref[...]
    lam = (jnp.exp(jnp.sum(lp[0:1] * lp[1:2], axis=-1, keepdims=True))
           - jnp.exp(jnp.sum(lp[2:3] * lp[3:4], axis=-1, keepdims=True)) + lambda_init)
    ys = []
    for hh in range(2):
        o = n[2 * hh] - lam * n[2 * hh + 1]
        ms = jnp.mean(o * o, axis=0, keepdims=True)
        ys.append((o * lax.rsqrt(ms + NORM_EPS) * g_ref[...] * (1.0 - lambda_init)).T)
    o_ref[0] = jnp.concatenate(ys, axis=1).astype(BF16)


def _diff(lam_p, subln_g2, q, k, v, lambda_init, tk):
    return _attention_calls(
        functools.partial(_diff_attn, lambda_init=lambda_init), "diff_attn", 4, q, k, v, LANES, LANES,
        tk, DIFF_TQ, extra_in=(lam_p, subln_g2),
        extra_specs=(pl.BlockSpec(lam_p.shape, lambda b, p, i: (0, 0)),
                     pl.BlockSpec((64, 1), lambda b, p, i: (0, 0))))


def _mla_attn(q_ref, k_ref, v_ref, o_ref, m_scr, acc_scr, gap_scr, *, n_lat, tk):
    q_maps = [q_ref[0, :, 0:256], q_ref[0, :, 256:512]]
    n = _flash(q_maps, (0, 1), k_ref, v_ref, m_scr, acc_scr, gap_scr, n_lat=n_lat, tk=tk)
    o_ref[0] = jnp.concatenate([n[0].T, n[1].T], axis=1).astype(BF16)


def _mla(q, k, v, tk):
    return _attention_calls(_mla_attn, "mla_attn", 2, q, k, v, 512, 256, tk, MLA_TQ)


def _ret_tables(dl, backward):
    L = RET_CHUNK
    lg = -(jnp.maximum(-dl, 0.0) + jnp.log(1.0 + jnp.exp(-jnp.abs(dl))))
    row = lax.broadcasted_iota(jnp.int32, (L, LANES), 0).astype(F32)
    col = lax.broadcasted_iota(jnp.int32, (L, LANES), 1)
    colf = col.astype(F32)
    if backward:
        qd, kd, rel = jnp.exp((L - row) * lg), jnp.exp(row * lg), colf - row
    else:
        qd, kd, rel = jnp.exp((row + 1.0) * lg), jnp.exp((L - 1.0 - row) * lg), row - colf
    decs = []
    for hh in range(2):
        lg_h = jnp.sum(jnp.where(col[0:1] == 64 * hh, lg, 0.0), axis=-1, keepdims=True)
        decs.append(jnp.where(rel >= 0, jnp.exp(jnp.maximum(rel, 0.0) * lg_h), 0.0))
    return qd, kd, decs[0], decs[1], jnp.exp(float(L) * lg)


def _ret_chunks(chains):
    col = _lane((RET_CHUNK, LANES))
    lo = col < 64
    rowi = lax.broadcasted_iota(jnp.int32, (LANES, LANES), 0)
    coli = lax.broadcasted_iota(jnp.int32, (LANES, LANES), 1)
    diag = (rowi < 64) == (coli < 64)
    inter = [_dot((q.astype(F32) * tab[0]).astype(BF16), s_ref[...].astype(BF16))
             for q, k, v, tab, cd, s_ref, store in chains]
    inc = [_dot_tn((k.astype(F32) * tab[1]).astype(BF16), v) for q, k, v, tab, cd, s_ref, store in chains]
    att = [[_dot_nt(jnp.where(lo == (hh == 0), q, jnp.zeros_like(q)), k) * tab[2 + hh] for hh in range(2)]
           for q, k, v, tab, cd, s_ref, store in chains]
    intra = [[_dot(a.astype(BF16), c[2]) for a in pair] for pair, c in zip(att, chains)]
    for c, o_inter, o_intra, s_inc in zip(chains, inter, intra, inc):
        c[6](o_inter + jnp.where(lo, o_intra[0], o_intra[1]))
        c[5][...] = jnp.where(diag, c[5][...] * c[4] + s_inc, 0.0)


def _ret_kernel(dl_ref, qf_ref, kf_ref, vf_ref, qb_ref, kb_ref, vb_ref, of_ref, ob_ref,
                s_scr, tab_scr, cd_scr):
    nb = qf_ref.shape[0]

    @pl.when(pl.program_id(0) == 0)
    def _():
        s_scr[...] = jnp.zeros(s_scr.shape, F32)
        for d in range(2):
            for p in range(2):
                tabs = _ret_tables(dl_ref[d, p], backward=(d == 1))
                for t in range(4):
                    tab_scr[d, p, t] = tabs[t]
                cd_scr[d, p] = tabs[4]

    chains = []
    for d, (q_ref, k_ref, v_ref, o_ref) in enumerate(((qf_ref, kf_ref, vf_ref, of_ref),
                                                      (qb_ref, kb_ref, vb_ref, ob_ref))):
        for p in range(2):
            tab = [tab_scr[d, p, t] for t in range(4)]
            cd = cd_scr[d, p]
            ln = slice(p * LANES, (p + 1) * LANES)
            for b in range(nb):
                def store(o, o_ref=o_ref, b=b, ln=ln):
                    o_ref[b, :, ln] = o
                chains.append((q_ref[b, :, ln], k_ref[b, :, ln], v_ref[b, :, ln], tab, cd,
                               s_scr.at[d, b, p], store))
    _ret_chunks(chains)


def _retention(dl_lane, q, k, v):
    B, S, _ = q.shape
    L = RET_CHUNK
    nc = S // L
    ncc = CTX_LEN // L
    fwd = pl.BlockSpec((B, L, 256), lambda s: (0, s, 0))
    bwd = pl.BlockSpec((B, L, 256), lambda s: (0, jnp.where(s < ncc, ncc - 1 - s, nc - 1 + ncc - s), 0))
    return pl.pallas_call(
        _ret_kernel,
        grid=(nc,),
        in_specs=[pl.BlockSpec(dl_lane.shape, lambda s: (0, 0, 0, 0)), fwd, fwd, fwd, bwd, bwd, bwd],
        out_specs=[fwd, bwd],
        out_shape=[jax.ShapeDtypeStruct((B, S, 256), F32)] * 2,
        scratch_shapes=[pltpu.VMEM((2, B, 2, LANES, LANES), F32),
                        pltpu.VMEM((2, 2, 4, L, LANES), F32), pltpu.VMEM((2, 2, 1, LANES), F32)],
        compiler_params=_cparams(("arbitrary",)),
        name="retention",
    )(dl_lane, q, k, v, q, k, v)


def _gated_group_norm(o, g, gn):
    lo = _lane(o.shape) < 64

    def halves(t):
        a = jnp.sum(jnp.where(lo, t, 0.0), axis=-1, keepdims=True) * (1.0 / 64)
        b = jnp.sum(jnp.where(lo, 0.0, t), axis=-1, keepdims=True) * (1.0 / 64)
        return jnp.where(lo, a, b)

    d = o - halves(o)
    var = halves(d * d)
    return d * lax.rsqrt(var + NORM_EPS) * gn * (g * jax.nn.sigmoid(g))


def _outproj_kernel(x_ref, ya_ref, yb_ref, of_ref, ob_ref, rg_ref, gn_ref, ym_ref, w_ref, g1_ref,
                    sh_ref, sc_ref, ng_ref, wr_ref, br_ref, xo_ref, h_ref, rt_ref, cnt_ref, carry):
    mix = (_dot(ya_ref[0], w_ref[0:256]) + _dot(yb_ref[0], w_ref[256:512])
           + _dot(ym_ref[0], w_ref[768:1024]))
    for j in range(2):
        ln = slice(j * LANES, (j + 1) * LANES)
        yr = _gated_group_norm(of_ref[0, :, ln] + ob_ref[0, :, ln], rg_ref[0, :, ln].astype(F32),
                               gn_ref[:, ln])
        mix = mix + _dot(yr.astype(BF16), w_ref[512 + j * LANES:512 + (j + 1) * LANES])
    x = x_ref[0] + g1_ref[...] * mix
    xo_ref[0] = x
    h = _rms(x, ng_ref[...]) * (1.0 + sc_ref[...]) + sh_ref[...]
    h_ref[0] = h
    h_hi = h.astype(BF16)
    h_lo = (h - h_hi.astype(F32)).astype(BF16)
    logits = (_dot(h_hi, wr_ref[0]) + _dot(h_lo, wr_ref[0]) + _dot(h_hi, wr_ref[1])
              + br_ref[...])
    lane = _lane(logits.shape).astype(F32)

    def first_argmax(vals, mask):
        mx = jnp.max(jnp.where(mask, vals, NEG_INF), axis=-1, keepdims=True)
        idx = jnp.min(jnp.where(mask & (vals == mx), lane, float(LANES)), axis=-1, keepdims=True)
        return mx, idx

    gmask = lane < MOE_GROUPS
    g_max, g_idx = first_argmax(logits, gmask)
    g_w = 1.0 / jnp.sum(jnp.exp(jnp.where(gmask, logits - g_max, NEG_INF)), axis=-1, keepdims=True)
    e_lo = MOE_GROUPS + g_idx * MOE_PER_GROUP
    emask = (lane >= e_lo) & (lane < e_lo + MOE_PER_GROUP)
    v1, i1 = first_argmax(logits, emask)
    v2, i2 = first_argmax(logits, emask & (lane != i1))
    t = jnp.exp(v2 - v1)
    w1 = g_w / (1.0 + t)
    w2 = w1 * t
    e1, e2 = i1 - MOE_GROUPS, i2 - MOE_GROUPS

    @pl.when((pl.program_id(0) == 0) & (pl.program_id(1) == 0))
    def _():
        carry[...] = jnp.zeros(carry.shape, F32)

    oh1 = jnp.where(lane == e1, 1.0, 0.0)
    oh2 = jnp.where(lane == e2, 1.0, 0.0)
    cnt = oh1 + oh2
    tm = cnt.shape[0]
    tri = jnp.where(lax.broadcasted_iota(jnp.int32, (tm, tm), 0)
                    > lax.broadcasted_iota(jnp.int32, (tm, tm), 1), 1.0, 0.0).astype(BF16)
    before = _dot(tri, cnt.astype(BF16)) + carry[...]
    r1 = jnp.sum(oh1 * before, axis=-1, keepdims=True)
    r2 = jnp.sum(oh2 * before, axis=-1, keepdims=True)
    carry[...] = carry[...] + jnp.sum(cnt, axis=0, keepdims=True)
    cnt_ref[...] = carry[...]
    vals = (e1, e2, w1, w2, r1, r2)
    rt = jnp.zeros(logits.shape, F32)
    for j, val in enumerate(vals):
        rt = jnp.where(lane == j, val, rt)
    rt_ref[0] = rt


def _outproj(x, ya, yb, o_f, o_b, rg, gn_g, ym, w_out_p, mod, nbatch, norm_g, w_route, b_route):
    B, S, D = x.shape
    nt = S // TM
    tok = lambda w: pl.BlockSpec((1, TM, w), lambda b, i: (b, i, 0))
    full = lambda a: pl.BlockSpec(a.shape, lambda b, i: (0,) * a.ndim)
    return pl.pallas_call(
        _outproj_kernel,
        grid=(B, nt),
        in_specs=[tok(D), tok(256), tok(256), tok(256), tok(256), tok(256), full(gn_g), tok(256),
                  full(w_out_p),
                  _mod_spec(2, nbatch), _mod_spec(3, nbatch), _mod_spec(4, nbatch), full(norm_g),
                  full(w_route), full(b_route)],
        out_specs=[tok(D), tok(D), tok(LANES), pl.BlockSpec((1, LANES), lambda b, i: (0, 0))],
        out_shape=[jax.ShapeDtypeStruct((B, S, D), F32), jax.ShapeDtypeStruct((B, S, D), F32),
                   jax.ShapeDtypeStruct((B, S, LANES), F32), jax.ShapeDtypeStruct((1, LANES), F32)],
        scratch_shapes=[pltpu.VMEM((1, LANES), F32)],
        compiler_params=_cparams(("arbitrary", "arbitrary")),
        name="outproj_router",
    )(x, ya, yb, o_f, o_b, rg, gn_g, ym, w_out_p, mod, mod, mod, norm_g, w_route, b_route)


def _dispatch_kernel(dest_ref, h_ref, xb_in_ref, xb_ref, sem):
    del xb_in_ref
    base = pl.program_id(0) * TM

    def copy(r, k):
        d = dest_ref[(base + r) * 2 + k]
        return pltpu.make_async_copy(h_ref.at[pl.ds(r, 1)], xb_ref.at[pl.ds(d, 1)], sem)

    def issue(r, c):
        copy(r, 0).start()
        copy(r, 1).start(priority=1)
        return c

    def drain(r, c):
        copy(r, 0).wait()
        copy(r, 1).wait()
        return c

    lax.fori_loop(0, TM, issue, 0, unroll=DMA_UNROLL)
    lax.fori_loop(0, TM, drain, 0, unroll=DMA_UNROLL)


def _dispatch(dest, h, slots):
    n, d = h.shape
    cap = slots.shape[0]
    return pl.pallas_call(
        _dispatch_kernel,
        grid_spec=pltpu.PrefetchScalarGridSpec(
            num_scalar_prefetch=1,
            grid=(n // TM,),
            in_specs=[pl.BlockSpec((TM, d), lambda i, dest: (i, 0)),
                      pl.BlockSpec(memory_space=pl.ANY)],
            out_specs=pl.BlockSpec(memory_space=pl.ANY),
            scratch_shapes=[pltpu.SemaphoreType.DMA(())]),
        out_shape=jax.ShapeDtypeStruct((cap, d), h.dtype),
        input_output_aliases={2: 0},
        compiler_params=_cparams(("arbitrary",)),
        name="moe_dispatch",
    )(dest, h, slots)


def _expert_kernel(be_ref, nu_ref, x_ref, wg_ref, wu_ref, wd_ref, y_ref, wg_s, wu_s, wd_s):
    i = pl.program_id(0)

    @pl.when((i == 0) | (be_ref[i] != be_ref[jnp.maximum(i - 1, 0)]))
    def _():
        wg_s[...] = wg_ref[0].astype(BF16)
        wu_s[...] = wu_ref[0].astype(BF16)
        wd_s[...] = wd_ref[0].astype(BF16)

    @pl.when(i < nu_ref[0])
    def _():
        xb = x_ref[...].astype(BF16)
        gate = _dot(xb, wg_s[...])
        up = _dot(xb, wu_s[...])
        hid = (gate * jax.nn.sigmoid(gate) * up).astype(BF16)
        y_ref[...] = _dot(hid, wd_s[...])

    @pl.when(i >= nu_ref[0])
    def _():
        y_ref[...] = jnp.zeros(y_ref.shape, y_ref.dtype)


def _experts(block_e, n_used, xb, w_gate, w_up, w_down, layer):
    cap, d = xb.shape
    hdim = w_gate.shape[-1]
    return pl.pallas_call(
        _expert_kernel,
        grid_spec=pltpu.PrefetchScalarGridSpec(
            num_scalar_prefetch=2,
            grid=(cap // MOE_ROWS,),
            in_specs=[pl.BlockSpec((MOE_ROWS, d), lambda i, be, nu: (i, 0)),
                      pl.BlockSpec((None, 1, d, hdim), lambda i, be, nu: (layer, be[i], 0, 0)),
                      pl.BlockSpec((None, 1, d, hdim), lambda i, be, nu: (layer, be[i], 0, 0)),
                      pl.BlockSpec((None, 1, hdim, d), lambda i, be, nu: (layer, be[i], 0, 0))],
            out_specs=pl.BlockSpec((MOE_ROWS, d), lambda i, be, nu: (i, 0)),
            scratch_shapes=[pltpu.VMEM((d, hdim), BF16), pltpu.VMEM((d, hdim), BF16),
                            pltpu.VMEM((hdim, d), BF16)]),
        out_shape=jax.ShapeDtypeStruct((cap, d), F32),
        compiler_params=_cparams(("arbitrary",)),
        name="moe_experts",
    )(block_e, n_used, xb, w_gate, w_up, w_down)


def _combine_kernel(dest_ref, x_ref, rt_ref, g2_ref, fg_ref, yb_ref, o_ref, gath, sem, *,
                    tile_off, tiles_per_batch, final):
    b, i = pl.program_id(0), pl.program_id(1)
    base = (b * tiles_per_batch + i + tile_off) * TM

    def copy(r, k):
        d = dest_ref[(base + r) * 2 + k]
        return pltpu.make_async_copy(yb_ref.at[pl.ds(d, 1)], gath.at[k, pl.ds(r, 1)], sem)

    def issue(r, c):
        copy(r, 0).start()
        copy(r, 1).start(priority=1)
        return c

    def drain(r, c):
        copy(r, 0).wait()
        copy(r, 1).wait()
        return c

    lax.fori_loop(0, TM, issue, 0, unroll=DMA_UNROLL)
    lax.fori_loop(0, TM, drain, 0, unroll=DMA_UNROLL)
    rt = rt_ref[0]
    m = rt[:, 2:3] * gath[0] + rt[:, 3:4] * gath[1]
    x = x_ref[0] + g2_ref[...] * m
    if final:
        x = _rms(x, fg_ref[...])
    o_ref[0] = x


def _combine(dest, x, route, mod, nbatch, final_g, yb, final):
    B, S, D = x.shape
    nt = S // TM
    off = 1 if final else 0
    g2_spec = pl.BlockSpec((None, 1, D_MODEL),
                           lambda b, i, dest: (jnp.where(i + off == 0, nbatch, b), 0, 5))
    return pl.pallas_call(
        functools.partial(_combine_kernel, tile_off=off, tiles_per_batch=nt, final=final),
        grid_spec=pltpu.PrefetchScalarGridSpec(
            num_scalar_prefetch=1,
            grid=(B, nt - off),
            in_specs=[pl.BlockSpec((1, TM, D), lambda b, i, dest: (b, i + off, 0)),
                      pl.BlockSpec((1, TM, LANES), lambda b, i, dest: (b, i + off, 0)),
                      g2_spec,
                      pl.BlockSpec((1, D), lambda b, i, dest: (0, 0)),
                      pl.BlockSpec(memory_space=pl.ANY)],
            out_specs=pl.BlockSpec((1, TM, D), lambda b, i, dest: (b, i, 0)),
            scratch_shapes=[pltpu.VMEM((2, TM, D), F32), pltpu.SemaphoreType.DMA(())]),
        out_shape=jax.ShapeDtypeStruct((B, S - off * TM, D), F32),
        compiler_params=_cparams(("arbitrary", "arbitrary")),
        name="moe_combine",
    )(dest, x, route, mod, final_g, yb)


def _rope_tables(seq):
    n_rows = seq // GRID_W
    pos = jnp.arange(max(n_rows, GRID_W), dtype=F32)[:, None]
    lane = jnp.arange(LANES, dtype=jnp.int32)
    tabs = []
    for r in (64, 32):
        nf = r // 4
        inv = 1.0 / (ROPE_BASE ** (jnp.arange(nf, dtype=F32) / nf))
        ang = pos * inv[lane % nf][None, :]
        first = ((lane % (2 * nf)) < nf)[None, :]
        by_row = ((lane % r) < 2 * nf)[None, None, :]
        for small in (jnp.cos(ang), jnp.where(first, -jnp.sin(ang), jnp.sin(ang))):
            full = jnp.where(by_row, small[:n_rows, None, :], small[None, :GRID_W, :])
            tabs.append(full.reshape(seq, LANES))
    ident = (jnp.ones((CTX_LEN, LANES), F32), jnp.zeros((CTX_LEN, LANES), F32))
    return [jnp.concatenate([ident[j % 2], tab], axis=0) for j, tab in enumerate(tabs)]


def _layer_weights(w_in, w_out, w_uq, w_ukv, w_group, b_group, w_expert, b_expert):
    perm = jnp.array([0, 2, 1, 3])
    wq = w_in[:, :256].reshape(D_MODEL, 4, 64)[:, perm].reshape(D_MODEL, 256)
    w_in_p = jnp.concatenate([wq, w_in[:, 256:], jnp.zeros((D_MODEL, D_IN_PAD - w_in.shape[1]), F32)],
                             axis=1).astype(BF16)
    wo = jnp.concatenate([w_out[:256].reshape(4, 64, D_MODEL)[perm].reshape(256, D_MODEL), w_out[256:]],
                         axis=0).astype(BF16)
    uq = w_uq.reshape(256, 4, 96)
    blocks = []
    for h in range(4):
        blk = jnp.zeros((256, 256), F32)
        blk = blk.at[:, 64 * (h % 2):64 * (h % 2) + 64].set(uq[:, h, :64])
        blk = blk.at[:, LANES:LANES + 32].set(uq[:, h, 64:])
        blocks.append(blk)
    wuq = jnp.concatenate(blocks, axis=1).astype(BF16)
    ukv = w_ukv.reshape(128, 4, 128)
    wukk = ukv[:, :, :64].reshape(128, 256).astype(BF16)
    wukv = ukv[:, :, 64:].reshape(128, 256).astype(BF16)
    pad = LANES - MOE_GROUPS - MOE_EXPERTS
    w_route = jnp.concatenate([w_group, w_expert, jnp.zeros((D_MODEL, pad), F32)], axis=1)
    w_hi = w_route.astype(BF16)
    w_route = jnp.stack([w_hi, (w_route - w_hi.astype(F32)).astype(BF16)])
    b_route = jnp.concatenate([b_group, b_expert, jnp.zeros((pad,), F32)]).reshape(1, LANES)
    return w_in_p, wo, wuq, wukk, wukv, w_route, b_route


def _moe_plan(route, counts, cap):
    e = route[:, :2].astype(jnp.int32)
    rank = route[:, 4:6].astype(jnp.int32)
    cnt = counts[0, :MOE_EXPERTS].astype(jnp.int32)
    padded = (cnt + MOE_ROWS - 1) // MOE_ROWS * MOE_ROWS
    pend = jnp.cumsum(padded)
    pstart = pend - padded
    dest = (pstart[e] + rank).reshape(-1)
    nblk = cap // MOE_ROWS
    blk_start = jnp.arange(nblk, dtype=jnp.int32) * MOE_ROWS
    block_e = jnp.minimum(jnp.sum((pend[None, :] <= blk_start[:, None]).astype(jnp.int32), axis=1),
                          MOE_EXPERTS - 1)
    n_used = (pend[-1:] // MOE_ROWS).astype(jnp.int32)
    return dest, block_e, n_used


def kernel(x, c, ctx, c_ctx, w_mod, b_mod, norm1_g, norm2_g, w_in, w_out, swa_sink, diff_lambda, diff_subln_g, ret_decay_logit, ret_gn_g, mla_q_norm_g, mla_w_uq, mla_kv_norm_g, mla_w_ukv, moe_w_group, moe_b_group, moe_w_expert, moe_b_expert, moe_w_gate, moe_w_up, moe_w_down, final_norm_g):
    B, T, D = x.shape
    S = CTX_LEN + T
    n_tok = B * S
    diff_tk, mla_tk = min(DIFF_TK, T), min(MLA_TK, T)
    cap = -(-(2 * n_tok + MOE_EXPERTS * (MOE_ROWS - 1)) // MOE_ROWS) * MOE_ROWS
    xs = jnp.concatenate([ctx, x], axis=1)
    cvec = jnp.zeros((8, D), F32).at[:B].set(c).at[B].set(c_ctx)
    tabs = _rope_tables(T)
    row = lambda v: v.reshape(1, -1)

    for l in range(DEPTH):
        lambda_init = 0.8 - 0.6 * math.exp(-0.3 * l)
        w_in_p, wo, wuq, wukk, wukv, w_route, b_route = _layer_weights(
            w_in[l], w_out[l], mla_w_uq[l], mla_w_ukv[l], moe_w_group[l], moe_b_group[l],
            moe_w_expert[l], moe_b_expert[l])
        mod = _modulation(cvec, w_mod[l], b_mod[l]).reshape(8, 1, 6 * D)
        (swaq, swak, swav, dq, dk, dv, rq, rk, rv, rg, mq, mk, mv) = _inproj(
            xs, mod, B, row(norm1_g[l]), w_in_p, tabs, row(mla_q_norm_g[l]), wuq,
            row(mla_kv_norm_g[l]), wukk, wukv)
        sink_p = swa_sink[l]
        ya = _swa(sink_p, swaq, swak, swav)
        yb = _diff(diff_lambda[l], diff_subln_g[l].reshape(64, 1), dq, dk, dv, lambda_init,
                   diff_tk)
        dl_lane = jnp.repeat(ret_decay_logit[l], 64, axis=-1).reshape(2, 2, 1, LANES)
        o_f, o_b = _retention(dl_lane, rq, rk, rv)
        ym = _mla(mq, mk, mv, mla_tk)
        xs, h2, route, counts = _outproj(xs, ya, yb, o_f, o_b, rg, row(ret_gn_g[l]), ym, wo, mod, B,
                                         row(norm2_g[l]), w_route, b_route)
        dest, block_e, n_used = _moe_plan(route.reshape(n_tok, LANES), counts, cap)
        xb = _dispatch(dest, h2.reshape(n_tok, D), jnp.zeros((cap, D), F32) if l == 0 else xb)
        yb_e = _experts(block_e, n_used, xb, moe_w_gate, moe_w_up, moe_w_down, l)
        final = l == DEPTH - 1
        xs = _combine(dest, xs, route, mod, B, row(final_norm_g), yb_e, final)
    return xs
```

```python
import functools
import math

import jax
import jax.numpy as jnp
from jax import lax
from jax.experimental import pallas as pl
from jax.experimental.pallas import tpu as pltpu

F32 = jnp.float32
BF16 = jnp.bfloat16

D_MODEL = 1024
CTX_LEN = 256
GRID_W = 64
ROPE_BASE = 10000.0
NORM_EPS = 1e-6
NEG_INF = -1e30
DEPTH = 2

SWA_WINDOW = 128
RET_CHUNK = 128
MLA_SCALE = (64 + 32) ** -0.5
LOG2E = math.log2(math.e)
MOE_GROUPS = 4
MOE_PER_GROUP = 8
MOE_EXPERTS = 32
MOE_HIDDEN = 512

LANES = 128
TM = 256
DIFF_TQ = 2048
MLA_TQ = 2048
DIFF_TK = 1024
MLA_TK = 2048
VT_ROWS = 80
GAP_LIMIT = 64.0
MOE_ROWS = 1024
DMA_UNROLL = 8
VMEM_LIMIT = 56 * 1024 * 1024

C_SWA_Q, C_SWA_K, C_SWA_V = 0, 256, 384
C_DIF_Q, C_DIF_K, C_DIF_V = 512, 768, 1024
C_RET_Q, C_RET_K, C_RET_V, C_RET_G = 1280, 1536, 1792, 2048
C_MLA_Q, C_MLA_KV, C_MLA_KR = 2304, 2560, 2688
D_IN_PAD = 2816


def _cparams(sem):
    return pltpu.CompilerParams(dimension_semantics=sem, vmem_limit_bytes=VMEM_LIMIT)


def _dot(a, b):
    return jnp.dot(a, b, preferred_element_type=F32)


def _dot_nt(a, b):
    return lax.dot_general(a, b, (((1,), (1,)), ((), ())), preferred_element_type=F32)


def _dot_tn(a, b):
    return lax.dot_general(a, b, (((0,), (0,)), ((), ())), preferred_element_type=F32)


def _rms(x, g):
    ms = jnp.mean(x * x, axis=-1, keepdims=True)
    return x * lax.rsqrt(ms + NORM_EPS) * g


def _lane(shape):
    return lax.broadcasted_iota(jnp.int32, shape, len(shape) - 1)


def _rope(z, cos, sin, nf):
    first = (_lane(z.shape) % (2 * nf)) < nf
    partner = jnp.where(first, pltpu.roll(z, LANES - nf, 1), pltpu.roll(z, nf, 1))
    return z * cos + partner * sin


def _mod_kernel(a_ref, w_ref, b_ref, o_ref):
    a = a_ref[...]
    act = a * jax.nn.sigmoid(a)
    o_ref[...] = jnp.dot(act, w_ref[...], preferred_element_type=F32,
                         precision=lax.Precision.HIGHEST) + b_ref[...]


def _modulation(cvec, w_mod, b_mod):
    n = w_mod.shape[1]
    bn = 512
    return pl.pallas_call(
        _mod_kernel,
        grid=(n // bn,),
        in_specs=[pl.BlockSpec((8, D_MODEL), lambda j: (0, 0)),
                  pl.BlockSpec((D_MODEL, bn), lambda j: (0, j)),
                  pl.BlockSpec((1, bn), lambda j: (0, j))],
        out_specs=pl.BlockSpec((8, bn), lambda j: (0, j)),
        out_shape=jax.ShapeDtypeStruct((8, n), F32),
        compiler_params=_cparams(("parallel",)),
        name="modulation",
    )(cvec, w_mod, b_mod.reshape(1, n))


def _mod_spec(chunk, nbatch):
    return pl.BlockSpec((None, 1, D_MODEL),
                        lambda b, i: (jnp.where(i == 0, nbatch, b), 0, chunk))


def _inproj_kernel(x_ref, g_ref, sh_ref, sc_ref, w_ref, c64_ref, s64_ref, c32_ref, s32_ref,
                   qng_ref, wuq_ref, kvng_ref, wukk_ref, wukv_ref,
                   swaq, swak, swav, dq, dk, dv, rq, rk, rv, rg, mq, mk, mv):
    x = x_ref[0]
    h = _rms(x, g_ref[...]) * (1.0 + sc_ref[...]) + sh_ref[...]
    hb = h.astype(BF16)
    c64, s64, c32, s32 = c64_ref[...], s64_ref[...], c32_ref[...], s32_ref[...]

    def proj(c0, width):
        return _dot(hb, w_ref[:, c0:c0 + width])

    def rope_blocks(z, out, nf, scale):
        cos, sin = (c64, s64) if nf == 16 else (c32, s32)
        for j in range(z.shape[1] // LANES):
            zz = _rope(z[:, j * LANES:(j + 1) * LANES], cos, sin, nf)
            if scale != 1.0:
                zz = zz * scale
            out[0, :, j * LANES:(j + 1) * LANES] = zz.astype(out.dtype)

    rope_blocks(proj(C_SWA_Q, 256), swaq, 16, 0.125)
    rope_blocks(proj(C_SWA_K, 128), swak, 16, 1.0)
    swav[0] = proj(C_SWA_V, 128).astype(BF16)
    def values_transposed(z, out):
        zt = z.T.astype(BF16)
        for hh in range(4):
            out[0, hh * VT_ROWS:hh * VT_ROWS + 64, :] = zt[hh * 64:(hh + 1) * 64]
            out[0, hh * VT_ROWS + 64:(hh + 1) * VT_ROWS, :] = jnp.ones((VT_ROWS - 64, zt.shape[1]), BF16)

    rope_blocks(proj(C_DIF_Q, 256), dq, 8, 32 ** -0.5 * LOG2E)
    rope_blocks(proj(C_DIF_K, 256), dk, 8, 1.0)
    values_transposed(proj(C_DIF_V, 256), dv)
    rope_blocks(proj(C_RET_Q, 256), rq, 16, 0.125)
    rope_blocks(proj(C_RET_K, 256), rk, 16, 1.0)
    rv[0] = proj(C_RET_V, 256).astype(BF16)
    rg[0] = proj(C_RET_G, 256).astype(BF16)

    ql = _rms(proj(C_MLA_Q, 256), qng_ref[...]).astype(BF16)
    qa = _dot(ql, wuq_ref[...]) * (MLA_SCALE * LOG2E)
    for hh in range(4):
        mq[0, :, hh * 256:hh * 256 + LANES] = qa[:, hh * 256:hh * 256 + LANES].astype(BF16)
        zr = _rope(qa[:, hh * 256 + LANES:(hh + 1) * 256], c32, s32, 8)
        mq[0, :, hh * 256 + LANES:(hh + 1) * 256] = zr.astype(BF16)
    kvl = _rms(proj(C_MLA_KV, 128), kvng_ref[...]).astype(BF16)
    kn = _dot(kvl, wukk_ref[...])
    kr = _rope(proj(C_MLA_KR, 128), c32, s32, 8).astype(BF16)
    for p in range(2):
        mk[0, :, p * 256:p * 256 + LANES] = kn[:, p * LANES:(p + 1) * LANES].astype(BF16)
        mk[0, :, p * 256 + LANES:(p + 1) * 256] = kr
    values_transposed(_dot(kvl, wukv_ref[...]), mv)


def _inproj(x, mod, nbatch, norm_g, w_in_p, tabs, qng, wuq, kvng, wukk, wukv):
    B, S, D = x.shape
    nt = S // TM
    tok = lambda w: pl.BlockSpec((1, TM, w), lambda b, i: (b, i, 0))
    full = lambda a: pl.BlockSpec(a.shape, lambda b, i: (0,) * a.ndim)
    tab = pl.BlockSpec((TM, LANES), lambda b, i: (i, 0))
    widths = [256, 128, 128, 256, 256, None, 256, 256, 256, 256, 1024, 512, None]
    vt_spec = pl.BlockSpec((1, 4 * VT_ROWS, TM), lambda b, i: (b, 0, i))
    vt_shape = jax.ShapeDtypeStruct((B, 4 * VT_ROWS, S), BF16)
    return pl.pallas_call(
        _inproj_kernel,
        grid=(B, nt),
        in_specs=[tok(D), full(norm_g), _mod_spec(0, nbatch), _mod_spec(1, nbatch), full(w_in_p),
                  tab, tab, tab, tab, full(qng), full(wuq), full(kvng), full(wukk), full(wukv)],
        out_specs=[vt_spec if w is None else tok(w) for w in widths],
        out_shape=[vt_shape if w is None else jax.ShapeDtypeStruct((B, S, w), BF16) for w in widths],
        compiler_params=_cparams(("parallel", "parallel")),
        name="inproj",
    )(x, norm_g, mod, mod, w_in_p, *tabs, qng, wuq, kvng, wukk, wukv)


def _swa_kernel(sink_ref, q_ref, kp_ref, ko_ref, kn_ref, vp_ref, vo_ref, vn_ref, kc_ref, vc_ref,
                o_ref, *, n_ctx_tiles, seq):
    i = pl.program_id(1)
    blk = RET_CHUNK
    kb = [kp_ref[0], ko_ref[0, 0:blk], ko_ref[0, blk:2 * blk], kn_ref[0]]
    vb = [vp_ref[0], vo_ref[0, 0:blk], vo_ref[0, blk:2 * blk], vn_ref[0]]
    nk = 3 * blk + CTX_LEN
    r_io = lax.broadcasted_iota(jnp.int32, (blk, nk), 0)
    c_io = lax.broadcasted_iota(jnp.int32, (blk, nk), 1)
    lane = _lane((blk, LANES))
    chains = [(s, r, g) for s in range(2) for r in range(2) for g in range(2)]
    k_all, v_all, valid = [], [], []
    for s in range(2):
        n = 2 * (i - n_ctx_tiles) + s
        kpos = (n - 1) * blk + c_io
        in_band = ((c_io >= r_io) & (c_io <= r_io + 2 * SWA_WINDOW) & (kpos >= 0) & (kpos < seq)
                   & (n >= 0))
        valid.append(in_band | (c_io >= 3 * blk))
        k_all.append(jnp.concatenate(kb[s:s + 3] + [kc_ref[0]], axis=0))
        v_all.append(jnp.concatenate(vb[s:s + 3] + [vc_ref[0]], axis=0))
    qm = [jnp.where((lane >= 64 * g) & (lane < 64 * (g + 1)),
                    q_ref[0, s * blk:(s + 1) * blk, r * LANES:(r + 1) * LANES],
                    jnp.zeros((blk, LANES), BF16)) for s, r, g in chains]
    sc = [jnp.where(valid[c[0]], _dot_nt(q, k_all[c[0]]), NEG_INF) for q, c in zip(qm, chains)]
    sink = [sink_ref[2 * g + r] for s, r, g in chains]
    m = [jnp.maximum(jnp.max(x, axis=-1, keepdims=True), sk) for x, sk in zip(sc, sink)]
    p = [jnp.exp(x - mm) for x, mm in zip(sc, m)]
    l = [jnp.sum(pp, axis=-1, keepdims=True) + jnp.exp(sk - mm) for pp, sk, mm in zip(p, sink, m)]
    o = [_dot(pp.astype(BF16), v_all[c[0]]) / ll for pp, ll, c in zip(p, l, chains)]
    for s in range(2):
        for r in range(2):
            j = 4 * s + 2 * r
            o_ref[0, s * blk:(s + 1) * blk, r * LANES:(r + 1) * LANES] = (
                jnp.where(lane < 64, o[j], o[j + 1]).astype(BF16))


def _swa(sink, q, k, v):
    B, S, _ = q.shape
    blk = RET_CHUNK
    nb = S // blk
    nt = S // (2 * blk)
    edge = lambda f: pl.BlockSpec((1, blk, LANES), f)
    prev = lambda b, i: (b, jnp.maximum(2 * i - 1, 0), 0)
    own = lambda b, i: (b, i, 0)
    nxt = lambda b, i: (b, jnp.minimum(2 * i + 2, nb - 1), 0)
    own_kv = pl.BlockSpec((1, 2 * blk, LANES), own)
    ctx = pl.BlockSpec((1, CTX_LEN, LANES), lambda b, i: (b, 0, 0))
    return pl.pallas_call(
        functools.partial(_swa_kernel, n_ctx_tiles=CTX_LEN // (2 * blk), seq=S - CTX_LEN),
        grid=(B, nt),
        in_specs=[pl.BlockSpec(memory_space=pltpu.SMEM),
                  pl.BlockSpec((1, 2 * blk, 256), own),
                  edge(prev), own_kv, edge(nxt), edge(prev), own_kv, edge(nxt), ctx, ctx],
        out_specs=pl.BlockSpec((1, 2 * blk, 256), own),
        out_shape=jax.ShapeDtypeStruct((B, S, 256), BF16),
        compiler_params=_cparams(("parallel", "parallel")),
        name="swa",
    )(sink, q, k, k, k, v, v, v, k, v)


def _flash(q_maps, v_blk, k_ref, v_ref, m_scr, acc_scr, gap_scr, *, n_lat, tk):
    nm = len(q_maps)
    qt_maps = [q.astype(F32).T.astype(BF16) for q in q_maps]

    def chunk(start, size, mode):
        kc = k_ref[0, pl.ds(start, size), :]
        for a in range(nm):
            vt = v_ref[0, v_blk[a] * VT_ROWS:(v_blk[a] + 1) * VT_ROWS, pl.ds(start, size)]
            st = _dot(kc, qt_maps[a])
            m_prev = None if mode == "first" else m_scr[a]
            if mode == "fast":
                pf = jnp.exp2(st - m_prev)
                rise = jnp.maximum(jnp.max(pf, axis=0, keepdims=True), 1.0)
                acc_scr[a] = (acc_scr[a] + _dot(vt, pf.astype(BF16))) * (1.0 / rise)
                m_scr[a] = m_prev + jnp.log2(rise)
                gap_scr[a] = jnp.maximum(gap_scr[a], rise)
                continue
            cm = jnp.max(st, axis=0, keepdims=True)
            if mode == "first":
                acc_scr[a] = _dot(vt, jnp.exp2(st - cm).astype(BF16))
                m_scr[a] = cm
                gap_scr[a] = jnp.ones(cm.shape, F32)
            else:
                m_new = jnp.maximum(m_prev, cm)
                p = jnp.exp2(st - m_new).astype(BF16)
                acc_scr[a] = acc_scr[a] * jnp.exp2(m_prev - m_new) + _dot(vt, p)
                m_scr[a] = m_new

    def sweep(mode):
        chunk(0, CTX_LEN, "first")
        if n_lat:

            def body(c, carry):
                chunk(pl.multiple_of(CTX_LEN + c * tk, LANES), tk, mode)
                return carry

            lax.fori_loop(0, n_lat, body, 0)

    sweep("fast")
    if n_lat:
        worst = jnp.max(jnp.concatenate([gap_scr[a] for a in range(nm)], axis=0))

        @pl.when(worst > 2.0 ** GAP_LIMIT)
        def _():
            sweep("exact")

    return [acc_scr[a, 0:64, :] / acc_scr[a, 64:65, :] for a in range(nm)]


def _attention_calls(kern, name, nm, q, k, vt, qw, kw, tk, tq, extra_in=(), extra_specs=()):
    B, S, _ = q.shape
    T = S - CTX_LEN

    def call(tq, n_tiles, n_lat, q_spec, kv_rows, call_name):
        return pl.pallas_call(
            functools.partial(kern, n_lat=n_lat, tk=tk),
            grid=(B, 2, n_tiles),
            in_specs=list(extra_specs) + [
                q_spec,
                pl.BlockSpec((1, kv_rows, kw), lambda b, p, i: (b, 0, p)),
                pl.BlockSpec((1, 2 * VT_ROWS, kv_rows), lambda b, p, i: (b, p, 0))],
            out_specs=pl.BlockSpec((1, tq, LANES), lambda b, p, i: (b, i, p)),
            out_shape=jax.ShapeDtypeStruct((B, tq * n_tiles, 256), BF16),
            scratch_shapes=[pltpu.VMEM((nm, 1, tq), F32), pltpu.VMEM((nm, VT_ROWS, tq), F32),
                            pltpu.VMEM((nm, 1, tq), F32)],
            compiler_params=_cparams(("parallel", "parallel", "parallel")),
            name=call_name,
        )(*extra_in, q, k, vt)

    y_ctx = call(CTX_LEN, 1, 0, pl.BlockSpec((1, CTX_LEN, qw), lambda b, p, i: (b, 0, p)),
                 CTX_LEN, name + "_ctx")
    tq = min(tq, T)
    y_lat = call(tq, T // tq, T // tk,
                 pl.BlockSpec((pl.Element(1), pl.Element(tq), pl.Element(qw)),
                              lambda b, p, i: (b, (i * (tq // CTX_LEN) + 1) * CTX_LEN, p * qw)),
                 S, name)
    return jnp.concatenate([y_ctx, y_lat], axis=1)


def _diff_attn(lam_ref, g_ref, q_ref, k_ref, v_ref, o_ref, m_scr, acc_scr, gap_scr, *,
               n_lat, tk, lambda_init):
    qb = q_ref[0]
    lane = _lane(qb.shape)
    q_maps = []
    for a in range(4):
        msk = (lane >= 32 * a) & (lane < 32 * (a + 1))
        q_maps.append(jnp.where(msk, qb, jnp.zeros_like(qb)))
    n = _flash(q_maps, (0, 0, 1, 1), k_ref, v_ref, m_scr, acc_scr, gap_scr, n_lat=n_lat, tk=tk)
    lp = lam_ref[...]
    lam = (jnp.exp(jnp.sum(lp[0:1] * lp[1:2], axis=-1, keepdims=True))
           - jnp.exp(jnp.sum(lp[2:3] * lp[3:4], axis=-1, keepdims=True)) + lambda_init)
    ys = []
    for hh in range(2):
        o = n[2 * hh] - lam * n[2 * hh + 1]
        ms = jnp.mean(o * o, axis=0, keepdims=True)
        ys.append((o * lax.rsqrt(ms + NORM_EPS) * g_ref[...] * (1.0 - lambda_init)).T)
    o_ref[0] = jnp.concatenate(ys, axis=1).astype(BF16)


def _diff(lam_p, subln_g2, q, k, v, lambda_init, tk):
    return _attention_calls(
        functools.partial(_diff_attn, lambda_init=lambda_init), "diff_attn", 4, q, k, v, LANES, LANES,
        tk, DIFF_TQ, extra_in=(lam_p, subln_g2),
        extra_specs=(pl.BlockSpec(lam_p.shape, lambda b, p, i: (0, 0)),
                     pl.BlockSpec((64, 1), lambda b, p, i: (0, 0))))


def _mla_attn(q_ref, k_ref, v_ref, o_ref, m_scr, acc_scr, gap_scr, *, n_lat, tk):
    q_maps = [q_ref[0, :, 0:256], q_ref[0, :, 256:512]]
    n = _flash(q_maps, (0, 1), k_ref, v_ref, m_scr, acc_scr, gap_scr, n_lat=n_lat, tk=tk)
    o_ref[0] = jnp.concatenate([n[0].T, n[1].T], axis=1).astype(BF16)


def _mla(q, k, v, tk):
    return _attention_calls(_mla_attn, "mla_attn", 2, q, k, v, 512, 256, tk, MLA_TQ)


def _ret_tables(dl, backward):
    L = RET_CHUNK
    lg = -(jnp.maximum(-dl, 0.0) + jnp.log(1.0 + jnp.exp(-jnp.abs(dl))))
    row = lax.broadcasted_iota(jnp.int32, (L, LANES), 0).astype(F32)
    col = lax.broadcasted_iota(jnp.int32, (L, LANES), 1)
    colf = col.astype(F32)
    if backward:
        qd, kd, rel = jnp.exp((L - row) * lg), jnp.exp(row * lg), colf - row
    else:
        qd, kd, rel = jnp.exp((row + 1.0) * lg), jnp.exp((L - 1.0 - row) * lg), row - colf
    decs = []
    for hh in range(2):
        lg_h = jnp.sum(jnp.where(col[0:1] == 64 * hh, lg, 0.0), axis=-1, keepdims=True)
        decs.append(jnp.where(rel >= 0, jnp.exp(jnp.maximum(rel, 0.0) * lg_h), 0.0))
    return qd, kd, decs[0], decs[1], jnp.exp(float(L) * lg)


def _ret_chunks(chains):
    col = _lane((RET_CHUNK, LANES))
    lo = col < 64
    rowi = lax.broadcasted_iota(jnp.int32, (LANES, LANES), 0)
    coli = lax.broadcasted_iota(jnp.int32, (LANES, LANES), 1)
    diag = (rowi < 64) == (coli < 64)
    inter = [_dot((q.astype(F32) * tab[0]).astype(BF16), s_ref[...].astype(BF16))
             for q, k, v, tab, cd, s_ref, store in chains]
    inc = [_dot_tn((k.astype(F32) * tab[1]).astype(BF16), v) for q, k, v, tab, cd, s_ref, store in chains]
    att = [[_dot_nt(jnp.where(lo == (hh == 0), q, jnp.zeros_like(q)), k) * tab[2 + hh] for hh in range(2)]
           for q, k, v, tab, cd, s_ref, store in chains]
    intra = [[_dot(a.astype(BF16), c[2]) for a in pair] for pair, c in zip(att, chains)]
    for c, o_inter, o_intra, s_inc in zip(chains, inter, intra, inc):
        c[6](o_inter + jnp.where(lo, o_intra[0], o_intra[1]))
        c[5][...] = jnp.where(diag, c[5][...] * c[4] + s_inc, 0.0)


def _ret_kernel(dl_ref, qf_ref, kf_ref, vf_ref, qb_ref, kb_ref, vb_ref, of_ref, ob_ref,
                s_scr, tab_scr, cd_scr):
    nb = qf_ref.shape[0]

    @pl.when(pl.program_id(0) == 0)
    def _():
        s_scr[...] = jnp.zeros(s_scr.shape, F32)
        for d in range(2):
            for p in range(2):
                tabs = _ret_tables(dl_ref[d, p], backward=(d == 1))
                for t in range(4):
                    tab_scr[d, p, t] = tabs[t]
                cd_scr[d, p] = tabs[4]

    chains = []
    for d, (q_ref, k_ref, v_ref, o_ref) in enumerate(((qf_ref, kf_ref, vf_ref, of_ref),
                                                      (qb_ref, kb_ref, vb_ref, ob_ref))):
        for p in range(2):
            tab = [tab_scr[d, p, t] for t in range(4)]
            cd = cd_scr[d, p]
            ln = slice(p * LANES, (p + 1) * LANES)
            for b in range(nb):
                def store(o, o_ref=o_ref, b=b, ln=ln):
                    o_ref[b, :, ln] = o
                chains.append((q_ref[b, :, ln], k_ref[b, :, ln], v_ref[b, :, ln], tab, cd,
                               s_scr.at[d, b, p], store))
    _ret_chunks(chains)


def _retention(dl_lane, q, k, v):
    B, S, _ = q.shape
    L = RET_CHUNK
    nc = S // L
    ncc = CTX_LEN // L
    fwd = pl.BlockSpec((B, L, 256), lambda s: (0, s, 0))
    bwd = pl.BlockSpec((B, L, 256), lambda s: (0, jnp.where(s < ncc, ncc - 1 - s, nc - 1 + ncc - s), 0))
    return pl.pallas_call(
        _ret_kernel,
        grid=(nc,),
        in_specs=[pl.BlockSpec(dl_lane.shape, lambda s: (0, 0, 0, 0)), fwd, fwd, fwd, bwd, bwd, bwd],
        out_specs=[fwd, bwd],
        out_shape=[jax.ShapeDtypeStruct((B, S, 256), F32)] * 2,
        scratch_shapes=[pltpu.VMEM((2, B, 2, LANES, LANES), F32),
                        pltpu.VMEM((2, 2, 4, L, LANES), F32), pltpu.VMEM((2, 2, 1, LANES), F32)],
        compiler_params=_cparams(("arbitrary",)),
        name="retention",
    )(dl_lane, q, k, v, q, k, v)


def _gated_group_norm(o, g, gn):
    lo = _lane(o.shape) < 64

    def halves(t):
        a = jnp.sum(jnp.where(lo, t, 0.0), axis=-1, keepdims=True) * (1.0 / 64)
        b = jnp.sum(jnp.where(lo, 0.0, t), axis=-1, keepdims=True) * (1.0 / 64)
        return jnp.where(lo, a, b)

    d = o - halves(o)
    var = halves(d * d)
    return d * lax.rsqrt(var + NORM_EPS) * gn * (g * jax.nn.sigmoid(g))


def _outproj_kernel(x_ref, ya_ref, yb_ref, of_ref, ob_ref, rg_ref, gn_ref, ym_ref, w_ref, g1_ref,
                    sh_ref, sc_ref, ng_ref, wr_ref, br_ref, xo_ref, h_ref, rt_ref, cnt_ref, carry):
    mix = (_dot(ya_ref[0], w_ref[0:256]) + _dot(yb_ref[0], w_ref[256:512])
           + _dot(ym_ref[0], w_ref[768:1024]))
    for j in range(2):
        ln = slice(j * LANES, (j + 1) * LANES)
        yr = _gated_group_norm(of_ref[0, :, ln] + ob_ref[0, :, ln], rg_ref[0, :, ln].astype(F32),
                               gn_ref[:, ln])
        mix = mix + _dot(yr.astype(BF16), w_ref[512 + j * LANES:512 + (j + 1) * LANES])
    x = x_ref[0] + g1_ref[...] * mix
    xo_ref[0] = x
    h = _rms(x, ng_ref[...]) * (1.0 + sc_ref[...]) + sh_ref[...]
    h_ref[0] = h
    h_hi = h.astype(BF16)
    h_lo = (h - h_hi.astype(F32)).astype(BF16)
    logits = (_dot(h_hi, wr_ref[0]) + _dot(h_lo, wr_ref[0]) + _dot(h_hi, wr_ref[1])
              + br_ref[...])
    lane = _lane(logits.shape).astype(F32)

    def first_argmax(vals, mask):
        mx = jnp.max(jnp.where(mask, vals, NEG_INF), axis=-1, keepdims=True)
        idx = jnp.min(jnp.where(mask & (vals == mx), lane, float(LANES)), axis=-1, keepdims=True)
        return mx, idx

    gmask = lane < MOE_GROUPS
    g_max, g_idx = first_argmax(logits, gmask)
    g_w = 1.0 / jnp.sum(jnp.exp(jnp.where(gmask, logits - g_max, NEG_INF)), axis=-1, keepdims=True)
    e_lo = MOE_GROUPS + g_idx * MOE_PER_GROUP
    emask = (lane >= e_lo) & (lane < e_lo + MOE_PER_GROUP)
    v1, i1 = first_argmax(logits, emask)
    v2, i2 = first_argmax(logits, emask & (lane != i1))
    t = jnp.exp(v2 - v1)
    w1 = g_w / (1.0 + t)
    w2 = w1 * t
    e1, e2 = i1 - MOE_GROUPS, i2 - MOE_GROUPS

    @pl.when((pl.program_id(0) == 0) & (pl.program_id(1) == 0))
    def _():
        carry[...] = jnp.zeros(carry.shape, F32)

    oh1 = jnp.where(lane == e1, 1.0, 0.0)
    oh2 = jnp.where(lane == e2, 1.0, 0.0)
    cnt = oh1 + oh2
    tm = cnt.shape[0]
    tri = jnp.where(lax.broadcasted_iota(jnp.int32, (tm, tm), 0)
                    > lax.broadcasted_iota(jnp.int32, (tm, tm), 1), 1.0, 0.0).astype(BF16)
    before = _dot(tri, cnt.astype(BF16)) + carry[...]
    r1 = jnp.sum(oh1 * before, axis=-1, keepdims=True)
    r2 = jnp.sum(oh2 * before, axis=-1, keepdims=True)
    carry[...] = carry[...] + jnp.sum(cnt, axis=0, keepdims=True)
    cnt_ref[...] = carry[...]
    vals = (e1, e2, w1, w2, r1, r2)
    rt = jnp.zeros(logits.shape, F32)
    for j, val in enumerate(vals):
        rt = jnp.where(lane == j, val, rt)
    rt_ref[0] = rt


def _outproj(x, ya, yb, o_f, o_b, rg, gn_g, ym, w_out_p, mod, nbatch, norm_g, w_route, b_route):
    B, S, D = x.shape
    nt = S // TM
    tok = lambda w: pl.BlockSpec((1, TM, w), lambda b, i: (b, i, 0))
    full = lambda a: pl.BlockSpec(a.shape, lambda b, i: (0,) * a.ndim)
    return pl.pallas_call(
        _outproj_kernel,
        grid=(B, nt),
        in_specs=[tok(D), tok(256), tok(256), tok(256), tok(256), tok(256), full(gn_g), tok(256),
                  full(w_out_p),
                  _mod_spec(2, nbatch), _mod_spec(3, nbatch), _mod_spec(4, nbatch), full(norm_g),
                  full(w_route), full(b_route)],
        out_specs=[tok(D), tok(D), tok(LANES), pl.BlockSpec((1, LANES), lambda b, i: (0, 0))],
        out_shape=[jax.ShapeDtypeStruct((B, S, D), F32), jax.ShapeDtypeStruct((B, S, D), F32),
                   jax.ShapeDtypeStruct((B, S, LANES), F32), jax.ShapeDtypeStruct((1, LANES), F32)],
        scratch_shapes=[pltpu.VMEM((1, LANES), F32)],
        compiler_params=_cparams(("arbitrary", "arbitrary")),
        name="outproj_router",
    )(x, ya, yb, o_f, o_b, rg, gn_g, ym, w_out_p, mod, mod, mod, norm_g, w_route, b_route)


def _dispatch_kernel(dest_ref, h_ref, xb_in_ref, xb_ref, sem):
    del xb_in_ref
    base = pl.program_id(0) * TM

    def copy(r, k):
        d = dest_ref[(base + r) * 2 + k]
        return pltpu.make_async_copy(h_ref.at[pl.ds(r, 1)], xb_ref.at[pl.ds(d, 1)], sem)

    def issue(r, c):
        copy(r, 0).start()
        copy(r, 1).start()
        return c

    def drain(r, c):
        copy(r, 0).wait()
        copy(r, 1).wait()
        return c

    lax.fori_loop(0, TM, issue, 0, unroll=DMA_UNROLL)
    lax.fori_loop(0, TM, drain, 0, unroll=DMA_UNROLL)


def _dispatch(dest, h, slots):
    n, d = h.shape
    cap = slots.shape[0]
    return pl.pallas_call(
        _dispatch_kernel,
        grid_spec=pltpu.PrefetchScalarGridSpec(
            num_scalar_prefetch=1,
            grid=(n // TM,),
            in_specs=[pl.BlockSpec((TM, d), lambda i, dest: (i, 0)),
                      pl.BlockSpec(memory_space=pl.ANY)],
            out_specs=pl.BlockSpec(memory_space=pl.ANY),
            scratch_shapes=[pltpu.SemaphoreType.DMA(())]),
        out_shape=jax.ShapeDtypeStruct((cap, d), h.dtype),
        input_output_aliases={2: 0},
        compiler_params=_cparams(("arbitrary",)),
        name="moe_dispatch",
    )(dest, h, slots)


def _expert_kernel(be_ref, nu_ref, x_ref, wg_ref, wu_ref, wd_ref, y_ref, wg_s, wu_s, wd_s):
    i = pl.program_id(0)

    @pl.when((i == 0) | (be_ref[i] != be_ref[jnp.maximum(i - 1, 0)]))
    def _():
        wg_s[...] = wg_ref[0].astype(BF16)
        wu_s[...] = wu_ref[0].astype(BF16)
        wd_s[...] = wd_ref[0].astype(BF16)

    @pl.when(i < nu_ref[0])
    def _():
        xb = x_ref[...].astype(BF16)
        gate = _dot(xb, wg_s[...])
        up = _dot(xb, wu_s[...])
        hid = (gate * jax.nn.sigmoid(gate) * up).astype(BF16)
        y_ref[...] = _dot(hid, wd_s[...])

    @pl.when(i >= nu_ref[0])
    def _():
        y_ref[...] = jnp.zeros(y_ref.shape, y_ref.dtype)


def _experts(block_e, n_used, xb, w_gate, w_up, w_down, layer):
    cap, d = xb.shape
    hdim = w_gate.shape[-1]
    return pl.pallas_call(
        _expert_kernel,
        grid_spec=pltpu.PrefetchScalarGridSpec(
            num_scalar_prefetch=2,
            grid=(cap // MOE_ROWS,),
            in_specs=[pl.BlockSpec((MOE_ROWS, d), lambda i, be, nu: (i, 0)),
                      pl.BlockSpec((None, 1, d, hdim), lambda i, be, nu: (layer, be[i], 0, 0)),
                      pl.BlockSpec((None, 1, d, hdim), lambda i, be, nu: (layer, be[i], 0, 0)),
                      pl.BlockSpec((None, 1, hdim, d), lambda i, be, nu: (layer, be[i], 0, 0))],
            out_specs=pl.BlockSpec((MOE_ROWS, d), lambda i, be, nu: (i, 0)),
            scratch_shapes=[pltpu.VMEM((d, hdim), BF16), pltpu.VMEM((d, hdim), BF16),
                            pltpu.VMEM((hdim, d), BF16)]),
        out_shape=jax.ShapeDtypeStruct((cap, d), F32),
        compiler_params=_cparams(("arbitrary",)),
        name="moe_experts",
    )(block_e, n_used, xb, w_gate, w_up, w_down)


def _combine_kernel(dest_ref, x_ref, rt_ref, g2_ref, fg_ref, yb_ref, o_ref, gath, sem, *,
                    tile_off, tiles_per_batch, final):
    b, i = pl.program_id(0), pl.program_id(1)
    base = (b * tiles_per_batch + i + tile_off) * TM

    def copy(r, k):
        d = dest_ref[(base + r) * 2 + k]
        return pltpu.make_async_copy(yb_ref.at[pl.ds(d, 1)], gath.at[k, pl.ds(r, 1)], sem)

    def issue(r, c):
        copy(r, 0).start()
        copy(r, 1).start()
        return c

    def drain(r, c):
        copy(r, 0).wait()
        copy(r, 1).wait()
        return c

    lax.fori_loop(0, TM, issue, 0, unroll=DMA_UNROLL)
    lax.fori_loop(0, TM, drain, 0, unroll=DMA_UNROLL)
    rt = rt_ref[0]
    m = rt[:, 2:3] * gath[0] + rt[:, 3:4] * gath[1]
    x = x_ref[0] + g2_ref[...] * m
    if final:
        x = _rms(x, fg_ref[...])
    o_ref[0] = x


def _combine(dest, x, route, mod, nbatch, final_g, yb, final):
    B, S, D = x.shape
    nt = S // TM
    off = 1 if final else 0
    g2_spec = pl.BlockSpec((None, 1, D_MODEL),
                           lambda b, i, dest: (jnp.where(i + off == 0, nbatch, b), 0, 5))
    return pl.pallas_call(
        functools.partial(_combine_kernel, tile_off=off, tiles_per_batch=nt, final=final),
        grid_spec=pltpu.PrefetchScalarGridSpec(
            num_scalar_prefetch=1,
            grid=(B, nt - off),
            in_specs=[pl.BlockSpec((1, TM, D), lambda b, i, dest: (b, i + off, 0)),
                      pl.BlockSpec((1, TM, LANES), lambda b, i, dest: (b, i + off, 0)),
                      g2_spec,
                      pl.BlockSpec((1, D), lambda b, i, dest: (0, 0)),
                      pl.BlockSpec(memory_space=pl.ANY)],
            out_specs=pl.BlockSpec((1, TM, D), lambda b, i, dest: (b, i, 0)),
            scratch_shapes=[pltpu.VMEM((2, TM, D), F32), pltpu.SemaphoreType.DMA(())]),
        out_shape=jax.ShapeDtypeStruct((B, S - off * TM, D), F32),
        compiler_params=_cparams(("arbitrary", "arbitrary")),
        name="moe_combine",
    )(dest, x, route, mod, final_g, yb)


def _rope_tables(seq):
    n_rows = seq // GRID_W
    pos = jnp.arange(max(n_rows, GRID_W), dtype=F32)[:, None]
    lane = jnp.arange(LANES, dtype=jnp.int32)
    tabs = []
    for r in (64, 32):
        nf = r // 4
        inv = 1.0 / (ROPE_BASE ** (jnp.arange(nf, dtype=F32) / nf))
        ang = pos * inv[lane % nf][None, :]
        first = ((lane % (2 * nf)) < nf)[None, :]
        by_row = ((lane % r) < 2 * nf)[None, None, :]
        for small in (jnp.cos(ang), jnp.where(first, -jnp.sin(ang), jnp.sin(ang))):
            full = jnp.where(by_row, small[:n_rows, None, :], small[None, :GRID_W, :])
            tabs.append(full.reshape(seq, LANES))
    ident = (jnp.ones((CTX_LEN, LANES), F32), jnp.zeros((CTX_LEN, LANES), F32))
    return [jnp.concatenate([ident[j % 2], tab], axis=0) for j, tab in enumerate(tabs)]


def _layer_weights(w_in, w_out, w_uq, w_ukv, w_group, b_group, w_expert, b_expert):
    perm = jnp.array([0, 2, 1, 3])
    wq = w_in[:, :256].reshape(D_MODEL, 4, 64)[:, perm].reshape(D_MODEL, 256)
    w_in_p = jnp.concatenate([wq, w_in[:, 256:], jnp.zeros((D_MODEL, D_IN_PAD - w_in.shape[1]), F32)],
                             axis=1).astype(BF16)
    wo = jnp.concatenate([w_out[:256].reshape(4, 64, D_MODEL)[perm].reshape(256, D_MODEL), w_out[256:]],
                         axis=0).astype(BF16)
    uq = w_uq.reshape(256, 4, 96)
    blocks = []
    for h in range(4):
        blk = jnp.zeros((256, 256), F32)
        blk = blk.at[:, 64 * (h % 2):64 * (h % 2) + 64].set(uq[:, h, :64])
        blk = blk.at[:, LANES:LANES + 32].set(uq[:, h, 64:])
        blocks.append(blk)
    wuq = jnp.concatenate(blocks, axis=1).astype(BF16)
    ukv = w_ukv.reshape(128, 4, 128)
    wukk = ukv[:, :, :64].reshape(128, 256).astype(BF16)
    wukv = ukv[:, :, 64:].reshape(128, 256).astype(BF16)
    pad = LANES - MOE_GROUPS - MOE_EXPERTS
    w_route = jnp.concatenate([w_group, w_expert, jnp.zeros((D_MODEL, pad), F32)], axis=1)
    w_hi = w_route.astype(BF16)
    w_route = jnp.stack([w_hi, (w_route - w_hi.astype(F32)).astype(BF16)])
    b_route = jnp.concatenate([b_group, b_expert, jnp.zeros((pad,), F32)]).reshape(1, LANES)
    return w_in_p, wo, wuq, wukk, wukv, w_route, b_route


def _moe_plan(route, counts, cap):
    e = route[:, :2].astype(jnp.int32)
    rank = route[:, 4:6].astype(jnp.int32)
    cnt = counts[0, :MOE_EXPERTS].astype(jnp.int32)
    padded = (cnt + MOE_ROWS - 1) // MOE_ROWS * MOE_ROWS
    pend = jnp.cumsum(padded)
    pstart = pend - padded
    dest = (pstart[e] + rank).reshape(-1)
    nblk = cap // MOE_ROWS
    blk_start = jnp.arange(nblk, dtype=jnp.int32) * MOE_ROWS
    block_e = jnp.minimum(jnp.sum((pend[None, :] <= blk_start[:, None]).astype(jnp.int32), axis=1),
                          MOE_EXPERTS - 1)
    n_used = (pend[-1:] // MOE_ROWS).astype(jnp.int32)
    return dest, block_e, n_used


def kernel(x, c, ctx, c_ctx, w_mod, b_mod, norm1_g, norm2_g, w_in, w_out, swa_sink, diff_lambda, diff_subln_g, ret_decay_logit, ret_gn_g, mla_q_norm_g, mla_w_uq, mla_kv_norm_g, mla_w_ukv, moe_w_group, moe_b_group, moe_w_expert, moe_b_expert, moe_w_gate, moe_w_up, moe_w_down, final_norm_g):
    B, T, D = x.shape
    S = CTX_LEN + T
    n_tok = B * S
    diff_tk, mla_tk = min(DIFF_TK, T), min(MLA_TK, T)
    cap = -(-(2 * n_tok + MOE_EXPERTS * (MOE_ROWS - 1)) // MOE_ROWS) * MOE_ROWS
    xs = jnp.concatenate([ctx, x], axis=1)
    cvec = jnp.zeros((8, D), F32).at[:B].set(c).at[B].set(c_ctx)
    tabs = _rope_tables(T)
    row = lambda v: v.reshape(1, -1)

    for l in range(DEPTH):
        lambda_init = 0.8 - 0.6 * math.exp(-0.3 * l)
        w_in_p, wo, wuq, wukk, wukv, w_route, b_route = _layer_weights(
            w_in[l], w_out[l], mla_w_uq[l], mla_w_ukv[l], moe_w_group[l], moe_b_group[l],
            moe_w_expert[l], moe_b_expert[l])
        mod = _modulation(cvec, w_mod[l], b_mod[l]).reshape(8, 1, 6 * D)
        (swaq, swak, swav, dq, dk, dv, rq, rk, rv, rg, mq, mk, mv) = _inproj(
            xs, mod, B, row(norm1_g[l]), w_in_p, tabs, row(mla_q_norm_g[l]), wuq,
            row(mla_kv_norm_g[l]), wukk, wukv)
        sink_p = swa_sink[l]
        ya = _swa(sink_p, swaq, swak, swav)
        yb = _diff(diff_lambda[l], diff_subln_g[l].reshape(64, 1), dq, dk, dv, lambda_init,
                   diff_tk)
        dl_lane = jnp.repeat(ret_decay_logit[l], 64, axis=-1).reshape(2, 2, 1, LANES)
        o_f, o_b = _retention(dl_lane, rq, rk, rv)
        ym = _mla(mq, mk, mv, mla_tk)
        xs, h2, route, counts = _outproj(xs, ya, yb, o_f, o_b, rg, row(ret_gn_g[l]), ym, wo, mod, B,
                                         row(norm2_g[l]), w_route, b_route)
        dest, block_e, n_used = _moe_plan(route.reshape(n_tok, LANES), counts, cap)
        xb = _dispatch(dest, h2.reshape(n_tok, D), jnp.zeros((cap, D), F32) if l == 0 else xb)
        yb_e = _experts(block_e, n_used, xb, moe_w_gate, moe_w_up, moe_w_down, l)
        final = l == DEPTH - 1
        xs = _combine(dest, xs, route, mod, B, row(final_norm_g), yb_e, final)
    return xs
```
